```python
import functools
import jax, jax.numpy as jnp
from jax import lax
import numpy as np

D_MODEL = 1024
BATCH = 2
SEQ = 16384
DEPTH = 1
DEC_BATCH = 32
DEC_SEQ = 16
PAST_LEN = 2048

CHUNK = 64
N_META = 16
GLA_HEADS = 4
GLA_DK = 128
GLA_DV = 256
GLA_RANK = 16
GLA_TAU = 16.0
GLA_BLOCK = 64
DSA_HEADS = 16
DSA_KV_HEADS = 4
DSA_HEAD_DIM = 64
DSA_GROUP = DSA_HEADS // DSA_KV_HEADS
DSA_SCALE = DSA_HEAD_DIM ** -0.5
IDX_HEADS = 8
IDX_DIM = 64
IDX_W_SCALE = (IDX_HEADS ** -0.5) * (IDX_DIM ** -0.5)
TOPK_MAX = 256
Q_BLOCK = 128
NORM_EPS = 1e-5
ALPHA = (2.0 * DEPTH) ** 0.25
BETA = (8.0 * DEPTH) ** -0.25
GLA_QK = GLA_HEADS * GLA_DK
GLA_V = GLA_HEADS * GLA_DV
DSA_Q = DSA_HEADS * DSA_HEAD_DIM
DSA_KV = DSA_KV_HEADS * DSA_HEAD_DIM
IDX_Q = IDX_HEADS * IDX_DIM
SPLITS = (GLA_QK, GLA_QK, GLA_V, GLA_RANK, GLA_V,
          DSA_Q, DSA_KV, DSA_KV, IDX_Q, IDX_DIM, IDX_HEADS, DSA_Q,
          D_MODEL, D_MODEL)
IN_COLS = sum(SPLITS)
SPLIT_OFFSETS = tuple(int(o) for o in np.cumsum(SPLITS)[:-1])

kernel_name = 'hybrid_gla_dsa_streaming_step'


def layer_norm(x, g, b):
    xf = x.astype(jnp.float32)
    mu = jnp.mean(xf, axis=-1, keepdims=True)
    xc = xf - mu
    var = jnp.mean(xc * xc, axis=-1, keepdims=True)
    return (xc * lax.rsqrt(var + NORM_EPS) * g.astype(jnp.float32) + b.astype(jnp.float32)).astype(x.dtype)


def gla_recurrence(q, k, v, log_f, s0):
    bsz, t = q.shape[0], q.shape[1]
    blk = min(GLA_BLOCK, t)
    nb = -(-t // blk)
    pad = nb * blk - t

    def to_blocks(a):
        a = jnp.pad(a.astype(jnp.float32), ((0, 0), (0, pad), (0, 0), (0, 0)))
        return a.reshape(bsz, nb, blk, a.shape[2], a.shape[3]).transpose(1, 0, 3, 2, 4)

    causal = jnp.tril(jnp.ones((blk, blk), dtype=bool))[:, :, None]

    def step(s, xs):
        qc, kc, vc, gc = xs
        b = jnp.cumsum(gc, axis=2)
        rel = jnp.where(causal, b[:, :, :, None, :] - b[:, :, None, :, :], -jnp.inf)
        a = jnp.einsum('bhid,bhjd,bhijd->bhij', qc, kc, jnp.exp(rel))
        o = (jnp.einsum('bhid,bhde->bhie', qc * jnp.exp(b), s)
             + jnp.einsum('bhij,bhje->bhie', a, vc))
        b_end = b[:, :, -1, :]
        s_new = (jnp.exp(b_end)[..., None] * s
                 + jnp.einsum('bhjd,bhje->bhde', kc * jnp.exp(b_end[:, :, None, :] - b), vc))
        return s_new, o

    s_t, o = lax.scan(step, s0.astype(jnp.float32),
                      (to_blocks(q), to_blocks(k), to_blocks(v), to_blocks(log_f)))
    o = o.transpose(1, 0, 3, 2, 4).reshape(bsz, nb * blk, q.shape[2], v.shape[3])[:, :t]
    return o, s_t


def dsa_select_attend(q, q_i, w_i, adm, k_all, v_all, ki_all, topk):
    bsz, nq = q.shape[0], q.shape[1]
    rel = jax.nn.relu(jnp.einsum('bqhd,bsd->bqhs', q_i.astype(jnp.float32), ki_all.astype(jnp.float32)))
    score = jnp.einsum('bqhs,bqh->bqs', rel, w_i.astype(jnp.float32))
    score = jnp.where(adm[None], score, -jnp.inf)
    top_val, top_idx = lax.top_k(score, topk)
    keep = jnp.isfinite(top_val)
    gather = jax.vmap(lambda rows, idx: rows[idx])
    k_sel = gather(k_all, top_idx)
    v_sel = gather(v_all, top_idx)
    qg = q.reshape(bsz, nq, DSA_KV_HEADS, DSA_GROUP, DSA_HEAD_DIM).astype(jnp.float32)
    logits = jnp.einsum('bqgrd,bqkgd->bqgrk', qg, k_sel.astype(jnp.float32)) * DSA_SCALE
    logits = jnp.where(keep[:, :, None, None, :], logits, -jnp.inf)
    p = jax.nn.softmax(logits, axis=-1)
    o = jnp.einsum('bqgrk,bqkgd->bqgrd', p, v_sel.astype(jnp.float32))
    return o.reshape(bsz, nq, DSA_Q).astype(q.dtype)


def chunk_ids(n):
    pos = jnp.arange(n)
    return jnp.where(pos < N_META, -1, (pos - N_META) // CHUNK)


def prompt_attend(q, q_i, w_i, k, v, k_i):
    bsz, t = q.shape[0], q.shape[1]
    nb = -(-t // Q_BLOCK)
    pad = nb * Q_BLOCK - t
    topk = min(TOPK_MAX, (t - N_META) // 4)
    chunk_k = chunk_ids(t)
    chunk_q = chunk_ids(nb * Q_BLOCK).reshape(nb, Q_BLOCK)

    def blocks(a):
        a = jnp.pad(a, [(0, 0), (0, pad)] + [(0, 0)] * (a.ndim - 2))
        return a.reshape((bsz, nb, Q_BLOCK) + a.shape[2:]).swapaxes(0, 1)

    def one_block(xs):
        qb, qib, wib, cq = xs
        adm = chunk_k[None, :] <= cq[:, None]
        return dsa_select_attend(qb, qib, wib, adm, k, v, k_i, topk)

    o = lax.map(one_block, (blocks(q), blocks(q_i), blocks(w_i), chunk_q))
    return o.swapaxes(0, 1).reshape(bsz, nb * Q_BLOCK, DSA_Q)[:, :t]


def sample_attend(cache_k_l, cache_v_l, cache_ik_l, q, q_i, w_i, k, v, k_i):
    k_all = jnp.concatenate([cache_k_l.astype(k.dtype), k], axis=1)
    v_all = jnp.concatenate([cache_v_l.astype(v.dtype), v], axis=1)
    ki_all = jnp.concatenate([cache_ik_l.astype(k_i.dtype), k_i], axis=1)
    n_keys = k_all.shape[1]
    topk = min(TOPK_MAX, n_keys // 4)
    adm = jnp.ones((q.shape[1], n_keys), dtype=bool)
    return dsa_select_attend(q, q_i, w_i, adm, k_all, v_all, ki_all, topk)


def layer_forward(h, s0, attend, w_in, gla_w2, gla_gate_b, gla_norm_g, idx_kn_g, idx_kn_b,
                  w_gla, w_dsa, gate_b, w_out, ln_g, ln_b):
    bsz, t, _ = h.shape
    (g_q, g_k, g_v, g_low, g_r, d_q, d_k, d_v, i_q, i_k, i_w, d_z, m_a, m_b) = jnp.split(
        h @ w_in, SPLIT_OFFSETS, axis=-1)
    q = (g_q * GLA_DK ** -0.5).reshape(bsz, t, GLA_HEADS, GLA_DK)
    k = g_k.reshape(bsz, t, GLA_HEADS, GLA_DK)
    v = g_v.reshape(bsz, t, GLA_HEADS, GLA_DV)
    log_f = (jax.nn.log_sigmoid((g_low @ gla_w2 + gla_gate_b).astype(jnp.float32)) / GLA_TAU
             ).reshape(bsz, t, GLA_HEADS, GLA_DK)
    o_a, s_t = gla_recurrence(q, k, v, log_f, s0)
    o_a = o_a * lax.rsqrt(jnp.mean(o_a * o_a, axis=-1, keepdims=True) + NORM_EPS) * gla_norm_g.astype(jnp.float32)
    y_a = (o_a.reshape(bsz, t, GLA_V).astype(h.dtype) * jax.nn.silu(g_r)) @ w_gla
    q_d = d_q.reshape(bsz, t, DSA_HEADS, DSA_HEAD_DIM)
    k_d = d_k.reshape(bsz, t, DSA_KV_HEADS, DSA_HEAD_DIM)
    v_d = d_v.reshape(bsz, t, DSA_KV_HEADS, DSA_HEAD_DIM)
    q_i = i_q.reshape(bsz, t, IDX_HEADS, IDX_DIM)
    k_i = layer_norm(i_k, idx_kn_g, idx_kn_b)
    w_i = i_w * IDX_W_SCALE
    o_b = attend(q_d, q_i, w_i, k_d, v_d, k_i)
    y_b = (o_b * jax.nn.silu(d_z)) @ w_dsa
    merged = jax.nn.sigmoid(m_a + gate_b[0]) * y_a + jax.nn.sigmoid(m_b + gate_b[1]) * y_b
    h_new = layer_norm(ALPHA * h + merged @ w_out, ln_g, ln_b)
    return h_new, k_d, v_d, k_i, s_t


def setup_inputs(seed: int = 0) -> dict:
    key = jax.random.key(seed)
    ks = jax.random.split(key, 24)
    n = jax.random.normal
    f32 = jnp.float32
    return {
        'x_prompt': n(ks[0], (BATCH, SEQ, D_MODEL), f32),
        'x_sample': n(ks[1], (DEC_BATCH, DEC_SEQ, D_MODEL), f32),
        'cache_k': n(ks[2], (DEPTH, DEC_BATCH, PAST_LEN, DSA_KV_HEADS, DSA_HEAD_DIM), f32),
        'cache_v': n(ks[3], (DEPTH, DEC_BATCH, PAST_LEN, DSA_KV_HEADS, DSA_HEAD_DIM), f32),
        'cache_idx_k': n(ks[4], (DEPTH, DEC_BATCH, PAST_LEN, IDX_DIM), f32),
        'state_gla': n(ks[5], (DEPTH, DEC_BATCH, GLA_HEADS, GLA_DK, GLA_DV), f32),
        'meta': n(ks[6], (N_META, D_MODEL), f32),
        'ln_in_g': 1.0 + 0.02 * n(ks[7], (D_MODEL,), f32),
        'ln_in_b': 0.02 * n(ks[8], (D_MODEL,), f32),
        'w_in': n(ks[9], (DEPTH, D_MODEL, IN_COLS), f32) * D_MODEL ** -0.5,
        'gla_w2': n(ks[10], (DEPTH, GLA_RANK, GLA_QK), f32) * GLA_RANK ** -0.5,
        'gla_gate_b': 0.1 * n(ks[11], (DEPTH, GLA_QK), f32),
        'gla_norm_g': 1.0 + 0.02 * n(ks[12], (DEPTH, GLA_DV), f32),
        'idx_kn_g': 1.0 + 0.02 * n(ks[13], (DEPTH, IDX_DIM), f32),
        'idx_kn_b': 0.02 * n(ks[14], (DEPTH, IDX_DIM), f32),
        'w_gla': n(ks[15], (DEPTH, GLA_V, D_MODEL), f32) * (GLA_V ** -0.5) * BETA,
        'w_dsa': n(ks[16], (DEPTH, DSA_Q, D_MODEL), f32) * (DSA_Q ** -0.5) * BETA,
        'gate_b': 0.02 * n(ks[17], (DEPTH, 2, D_MODEL), f32),
        'w_out': n(ks[18], (DEPTH, D_MODEL, D_MODEL), f32) * (D_MODEL ** -0.5) * BETA,
        'ln_g': 1.0 + 0.02 * n(ks[19], (DEPTH, D_MODEL), f32),
        'ln_b': 0.02 * n(ks[20], (DEPTH, D_MODEL), f32),
    }


def reference(x_prompt, x_sample, cache_k, cache_v, cache_idx_k, state_gla, meta, ln_in_g, ln_in_b,
              w_in, gla_w2, gla_gate_b, gla_norm_g, idx_kn_g, idx_kn_b, w_gla, w_dsa, gate_b,
              w_out, ln_g, ln_b):
    bsz = x_prompt.shape[0]
    meta_rows = jnp.broadcast_to(meta.astype(x_prompt.dtype)[None], (bsz, N_META, meta.shape[-1]))
    h_p = layer_norm(jnp.concatenate([meta_rows, x_prompt], axis=1), ln_in_g, ln_in_b)
    h_s = layer_norm(x_sample, ln_in_g, ln_in_b)
    s0_p = jnp.zeros((bsz, GLA_HEADS, GLA_DK, GLA_DV), jnp.float32)
    kp, vp, ikp, sp = [], [], [], []
    ksm, vsm, iks, ssm = [], [], [], []
    for l in range(DEPTH):
        weights = (w_in[l], gla_w2[l], gla_gate_b[l], gla_norm_g[l], idx_kn_g[l], idx_kn_b[l],
                   w_gla[l], w_dsa[l], gate_b[l], w_out[l], ln_g[l], ln_b[l])
        h_p, k_l, v_l, ik_l, s_l = layer_forward(h_p, s0_p, prompt_attend, *weights)
        kp.append(k_l); vp.append(v_l); ikp.append(ik_l); sp.append(s_l)
        attend_s = functools.partial(sample_attend, cache_k[l], cache_v[l], cache_idx_k[l])
        h_s, k_l, v_l, ik_l, s_l = layer_forward(h_s, state_gla[l], attend_s, *weights)
        ksm.append(k_l); vsm.append(v_l); iks.append(ik_l); ssm.append(s_l)
    y_prompt = h_p[:, N_META:]
    y_sample = h_s
    k_prompt = jnp.stack(kp, axis=0)
    v_prompt = jnp.stack(vp, axis=0)
    idx_k_prompt = jnp.stack(ikp, axis=0)
    gla_prompt = jnp.stack(sp, axis=0)
    k_sample = jnp.stack(ksm, axis=0)
    v_sample = jnp.stack(vsm, axis=0)
    idx_k_sample = jnp.stack(iks, axis=0)
    gla_sample = jnp.stack(ssm, axis=0)
    return (y_prompt, y_sample, k_prompt, v_prompt, idx_k_prompt, gla_prompt,
            k_sample, v_sample, idx_k_sample, gla_sample)
```

```python
import functools
import math

import numpy as np
import jax
import jax.numpy as jnp
from jax import lax
from jax.experimental import pallas as pl
from jax.experimental.pallas import tpu as pltpu

F32 = jnp.float32
BF16 = jnp.bfloat16
I32 = jnp.int32

D_MODEL = 1024
CHUNK = 64
N_META = 16
GLA_HEADS = 4
GLA_DK = 128
GLA_DV = 256
GLA_RANK = 16
GLA_TAU = 16.0
GLA_BLOCK = 64
DSA_HEADS = 16
DSA_KV_HEADS = 4
DSA_HEAD_DIM = 64
DSA_GROUP = DSA_HEADS // DSA_KV_HEADS
DSA_SCALE = DSA_HEAD_DIM ** -0.5
IDX_HEADS = 8
IDX_DIM = 64
IDX_W_SCALE = (IDX_HEADS ** -0.5) * (IDX_DIM ** -0.5)
TOPK_MAX = 256
NORM_EPS = 1e-5
GLA_QK = GLA_HEADS * GLA_DK
GLA_V = GLA_HEADS * GLA_DV
DSA_Q = DSA_HEADS * DSA_HEAD_DIM
DSA_KV = DSA_KV_HEADS * DSA_HEAD_DIM
IDX_Q = IDX_HEADS * IDX_DIM
SPLITS = (GLA_QK, GLA_QK, GLA_V, GLA_RANK, GLA_V,
          DSA_Q, DSA_KV, DSA_KV, IDX_Q, IDX_DIM, IDX_HEADS, DSA_Q,
          D_MODEL, D_MODEL)
_OFF = tuple(int(o) for o in np.cumsum((0,) + SPLITS))
(_GQ, _GK, _GV, _GLOW, _GR, _DQ, _DK, _DV, _IQ, _IK, _IW, _DZ, _MA, _MB) = range(14)

LANES = 128
VMEM_LIMIT_BYTES = 56 * 1024 * 1024

F_GQ, F_GK, F_DK, F_DV, F_SMALL = 0, 512, 1024, 1280, 1536
F_COLS = F_SMALL + LANES
SM_IK, SM_GLOW, SM_IW = 0, IDX_DIM, IDX_DIM + GLA_RANK
B_GV, B_GR, B_DQ, B_DZ, B_MA, B_MB, B_IQ, B_PAD = 0, 1024, 2048, 3072, 4096, 5120, 6144, 6656
B_COLS = 7168

ROW_TILE = 2048
KEY_TILE = 512
NEG_BIG = -1e30
INT_MIN = -(2 ** 31)
NEG_INF_KEY = int(np.array(-np.inf, np.float32).view(np.int32)) ^ 0x7FFFFFFF


def _col(w, idx):
    return w[:, _OFF[idx]:_OFF[idx + 1]]


def _layer_norm_rows(x, g, b):
    mu = jnp.mean(x, axis=-1, keepdims=True)
    xc = x - mu
    var = jnp.mean(xc * xc, axis=-1, keepdims=True)
    return xc * lax.rsqrt(var + NORM_EPS) * g + b


def _dot(a, b):
    return jnp.dot(a, b, preferred_element_type=F32)


def _dot_nt(a, b):
    return lax.dot_general(a, b, (((1,), (1,)), ((), ())), preferred_element_type=F32)


def _dot_tn(a, b):
    return lax.dot_general(a, b, (((0,), (0,)), ((), ())), preferred_element_type=F32)


def _proj_f32_kernel(x_ref, g_ref, b_ref, w_ref, kg_ref, kb_ref, o_ref):
    hn = _layer_norm_rows(x_ref[...], g_ref[...], b_ref[...])
    y = _dot(hn.astype(BF16), w_ref[...])
    o_ref[...] = y
    ik = y[:, F_SMALL:F_SMALL + IDX_DIM]
    o_ref[:, F_SMALL:F_SMALL + IDX_DIM] = _layer_norm_rows(ik, kg_ref[...], kb_ref[...])


def _proj_f32(x, g, b, w, kg, kb, tm):
    rows = x.shape[0]
    return pl.pallas_call(
        _proj_f32_kernel,
        grid=(rows // tm,),
        in_specs=[
            pl.BlockSpec((tm, D_MODEL), lambda i: (i, 0)),
            pl.BlockSpec((1, D_MODEL), lambda i: (0, 0)),
            pl.BlockSpec((1, D_MODEL), lambda i: (0, 0)),
            pl.BlockSpec((D_MODEL, F_COLS), lambda i: (0, 0)),
            pl.BlockSpec((1, IDX_DIM), lambda i: (0, 0)),
            pl.BlockSpec((1, IDX_DIM), lambda i: (0, 0)),
        ],
        out_specs=pl.BlockSpec((tm, F_COLS), lambda i: (i, 0)),
        out_shape=jax.ShapeDtypeStruct((rows, F_COLS), F32),
        compiler_params=pltpu.CompilerParams(
            dimension_semantics=("parallel",), vmem_limit_bytes=VMEM_LIMIT_BYTES),
        name="proj_f32",
    )(x, g, b, w, kg, kb)


def _proj_bf16_kernel(x_ref, g_ref, b_ref, w_ref, o_ref, hn_ref):
    @pl.when(pl.program_id(1) == 0)
    def _():
        hn_ref[...] = _layer_norm_rows(x_ref[...], g_ref[...], b_ref[...]).astype(BF16)

    o_ref[...] = _dot(hn_ref[...], w_ref[...]).astype(BF16)


def _proj_bf16(x, g, b, w, tm, tn):
    rows = x.shape[0]
    return pl.pallas_call(
        _proj_bf16_kernel,
        grid=(rows // tm, B_COLS // tn),
        in_specs=[
            pl.BlockSpec((tm, D_MODEL), lambda i, j: (i, 0)),
            pl.BlockSpec((1, D_MODEL), lambda i, j: (0, 0)),
            pl.BlockSpec((1, D_MODEL), lambda i, j: (0, 0)),
            pl.BlockSpec((D_MODEL, tn), lambda i, j: (0, j)),
        ],
        out_specs=pl.BlockSpec((tm, tn), lambda i, j: (i, j)),
        out_shape=jax.ShapeDtypeStruct((rows, B_COLS), BF16),
        scratch_shapes=[pltpu.VMEM((tm, D_MODEL), BF16)],
        compiler_params=pltpu.CompilerParams(
            dimension_semantics=("parallel", "arbitrary"), vmem_limit_bytes=VMEM_LIMIT_BYTES),
        name="proj_bf16",
    )(x, g, b, w)


def _gla_levels(blk):
    levels = []
    s = blk // 2
    while s >= 1:
        levels.append(s)
        s //= 2
    return levels


def _gla_decay_matrix(blk):
    i = np.arange(blk)[:, None]
    t = np.arange(blk)[None, :]
    mats = [(t <= i).astype(np.float32), (t > i).astype(np.float32)]
    for s in _gla_levels(blk):
        mid = (i // (2 * s)) * (2 * s) + s - 1
        lower = (i // s) % 2 == 1
        m = np.where(lower, ((t > mid) & (t <= i)), False).astype(np.float32)
        n = np.where(~lower, ((t > i) & (t <= mid)), False).astype(np.float32)
        mats.append(m + n)
    return np.concatenate(mats, axis=0)


def _gla_kernel(q_ref, k_ref, sm_ref, v_ref, r_ref, c_ref, w2_ref, gb_ref, ng_ref, s0_ref,
                o_ref, st_ref, s_scr, *, blk, n_blk, t_valid):
    ti = pl.program_id(1)
    levels = _gla_levels(blk)

    @pl.when(ti == 0)
    def _():
        s_scr[...] = s0_ref[0]

    ri = lax.broadcasted_iota(I32, (blk, blk), 0)
    ci = lax.broadcasted_iota(I32, (blk, blk), 1)
    eye = ri == ci
    lvl_masks = []
    for s in levels:
        sh = s.bit_length() - 1
        same = (ri >> (sh + 1)) == (ci >> (sh + 1))
        lvl_masks.append(same & (((ri >> sh) & 1) == 1) & (((ci >> sh) & 1) == 0))
    row_iota = lax.broadcasted_iota(I32, (blk, 1), 0)
    cmat = c_ref[...]
    w2 = w2_ref[...]
    gbias = gb_ref[...]
    ng = ng_ref[...]

    def block(j, carry):
        r0 = pl.multiple_of(j * blk, blk)
        rows = pl.ds(r0, blk)
        valid = (ti * (n_blk * blk) + r0 + row_iota) < t_valid
        gq = q_ref[rows, :] * (GLA_DK ** -0.5)
        gk = jnp.where(valid, k_ref[rows, :], 0.0)
        x = _dot(sm_ref[rows, :].astype(BF16), w2) + gbias
        logf = (jnp.minimum(x, 0.0) - jnp.log1p(jnp.exp(-jnp.abs(x)))) * (1.0 / GLA_TAU)
        logf = jnp.where(valid, logf, 0.0)
        hi = logf.astype(BF16)
        r1 = logf - hi.astype(F32)
        mid = r1.astype(BF16)
        lo = (r1 - mid.astype(F32)).astype(BF16)
        e_all = _dot(cmat, hi) + _dot(cmat, mid) + _dot(cmat, lo)
        for h in range(GLA_HEADS):
            ksl = slice(h * GLA_DK, (h + 1) * GLA_DK)
            vsl = slice(h * GLA_DV, (h + 1) * GLA_DV)
            qh = gq[:, ksl]
            kh = gk[:, ksl]
            vh = v_ref[rows, vsl]
            b_h = e_all[0:blk, ksl]
            rev_h = e_all[blk:2 * blk, ksl]
            st = s_scr[h]
            o = _dot_nt((qh * jnp.exp(b_h)).astype(BF16), st.astype(BF16))
            a = jnp.where(eye, _dot_nt(qh.astype(BF16), kh.astype(BF16)), 0.0)
            for li in range(len(levels)):
                e = jnp.exp(e_all[(2 + li) * blk:(3 + li) * blk, ksl])
                p = _dot_nt((qh * e).astype(BF16), (kh * e).astype(BF16))
                a = a + jnp.where(lvl_masks[li], p, 0.0)
            o = o + _dot(a.astype(BF16), vh)
            kd = (kh * jnp.exp(rev_h)).astype(BF16)
            dec = jnp.exp(b_h[blk - 1:blk, :])
            s_scr[h] = st * dec + _dot_tn(vh, kd)
            on = o * lax.rsqrt(jnp.mean(o * o, axis=-1, keepdims=True) + NORM_EPS) * ng
            gr = r_ref[rows, vsl].astype(F32)
            o_ref[rows, vsl] = (on * (gr * jax.nn.sigmoid(gr))).astype(BF16)
        return carry

    lax.fori_loop(0, n_blk, block, 0)

    @pl.when(ti == pl.num_programs(1) - 1)
    def _():
        st_ref[0] = s_scr[...]


def _gla(pf, pb, cmat, w2pad, gbias, ng, s0t, *, n_seq, row0, rows_per_seq, t_valid, blk, n_blk):
    rb = blk * n_blk
    steps = rows_per_seq // rb
    base = row0 // rb

    def rmap(c):
        return lambda s, t: (base + s * steps + t, c)

    kern = functools.partial(_gla_kernel, blk=blk, n_blk=n_blk, t_valid=t_valid)
    return pl.pallas_call(
        kern,
        grid=(n_seq, steps),
        in_specs=[
            pl.BlockSpec((rb, GLA_QK), rmap(F_GQ // GLA_QK)),
            pl.BlockSpec((rb, GLA_QK), rmap(F_GK // GLA_QK)),
            pl.BlockSpec((rb, LANES), rmap(F_SMALL // LANES)),
            pl.BlockSpec((rb, GLA_V), rmap(B_GV // GLA_V)),
            pl.BlockSpec((rb, GLA_V), rmap(B_GR // GLA_V)),
            pl.BlockSpec(cmat.shape, lambda s, t: (0, 0)),
            pl.BlockSpec(w2pad.shape, lambda s, t: (0, 0)),
            pl.BlockSpec((1, GLA_QK), lambda s, t: (0, 0)),
            pl.BlockSpec((1, GLA_DV), lambda s, t: (0, 0)),
            pl.BlockSpec((1, GLA_HEADS, GLA_DV, GLA_DK), lambda s, t: (s, 0, 0, 0)),
        ],
        out_specs=[
            pl.BlockSpec((rb, GLA_V), lambda s, t: (s * steps + t, 0)),
            pl.BlockSpec((1, GLA_HEADS, GLA_DV, GLA_DK), lambda s, t: (s, 0, 0, 0)),
        ],
        out_shape=[
            jax.ShapeDtypeStruct((n_seq * rows_per_seq, GLA_V), BF16),
            jax.ShapeDtypeStruct((n_seq, GLA_HEADS, GLA_DV, GLA_DK), F32),
        ],
        scratch_shapes=[pltpu.VMEM((GLA_HEADS, GLA_DV, GLA_DK), F32)],
        compiler_params=pltpu.CompilerParams(
            dimension_semantics=("parallel", "arbitrary"), vmem_limit_bytes=VMEM_LIMIT_BYTES),
        name="gla",
    )(pf, pf, pf, pb, pb, cmat, w2pad, gbias, ng, s0t)


def _dsa_kernel(iq_ref, sm_ref, qd_ref, kit_ref, kt_ref, v_ref, o_ref,
                key_scr, m_scr, l_scr, acc_scr, *, tq, tk, n_keys, topk, causal, n_kb_total):
    qb = pl.program_id(1)
    n_rows_g = DSA_GROUP * tq

    if causal:
        q_chunk_max = (qb * tq + tq - 1 - N_META) >> 6
        limit = jnp.minimum(N_META + CHUNK * (q_chunk_max + 1), n_keys)
        n_kb = (limit + tk - 1) // tk
    else:
        n_kb = n_kb_total

    q_pos = qb * tq + lax.broadcasted_iota(I32, (tq, 1), 0)
    q_chunk = (q_pos - N_META) >> 6
    lane_iota = lax.broadcasted_iota(I32, (1, tk), 1)

    iq = iq_ref[...]
    iq_hm = jnp.concatenate([iq[:, h * IDX_DIM:(h + 1) * IDX_DIM] for h in range(IDX_HEADS)], axis=0)
    w_i = sm_ref[:, SM_IW:SM_IW + IDX_HEADS] * IDX_W_SCALE
    w_cols = [jnp.broadcast_to(w_i[:, h:h + 1], (tq, LANES)) for h in range(IDX_HEADS)]

    def score_block(kb, carry):
        k0 = pl.multiple_of(kb * tk, tk)
        r = _dot(iq_hm, kit_ref[0, :, pl.ds(k0, tk)])
        sc = None
        for h in range(IDX_HEADS):
            rh = jnp.maximum(r[h * tq:(h + 1) * tq, :], 0.0)
            wh = jnp.concatenate([w_cols[h]] * (tk // LANES), axis=1)
            sc = rh * wh if sc is None else sc + rh * wh
        key_pos = k0 + lane_iota
        adm = key_pos < n_keys
        if causal:
            adm = adm & (((key_pos - N_META) >> 6) <= q_chunk)
        sc = jnp.where(adm, sc, -jnp.inf)
        bits = pltpu.bitcast(sc, I32)
        key_scr[:, pl.ds(k0, tk)] = bits ^ ((bits >> 31) & 0x7FFFFFFF)
        return carry

    lax.fori_loop(0, n_kb, score_block, 0)

    n_cols = n_kb * (tk // LANES)

    def count_ge(cand):
        cb = jnp.broadcast_to(cand, (tq, LANES))

        def body(c, acc):
            c0 = pl.multiple_of(c * LANES, LANES)
            return acc + jnp.where(key_scr[:, pl.ds(c0, LANES)] >= cb, 1, 0)

        acc = lax.fori_loop(0, n_cols, body, jnp.zeros((tq, LANES), I32))
        return jnp.sum(acc, axis=1, keepdims=True)

    def bit_step(i, carry):
        lo, cge = carry
        cand = lo + jnp.left_shift(jnp.int32(1), 31 - i)
        cnt = count_ge(cand)
        take = cnt >= topk
        return jnp.where(take, cand, lo), jnp.where(take, cnt, cge)

    lo0 = jnp.full((tq, 1), INT_MIN, I32)
    cge0 = jnp.zeros((tq, 1), I32) + n_cols * LANES
    thr, cge = lax.fori_loop(0, 32, bit_step, (lo0, cge0))
    need_tie = (cge > topk) & (thr > NEG_INF_KEY)
    thr = jnp.maximum(thr, NEG_INF_KEY + 1)

    @pl.when(jnp.max(need_tie.astype(I32)) > 0)
    def _():
        want = topk - count_ge(thr + 1)
        thr_b = jnp.broadcast_to(thr, (tq, LANES))
        col_iota = lax.broadcasted_iota(I32, (tq, LANES), 1)

        def count_tied_below(m):
            mb = jnp.broadcast_to(m, (tq, LANES))

            def body(c, acc):
                c0 = pl.multiple_of(c * LANES, LANES)
                hit = (key_scr[:, pl.ds(c0, LANES)] == thr_b) & ((c0 + col_iota) < mb)
                return acc + jnp.where(hit, 1, 0)

            acc = lax.fori_loop(0, n_cols, body, jnp.zeros((tq, LANES), I32))
            return jnp.sum(acc, axis=1, keepdims=True)

        n_bits = int(math.ceil(math.log2(n_kb_total * tk + 1)))

        def idx_step(i, m):
            cand = m + jnp.left_shift(jnp.int32(1), n_bits - 1 - i)
            return jnp.where(count_tied_below(cand) <= want, cand, m)

        m_keep = lax.fori_loop(0, n_bits, idx_step, jnp.zeros((tq, 1), I32))
        mk_b = jnp.broadcast_to(jnp.where(need_tie, m_keep, jnp.int32(2 ** 30)), (tq, LANES))

        def demote(c, carry):
            c0 = pl.multiple_of(c * LANES, LANES)
            kv = key_scr[:, pl.ds(c0, LANES)]
            drop = (kv == thr_b) & ((c0 + col_iota) >= mk_b)
            key_scr[:, pl.ds(c0, LANES)] = jnp.where(drop, kv - 1, kv)
            return carry

        lax.fori_loop(0, n_cols, demote, 0)

    qd = qd_ref[...]
    q_groups = []
    for g in range(DSA_KV_HEADS):
        qg = jnp.concatenate(
            [qd[:, (g * DSA_GROUP + r) * DSA_HEAD_DIM:(g * DSA_GROUP + r + 1) * DSA_HEAD_DIM]
             for r in range(DSA_GROUP)], axis=0)
        q_groups.append((qg.astype(F32) * DSA_SCALE).astype(BF16))
    m_scr[...] = jnp.full(m_scr.shape, NEG_BIG, F32)
    l_scr[...] = jnp.zeros(l_scr.shape, F32)
    acc_scr[...] = jnp.zeros(acc_scr.shape, F32)
    thr_t = jnp.broadcast_to(thr, (tq, LANES))

    def attend_block(kb, carry):
        k0 = pl.multiple_of(kb * tk, tk)
        sel = key_scr[:, pl.ds(k0, tk)] >= jnp.concatenate([thr_t] * (tk // LANES), axis=1)
        bias = jnp.where(sel, 0.0, NEG_BIG)
        bias_g = jnp.concatenate([bias] * DSA_GROUP, axis=0)
        for g in range(DSA_KV_HEADS):
            s = _dot(q_groups[g], kt_ref[0, g, :, pl.ds(k0, tk)]) + bias_g
            m_old = m_scr[g]
            m_new = jnp.maximum(m_old, jnp.max(s, axis=1, keepdims=True))
            alpha = jnp.exp(m_old - m_new)
            p = jnp.exp(s - m_new)
            p = jnp.where(bias_g < 0.0, 0.0, p)
            l_scr[g] = alpha * l_scr[g] + jnp.sum(p, axis=1, keepdims=True)
            acc_scr[g] = alpha * acc_scr[g] + _dot(p.astype(BF16), v_ref[0, g, pl.ds(k0, tk), :])
            m_scr[g] = m_new
        return carry

    lax.fori_loop(0, n_kb, attend_block, 0)

    for g in range(DSA_KV_HEADS):
        og = acc_scr[g] / l_scr[g]
        for r in range(DSA_GROUP):
            c0 = (g * DSA_GROUP + r) * DSA_HEAD_DIM
            o_ref[:, c0:c0 + DSA_HEAD_DIM] = og[r * tq:(r + 1) * tq, :].astype(BF16)


def _dsa(pf, pb, kit, kt, v, *, n_seq, row0, rows_per_seq, tq, tk, n_keys, topk, causal):
    nk_pad = kit.shape[-1]
    n_kb_total = nk_pad // tk
    steps = rows_per_seq // tq
    base = row0 // tq

    def rmap(c):
        return lambda s, t: (base + s * steps + t, c)

    kern = functools.partial(_dsa_kernel, tq=tq, tk=tk, n_keys=n_keys, topk=topk, causal=causal,
                             n_kb_total=n_kb_total)
    single = pl.Buffered(1)
    return pl.pallas_call(
        kern,
        grid=(n_seq, steps),
        in_specs=[
            pl.BlockSpec((tq, IDX_Q), rmap(B_IQ // IDX_Q)),
            pl.BlockSpec((tq, LANES), rmap(F_SMALL // LANES)),
            pl.BlockSpec((tq, DSA_Q), rmap(B_DQ // DSA_Q)),
            pl.BlockSpec((1, IDX_DIM, nk_pad), lambda s, t: (s, 0, 0), pipeline_mode=single),
            pl.BlockSpec((1, DSA_KV_HEADS, DSA_HEAD_DIM, nk_pad), lambda s, t: (s, 0, 0, 0),
                         pipeline_mode=single),
            pl.BlockSpec((1, DSA_KV_HEADS, nk_pad, DSA_HEAD_DIM), lambda s, t: (s, 0, 0, 0),
                         pipeline_mode=single),
        ],
        out_specs=pl.BlockSpec((tq, DSA_Q), lambda s, t: (s * steps + t, 0)),
        out_shape=jax.ShapeDtypeStruct((n_seq * rows_per_seq, DSA_Q), BF16),
        scratch_shapes=[
            pltpu.VMEM((tq, nk_pad), I32),
            pltpu.VMEM((DSA_KV_HEADS, DSA_GROUP * tq, 1), F32),
            pltpu.VMEM((DSA_KV_HEADS, DSA_GROUP * tq, 1), F32),
            pltpu.VMEM((DSA_KV_HEADS, DSA_GROUP * tq, DSA_HEAD_DIM), F32),
        ],
        compiler_params=pltpu.CompilerParams(
            dimension_semantics=("parallel", "arbitrary"), vmem_limit_bytes=VMEM_LIMIT_BYTES),
        name="dsa",
    )(pb, pf, pb, kit, kt, v)


def _out_kernel(x_ref, ya_ref, ob_ref, z_ref, ma_ref, mb_ref, wg_ref, wd_ref, wo_ref,
                gate_ref, lig_ref, lib_ref, lg_ref, lb_ref, o_ref, *, alpha):
    hn = _layer_norm_rows(x_ref[...], lig_ref[...], lib_ref[...])
    y_a = _dot(ya_ref[...], wg_ref[...])
    z = z_ref[...].astype(F32)
    yb_in = ob_ref[...].astype(F32) * (z * jax.nn.sigmoid(z))
    y_b = _dot(yb_in.astype(BF16), wd_ref[...])
    ga = jax.nn.sigmoid(ma_ref[...].astype(F32) + gate_ref[0:1, :])
    gb = jax.nn.sigmoid(mb_ref[...].astype(F32) + gate_ref[1:2, :])
    merged = ga * y_a + gb * y_b
    y = alpha * hn + _dot(merged.astype(BF16), wo_ref[...])
    o_ref[...] = _layer_norm_rows(y, lg_ref[...], lb_ref[...])


def _out(x, ya, ob, pb, wg, wd, wo, gate, lig, lib, lg, lb, tm, alpha):
    rows = x.shape[0]
    row = lambda c: pl.BlockSpec((tm, D_MODEL), lambda i, c=c: (i, c))
    full = lambda a: pl.BlockSpec(a.shape, lambda i: (0, 0))
    return pl.pallas_call(
        functools.partial(_out_kernel, alpha=alpha),
        grid=(rows // tm,),
        in_specs=[row(0), row(0), row(0), row(B_DZ // D_MODEL), row(B_MA // D_MODEL), row(B_MB // D_MODEL),
                  full(wg), full(wd), full(wo), full(gate), full(lig), full(lib), full(lg), full(lb)],
        out_specs=pl.BlockSpec((tm, D_MODEL), lambda i: (i, 0)),
        out_shape=jax.ShapeDtypeStruct((rows, D_MODEL), F32),
        compiler_params=pltpu.CompilerParams(
            dimension_semantics=("parallel",), vmem_limit_bytes=VMEM_LIMIT_BYTES),
        name="out_proj",
    )(x, ya, ob, pb, pb, pb, wg, wd, wo, gate, lig, lib, lg, lb)


def _round_up(a, b):
    return -(-a // b) * b


def _forward(x_prompt, x_sample, cache_k, cache_v, cache_idx_k, state_gla, meta, ln_in_g, ln_in_b,
             w_in, gla_w2, gla_gate_b, gla_norm_g, idx_kn_g, idx_kn_b, w_gla, w_dsa, gate_b,
             w_out, ln_g, ln_b):
    depth = w_in.shape[0]
    assert depth == 1, "single-layer trunk"
    bsz, seq, _ = x_prompt.shape
    dbsz, dseq, _ = x_sample.shape
    past = cache_k.shape[2]
    t_p = N_META + seq
    tq_p = 128
    tp = _round_up(t_p, KEY_TILE)
    assert tp % tq_p == 0 and tp % GLA_BLOCK == 0
    row_s = bsz * tp
    rows = _round_up(row_s + dbsz * dseq, ROW_TILE)
    alpha = (2.0 * depth) ** 0.25

    meta_rows = jnp.broadcast_to(meta.astype(F32)[None], (bsz, N_META, D_MODEL))
    xp = jnp.concatenate([meta_rows, x_prompt, jnp.zeros((bsz, tp - t_p, D_MODEL), F32)], axis=1)
    x_all = jnp.concatenate([xp.reshape(bsz * tp, D_MODEL), x_sample.reshape(dbsz * dseq, D_MODEL),
                             jnp.zeros((rows - row_s - dbsz * dseq, D_MODEL), F32)], axis=0)

    w = w_in[0]
    w_f = jnp.concatenate([_col(w, _GQ), _col(w, _GK), _col(w, _DK), _col(w, _DV), _col(w, _IK),
                           _col(w, _GLOW), _col(w, _IW),
                           jnp.zeros((D_MODEL, LANES - IDX_DIM - GLA_RANK - IDX_HEADS), F32)], axis=1).astype(BF16)
    w_b = jnp.concatenate([_col(w, _GV), _col(w, _GR), _col(w, _DQ), _col(w, _DZ), _col(w, _MA), _col(w, _MB),
                           _col(w, _IQ), jnp.zeros((D_MODEL, B_COLS - B_PAD), F32)], axis=1).astype(BF16)
    lig = ln_in_g.reshape(1, D_MODEL)
    lib = ln_in_b.reshape(1, D_MODEL)

    pf = _proj_f32(x_all, lig, lib, w_f, idx_kn_g[0].reshape(1, IDX_DIM), idx_kn_b[0].reshape(1, IDX_DIM),
                   ROW_TILE // 4)
    pb = _proj_bf16(x_all, lig, lib, w_b, ROW_TILE // 2, 1024)

    w2pad = jnp.zeros((LANES, GLA_QK), F32).at[SM_GLOW:SM_GLOW + GLA_RANK].set(gla_w2[0]).astype(BF16)
    gbias = gla_gate_b[0].reshape(1, GLA_QK)
    ng = gla_norm_g[0].reshape(1, GLA_DV)
    blk_p = min(GLA_BLOCK, t_p)
    cm_p = jnp.asarray(_gla_decay_matrix(blk_p), BF16)
    s0_p = jnp.zeros((bsz, GLA_HEADS, GLA_DV, GLA_DK), F32)
    ya_p, st_p = _gla(pf, pb, cm_p, w2pad, gbias, ng, s0_p, n_seq=bsz, row0=0, rows_per_seq=tp,
                      t_valid=t_p, blk=blk_p, n_blk=KEY_TILE // blk_p)
    blk_s = min(GLA_BLOCK, dseq)
    assert dseq % blk_s == 0 and row_s % dseq == 0
    cm_s = jnp.asarray(_gla_decay_matrix(blk_s), BF16)
    s0_s = jnp.swapaxes(state_gla[0], -1, -2)
    ya_s, st_s = _gla(pf, pb, cm_s, w2pad, gbias, ng, s0_s, n_seq=dbsz, row0=row_s, rows_per_seq=dseq,
                      t_valid=dseq, blk=blk_s, n_blk=dseq // blk_s)

    dk = pf[:, F_DK:F_DK + DSA_KV]
    dv = pf[:, F_DV:F_DV + DSA_KV]
    ki = pf[:, F_SMALL:F_SMALL + IDX_DIM]
    dk_p = dk[:row_s].reshape(bsz, tp, DSA_KV_HEADS, DSA_HEAD_DIM)
    dv_p = dv[:row_s].reshape(bsz, tp, DSA_KV_HEADS, DSA_HEAD_DIM)
    ki_p = ki[:row_s].reshape(bsz, tp, IDX_DIM)
    kt_p = jnp.transpose(dk_p, (0, 2, 3, 1)).astype(BF16)
    v_p = jnp.transpose(dv_p, (0, 2, 1, 3)).astype(BF16)
    kit_p = jnp.transpose(ki_p, (0, 2, 1)).astype(BF16)
    topk_p = min(TOPK_MAX, (t_p - N_META) // 4)
    ob_p = _dsa(pf, pb, kit_p, kt_p, v_p, n_seq=bsz, row0=0, rows_per_seq=tp, tq=tq_p, tk=KEY_TILE,
                n_keys=t_p, topk=topk_p, causal=True)

    n_keys_s = past + dseq
    nk_s = _round_up(n_keys_s, KEY_TILE)
    dk_s = dk[row_s:row_s + dbsz * dseq].reshape(dbsz, dseq, DSA_KV_HEADS, DSA_HEAD_DIM)
    dv_s = dv[row_s:row_s + dbsz * dseq].reshape(dbsz, dseq, DSA_KV_HEADS, DSA_HEAD_DIM)
    ki_s = ki[row_s:row_s + dbsz * dseq].reshape(dbsz, dseq, IDX_DIM)
    kpad = jnp.zeros((dbsz, nk_s - n_keys_s, DSA_KV_HEADS, DSA_HEAD_DIM), F32)
    k_all = jnp.concatenate([cache_k[0], dk_s, kpad], axis=1)
    v_all = jnp.concatenate([cache_v[0], dv_s, kpad], axis=1)
    ki_all = jnp.concatenate([cache_idx_k[0], ki_s, jnp.zeros((dbsz, nk_s - n_keys_s, IDX_DIM), F32)], axis=1)
    kt_s = jnp.transpose(k_all, (0, 2, 3, 1)).astype(BF16)
    v_s = jnp.transpose(v_all, (0, 2, 1, 3)).astype(BF16)
    kit_s = jnp.transpose(ki_all, (0, 2, 1)).astype(BF16)
    topk_s = min(TOPK_MAX, n_keys_s // 4)
    ob_s = _dsa(pf, pb, kit_s, kt_s, v_s, n_seq=dbsz, row0=row_s, rows_per_seq=dseq, tq=dseq, tk=KEY_TILE,
                n_keys=n_keys_s, topk=topk_s, causal=False)

    tail = jnp.zeros((rows - row_s - dbsz * dseq, D_MODEL), BF16)
    ya = jnp.concatenate([ya_p, ya_s, tail], axis=0)
    ob = jnp.concatenate([ob_p, ob_s, tail], axis=0)
    y = _out(x_all, ya, ob, pb, w_gla[0].astype(BF16), w_dsa[0].astype(BF16), w_out[0].astype(BF16),
             gate_b[0], lig, lib, ln_g[0].reshape(1, D_MODEL), ln_b[0].reshape(1, D_MODEL),
             ROW_TILE // 4, alpha)

    y_prompt = y[:row_s].reshape(bsz, tp, D_MODEL)[:, N_META:t_p]
    y_sample = y[row_s:row_s + dbsz * dseq].reshape(dbsz, dseq, D_MODEL)
    k_prompt = dk_p[:, :t_p][None]
    v_prompt = dv_p[:, :t_p][None]
    idx_k_prompt = ki_p[:, :t_p][None]
    gla_prompt = jnp.swapaxes(st_p, -1, -2)[None]
    k_sample = dk_s[None]
    v_sample = dv_s[None]
    idx_k_sample = ki_s[None]
    gla_sample = jnp.swapaxes(st_s, -1, -2)[None]
    return (y_prompt, y_sample, k_prompt, v_prompt, idx_k_prompt, gla_prompt,
            k_sample, v_sample, idx_k_sample, gla_sample)


def kernel(x_prompt, x_sample, cache_k, cache_v, cache_idx_k, state_gla, meta, ln_in_g, ln_in_b,
           w_in, gla_w2, gla_gate_b, gla_norm_g, idx_kn_g, idx_kn_b, w_gla, w_dsa, gate_b,
           w_out, ln_g, ln_b):
    return _forward(x_prompt, x_sample, cache_k, cache_v, cache_idx_k, state_gla, meta, ln_in_g, ln_in_b,
                    w_in, gla_w2, gla_gate_b, gla_norm_g, idx_kn_g, idx_kn_b, w_gla, w_dsa, gate_b,
                    w_out, ln_g, ln_b)
```

```python
import functools
import math

import numpy as np
import jax
import jax.numpy as jnp
from jax import lax
from jax.experimental import pallas as pl
from jax.experimental.pallas import tpu as pltpu

F32 = jnp.float32
BF16 = jnp.bfloat16
I32 = jnp.int32

D_MODEL = 1024
CHUNK = 64
N_META = 16
GLA_HEADS = 4
GLA_DK = 128
GLA_DV = 256
GLA_RANK = 16
GLA_TAU = 16.0
GLA_BLOCK = 64
DSA_HEADS = 16
DSA_KV_HEADS = 4
DSA_HEAD_DIM = 64
DSA_GROUP = DSA_HEADS // DSA_KV_HEADS
DSA_SCALE = DSA_HEAD_DIM ** -0.5
IDX_HEADS = 8
IDX_DIM = 64
IDX_W_SCALE = (IDX_HEADS ** -0.5) * (IDX_DIM ** -0.5)
TOPK_MAX = 256
NORM_EPS = 1e-5
GLA_QK = GLA_HEADS * GLA_DK
GLA_V = GLA_HEADS * GLA_DV
DSA_Q = DSA_HEADS * DSA_HEAD_DIM
DSA_KV = DSA_KV_HEADS * DSA_HEAD_DIM
IDX_Q = IDX_HEADS * IDX_DIM
SPLITS = (GLA_QK, GLA_QK, GLA_V, GLA_RANK, GLA_V,
          DSA_Q, DSA_KV, DSA_KV, IDX_Q, IDX_DIM, IDX_HEADS, DSA_Q,
          D_MODEL, D_MODEL)
_OFF = tuple(int(o) for o in np.cumsum((0,) + SPLITS))
(_GQ, _GK, _GV, _GLOW, _GR, _DQ, _DK, _DV, _IQ, _IK, _IW, _DZ, _MA, _MB) = range(14)

LANES = 128
VMEM_LIMIT_BYTES = 56 * 1024 * 1024

F_GQ, F_GK, F_DK, F_DV, F_SMALL = 0, 512, 1024, 1280, 1536
F_COLS = F_SMALL + LANES
SM_IK, SM_GLOW, SM_IW = 0, IDX_DIM, IDX_DIM + GLA_RANK
B_GV, B_GR, B_DQ, B_DZ, B_MA, B_MB, B_IQ, B_PAD = 0, 1024, 2048, 3072, 4096, 5120, 6144, 6656
B_COLS = 7168

ROW_TILE = 2048
KEY_TILE = 512
NEG_BIG = -1e30
INT_MIN = -(2 ** 31)
NEG_INF_KEY = int(np.array(-np.inf, np.float32).view(np.int32)) ^ 0x7FFFFFFF


def _col(w, idx):
    return w[:, _OFF[idx]:_OFF[idx + 1]]


def _layer_norm_rows(x, g, b):
    mu = jnp.mean(x, axis=-1, keepdims=True)
    xc = x - mu
    var = jnp.mean(xc * xc, axis=-1, keepdims=True)
    return xc * lax.rsqrt(var + NORM_EPS) * g + b


def _dot(a, b):
    return jnp.dot(a, b, preferred_element_type=F32)


def _dot_nt(a, b):
    return lax.dot_general(a, b, (((1,), (1,)), ((), ())), preferred_element_type=F32)


def _dot_tn(a, b):
    return lax.dot_general(a, b, (((0,), (0,)), ((), ())), preferred_element_type=F32)


def _proj_f32_kernel(x_ref, g_ref, b_ref, w_ref, kg_ref, kb_ref, o_ref):
    hn = _layer_norm_rows(x_ref[...], g_ref[...], b_ref[...])
    y = _dot(hn.astype(BF16), w_ref[...])
    o_ref[...] = y
    ik = y[:, F_SMALL:F_SMALL + IDX_DIM]
    o_ref[:, F_SMALL:F_SMALL + IDX_DIM] = _layer_norm_rows(ik, kg_ref[...], kb_ref[...])


def _proj_f32(x, g, b, w, kg, kb, tm):
    rows = x.shape[0]
    return pl.pallas_call(
        _proj_f32_kernel,
        grid=(rows // tm,),
        in_specs=[
            pl.BlockSpec((tm, D_MODEL), lambda i: (i, 0)),
            pl.BlockSpec((1, D_MODEL), lambda i: (0, 0)),
            pl.BlockSpec((1, D_MODEL), lambda i: (0, 0)),
            pl.BlockSpec((D_MODEL, F_COLS), lambda i: (0, 0)),
            pl.BlockSpec((1, IDX_DIM), lambda i: (0, 0)),
            pl.BlockSpec((1, IDX_DIM), lambda i: (0, 0)),
        ],
        out_specs=pl.BlockSpec((tm, F_COLS), lambda i: (i, 0)),
        out_shape=jax.ShapeDtypeStruct((rows, F_COLS), F32),
        compiler_params=pltpu.CompilerParams(
            dimension_semantics=("parallel",), vmem_limit_bytes=VMEM_LIMIT_BYTES),
        name="proj_f32",
    )(x, g, b, w, kg, kb)


def _proj_bf16_kernel(x_ref, g_ref, b_ref, w_ref, o_ref, hn_ref):
    @pl.when(pl.program_id(1) == 0)
    def _():
        hn_ref[...] = _layer_norm_rows(x_ref[...], g_ref[...], b_ref[...]).astype(BF16)

    o_ref[...] = _dot(hn_ref[...], w_ref[...]).astype(BF16)


def _proj_bf16(x, g, b, w, tm, tn):
    rows = x.shape[0]
    return pl.pallas_call(
        _proj_bf16_kernel,
        grid=(rows // tm, B_COLS // tn),
        in_specs=[
            pl.BlockSpec((tm, D_MODEL), lambda i, j: (i, 0)),
            pl.BlockSpec((1, D_MODEL), lambda i, j: (0, 0)),
            pl.BlockSpec((1, D_MODEL), lambda i, j: (0, 0)),
            pl.BlockSpec((D_MODEL, tn), lambda i, j: (0, j)),
        ],
        out_specs=pl.BlockSpec((tm, tn), lambda i, j: (i, j)),
        out_shape=jax.ShapeDtypeStruct((rows, B_COLS), BF16),
        scratch_shapes=[pltpu.VMEM((tm, D_MODEL), BF16)],
        compiler_params=pltpu.CompilerParams(
            dimension_semantics=("parallel", "arbitrary"), vmem_limit_bytes=VMEM_LIMIT_BYTES),
        name="proj_bf16",
    )(x, g, b, w)


def _gla_levels(blk):
    levels = []
    s = blk // 2
    while s >= 1:
        levels.append(s)
        s //= 2
    return levels


def _gla_decay_matrix(blk):
    i = np.arange(blk)[:, None]
    t = np.arange(blk)[None, :]
    mats = [(t <= i).astype(np.float32), (t > i).astype(np.float32)]
    for s in _gla_levels(blk):
        mid = (i // (2 * s)) * (2 * s) + s - 1
        lower = (i // s) % 2 == 1
        m = np.where(lower, ((t > mid) & (t <= i)), False).astype(np.float32)
        n = np.where(~lower, ((t > i) & (t <= mid)), False).astype(np.float32)
        mats.append(m + n)
    return np.concatenate(mats, axis=0)


def _gla_kernel(q_ref, k_ref, sm_ref, v_ref, r_ref, c_ref, w2_ref, gb_ref, ng_ref, s0_ref,
                o_ref, st_ref, s_scr, *, blk, n_blk, t_valid):
    ti = pl.program_id(1)
    levels = _gla_levels(blk)

    @pl.when(ti == 0)
    def _():
        s_scr[...] = s0_ref[0]

    ri = lax.broadcasted_iota(I32, (blk, blk), 0)
    ci = lax.broadcasted_iota(I32, (blk, blk), 1)
    eye = ri == ci
    lvl_masks = []
    for s in levels:
        sh = s.bit_length() - 1
        same = (ri >> (sh + 1)) == (ci >> (sh + 1))
        lvl_masks.append(same & (((ri >> sh) & 1) == 1) & (((ci >> sh) & 1) == 0))
    row_iota = lax.broadcasted_iota(I32, (blk, 1), 0)
    cmat = c_ref[...]
    w2 = w2_ref[...]
    gbias = gb_ref[...]
    ng = ng_ref[...]

    def block(j, carry):
        r0 = pl.multiple_of(j * blk, blk)
        rows = pl.ds(r0, blk)
        valid = (ti * (n_blk * blk) + r0 + row_iota) < t_valid
        gq = q_ref[rows, :] * (GLA_DK ** -0.5)
        gk = jnp.where(valid, k_ref[rows, :], 0.0)
        x = _dot(sm_ref[rows, :].astype(BF16), w2) + gbias
        logf = (jnp.minimum(x, 0.0) - jnp.log1p(jnp.exp(-jnp.abs(x)))) * (1.0 / GLA_TAU)
        logf = jnp.where(valid, logf, 0.0)
        hi = logf.astype(BF16)
        r1 = logf - hi.astype(F32)
        mid = r1.astype(BF16)
        lo = (r1 - mid.astype(F32)).astype(BF16)
        e_all = _dot(cmat, hi) + _dot(cmat, mid) + _dot(cmat, lo)
        for h in range(GLA_HEADS):
            ksl = slice(h * GLA_DK, (h + 1) * GLA_DK)
            vsl = slice(h * GLA_DV, (h + 1) * GLA_DV)
            qh = gq[:, ksl]
            kh = gk[:, ksl]
            vh = v_ref[rows, vsl]
            b_h = e_all[0:blk, ksl]
            rev_h = e_all[blk:2 * blk, ksl]
            st = s_scr[h]
            o = _dot_nt((qh * jnp.exp(b_h)).astype(BF16), st.astype(BF16))
            a = jnp.where(eye, _dot_nt(qh.astype(BF16), kh.astype(BF16)), 0.0)
            for li in range(len(levels)):
                e = jnp.exp(e_all[(2 + li) * blk:(3 + li) * blk, ksl])
                p = _dot_nt((qh * e).astype(BF16), (kh * e).astype(BF16))
                a = a + jnp.where(lvl_masks[li], p, 0.0)
            o = o + _dot(a.astype(BF16), vh)
            kd = (kh * jnp.exp(rev_h)).astype(BF16)
            dec = jnp.exp(b_h[blk - 1:blk, :])
            s_scr[h] = st * dec + _dot_tn(vh, kd)
            on = o * lax.rsqrt(jnp.mean(o * o, axis=-1, keepdims=True) + NORM_EPS) * ng
            gr = r_ref[rows, vsl].astype(F32)
            o_ref[rows, vsl] = (on * (gr * jax.nn.sigmoid(gr))).astype(BF16)
        return carry

    lax.fori_loop(0, n_blk, block, 0)

    @pl.when(ti == pl.num_programs(1) - 1)
    def _():
        st_ref[0] = s_scr[...]


def _gla(pf, pb, cmat, w2pad, gbias, ng, s0t, *, n_seq, row0, rows_per_seq, t_valid, blk, n_blk):
    rb = blk * n_blk
    steps = rows_per_seq // rb
    base = row0 // rb

    def rmap(c):
        return lambda s, t: (base + s * steps + t, c)

    kern = functools.partial(_gla_kernel, blk=blk, n_blk=n_blk, t_valid=t_valid)
    return pl.pallas_call(
        kern,
        grid=(n_seq, steps),
        in_specs=[
            pl.BlockSpec((rb, GLA_QK), rmap(F_GQ // GLA_QK)),
            pl.BlockSpec((rb, GLA_QK), rmap(F_GK // GLA_QK)),
            pl.BlockSpec((rb, LANES), rmap(F_SMALL // LANES)),
            pl.BlockSpec((rb, GLA_V), rmap(B_GV // GLA_V)),
            pl.BlockSpec((rb, GLA_V), rmap(B_GR // GLA_V)),
            pl.BlockSpec(cmat.shape, lambda s, t: (0, 0)),
            pl.BlockSpec(w2pad.shape, lambda s, t: (0, 0)),
            pl.BlockSpec((1, GLA_QK), lambda s, t: (0, 0)),
            pl.BlockSpec((1, GLA_DV), lambda s, t: (0, 0)),
            pl.BlockSpec((1, GLA_HEADS, GLA_DV, GLA_DK), lambda s, t: (s, 0, 0, 0)),
        ],
        out_specs=[
            pl.BlockSpec((rb, GLA_V), lambda s, t: (s * steps + t, 0)),
            pl.BlockSpec((1, GLA_HEADS, GLA_DV, GLA_DK), lambda s, t: (s, 0, 0, 0)),
        ],
        out_shape=[
            jax.ShapeDtypeStruct((n_seq * rows_per_seq, GLA_V), BF16),
            jax.ShapeDtypeStruct((n_seq, GLA_HEADS, GLA_DV, GLA_DK), F32),
        ],
        scratch_shapes=[pltpu.VMEM((GLA_HEADS, GLA_DV, GLA_DK), F32)],
        compiler_params=pltpu.CompilerParams(
            dimension_semantics=("parallel", "arbitrary"), vmem_limit_bytes=VMEM_LIMIT_BYTES),
        name="gla",
    )(pf, pf, pf, pb, pb, cmat, w2pad, gbias, ng, s0t)


def _dsa_kernel(iq_ref, sm_ref, qd_ref, kit_ref, kt_ref, v_ref, o_ref,
                key_scr, m_scr, l_scr, acc_scr, *, tq, tk, n_keys, topk, causal, n_kb_total):
    qb = pl.program_id(1)
    n_rows_g = DSA_GROUP * tq

    if causal:
        q_chunk_max = (qb * tq + tq - 1 - N_META) >> 6
        limit = jnp.minimum(N_META + CHUNK * (q_chunk_max + 1), n_keys)
        n_kb = (limit + tk - 1) // tk
    else:
        n_kb = n_kb_total

    q_pos = qb * tq + lax.broadcasted_iota(I32, (tq, 1), 0)
    q_chunk = (q_pos - N_META) >> 6
    lane_iota = lax.broadcasted_iota(I32, (1, tk), 1)

    iq = iq_ref[...]
    iq_hm = jnp.concatenate([iq[:, h * IDX_DIM:(h + 1) * IDX_DIM] for h in range(IDX_HEADS)], axis=0)
    w_i = sm_ref[:, SM_IW:SM_IW + IDX_HEADS] * IDX_W_SCALE
    w_cols = [jnp.broadcast_to(w_i[:, h:h + 1], (tq, LANES)) for h in range(IDX_HEADS)]

    def score_block(kb, carry):
        k0 = pl.multiple_of(kb * tk, tk)
        r = _dot(iq_hm, kit_ref[0, :, pl.ds(k0, tk)])
        sc = None
        for h in range(IDX_HEADS):
            rh = jnp.maximum(r[h * tq:(h + 1) * tq, :], 0.0)
            wh = jnp.concatenate([w_cols[h]] * (tk // LANES), axis=1)
            sc = rh * wh if sc is None else sc + rh * wh
        key_pos = k0 + lane_iota
        adm = key_pos < n_keys
        if causal:
            adm = adm & (((key_pos - N_META) >> 6) <= q_chunk)
        sc = jnp.where(adm, sc, -jnp.inf)
        bits = pltpu.bitcast(sc, I32)
        key_scr[:, pl.ds(k0, tk)] = bits ^ ((bits >> 31) & 0x7FFFFFFF)
        return carry

    lax.fori_loop(0, n_kb, score_block, 0)

    n_cols = n_kb * (tk // LANES)

    def count_ge(cand):
        cb = jnp.broadcast_to(cand, (tq, LANES))

        def body(c, acc):
            c0 = pl.multiple_of(c * LANES, LANES)
            return acc + jnp.where(key_scr[:, pl.ds(c0, LANES)] >= cb, 1, 0)

        acc = lax.fori_loop(0, n_cols, body, jnp.zeros((tq, LANES), I32))
        return jnp.sum(acc, axis=1, keepdims=True)

    def bit_step(i, carry):
        lo, cge = carry
        cand = lo + jnp.left_shift(jnp.int32(1), 31 - i)
        cnt = count_ge(cand)
        take = cnt >= topk
        return jnp.where(take, cand, lo), jnp.where(take, cnt, cge)

    lo0 = jnp.full((tq, 1), INT_MIN, I32)
    cge0 = jnp.zeros((tq, 1), I32) + n_cols * LANES
    thr, cge = lax.fori_loop(0, 32, bit_step, (lo0, cge0))
    need_tie = (cge > topk) & (thr > NEG_INF_KEY)
    thr = jnp.maximum(thr, NEG_INF_KEY + 1)

    @pl.when(jnp.max(need_tie.astype(I32)) > 0)
    def _():
        want = topk - count_ge(thr + 1)
        thr_b = jnp.broadcast_to(thr, (tq, LANES))
        col_iota = lax.broadcasted_iota(I32, (tq, LANES), 1)

        def count_tied_below(m):
            mb = jnp.broadcast_to(m, (tq, LANES))

            def body(c, acc):
                c0 = pl.multiple_of(c * LANES, LANES)
                hit = (key_scr[:, pl.ds(c0, LANES)] == thr_b) & ((c0 + col_iota) < mb)
                return acc + jnp.where(hit, 1, 0)

            acc = lax.fori_loop(0, n_cols, body, jnp.zeros((tq, LANES), I32))
            return jnp.sum(acc, axis=1, keepdims=True)

        n_bits = int(math.ceil(math.log2(n_kb_total * tk + 1)))

        def idx_step(i, m):
            cand = m + jnp.left_shift(jnp.int32(1), n_bits - 1 - i)
            return jnp.where(count_tied_below(cand) <= want, cand, m)

        m_keep = lax.fori_loop(0, n_bits, idx_step, jnp.zeros((tq, 1), I32))
        mk_b = jnp.broadcast_to(jnp.where(need_tie, m_keep, jnp.int32(2 ** 30)), (tq, LANES))

        def demote(c, carry):
            c0 = pl.multiple_of(c * LANES, LANES)
            kv = key_scr[:, pl.ds(c0, LANES)]
            drop = (kv == thr_b) & ((c0 + col_iota) >= mk_b)
            key_scr[:, pl.ds(c0, LANES)] = jnp.where(drop, kv - 1, kv)
            return carry

        lax.fori_loop(0, n_cols, demote, 0)

    qd = qd_ref[...]
    q_groups = []
    for g in range(DSA_KV_HEADS):
        q_groups.append(jnp.concatenate(
            [qd[:, (r * DSA_KV_HEADS + g) * DSA_HEAD_DIM:(r * DSA_KV_HEADS + g + 1) * DSA_HEAD_DIM]
             for r in range(DSA_GROUP)], axis=0))
    m_scr[...] = jnp.full(m_scr.shape, NEG_BIG, F32)
    l_scr[...] = jnp.zeros(l_scr.shape, F32)
    acc_scr[...] = jnp.zeros(acc_scr.shape, F32)
    thr_t = jnp.broadcast_to(thr, (tq, LANES))

    def attend_block(kb, carry):
        k0 = pl.multiple_of(kb * tk, tk)
        sel = key_scr[:, pl.ds(k0, tk)] >= jnp.concatenate([thr_t] * (tk // LANES), axis=1)
        bias = jnp.where(sel, 0.0, NEG_BIG)
        bias_g = jnp.concatenate([bias] * DSA_GROUP, axis=0)
        for g in range(DSA_KV_HEADS):
            s = _dot(q_groups[g], kt_ref[0, g, :, pl.ds(k0, tk)]) + bias_g
            m_old = m_scr[g]
            m_new = jnp.maximum(m_old, jnp.max(s, axis=1, keepdims=True))
            alpha = jnp.exp(m_old - m_new)
            p = jnp.exp(s - m_new)
            p = jnp.where(bias_g < 0.0, 0.0, p)
            l_scr[g] = alpha * l_scr[g] + jnp.sum(p, axis=1, keepdims=True)
            acc_scr[g] = alpha * acc_scr[g] + _dot(p.astype(BF16), v_ref[0, g, pl.ds(k0, tk), :])
            m_scr[g] = m_new
        return carry

    lax.fori_loop(0, n_kb, attend_block, 0)

    for g in range(DSA_KV_HEADS):
        og = acc_scr[g] / l_scr[g]
        for r in range(DSA_GROUP):
            c0 = (g * DSA_GROUP + r) * DSA_HEAD_DIM
            o_ref[:, c0:c0 + DSA_HEAD_DIM] = og[r * tq:(r + 1) * tq, :].astype(BF16)


def _dsa(pf, pb, kit, kt, v, *, n_seq, row0, rows_per_seq, tq, tk, n_keys, topk, causal):
    nk_pad = kit.shape[-1]
    n_kb_total = nk_pad // tk
    steps = rows_per_seq // tq
    base = row0 // tq

    def rmap(c):
        return lambda s, t: (base + s * steps + t, c)

    kern = functools.partial(_dsa_kernel, tq=tq, tk=tk, n_keys=n_keys, topk=topk, causal=causal,
                             n_kb_total=n_kb_total)
    single = pl.Buffered(1)
    return pl.pallas_call(
        kern,
        grid=(n_seq, steps),
        in_specs=[
            pl.BlockSpec((tq, IDX_Q), rmap(B_IQ // IDX_Q)),
            pl.BlockSpec((tq, LANES), rmap(F_SMALL // LANES)),
            pl.BlockSpec((tq, DSA_Q), rmap(B_DQ // DSA_Q)),
            pl.BlockSpec((1, IDX_DIM, nk_pad), lambda s, t: (s, 0, 0), pipeline_mode=single),
            pl.BlockSpec((1, DSA_KV_HEADS, DSA_HEAD_DIM, nk_pad), lambda s, t: (s, 0, 0, 0),
                         pipeline_mode=single),
            pl.BlockSpec((1, DSA_KV_HEADS, nk_pad, DSA_HEAD_DIM), lambda s, t: (s, 0, 0, 0),
                         pipeline_mode=single),
        ],
        out_specs=pl.BlockSpec((tq, DSA_Q), lambda s, t: (s * steps + t, 0)),
        out_shape=jax.ShapeDtypeStruct((n_seq * rows_per_seq, DSA_Q), BF16),
        scratch_shapes=[
            pltpu.VMEM((tq, nk_pad), I32),
            pltpu.VMEM((DSA_KV_HEADS, DSA_GROUP * tq, 1), F32),
            pltpu.VMEM((DSA_KV_HEADS, DSA_GROUP * tq, 1), F32),
            pltpu.VMEM((DSA_KV_HEADS, DSA_GROUP * tq, DSA_HEAD_DIM), F32),
        ],
        compiler_params=pltpu.CompilerParams(
            dimension_semantics=("parallel", "arbitrary"), vmem_limit_bytes=VMEM_LIMIT_BYTES),
        name="dsa",
    )(pb, pf, pb, kit, kt, v)


V_ROWS = 80
V_ONES_ROW = DSA_HEAD_DIM


def _dsa_t_kernel(iq_ref, sm_ref, qd_ref, ki_ref, k_ref, vt_ref, o_ref,
                  key_scr, m_scr, acc_scr, *, tq, tk, n_keys, topk):
    qb = pl.program_id(1)
    q0 = qb * tq
    q_chunk_max = (q0 + tq - 1 - N_META) >> 6
    limit = jnp.minimum(N_META + CHUNK * (q_chunk_max + 1), n_keys)
    n_kb = (limit + tk - 1) // tk
    n_full = jnp.minimum(jnp.minimum((q0 + N_META) // tk, n_keys // tk), n_kb)

    q_chunk = (q0 + lax.broadcasted_iota(I32, (1, tq), 1) - N_META) >> 6
    row_iota = lax.broadcasted_iota(I32, (tk, tq), 0)

    iq = iq_ref[...]
    iq_hm = jnp.concatenate([iq[:, h * IDX_DIM:(h + 1) * IDX_DIM] for h in range(IDX_HEADS)], axis=0)
    w_t = sm_ref[...].T[SM_IW:SM_IW + IDX_HEADS, :] * IDX_W_SCALE

    def score_block(kb, masked):
        k0 = pl.multiple_of(kb * tk, tk)
        r = _dot_nt(ki_ref[0, pl.ds(k0, tk), :], iq_hm)
        sc = None
        for h in range(IDX_HEADS):
            t = jnp.maximum(r[:, h * tq:(h + 1) * tq], 0.0) * w_t[h:h + 1, :]
            sc = t if sc is None else sc + t
        if masked:
            key_pos = k0 + row_iota
            adm = (key_pos < n_keys) & (((key_pos - N_META) >> 6) <= q_chunk)
            sc = jnp.where(adm, sc, -jnp.inf)
        bits = pltpu.bitcast(sc, I32)
        key_scr[pl.ds(k0, tk), :] = bits ^ ((bits >> 31) & 0x7FFFFFFF)

    def score_full(kb, carry):
        score_block(kb, False)
        return carry

    def score_edge(kb, carry):
        score_block(kb, True)
        return carry

    lax.fori_loop(0, n_full, score_full, 0)
    lax.fori_loop(n_full, n_kb, score_edge, 0)

    n_chunks = n_kb * (tk // LANES)

    def count_rows(pred):
        def body(c, acc):
            c0 = pl.multiple_of(c * LANES, LANES)
            return acc + jnp.where(pred(key_scr[pl.ds(c0, LANES), :], c0), 1, 0)

        acc = lax.fori_loop(0, n_chunks, body, jnp.zeros((LANES, tq), I32))
        return jnp.sum(acc, axis=0, keepdims=True)

    def count_ge(cand):
        return count_rows(lambda kv, c0: kv >= cand)

    def search_cond(carry):
        i, lo, cge = carry
        return (i < 32) & (jnp.max(jnp.abs(cge - topk)) > 0)

    def search_step(carry):
        i, lo, cge = carry
        cand = lo + jnp.left_shift(jnp.int32(1), 31 - i)
        cnt = count_ge(cand)
        take = cnt >= topk
        return i + 1, jnp.where(take, cand, lo), jnp.where(take, cnt, cge)

    lo0 = jnp.full((1, tq), INT_MIN, I32)
    cge0 = jnp.zeros((1, tq), I32) + n_chunks * LANES
    _, thr, cge = lax.while_loop(search_cond, search_step, (jnp.int32(0), lo0, cge0))
    need_tie = (cge > topk) & (thr > NEG_INF_KEY)
    thr = jnp.maximum(thr, NEG_INF_KEY + 1)

    @pl.when(jnp.max(need_tie.astype(I32)) > 0)
    def _():
        want = topk - count_ge(thr + 1)
        sub_iota = lax.broadcasted_iota(I32, (LANES, tq), 0)
        n_bits = int(math.ceil(math.log2(key_scr.shape[0] + 1)))

        def idx_step(i, m):
            cand = m + jnp.left_shift(jnp.int32(1), n_bits - 1 - i)
            tied_below = count_rows(lambda kv, c0: (kv == thr) & ((c0 + sub_iota) < cand))
            return jnp.where(tied_below <= want, cand, m)

        m_keep = lax.fori_loop(0, n_bits, idx_step, jnp.zeros((1, tq), I32))
        m_keep = jnp.where(need_tie, m_keep, jnp.int32(2 ** 30))

        def demote(c, carry):
            c0 = pl.multiple_of(c * LANES, LANES)
            kv = key_scr[pl.ds(c0, LANES), :]
            drop = (kv == thr) & ((c0 + sub_iota) >= m_keep)
            key_scr[pl.ds(c0, LANES), :] = jnp.where(drop, kv - 1, kv)
            return carry

        lax.fori_loop(0, n_chunks, demote, 0)

    qd = qd_ref[...]
    win = lax.broadcasted_iota(I32, (tq, DSA_KV), 1) >> 6
    q_pad = []
    for g in range(DSA_KV_HEADS):
        q_pad.append(jnp.concatenate(
            [jnp.where(win == g, qd[:, r * DSA_KV:(r + 1) * DSA_KV].astype(F32), 0.0).astype(BF16)
             for r in range(DSA_GROUP)], axis=0))
    m_scr[...] = jnp.full(m_scr.shape, NEG_BIG, F32)
    acc_scr[...] = jnp.zeros(acc_scr.shape, F32)

    def attend_block(kb, carry):
        k0 = pl.multiple_of(kb * tk, tk)
        bias = jnp.where(key_scr[pl.ds(k0, tk), :] >= thr, 0.0, NEG_BIG)
        bias_g = jnp.concatenate([bias] * DSA_GROUP, axis=1)
        kblk = k_ref[0, pl.ds(k0, tk), :]
        for g in range(DSA_KV_HEADS):
            s = _dot_nt(kblk, q_pad[g]) + bias_g
            m_old = m_scr[g]
            m_new = jnp.maximum(m_old, jnp.max(s, axis=0, keepdims=True))
            alpha = jnp.exp(m_old[0:1] - m_new[0:1])
            p = jnp.exp(s - m_new[0:1]).astype(BF16)
            acc_scr[g] = acc_scr[g] * alpha + _dot(vt_ref[0, g, :, pl.ds(k0, tk)], p)
            m_scr[g] = m_new
        return carry

    lax.fori_loop(0, n_kb, attend_block, 0)

    for g in range(DSA_KV_HEADS):
        acc = acc_scr[g]
        og = acc[0:DSA_HEAD_DIM] / acc[V_ONES_ROW:V_ONES_ROW + 1]
        for r in range(DSA_GROUP):
            c0 = (g * DSA_GROUP + r) * DSA_HEAD_DIM
            o_ref[:, c0:c0 + DSA_HEAD_DIM] = og[:, r * tq:(r + 1) * tq].T.astype(BF16)


def _dsa_t(pf, pb, ki, k, vt, *, n_seq, rows_per_seq, tk, n_keys, topk):
    tq = LANES
    nk_pad = k.shape[1]
    steps = rows_per_seq // tq

    def rmap(c):
        return lambda s, t: (s * steps + t, c)

    kern = functools.partial(_dsa_t_kernel, tq=tq, tk=tk, n_keys=n_keys, topk=topk)
    single = pl.Buffered(1)
    return pl.pallas_call(
        kern,
        grid=(n_seq, steps),
        in_specs=[
            pl.BlockSpec((tq, IDX_Q), rmap(B_IQ // IDX_Q)),
            pl.BlockSpec((tq, LANES), rmap(F_SMALL // LANES)),
            pl.BlockSpec((tq, DSA_Q), rmap(B_DQ // DSA_Q)),
            pl.BlockSpec((1, nk_pad, IDX_DIM), lambda s, t: (s, 0, 0), pipeline_mode=single),
            pl.BlockSpec((1, nk_pad, DSA_KV), lambda s, t: (s, 0, 0), pipeline_mode=single),
            pl.BlockSpec((1, DSA_KV_HEADS, V_ROWS, nk_pad), lambda s, t: (s, 0, 0, 0), pipeline_mode=single),
        ],
        out_specs=pl.BlockSpec((tq, DSA_Q), lambda s, t: (s * steps + t, 0)),
        out_shape=jax.ShapeDtypeStruct((n_seq * rows_per_seq, DSA_Q), BF16),
        scratch_shapes=[
            pltpu.VMEM((nk_pad, tq), I32),
            pltpu.VMEM((DSA_KV_HEADS, 8, DSA_GROUP * tq), F32),
            pltpu.VMEM((DSA_KV_HEADS, V_ROWS, DSA_GROUP * tq), F32),
        ],
        compiler_params=pltpu.CompilerParams(
            dimension_semantics=("parallel", "arbitrary"), vmem_limit_bytes=VMEM_LIMIT_BYTES),
        name="dsa_t",
    )(pb, pf, pb, ki, k, vt)


def _out_kernel(x_ref, ya_ref, ob_ref, z_ref, ma_ref, mb_ref, wg_ref, wd_ref, wo_ref,
                gate_ref, lig_ref, lib_ref, lg_ref, lb_ref, o_ref, *, alpha):
    hn = _layer_norm_rows(x_ref[...], lig_ref[...], lib_ref[...])
    y_a = _dot(ya_ref[...], wg_ref[...])
    z = z_ref[...].astype(F32)
    yb_in = ob_ref[...].astype(F32) * (z * jax.nn.sigmoid(z))
    y_b = _dot(yb_in.astype(BF16), wd_ref[...])
    ga = jax.nn.sigmoid(ma_ref[...].astype(F32) + gate_ref[0:1, :])
    gb = jax.nn.sigmoid(mb_ref[...].astype(F32) + gate_ref[1:2, :])
    merged = ga * y_a + gb * y_b
    y = alpha * hn + _dot(merged.astype(BF16), wo_ref[...])
    o_ref[...] = _layer_norm_rows(y, lg_ref[...], lb_ref[...])


def _out(x, ya, ob, pb, wg, wd, wo, gate, lig, lib, lg, lb, tm, alpha):
    rows = x.shape[0]
    row = lambda c: pl.BlockSpec((tm, D_MODEL), lambda i, c=c: (i, c))
    full = lambda a: pl.BlockSpec(a.shape, lambda i: (0, 0))
    return pl.pallas_call(
        functools.partial(_out_kernel, alpha=alpha),
        grid=(rows // tm,),
        in_specs=[row(0), row(0), row(0), row(B_DZ // D_MODEL), row(B_MA // D_MODEL), row(B_MB // D_MODEL),
                  full(wg), full(wd), full(wo), full(gate), full(lig), full(lib), full(lg), full(lb)],
        out_specs=pl.BlockSpec((tm, D_MODEL), lambda i: (i, 0)),
        out_shape=jax.ShapeDtypeStruct((rows, D_MODEL), F32),
        compiler_params=pltpu.CompilerParams(
            dimension_semantics=("parallel",), vmem_limit_bytes=VMEM_LIMIT_BYTES),
        name="out_proj",
    )(x, ya, ob, pb, pb, pb, wg, wd, wo, gate, lig, lib, lg, lb)


def _round_up(a, b):
    return -(-a // b) * b


def _forward(x_prompt, x_sample, cache_k, cache_v, cache_idx_k, state_gla, meta, ln_in_g, ln_in_b,
             w_in, gla_w2, gla_gate_b, gla_norm_g, idx_kn_g, idx_kn_b, w_gla, w_dsa, gate_b,
             w_out, ln_g, ln_b):
    depth = w_in.shape[0]
    assert depth == 1, "single-layer trunk"
    bsz, seq, _ = x_prompt.shape
    dbsz, dseq, _ = x_sample.shape
    past = cache_k.shape[2]
    t_p = N_META + seq
    tq_p = 128
    tp = _round_up(t_p, KEY_TILE)
    assert tp % tq_p == 0 and tp % GLA_BLOCK == 0
    row_s = bsz * tp
    rows = _round_up(row_s + dbsz * dseq, ROW_TILE)
    alpha = (2.0 * depth) ** 0.25

    meta_rows = jnp.broadcast_to(meta.astype(F32)[None], (bsz, N_META, D_MODEL))
    xp = jnp.concatenate([meta_rows, x_prompt, jnp.zeros((bsz, tp - t_p, D_MODEL), F32)], axis=1)
    x_all = jnp.concatenate([xp.reshape(bsz * tp, D_MODEL), x_sample.reshape(dbsz * dseq, D_MODEL),
                             jnp.zeros((rows - row_s - dbsz * dseq, D_MODEL), F32)], axis=0)

    w = w_in[0]
    w_f = jnp.concatenate([_col(w, _GQ), _col(w, _GK), _col(w, _DK), _col(w, _DV), _col(w, _IK),
                           _col(w, _GLOW), _col(w, _IW),
                           jnp.zeros((D_MODEL, LANES - IDX_DIM - GLA_RANK - IDX_HEADS), F32)], axis=1).astype(BF16)
    w_dq = _col(w, _DQ).reshape(D_MODEL, DSA_KV_HEADS, DSA_GROUP, DSA_HEAD_DIM)
    w_dq = (jnp.swapaxes(w_dq, 1, 2) * DSA_SCALE).reshape(D_MODEL, DSA_Q)
    w_b = jnp.concatenate([_col(w, _GV), _col(w, _GR), w_dq, _col(w, _DZ), _col(w, _MA), _col(w, _MB),
                           _col(w, _IQ), jnp.zeros((D_MODEL, B_COLS - B_PAD), F32)], axis=1).astype(BF16)
    lig = ln_in_g.reshape(1, D_MODEL)
    lib = ln_in_b.reshape(1, D_MODEL)

    pf = _proj_f32(x_all, lig, lib, w_f, idx_kn_g[0].reshape(1, IDX_DIM), idx_kn_b[0].reshape(1, IDX_DIM),
                   ROW_TILE // 4)
    pb = _proj_bf16(x_all, lig, lib, w_b, ROW_TILE // 2, 1024)

    w2pad = jnp.zeros((LANES, GLA_QK), F32).at[SM_GLOW:SM_GLOW + GLA_RANK].set(gla_w2[0]).astype(BF16)
    gbias = gla_gate_b[0].reshape(1, GLA_QK)
    ng = gla_norm_g[0].reshape(1, GLA_DV)
    blk_p = min(GLA_BLOCK, t_p)
    cm_p = jnp.asarray(_gla_decay_matrix(blk_p), BF16)
    s0_p = jnp.zeros((bsz, GLA_HEADS, GLA_DV, GLA_DK), F32)
    ya_p, st_p = _gla(pf, pb, cm_p, w2pad, gbias, ng, s0_p, n_seq=bsz, row0=0, rows_per_seq=tp,
                      t_valid=t_p, blk=blk_p, n_blk=KEY_TILE // blk_p)
    blk_s = min(GLA_BLOCK, dseq)
    assert dseq % blk_s == 0 and row_s % dseq == 0
    cm_s = jnp.asarray(_gla_decay_matrix(blk_s), BF16)
    s0_s = jnp.swapaxes(state_gla[0], -1, -2)
    ya_s, st_s = _gla(pf, pb, cm_s, w2pad, gbias, ng, s0_s, n_seq=dbsz, row0=row_s, rows_per_seq=dseq,
                      t_valid=dseq, blk=blk_s, n_blk=dseq // blk_s)

    dk = pf[:, F_DK:F_DK + DSA_KV]
    dv = pf[:, F_DV:F_DV + DSA_KV]
    ki = pf[:, F_SMALL:F_SMALL + IDX_DIM]
    dk_p = dk[:row_s].reshape(bsz, tp, DSA_KV_HEADS, DSA_HEAD_DIM)
    dv_p = dv[:row_s].reshape(bsz, tp, DSA_KV_HEADS, DSA_HEAD_DIM)
    ki_p = ki[:row_s].reshape(bsz, tp, IDX_DIM)
    vt_p = jnp.concatenate([jnp.transpose(dv_p, (0, 2, 3, 1)),
                            jnp.ones((bsz, DSA_KV_HEADS, 1, tp), F32),
                            jnp.zeros((bsz, DSA_KV_HEADS, V_ROWS - DSA_HEAD_DIM - 1, tp), F32)], axis=2).astype(BF16)
    topk_p = min(TOPK_MAX, (t_p - N_META) // 4)
    ob_p = _dsa_t(pf, pb, ki_p.astype(BF16), dk[:row_s].reshape(bsz, tp, DSA_KV).astype(BF16), vt_p,
                  n_seq=bsz, rows_per_seq=tp, tk=KEY_TILE, n_keys=t_p, topk=topk_p)

    n_keys_s = past + dseq
    nk_s = _round_up(n_keys_s, KEY_TILE)
    dk_s = dk[row_s:row_s + dbsz * dseq].reshape(dbsz, dseq, DSA_KV_HEADS, DSA_HEAD_DIM)
    dv_s = dv[row_s:row_s + dbsz * dseq].reshape(dbsz, dseq, DSA_KV_HEADS, DSA_HEAD_DIM)
    ki_s = ki[row_s:row_s + dbsz * dseq].reshape(dbsz, dseq, IDX_DIM)
    kpad = jnp.zeros((dbsz, nk_s - n_keys_s, DSA_KV_HEADS, DSA_HEAD_DIM), F32)
    k_all = jnp.concatenate([cache_k[0], dk_s, kpad], axis=1)
    v_all = jnp.concatenate([cache_v[0], dv_s, kpad], axis=1)
    ki_all = jnp.concatenate([cache_idx_k[0], ki_s, jnp.zeros((dbsz, nk_s - n_keys_s, IDX_DIM), F32)], axis=1)
    kt_s = jnp.transpose(k_all, (0, 2, 3, 1)).astype(BF16)
    v_s = jnp.transpose(v_all, (0, 2, 1, 3)).astype(BF16)
    kit_s = jnp.transpose(ki_all, (0, 2, 1)).astype(BF16)
    topk_s = min(TOPK_MAX, n_keys_s // 4)
    ob_s = _dsa(pf, pb, kit_s, kt_s, v_s, n_seq=dbsz, row0=row_s, rows_per_seq=dseq, tq=dseq, tk=KEY_TILE,
                n_keys=n_keys_s, topk=topk_s, causal=False)

    tail = jnp.zeros((rows - row_s - dbsz * dseq, D_MODEL), BF16)
    ya = jnp.concatenate([ya_p, ya_s, tail], axis=0)
    ob = jnp.concatenate([ob_p, ob_s, tail], axis=0)
    y = _out(x_all, ya, ob, pb, w_gla[0].astype(BF16), w_dsa[0].astype(BF16), w_out[0].astype(BF16),
             gate_b[0], lig, lib, ln_g[0].reshape(1, D_MODEL), ln_b[0].reshape(1, D_MODEL),
             ROW_TILE // 4, alpha)

    y_prompt = y[:row_s].reshape(bsz, tp, D_MODEL)[:, N_META:t_p]
    y_sample = y[row_s:row_s + dbsz * dseq].reshape(dbsz, dseq, D_MODEL)
    k_prompt = dk_p[:, :t_p][None]
    v_prompt = dv_p[:, :t_p][None]
    idx_k_prompt = ki_p[:, :t_p][None]
    gla_prompt = jnp.swapaxes(st_p, -1, -2)[None]
    k_sample = dk_s[None]
    v_sample = dv_s[None]
    idx_k_sample = ki_s[None]
    gla_sample = jnp.swapaxes(st_s, -1, -2)[None]
    return (y_prompt, y_sample, k_prompt, v_prompt, idx_k_prompt, gla_prompt,
            k_sample, v_sample, idx_k_sample, gla_sample)


def kernel(x_prompt, x_sample, cache_k, cache_v, cache_idx_k, state_gla, meta, ln_in_g, ln_in_b,
           w_in, gla_w2, gla_gate_b, gla_norm_g, idx_kn_g, idx_kn_b, w_gla, w_dsa, gate_b,
           w_out, ln_g, ln_b):
    return _forward(x_prompt, x_sample, cache_k, cache_v, cache_idx_k, state_gla, meta, ln_in_g, ln_in_b,
                    w_in, gla_w2, gla_gate_b, gla_norm_g, idx_kn_g, idx_kn_b, w_gla, w_dsa, gate_b,
                    w_out, ln_g, ln_b)
```

```python
import functools
import math

import numpy as np
import jax
import jax.numpy as jnp
from jax import lax
from jax.experimental import pallas as pl
from jax.experimental.pallas import tpu as pltpu

F32 = jnp.float32
BF16 = jnp.bfloat16
I32 = jnp.int32

D_MODEL = 1024
CHUNK = 64
N_META = 16
GLA_HEADS = 4
GLA_DK = 128
GLA_DV = 256
GLA_RANK = 16
GLA_TAU = 16.0
GLA_BLOCK = 64
DSA_HEADS = 16
DSA_KV_HEADS = 4
DSA_HEAD_DIM = 64
DSA_GROUP = DSA_HEADS // DSA_KV_HEADS
DSA_SCALE = DSA_HEAD_DIM ** -0.5
IDX_HEADS = 8
IDX_DIM = 64
IDX_W_SCALE = (IDX_HEADS ** -0.5) * (IDX_DIM ** -0.5)
TOPK_MAX = 256
NORM_EPS = 1e-5
GLA_QK = GLA_HEADS * GLA_DK
GLA_V = GLA_HEADS * GLA_DV
DSA_Q = DSA_HEADS * DSA_HEAD_DIM
DSA_KV = DSA_KV_HEADS * DSA_HEAD_DIM
IDX_Q = IDX_HEADS * IDX_DIM
SPLITS = (GLA_QK, GLA_QK, GLA_V, GLA_RANK, GLA_V,
          DSA_Q, DSA_KV, DSA_KV, IDX_Q, IDX_DIM, IDX_HEADS, DSA_Q,
          D_MODEL, D_MODEL)
_OFF = tuple(int(o) for o in np.cumsum((0,) + SPLITS))
(_GQ, _GK, _GV, _GLOW, _GR, _DQ, _DK, _DV, _IQ, _IK, _IW, _DZ, _MA, _MB) = range(14)

LANES = 128
VMEM_LIMIT_BYTES = 56 * 1024 * 1024

F_GQ, F_GK, F_DK, F_DV, F_SMALL = 0, 512, 1024, 1280, 1536
F_COLS = F_SMALL + LANES
SM_IK, SM_GLOW, SM_IW = 0, IDX_DIM, IDX_DIM + GLA_RANK
B_GV, B_GR, B_DQ, B_DZ, B_MA, B_MB, B_IQ, B_PAD = 0, 1024, 2048, 3072, 4096, 5120, 6144, 6656
B_COLS = 7168

ROW_TILE = 2048
KEY_TILE = 512
NEG_BIG = -1e30
INT_MIN = -(2 ** 31)
NEG_INF_KEY = int(np.array(-np.inf, np.float32).view(np.int32)) ^ 0x7FFFFFFF


def _col(w, idx):
    return w[:, _OFF[idx]:_OFF[idx + 1]]


def _layer_norm_rows(x, g, b):
    mu = jnp.mean(x, axis=-1, keepdims=True)
    xc = x - mu
    var = jnp.mean(xc * xc, axis=-1, keepdims=True)
    return xc * lax.rsqrt(var + NORM_EPS) * g + b


def _dot(a, b):
    return jnp.dot(a, b, preferred_element_type=F32)


def _dot_nt(a, b):
    return lax.dot_general(a, b, (((1,), (1,)), ((), ())), preferred_element_type=F32)


def _dot_tn(a, b):
    return lax.dot_general(a, b, (((0,), (0,)), ((), ())), preferred_element_type=F32)


def _proj_f32_kernel(x_ref, g_ref, b_ref, w_ref, kg_ref, kb_ref, o_ref):
    hn = _layer_norm_rows(x_ref[...], g_ref[...], b_ref[...])
    y = _dot(hn.astype(BF16), w_ref[...])
    o_ref[...] = y
    ik = y[:, F_SMALL:F_SMALL + IDX_DIM]
    o_ref[:, F_SMALL:F_SMALL + IDX_DIM] = _layer_norm_rows(ik, kg_ref[...], kb_ref[...])


def _proj_f32(x, g, b, w, kg, kb, tm):
    rows = x.shape[0]
    return pl.pallas_call(
        _proj_f32_kernel,
        grid=(rows // tm,),
        in_specs=[
            pl.BlockSpec((tm, D_MODEL), lambda i: (i, 0)),
            pl.BlockSpec((1, D_MODEL), lambda i: (0, 0)),
            pl.BlockSpec((1, D_MODEL), lambda i: (0, 0)),
            pl.BlockSpec((D_MODEL, F_COLS), lambda i: (0, 0)),
            pl.BlockSpec((1, IDX_DIM), lambda i: (0, 0)),
            pl.BlockSpec((1, IDX_DIM), lambda i: (0, 0)),
        ],
        out_specs=pl.BlockSpec((tm, F_COLS), lambda i: (i, 0)),
        out_shape=jax.ShapeDtypeStruct((rows, F_COLS), F32),
        compiler_params=pltpu.CompilerParams(
            dimension_semantics=("parallel",), vmem_limit_bytes=VMEM_LIMIT_BYTES),
        name="proj_f32",
    )(x, g, b, w, kg, kb)


def _proj_bf16_kernel(x_ref, g_ref, b_ref, w_ref, o_ref, hn_ref):
    @pl.when(pl.program_id(1) == 0)
    def _():
        hn_ref[...] = _layer_norm_rows(x_ref[...], g_ref[...], b_ref[...]).astype(BF16)

    o_ref[...] = _dot(hn_ref[...], w_ref[...]).astype(BF16)


def _proj_bf16(x, g, b, w, tm, tn):
    rows = x.shape[0]
    return pl.pallas_call(
        _proj_bf16_kernel,
        grid=(rows // tm, B_COLS // tn),
        in_specs=[
            pl.BlockSpec((tm, D_MODEL), lambda i, j: (i, 0)),
            pl.BlockSpec((1, D_MODEL), lambda i, j: (0, 0)),
            pl.BlockSpec((1, D_MODEL), lambda i, j: (0, 0)),
            pl.BlockSpec((D_MODEL, tn), lambda i, j: (0, j)),
        ],
        out_specs=pl.BlockSpec((tm, tn), lambda i, j: (i, j)),
        out_shape=jax.ShapeDtypeStruct((rows, B_COLS), BF16),
        scratch_shapes=[pltpu.VMEM((tm, D_MODEL), BF16)],
        compiler_params=pltpu.CompilerParams(
            dimension_semantics=("parallel", "arbitrary"), vmem_limit_bytes=VMEM_LIMIT_BYTES),
        name="proj_bf16",
    )(x, g, b, w)


def _gla_levels(blk):
    levels = []
    s = blk // 2
    while s >= 1:
        levels.append(s)
        s //= 2
    return levels


def _gla_decay_matrix(blk):
    i = np.arange(blk)[:, None]
    t = np.arange(blk)[None, :]
    mats = [(t <= i).astype(np.float32), (t > i).astype(np.float32)]
    for s in _gla_levels(blk):
        mid = (i // (2 * s)) * (2 * s) + s - 1
        lower = (i // s) % 2 == 1
        m = np.where(lower, ((t > mid) & (t <= i)), False).astype(np.float32)
        n = np.where(~lower, ((t > i) & (t <= mid)), False).astype(np.float32)
        mats.append(m + n)
    return np.concatenate(mats, axis=0)


def _gla_kernel(q_ref, k_ref, sm_ref, v_ref, r_ref, c_ref, w2_ref, gb_ref, ng_ref, s0_ref,
                o_ref, st_ref, s_scr, *, blk, n_blk, t_valid):
    ti = pl.program_id(1)
    levels = _gla_levels(blk)

    @pl.when(ti == 0)
    def _():
        s_scr[...] = s0_ref[0]

    ri = lax.broadcasted_iota(I32, (blk, blk), 0)
    ci = lax.broadcasted_iota(I32, (blk, blk), 1)
    eye = ri == ci
    lvl_masks = []
    for s in levels:
        sh = s.bit_length() - 1
        same = (ri >> (sh + 1)) == (ci >> (sh + 1))
        lvl_masks.append(same & (((ri >> sh) & 1) == 1) & (((ci >> sh) & 1) == 0))
    row_iota = lax.broadcasted_iota(I32, (blk, 1), 0)
    cmat = c_ref[...]
    w2 = w2_ref[...]
    gbias = gb_ref[...]
    ng = ng_ref[...]

    def block(j, carry):
        r0 = pl.multiple_of(j * blk, blk)
        rows = pl.ds(r0, blk)
        valid = (ti * (n_blk * blk) + r0 + row_iota) < t_valid
        gq = q_ref[rows, :] * (GLA_DK ** -0.5)
        gk = jnp.where(valid, k_ref[rows, :], 0.0)
        x = _dot(sm_ref[rows, :].astype(BF16), w2) + gbias
        logf = (jnp.minimum(x, 0.0) - jnp.log1p(jnp.exp(-jnp.abs(x)))) * (1.0 / GLA_TAU)
        logf = jnp.where(valid, logf, 0.0)
        hi = logf.astype(BF16)
        r1 = logf - hi.astype(F32)
        mid = r1.astype(BF16)
        lo = (r1 - mid.astype(F32)).astype(BF16)
        e_all = _dot(cmat, hi) + _dot(cmat, mid) + _dot(cmat, lo)
        for h in range(GLA_HEADS):
            ksl = slice(h * GLA_DK, (h + 1) * GLA_DK)
            vsl = slice(h * GLA_DV, (h + 1) * GLA_DV)
            qh = gq[:, ksl]
            kh = gk[:, ksl]
            vh = v_ref[rows, vsl]
            b_h = e_all[0:blk, ksl]
            rev_h = e_all[blk:2 * blk, ksl]
            st = s_scr[h]
            o = _dot_nt((qh * jnp.exp(b_h)).astype(BF16), st.astype(BF16))
            a = jnp.where(eye, _dot_nt(qh.astype(BF16), kh.astype(BF16)), 0.0)
            for li in range(len(levels)):
                e = jnp.exp(e_all[(2 + li) * blk:(3 + li) * blk, ksl])
                p = _dot_nt((qh * e).astype(BF16), (kh * e).astype(BF16))
                a = a + jnp.where(lvl_masks[li], p, 0.0)
            o = o + _dot(a.astype(BF16), vh)
            kd = (kh * jnp.exp(rev_h)).astype(BF16)
            dec = jnp.exp(b_h[blk - 1:blk, :])
            s_scr[h] = st * dec + _dot_tn(vh, kd)
            on = o * lax.rsqrt(jnp.mean(o * o, axis=-1, keepdims=True) + NORM_EPS) * ng
            gr = r_ref[rows, vsl].astype(F32)
            o_ref[rows, vsl] = (on * (gr * jax.nn.sigmoid(gr))).astype(BF16)
        return carry

    lax.fori_loop(0, n_blk, block, 0)

    @pl.when(ti == pl.num_programs(1) - 1)
    def _():
        st_ref[0] = s_scr[...]


def _gla(pf, pb, cmat, w2pad, gbias, ng, s0t, *, n_seq, row0, rows_per_seq, t_valid, blk, n_blk):
    rb = blk * n_blk
    steps = rows_per_seq // rb
    base = row0 // rb

    def rmap(c):
        return lambda s, t: (base + s * steps + t, c)

    kern = functools.partial(_gla_kernel, blk=blk, n_blk=n_blk, t_valid=t_valid)
    return pl.pallas_call(
        kern,
        grid=(n_seq, steps),
        in_specs=[
            pl.BlockSpec((rb, GLA_QK), rmap(F_GQ // GLA_QK)),
            pl.BlockSpec((rb, GLA_QK), rmap(F_GK // GLA_QK)),
            pl.BlockSpec((rb, LANES), rmap(F_SMALL // LANES)),
            pl.BlockSpec((rb, GLA_V), rmap(B_GV // GLA_V)),
            pl.BlockSpec((rb, GLA_V), rmap(B_GR // GLA_V)),
            pl.BlockSpec(cmat.shape, lambda s, t: (0, 0)),
            pl.BlockSpec(w2pad.shape, lambda s, t: (0, 0)),
            pl.BlockSpec((1, GLA_QK), lambda s, t: (0, 0)),
            pl.BlockSpec((1, GLA_DV), lambda s, t: (0, 0)),
            pl.BlockSpec((1, GLA_HEADS, GLA_DV, GLA_DK), lambda s, t: (s, 0, 0, 0)),
        ],
        out_specs=[
            pl.BlockSpec((rb, GLA_V), lambda s, t: (s * steps + t, 0)),
            pl.BlockSpec((1, GLA_HEADS, GLA_DV, GLA_DK), lambda s, t: (s, 0, 0, 0)),
        ],
        out_shape=[
            jax.ShapeDtypeStruct((n_seq * rows_per_seq, GLA_V), BF16),
            jax.ShapeDtypeStruct((n_seq, GLA_HEADS, GLA_DV, GLA_DK), F32),
        ],
        scratch_shapes=[pltpu.VMEM((GLA_HEADS, GLA_DV, GLA_DK), F32)],
        compiler_params=pltpu.CompilerParams(
            dimension_semantics=("parallel", "arbitrary"), vmem_limit_bytes=VMEM_LIMIT_BYTES),
        name="gla",
    )(pf, pf, pf, pb, pb, cmat, w2pad, gbias, ng, s0t)


def _dsa_kernel(iq_ref, sm_ref, qd_ref, kit_ref, kt_ref, v_ref, o_ref,
                key_scr, m_scr, l_scr, acc_scr, *, tq, tk, n_keys, topk, causal, n_kb_total):
    qb = pl.program_id(1)
    n_rows_g = DSA_GROUP * tq

    if causal:
        q_chunk_max = (qb * tq + tq - 1 - N_META) >> 6
        limit = jnp.minimum(N_META + CHUNK * (q_chunk_max + 1), n_keys)
        n_kb = (limit + tk - 1) // tk
    else:
        n_kb = n_kb_total

    q_pos = qb * tq + lax.broadcasted_iota(I32, (tq, 1), 0)
    q_chunk = (q_pos - N_META) >> 6
    lane_iota = lax.broadcasted_iota(I32, (1, tk), 1)

    iq = iq_ref[...]
    iq_hm = jnp.concatenate([iq[:, h * IDX_DIM:(h + 1) * IDX_DIM] for h in range(IDX_HEADS)], axis=0)
    w_i = sm_ref[:, SM_IW:SM_IW + IDX_HEADS] * IDX_W_SCALE
    w_cols = [jnp.broadcast_to(w_i[:, h:h + 1], (tq, LANES)) for h in range(IDX_HEADS)]

    def score_block(kb, carry):
        k0 = pl.multiple_of(kb * tk, tk)
        r = _dot(iq_hm, kit_ref[0, :, pl.ds(k0, tk)])
        sc = None
        for h in range(IDX_HEADS):
            rh = jnp.maximum(r[h * tq:(h + 1) * tq, :], 0.0)
            wh = jnp.concatenate([w_cols[h]] * (tk // LANES), axis=1)
            sc = rh * wh if sc is None else sc + rh * wh
        key_pos = k0 + lane_iota
        adm = key_pos < n_keys
        if causal:
            adm = adm & (((key_pos - N_META) >> 6) <= q_chunk)
        sc = jnp.where(adm, sc, -jnp.inf)
        bits = pltpu.bitcast(sc, I32)
        key_scr[:, pl.ds(k0, tk)] = bits ^ ((bits >> 31) & 0x7FFFFFFF)
        return carry

    lax.fori_loop(0, n_kb, score_block, 0)

    n_cols = n_kb * (tk // LANES)

    def count_ge(cand):
        cb = jnp.broadcast_to(cand, (tq, LANES))

        def body(c, acc):
            c0 = pl.multiple_of(c * LANES, LANES)
            return acc + jnp.where(key_scr[:, pl.ds(c0, LANES)] >= cb, 1, 0)

        acc = lax.fori_loop(0, n_cols, body, jnp.zeros((tq, LANES), I32))
        return jnp.sum(acc, axis=1, keepdims=True)

    def bit_step(i, carry):
        lo, cge = carry
        cand = lo + jnp.left_shift(jnp.int32(1), 31 - i)
        cnt = count_ge(cand)
        take = cnt >= topk
        return jnp.where(take, cand, lo), jnp.where(take, cnt, cge)

    lo0 = jnp.full((tq, 1), INT_MIN, I32)
    cge0 = jnp.zeros((tq, 1), I32) + n_cols * LANES
    thr, cge = lax.fori_loop(0, 32, bit_step, (lo0, cge0))
    need_tie = (cge > topk) & (thr > NEG_INF_KEY)
    thr = jnp.maximum(thr, NEG_INF_KEY + 1)

    @pl.when(jnp.max(need_tie.astype(I32)) > 0)
    def _():
        want = topk - count_ge(thr + 1)
        thr_b = jnp.broadcast_to(thr, (tq, LANES))
        col_iota = lax.broadcasted_iota(I32, (tq, LANES), 1)

        def count_tied_below(m):
            mb = jnp.broadcast_to(m, (tq, LANES))

            def body(c, acc):
                c0 = pl.multiple_of(c * LANES, LANES)
                hit = (key_scr[:, pl.ds(c0, LANES)] == thr_b) & ((c0 + col_iota) < mb)
                return acc + jnp.where(hit, 1, 0)

            acc = lax.fori_loop(0, n_cols, body, jnp.zeros((tq, LANES), I32))
            return jnp.sum(acc, axis=1, keepdims=True)

        n_bits = int(math.ceil(math.log2(n_kb_total * tk + 1)))

        def idx_step(i, m):
            cand = m + jnp.left_shift(jnp.int32(1), n_bits - 1 - i)
            return jnp.where(count_tied_below(cand) <= want, cand, m)

        m_keep = lax.fori_loop(0, n_bits, idx_step, jnp.zeros((tq, 1), I32))
        mk_b = jnp.broadcast_to(jnp.where(need_tie, m_keep, jnp.int32(2 ** 30)), (tq, LANES))

        def demote(c, carry):
            c0 = pl.multiple_of(c * LANES, LANES)
            kv = key_scr[:, pl.ds(c0, LANES)]
            drop = (kv == thr_b) & ((c0 + col_iota) >= mk_b)
            key_scr[:, pl.ds(c0, LANES)] = jnp.where(drop, kv - 1, kv)
            return carry

        lax.fori_loop(0, n_cols, demote, 0)

    qd = qd_ref[...]
    q_groups = []
    for g in range(DSA_KV_HEADS):
        q_groups.append(jnp.concatenate(
            [qd[:, (r * DSA_KV_HEADS + g) * DSA_HEAD_DIM:(r * DSA_KV_HEADS + g + 1) * DSA_HEAD_DIM]
             for r in range(DSA_GROUP)], axis=0))
    m_scr[...] = jnp.full(m_scr.shape, NEG_BIG, F32)
    l_scr[...] = jnp.zeros(l_scr.shape, F32)
    acc_scr[...] = jnp.zeros(acc_scr.shape, F32)
    thr_t = jnp.broadcast_to(thr, (tq, LANES))

    def attend_block(kb, carry):
        k0 = pl.multiple_of(kb * tk, tk)
        sel = key_scr[:, pl.ds(k0, tk)] >= jnp.concatenate([thr_t] * (tk // LANES), axis=1)
        bias = jnp.where(sel, 0.0, NEG_BIG)
        bias_g = jnp.concatenate([bias] * DSA_GROUP, axis=0)
        for g in range(DSA_KV_HEADS):
            s = _dot(q_groups[g], kt_ref[0, g, :, pl.ds(k0, tk)]) + bias_g
            m_old = m_scr[g]
            m_new = jnp.maximum(m_old, jnp.max(s, axis=1, keepdims=True))
            alpha = jnp.exp2(m_old - m_new)
            p = jnp.exp2(s - m_new)
            p = jnp.where(bias_g < 0.0, 0.0, p)
            l_scr[g] = alpha * l_scr[g] + jnp.sum(p, axis=1, keepdims=True)
            acc_scr[g] = alpha * acc_scr[g] + _dot(p.astype(BF16), v_ref[0, g, pl.ds(k0, tk), :])
            m_scr[g] = m_new
        return carry

    lax.fori_loop(0, n_kb, attend_block, 0)

    for g in range(DSA_KV_HEADS):
        og = acc_scr[g] / l_scr[g]
        for r in range(DSA_GROUP):
            c0 = (g * DSA_GROUP + r) * DSA_HEAD_DIM
            o_ref[:, c0:c0 + DSA_HEAD_DIM] = og[r * tq:(r + 1) * tq, :].astype(BF16)


def _dsa(pf, pb, kit, kt, v, *, n_seq, row0, rows_per_seq, tq, tk, n_keys, topk, causal):
    nk_pad = kit.shape[-1]
    n_kb_total = nk_pad // tk
    steps = rows_per_seq // tq
    base = row0 // tq

    def rmap(c):
        return lambda s, t: (base + s * steps + t, c)

    kern = functools.partial(_dsa_kernel, tq=tq, tk=tk, n_keys=n_keys, topk=topk, causal=causal,
                             n_kb_total=n_kb_total)
    single = pl.Buffered(1)
    return pl.pallas_call(
        kern,
        grid=(n_seq, steps),
        in_specs=[
            pl.BlockSpec((tq, IDX_Q), rmap(B_IQ // IDX_Q)),
            pl.BlockSpec((tq, LANES), rmap(F_SMALL // LANES)),
            pl.BlockSpec((tq, DSA_Q), rmap(B_DQ // DSA_Q)),
            pl.BlockSpec((1, IDX_DIM, nk_pad), lambda s, t: (s, 0, 0), pipeline_mode=single),
            pl.BlockSpec((1, DSA_KV_HEADS, DSA_HEAD_DIM, nk_pad), lambda s, t: (s, 0, 0, 0),
                         pipeline_mode=single),
            pl.BlockSpec((1, DSA_KV_HEADS, nk_pad, DSA_HEAD_DIM), lambda s, t: (s, 0, 0, 0),
                         pipeline_mode=single),
        ],
        out_specs=pl.BlockSpec((tq, DSA_Q), lambda s, t: (s * steps + t, 0)),
        out_shape=jax.ShapeDtypeStruct((n_seq * rows_per_seq, DSA_Q), BF16),
        scratch_shapes=[
            pltpu.VMEM((tq, nk_pad), I32),
            pltpu.VMEM((DSA_KV_HEADS, DSA_GROUP * tq, 1), F32),
            pltpu.VMEM((DSA_KV_HEADS, DSA_GROUP * tq, 1), F32),
            pltpu.VMEM((DSA_KV_HEADS, DSA_GROUP * tq, DSA_HEAD_DIM), F32),
        ],
        compiler_params=pltpu.CompilerParams(
            dimension_semantics=("parallel", "arbitrary"), vmem_limit_bytes=VMEM_LIMIT_BYTES),
        name="dsa",
    )(pb, pf, pb, kit, kt, v)


V_ROWS = 80
V_ONES_ROW = DSA_HEAD_DIM


def _dsa_t_kernel(iq_ref, sm_ref, qd_ref, ki_ref, k_ref, vt_ref, o_ref,
                  key_scr, m_scr, acc_scr, s_scr, *, tq, tk, n_keys, topk):
    qb = pl.program_id(1)
    q0 = qb * tq
    q_chunk_max = (q0 + tq - 1 - N_META) >> 6
    limit = jnp.minimum(N_META + CHUNK * (q_chunk_max + 1), n_keys)
    n_kb = (limit + tk - 1) // tk
    n_full = jnp.minimum(jnp.minimum((q0 + N_META) // tk, n_keys // tk), n_kb)

    q_chunk = (q0 + lax.broadcasted_iota(I32, (1, tq), 1) - N_META) >> 6
    row_iota = lax.broadcasted_iota(I32, (tk, tq), 0)

    iq = iq_ref[...]
    iq_hm = jnp.concatenate([iq[:, h * IDX_DIM:(h + 1) * IDX_DIM] for h in range(IDX_HEADS)], axis=0)
    w_t = sm_ref[...].T[SM_IW:SM_IW + IDX_HEADS, :] * IDX_W_SCALE

    def score_block(kb, masked):
        k0 = pl.multiple_of(kb * tk, tk)
        r = _dot_nt(ki_ref[0, pl.ds(k0, tk), :], iq_hm)
        sc = None
        for h in range(IDX_HEADS):
            t = jnp.maximum(r[:, h * tq:(h + 1) * tq], 0.0) * w_t[h:h + 1, :]
            sc = t if sc is None else sc + t
        if masked:
            key_pos = k0 + row_iota
            adm = (key_pos < n_keys) & (((key_pos - N_META) >> 6) <= q_chunk)
            sc = jnp.where(adm, sc, -jnp.inf)
        bits = pltpu.bitcast(sc, I32)
        key_scr[pl.ds(k0, tk), :] = bits ^ ((bits >> 31) & 0x7FFFFFFF)

    def score_full(kb, carry):
        score_block(kb, False)
        return carry

    def score_edge(kb, carry):
        score_block(kb, True)
        return carry

    lax.fori_loop(0, n_full, score_full, 0)
    lax.fori_loop(n_full, n_kb, score_edge, 0)

    n_chunks = n_kb * (tk // LANES)

    sub_per_blk = tk // LANES

    def count_rows(pred):
        def body(kb, acc):
            for u in range(sub_per_blk):
                c0 = pl.multiple_of(kb * tk + u * LANES, LANES)
                acc = acc + jnp.where(pred(key_scr[pl.ds(c0, LANES), :], c0), 1, 0)
            return acc

        acc = lax.fori_loop(0, n_kb, body, jnp.zeros((LANES, tq), I32))
        return jnp.sum(acc, axis=0, keepdims=True)

    def count_ge(cand):
        return count_rows(lambda kv, c0: kv >= cand)

    bits_per_round = 4

    def search_cond(carry):
        i, lo, cge = carry
        return (i < 32) & (jnp.max(jnp.abs(cge - topk)) > 0)

    def search_round(carry):
        i, lo, cge = carry
        for u in range(bits_per_round):
            cand = lo + jnp.left_shift(jnp.int32(1), 31 - u - i)
            cnt = count_ge(cand)
            take = cnt >= topk
            lo = jnp.where(take, cand, lo)
            cge = jnp.where(take, cnt, cge)
        return i + bits_per_round, lo, cge

    lo0 = jnp.full((1, tq), INT_MIN, I32)
    cge0 = jnp.zeros((1, tq), I32) + n_chunks * LANES
    _, thr, cge = lax.while_loop(search_cond, search_round, (jnp.int32(0), lo0, cge0))
    need_tie = (cge > topk) & (thr > NEG_INF_KEY)
    thr = jnp.maximum(thr, NEG_INF_KEY + 1)

    @pl.when(jnp.max(need_tie.astype(I32)) > 0)
    def _():
        want = topk - count_ge(thr + 1)
        sub_iota = lax.broadcasted_iota(I32, (LANES, tq), 0)
        n_bits = int(math.ceil(math.log2(key_scr.shape[0] + 1)))

        def idx_step(i, m):
            cand = m + jnp.left_shift(jnp.int32(1), n_bits - 1 - i)
            tied_below = count_rows(lambda kv, c0: (kv == thr) & ((c0 + sub_iota) < cand))
            return jnp.where(tied_below <= want, cand, m)

        m_keep = lax.fori_loop(0, n_bits, idx_step, jnp.zeros((1, tq), I32))
        m_keep = jnp.where(need_tie, m_keep, jnp.int32(2 ** 30))

        def demote(c, carry):
            c0 = pl.multiple_of(c * LANES, LANES)
            kv = key_scr[pl.ds(c0, LANES), :]
            drop = (kv == thr) & ((c0 + sub_iota) >= m_keep)
            key_scr[pl.ds(c0, LANES), :] = jnp.where(drop, kv - 1, kv)
            return carry

        lax.fori_loop(0, n_chunks, demote, 0)

    qd = qd_ref[...]
    win = lax.broadcasted_iota(I32, (tq, DSA_KV), 1) >> 6
    q_pad = []
    for g in range(DSA_KV_HEADS):
        q_pad.append(jnp.concatenate(
            [jnp.where(win == g, qd[:, r * DSA_KV:(r + 1) * DSA_KV].astype(F32), 0.0).astype(BF16)
             for r in range(DSA_GROUP)], axis=0))
    m_scr[...] = jnp.full(m_scr.shape, NEG_BIG, F32)
    acc_scr[...] = jnp.zeros(acc_scr.shape, F32)
    s_scr[0] = _dot_nt(k_ref[0, pl.ds(0, tk), :], q_pad[0])

    def attend_block(kb, carry):
        k0 = pl.multiple_of(kb * tk, tk)
        bias = jnp.where(key_scr[pl.ds(k0, tk), :] >= thr, 0.0, NEG_BIG)
        bias_g = jnp.concatenate([bias] * DSA_GROUP, axis=1)
        kblk = k_ref[0, pl.ds(k0, tk), :]
        k1 = pl.multiple_of(jnp.minimum(kb + 1, n_kb - 1) * tk, tk)
        for g in range(DSA_KV_HEADS):
            if g + 1 < DSA_KV_HEADS:
                s_scr[(g + 1) % 2] = _dot_nt(kblk, q_pad[g + 1])
            else:
                s_scr[0] = _dot_nt(k_ref[0, pl.ds(k1, tk), :], q_pad[0])
            s = s_scr[g % 2] + bias_g
            m_old = m_scr[g]
            m_new = jnp.maximum(m_old, jnp.max(s, axis=0, keepdims=True))
            alpha = jnp.exp2(m_old[0:1] - m_new[0:1])
            p = jnp.exp2(s - m_new[0:1]).astype(BF16)
            acc_scr[g] = acc_scr[g] * alpha + _dot(vt_ref[0, g, :, pl.ds(k0, tk)], p)
            m_scr[g] = m_new
        return carry

    lax.fori_loop(0, n_kb, attend_block, 0)

    for g in range(DSA_KV_HEADS):
        acc = acc_scr[g]
        og = acc[0:DSA_HEAD_DIM] / acc[V_ONES_ROW:V_ONES_ROW + 1]
        for r in range(DSA_GROUP):
            c0 = (g * DSA_GROUP + r) * DSA_HEAD_DIM
            o_ref[:, c0:c0 + DSA_HEAD_DIM] = og[:, r * tq:(r + 1) * tq].T.astype(BF16)


def _dsa_t(pf, pb, ki, k, vt, *, n_seq, rows_per_seq, tk, n_keys, topk):
    tq = LANES
    nk_pad = k.shape[1]
    steps = rows_per_seq // tq

    def rmap(c):
        return lambda s, t: (s * steps + t, c)

    kern = functools.partial(_dsa_t_kernel, tq=tq, tk=tk, n_keys=n_keys, topk=topk)
    single = pl.Buffered(1)
    return pl.pallas_call(
        kern,
        grid=(n_seq, steps),
        in_specs=[
            pl.BlockSpec((tq, IDX_Q), rmap(B_IQ // IDX_Q)),
            pl.BlockSpec((tq, LANES), rmap(F_SMALL // LANES)),
            pl.BlockSpec((tq, DSA_Q), rmap(B_DQ // DSA_Q)),
            pl.BlockSpec((1, nk_pad, IDX_DIM), lambda s, t: (s, 0, 0), pipeline_mode=single),
            pl.BlockSpec((1, nk_pad, DSA_KV), lambda s, t: (s, 0, 0), pipeline_mode=single),
            pl.BlockSpec((1, DSA_KV_HEADS, V_ROWS, nk_pad), lambda s, t: (s, 0, 0, 0), pipeline_mode=single),
        ],
        out_specs=pl.BlockSpec((tq, DSA_Q), lambda s, t: (s * steps + t, 0)),
        out_shape=jax.ShapeDtypeStruct((n_seq * rows_per_seq, DSA_Q), BF16),
        scratch_shapes=[
            pltpu.VMEM((nk_pad, tq), I32),
            pltpu.VMEM((DSA_KV_HEADS, 8, DSA_GROUP * tq), F32),
            pltpu.VMEM((DSA_KV_HEADS, V_ROWS, DSA_GROUP * tq), F32),
            pltpu.VMEM((2, tk, DSA_GROUP * tq), F32),
        ],
        compiler_params=pltpu.CompilerParams(
            dimension_semantics=("parallel", "arbitrary"), vmem_limit_bytes=VMEM_LIMIT_BYTES),
        name="dsa_t",
    )(pb, pf, pb, ki, k, vt)


def _out_kernel(x_ref, ya_ref, ob_ref, z_ref, ma_ref, mb_ref, wg_ref, wd_ref, wo_ref,
                gate_ref, lig_ref, lib_ref, lg_ref, lb_ref, o_ref, *, alpha):
    hn = _layer_norm_rows(x_ref[...], lig_ref[...], lib_ref[...])
    y_a = _dot(ya_ref[...], wg_ref[...])
    z = z_ref[...].astype(F32)
    yb_in = ob_ref[...].astype(F32) * (z * jax.nn.sigmoid(z))
    y_b = _dot(yb_in.astype(BF16), wd_ref[...])
    ga = jax.nn.sigmoid(ma_ref[...].astype(F32) + gate_ref[0:1, :])
    gb = jax.nn.sigmoid(mb_ref[...].astype(F32) + gate_ref[1:2, :])
    merged = ga * y_a + gb * y_b
    y = alpha * hn + _dot(merged.astype(BF16), wo_ref[...])
    o_ref[...] = _layer_norm_rows(y, lg_ref[...], lb_ref[...])


def _out(x, ya, ob, pb, wg, wd, wo, gate, lig, lib, lg, lb, tm, alpha):
    rows = x.shape[0]
    row = lambda c: pl.BlockSpec((tm, D_MODEL), lambda i, c=c: (i, c))
    full = lambda a: pl.BlockSpec(a.shape, lambda i: (0, 0))
    return pl.pallas_call(
        functools.partial(_out_kernel, alpha=alpha),
        grid=(rows // tm,),
        in_specs=[row(0), row(0), row(0), row(B_DZ // D_MODEL), row(B_MA // D_MODEL), row(B_MB // D_MODEL),
                  full(wg), full(wd), full(wo), full(gate), full(lig), full(lib), full(lg), full(lb)],
        out_specs=pl.BlockSpec((tm, D_MODEL), lambda i: (i, 0)),
        out_shape=jax.ShapeDtypeStruct((rows, D_MODEL), F32),
        compiler_params=pltpu.CompilerParams(
            dimension_semantics=("parallel",), vmem_limit_bytes=VMEM_LIMIT_BYTES),
        name="out_proj",
    )(x, ya, ob, pb, pb, pb, wg, wd, wo, gate, lig, lib, lg, lb)


def _round_up(a, b):
    return -(-a // b) * b


def _forward(x_prompt, x_sample, cache_k, cache_v, cache_idx_k, state_gla, meta, ln_in_g, ln_in_b,
             w_in, gla_w2, gla_gate_b, gla_norm_g, idx_kn_g, idx_kn_b, w_gla, w_dsa, gate_b,
             w_out, ln_g, ln_b):
    depth = w_in.shape[0]
    assert depth == 1, "single-layer trunk"
    bsz, seq, _ = x_prompt.shape
    dbsz, dseq, _ = x_sample.shape
    past = cache_k.shape[2]
    t_p = N_META + seq
    tq_p = 128
    tp = _round_up(t_p, KEY_TILE)
    assert tp % tq_p == 0 and tp % GLA_BLOCK == 0
    row_s = bsz * tp
    rows = _round_up(row_s + dbsz * dseq, ROW_TILE)
    alpha = (2.0 * depth) ** 0.25

    meta_rows = jnp.broadcast_to(meta.astype(F32)[None], (bsz, N_META, D_MODEL))
    xp = jnp.concatenate([meta_rows, x_prompt, jnp.zeros((bsz, tp - t_p, D_MODEL), F32)], axis=1)
    x_all = jnp.concatenate([xp.reshape(bsz * tp, D_MODEL), x_sample.reshape(dbsz * dseq, D_MODEL),
                             jnp.zeros((rows - row_s - dbsz * dseq, D_MODEL), F32)], axis=0)

    w = w_in[0]
    w_f = jnp.concatenate([_col(w, _GQ), _col(w, _GK), _col(w, _DK), _col(w, _DV), _col(w, _IK),
                           _col(w, _GLOW), _col(w, _IW),
                           jnp.zeros((D_MODEL, LANES - IDX_DIM - GLA_RANK - IDX_HEADS), F32)], axis=1).astype(BF16)
    w_dq = _col(w, _DQ).reshape(D_MODEL, DSA_KV_HEADS, DSA_GROUP, DSA_HEAD_DIM)
    w_dq = (jnp.swapaxes(w_dq, 1, 2) * (DSA_SCALE * math.log2(math.e))).reshape(D_MODEL, DSA_Q)
    w_b = jnp.concatenate([_col(w, _GV), _col(w, _GR), w_dq, _col(w, _DZ), _col(w, _MA), _col(w, _MB),
                           _col(w, _IQ), jnp.zeros((D_MODEL, B_COLS - B_PAD), F32)], axis=1).astype(BF16)
    lig = ln_in_g.reshape(1, D_MODEL)
    lib = ln_in_b.reshape(1, D_MODEL)

    pf = _proj_f32(x_all, lig, lib, w_f, idx_kn_g[0].reshape(1, IDX_DIM), idx_kn_b[0].reshape(1, IDX_DIM),
                   ROW_TILE // 4)
    pb = _proj_bf16(x_all, lig, lib, w_b, ROW_TILE // 2, 1024)

    w2pad = jnp.zeros((LANES, GLA_QK), F32).at[SM_GLOW:SM_GLOW + GLA_RANK].set(gla_w2[0]).astype(BF16)
    gbias = gla_gate_b[0].reshape(1, GLA_QK)
    ng = gla_norm_g[0].reshape(1, GLA_DV)
    blk_p = min(GLA_BLOCK, t_p)
    cm_p = jnp.asarray(_gla_decay_matrix(blk_p), BF16)
    s0_p = jnp.zeros((bsz, GLA_HEADS, GLA_DV, GLA_DK), F32)
    ya_p, st_p = _gla(pf, pb, cm_p, w2pad, gbias, ng, s0_p, n_seq=bsz, row0=0, rows_per_seq=tp,
                      t_valid=t_p, blk=blk_p, n_blk=KEY_TILE // blk_p)
    blk_s = min(GLA_BLOCK, dseq)
    assert dseq % blk_s == 0 and row_s % dseq == 0
    cm_s = jnp.asarray(_gla_decay_matrix(blk_s), BF16)
    s0_s = jnp.swapaxes(state_gla[0], -1, -2)
    ya_s, st_s = _gla(pf, pb, cm_s, w2pad, gbias, ng, s0_s, n_seq=dbsz, row0=row_s, rows_per_seq=dseq,
                      t_valid=dseq, blk=blk_s, n_blk=dseq // blk_s)

    dk = pf[:, F_DK:F_DK + DSA_KV]
    dv = pf[:, F_DV:F_DV + DSA_KV]
    ki = pf[:, F_SMALL:F_SMALL + IDX_DIM]
    dk_p = dk[:row_s].reshape(bsz, tp, DSA_KV_HEADS, DSA_HEAD_DIM)
    dv_p = dv[:row_s].reshape(bsz, tp, DSA_KV_HEADS, DSA_HEAD_DIM)
    ki_p = ki[:row_s].reshape(bsz, tp, IDX_DIM)
    vt_p = jnp.concatenate([jnp.transpose(dv_p, (0, 2, 3, 1)),
                            jnp.ones((bsz, DSA_KV_HEADS, 1, tp), F32),
                            jnp.zeros((bsz, DSA_KV_HEADS, V_ROWS - DSA_HEAD_DIM - 1, tp), F32)], axis=2).astype(BF16)
    topk_p = min(TOPK_MAX, (t_p - N_META) // 4)
    ob_p = _dsa_t(pf, pb, ki_p.astype(BF16), dk[:row_s].reshape(bsz, tp, DSA_KV).astype(BF16), vt_p,
                  n_seq=bsz, rows_per_seq=tp, tk=KEY_TILE, n_keys=t_p, topk=topk_p)

    n_keys_s = past + dseq
    nk_s = _round_up(n_keys_s, KEY_TILE)
    dk_s = dk[row_s:row_s + dbsz * dseq].reshape(dbsz, dseq, DSA_KV_HEADS, DSA_HEAD_DIM)
    dv_s = dv[row_s:row_s + dbsz * dseq].reshape(dbsz, dseq, DSA_KV_HEADS, DSA_HEAD_DIM)
    ki_s = ki[row_s:row_s + dbsz * dseq].reshape(dbsz, dseq, IDX_DIM)
    kpad = jnp.zeros((dbsz, nk_s - n_keys_s, DSA_KV_HEADS, DSA_HEAD_DIM), F32)
    k_all = jnp.concatenate([cache_k[0], dk_s, kpad], axis=1)
    v_all = jnp.concatenate([cache_v[0], dv_s, kpad], axis=1)
    ki_all = jnp.concatenate([cache_idx_k[0], ki_s, jnp.zeros((dbsz, nk_s - n_keys_s, IDX_DIM), F32)], axis=1)
    kt_s = jnp.transpose(k_all, (0, 2, 3, 1)).astype(BF16)
    v_s = jnp.transpose(v_all, (0, 2, 1, 3)).astype(BF16)
    kit_s = jnp.transpose(ki_all, (0, 2, 1)).astype(BF16)
    topk_s = min(TOPK_MAX, n_keys_s // 4)
    ob_s = _dsa(pf, pb, kit_s, kt_s, v_s, n_seq=dbsz, row0=row_s, rows_per_seq=dseq, tq=dseq, tk=KEY_TILE,
                n_keys=n_keys_s, topk=topk_s, causal=False)

    tail = jnp.zeros((rows - row_s - dbsz * dseq, D_MODEL), BF16)
    ya = jnp.concatenate([ya_p, ya_s, tail], axis=0)
    ob = jnp.concatenate([ob_p, ob_s, tail], axis=0)
    y = _out(x_all, ya, ob, pb, w_gla[0].astype(BF16), w_dsa[0].astype(BF16), w_out[0].astype(BF16),
             gate_b[0], lig, lib, ln_g[0].reshape(1, D_MODEL), ln_b[0].reshape(1, D_MODEL),
             ROW_TILE // 4, alpha)

    y_prompt = y[:row_s].reshape(bsz, tp, D_MODEL)[:, N_META:t_p]
    y_sample = y[row_s:row_s + dbsz * dseq].reshape(dbsz, dseq, D_MODEL)
    k_prompt = dk_p[:, :t_p][None]
    v_prompt = dv_p[:, :t_p][None]
    idx_k_prompt = ki_p[:, :t_p][None]
    gla_prompt = jnp.swapaxes(st_p, -1, -2)[None]
    k_sample = dk_s[None]
    v_sample = dv_s[None]
    idx_k_sample = ki_s[None]
    gla_sample = jnp.swapaxes(st_s, -1, -2)[None]
    return (y_prompt, y_sample, k_prompt, v_prompt, idx_k_prompt, gla_prompt,
            k_sample, v_sample, idx_k_sample, gla_sample)


def kernel(x_prompt, x_sample, cache_k, cache_v, cache_idx_k, state_gla, meta, ln_in_g, ln_in_b,
           w_in, gla_w2, gla_gate_b, gla_norm_g, idx_kn_g, idx_kn_b, w_gla, w_dsa, gate_b,
           w_out, ln_g, ln_b):
    return _forward(x_prompt, x_sample, cache_k, cache_v, cache_idx_k, state_gla, meta, ln_in_g, ln_in_b,
                    w_in, gla_w2, gla_gate_b, gla_norm_g, idx_kn_g, idx_kn_b, w_gla, w_dsa, gate_b,
                    w_out, ln_g, ln_b)
```

```python
import functools
import math

import numpy as np
import jax
import jax.numpy as jnp
from jax import lax
from jax.experimental import pallas as pl
from jax.experimental.pallas import tpu as pltpu

F32 = jnp.float32
BF16 = jnp.bfloat16
I32 = jnp.int32

D_MODEL = 1024
CHUNK = 64
N_META = 16
GLA_HEADS = 4
GLA_DK = 128
GLA_DV = 256
GLA_RANK = 16
GLA_TAU = 16.0
GLA_BLOCK = 64
DSA_HEADS = 16
DSA_KV_HEADS = 4
DSA_HEAD_DIM = 64
DSA_GROUP = DSA_HEADS // DSA_KV_HEADS
DSA_SCALE = DSA_HEAD_DIM ** -0.5
IDX_HEADS = 8
IDX_DIM = 64
IDX_W_SCALE = (IDX_HEADS ** -0.5) * (IDX_DIM ** -0.5)
TOPK_MAX = 256
NORM_EPS = 1e-5
GLA_QK = GLA_HEADS * GLA_DK
GLA_V = GLA_HEADS * GLA_DV
DSA_Q = DSA_HEADS * DSA_HEAD_DIM
DSA_KV = DSA_KV_HEADS * DSA_HEAD_DIM
IDX_Q = IDX_HEADS * IDX_DIM
SPLITS = (GLA_QK, GLA_QK, GLA_V, GLA_RANK, GLA_V,
          DSA_Q, DSA_KV, DSA_KV, IDX_Q, IDX_DIM, IDX_HEADS, DSA_Q,
          D_MODEL, D_MODEL)
_OFF = tuple(int(o) for o in np.cumsum((0,) + SPLITS))
(_GQ, _GK, _GV, _GLOW, _GR, _DQ, _DK, _DV, _IQ, _IK, _IW, _DZ, _MA, _MB) = range(14)

LANES = 128
VMEM_LIMIT_BYTES = 56 * 1024 * 1024

F_GQ, F_GK, F_DK, F_DV, F_SMALL = 0, 512, 1024, 1280, 1536
F_COLS = F_SMALL + LANES
SM_IK, SM_GLOW, SM_IW = 0, IDX_DIM, IDX_DIM + GLA_RANK
B_GV, B_GR, B_DQ, B_DZ, B_MA, B_MB, B_IQ, B_PAD = 0, 1024, 2048, 3072, 4096, 5120, 6144, 6656
B_COLS = 7168

ROW_TILE = 2048
KEY_TILE = 512
NEG_BIG = -1e30
INT_MIN = -(2 ** 31)
NEG_INF_KEY = int(np.array(-np.inf, np.float32).view(np.int32)) ^ 0x7FFFFFFF


def _col(w, idx):
    return w[:, _OFF[idx]:_OFF[idx + 1]]


def _layer_norm_rows(x, g, b):
    mu = jnp.mean(x, axis=-1, keepdims=True)
    xc = x - mu
    var = jnp.mean(xc * xc, axis=-1, keepdims=True)
    return xc * lax.rsqrt(var + NORM_EPS) * g + b


def _dot(a, b):
    return jnp.dot(a, b, preferred_element_type=F32)


def _dot_nt(a, b):
    return lax.dot_general(a, b, (((1,), (1,)), ((), ())), preferred_element_type=F32)


def _dot_tn(a, b):
    return lax.dot_general(a, b, (((0,), (0,)), ((), ())), preferred_element_type=F32)


def _proj_f32_kernel(x_ref, g_ref, b_ref, w_ref, kg_ref, kb_ref, o_ref):
    hn = _layer_norm_rows(x_ref[...], g_ref[...], b_ref[...])
    y = _dot(hn.astype(BF16), w_ref[...])
    o_ref[...] = y
    ik = y[:, F_SMALL:F_SMALL + IDX_DIM]
    o_ref[:, F_SMALL:F_SMALL + IDX_DIM] = _layer_norm_rows(ik, kg_ref[...], kb_ref[...])


def _proj_f32(x, g, b, w, kg, kb, tm):
    rows = x.shape[0]
    return pl.pallas_call(
        _proj_f32_kernel,
        grid=(rows // tm,),
        in_specs=[
            pl.BlockSpec((tm, D_MODEL), lambda i: (i, 0)),
            pl.BlockSpec((1, D_MODEL), lambda i: (0, 0)),
            pl.BlockSpec((1, D_MODEL), lambda i: (0, 0)),
            pl.BlockSpec((D_MODEL, F_COLS), lambda i: (0, 0)),
            pl.BlockSpec((1, IDX_DIM), lambda i: (0, 0)),
            pl.BlockSpec((1, IDX_DIM), lambda i: (0, 0)),
        ],
        out_specs=pl.BlockSpec((tm, F_COLS), lambda i: (i, 0)),
        out_shape=jax.ShapeDtypeStruct((rows, F_COLS), F32),
        compiler_params=pltpu.CompilerParams(
            dimension_semantics=("parallel",), vmem_limit_bytes=VMEM_LIMIT_BYTES),
        name="proj_f32",
    )(x, g, b, w, kg, kb)


def _proj_bf16_kernel(x_ref, g_ref, b_ref, w_ref, o_ref, hn_ref):
    @pl.when(pl.program_id(1) == 0)
    def _():
        hn_ref[...] = _layer_norm_rows(x_ref[...], g_ref[...], b_ref[...]).astype(BF16)

    o_ref[...] = _dot(hn_ref[...], w_ref[...]).astype(BF16)


def _proj_bf16(x, g, b, w, tm, tn):
    rows = x.shape[0]
    return pl.pallas_call(
        _proj_bf16_kernel,
        grid=(rows // tm, B_COLS // tn),
        in_specs=[
            pl.BlockSpec((tm, D_MODEL), lambda i, j: (i, 0)),
            pl.BlockSpec((1, D_MODEL), lambda i, j: (0, 0)),
            pl.BlockSpec((1, D_MODEL), lambda i, j: (0, 0)),
            pl.BlockSpec((D_MODEL, tn), lambda i, j: (0, j)),
        ],
        out_specs=pl.BlockSpec((tm, tn), lambda i, j: (i, j)),
        out_shape=jax.ShapeDtypeStruct((rows, B_COLS), BF16),
        scratch_shapes=[pltpu.VMEM((tm, D_MODEL), BF16)],
        compiler_params=pltpu.CompilerParams(
            dimension_semantics=("parallel", "arbitrary"), vmem_limit_bytes=VMEM_LIMIT_BYTES),
        name="proj_bf16",
    )(x, g, b, w)


def _gla_levels(blk):
    levels = []
    s = blk // 2
    while s >= 1:
        levels.append(s)
        s //= 2
    return levels


def _gla_decay_matrix(blk):
    i = np.arange(blk)[:, None]
    t = np.arange(blk)[None, :]
    mats = [(t <= i).astype(np.float32), (t > i).astype(np.float32)]
    for s in _gla_levels(blk):
        mid = (i // (2 * s)) * (2 * s) + s - 1
        lower = (i // s) % 2 == 1
        m = np.where(lower, ((t > mid) & (t <= i)), False).astype(np.float32)
        n = np.where(~lower, ((t > i) & (t <= mid)), False).astype(np.float32)
        mats.append(m + n)
    return np.concatenate(mats, axis=0)


def _gla_kernel(q_ref, k_ref, sm_ref, v_ref, r_ref, c_ref, w2_ref, gb_ref, ng_ref, s0_ref,
                o_ref, st_ref, s_scr, *, blk, n_blk, t_valid):
    ti = pl.program_id(1)
    levels = _gla_levels(blk)

    @pl.when(ti == 0)
    def _():
        s_scr[...] = s0_ref[0]

    ri = lax.broadcasted_iota(I32, (blk, blk), 0)
    ci = lax.broadcasted_iota(I32, (blk, blk), 1)
    eye = ri == ci
    lvl_masks = []
    for s in levels:
        sh = s.bit_length() - 1
        same = (ri >> (sh + 1)) == (ci >> (sh + 1))
        lvl_masks.append(same & (((ri >> sh) & 1) == 1) & (((ci >> sh) & 1) == 0))
    row_iota = lax.broadcasted_iota(I32, (blk, 1), 0)
    cmat = c_ref[...]
    w2 = w2_ref[...]
    gbias = gb_ref[...]
    ng = ng_ref[...]

    def block(j, carry):
        r0 = pl.multiple_of(j * blk, blk)
        rows = pl.ds(r0, blk)
        valid = (ti * (n_blk * blk) + r0 + row_iota) < t_valid
        gq = q_ref[rows, :] * (GLA_DK ** -0.5)
        gk = jnp.where(valid, k_ref[rows, :], 0.0)
        x = _dot(sm_ref[rows, :].astype(BF16), w2) + gbias
        logf = (jnp.minimum(x, 0.0) - jnp.log1p(jnp.exp(-jnp.abs(x)))) * (1.0 / GLA_TAU)
        logf = jnp.where(valid, logf, 0.0)
        hi = logf.astype(BF16)
        r1 = logf - hi.astype(F32)
        mid = r1.astype(BF16)
        lo = (r1 - mid.astype(F32)).astype(BF16)
        e_all = _dot(cmat, hi) + _dot(cmat, mid) + _dot(cmat, lo)
        for h in range(GLA_HEADS):
            ksl = slice(h * GLA_DK, (h + 1) * GLA_DK)
            vsl = slice(h * GLA_DV, (h + 1) * GLA_DV)
            qh = gq[:, ksl]
            kh = gk[:, ksl]
            vh = v_ref[rows, vsl]
            b_h = e_all[0:blk, ksl]
            rev_h = e_all[blk:2 * blk, ksl]
            st = s_scr[h]
            o = _dot_nt((qh * jnp.exp(b_h)).astype(BF16), st.astype(BF16))
            a = jnp.where(eye, _dot_nt(qh.astype(BF16), kh.astype(BF16)), 0.0)
            for li in range(len(levels)):
                e = jnp.exp(e_all[(2 + li) * blk:(3 + li) * blk, ksl])
                p = _dot_nt((qh * e).astype(BF16), (kh * e).astype(BF16))
                a = a + jnp.where(lvl_masks[li], p, 0.0)
            o = o + _dot(a.astype(BF16), vh)
            kd = (kh * jnp.exp(rev_h)).astype(BF16)
            dec = jnp.exp(b_h[blk - 1:blk, :])
            s_scr[h] = st * dec + _dot_tn(vh, kd)
            on = o * lax.rsqrt(jnp.mean(o * o, axis=-1, keepdims=True) + NORM_EPS) * ng
            gr = r_ref[rows, vsl].astype(F32)
            o_ref[rows, vsl] = (on * (gr * jax.nn.sigmoid(gr))).astype(BF16)
        return carry

    lax.fori_loop(0, n_blk, block, 0)

    @pl.when(ti == pl.num_programs(1) - 1)
    def _():
        st_ref[0] = s_scr[...]


def _gla(pf, pb, cmat, w2pad, gbias, ng, s0t, *, n_seq, row0, rows_per_seq, t_valid, blk, n_blk):
    rb = blk * n_blk
    steps = rows_per_seq // rb
    base = row0 // rb

    def rmap(c):
        return lambda s, t: (base + s * steps + t, c)

    kern = functools.partial(_gla_kernel, blk=blk, n_blk=n_blk, t_valid=t_valid)
    return pl.pallas_call(
        kern,
        grid=(n_seq, steps),
        in_specs=[
            pl.BlockSpec((rb, GLA_QK), rmap(F_GQ // GLA_QK)),
            pl.BlockSpec((rb, GLA_QK), rmap(F_GK // GLA_QK)),
            pl.BlockSpec((rb, LANES), rmap(F_SMALL // LANES)),
            pl.BlockSpec((rb, GLA_V), rmap(B_GV // GLA_V)),
            pl.BlockSpec((rb, GLA_V), rmap(B_GR // GLA_V)),
            pl.BlockSpec(cmat.shape, lambda s, t: (0, 0)),
            pl.BlockSpec(w2pad.shape, lambda s, t: (0, 0)),
            pl.BlockSpec((1, GLA_QK), lambda s, t: (0, 0)),
            pl.BlockSpec((1, GLA_DV), lambda s, t: (0, 0)),
            pl.BlockSpec((1, GLA_HEADS, GLA_DV, GLA_DK), lambda s, t: (s, 0, 0, 0)),
        ],
        out_specs=[
            pl.BlockSpec((rb, GLA_V), lambda s, t: (s * steps + t, 0)),
            pl.BlockSpec((1, GLA_HEADS, GLA_DV, GLA_DK), lambda s, t: (s, 0, 0, 0)),
        ],
        out_shape=[
            jax.ShapeDtypeStruct((n_seq * rows_per_seq, GLA_V), BF16),
            jax.ShapeDtypeStruct((n_seq, GLA_HEADS, GLA_DV, GLA_DK), F32),
        ],
        scratch_shapes=[pltpu.VMEM((GLA_HEADS, GLA_DV, GLA_DK), F32)],
        compiler_params=pltpu.CompilerParams(
            dimension_semantics=("parallel", "arbitrary"), vmem_limit_bytes=VMEM_LIMIT_BYTES),
        name="gla",
    )(pf, pf, pf, pb, pb, cmat, w2pad, gbias, ng, s0t)


def _dsa_kernel(iq_ref, sm_ref, qd_ref, kit_ref, kt_ref, v_ref, o_ref,
                key_scr, m_scr, l_scr, acc_scr, *, tq, tk, n_keys, topk, causal, n_kb_total):
    qb = pl.program_id(1)
    n_rows_g = DSA_GROUP * tq

    if causal:
        q_chunk_max = (qb * tq + tq - 1 - N_META) >> 6
        limit = jnp.minimum(N_META + CHUNK * (q_chunk_max + 1), n_keys)
        n_kb = (limit + tk - 1) // tk
    else:
        n_kb = n_kb_total

    q_pos = qb * tq + lax.broadcasted_iota(I32, (tq, 1), 0)
    q_chunk = (q_pos - N_META) >> 6
    lane_iota = lax.broadcasted_iota(I32, (1, tk), 1)

    iq = iq_ref[...]
    iq_hm = jnp.concatenate([iq[:, h * IDX_DIM:(h + 1) * IDX_DIM] for h in range(IDX_HEADS)], axis=0)
    w_i = sm_ref[:, SM_IW:SM_IW + IDX_HEADS] * IDX_W_SCALE
    w_cols = [jnp.broadcast_to(w_i[:, h:h + 1], (tq, LANES)) for h in range(IDX_HEADS)]

    def score_block(kb, carry):
        k0 = pl.multiple_of(kb * tk, tk)
        r = _dot(iq_hm, kit_ref[0, :, pl.ds(k0, tk)])
        sc = None
        for h in range(IDX_HEADS):
            rh = jnp.maximum(r[h * tq:(h + 1) * tq, :], 0.0)
            wh = jnp.concatenate([w_cols[h]] * (tk // LANES), axis=1)
            sc = rh * wh if sc is None else sc + rh * wh
        key_pos = k0 + lane_iota
        adm = key_pos < n_keys
        if causal:
            adm = adm & (((key_pos - N_META) >> 6) <= q_chunk)
        sc = jnp.where(adm, sc, -jnp.inf)
        bits = pltpu.bitcast(sc, I32)
        key_scr[:, pl.ds(k0, tk)] = bits ^ ((bits >> 31) & 0x7FFFFFFF)
        return carry

    lax.fori_loop(0, n_kb, score_block, 0)

    n_cols = n_kb * (tk // LANES)

    def count_ge(cand):
        cb = jnp.broadcast_to(cand, (tq, LANES))

        def body(c, acc):
            c0 = pl.multiple_of(c * LANES, LANES)
            return acc + jnp.where(key_scr[:, pl.ds(c0, LANES)] >= cb, 1, 0)

        acc = lax.fori_loop(0, n_cols, body, jnp.zeros((tq, LANES), I32))
        return jnp.sum(acc, axis=1, keepdims=True)

    def bit_step(i, carry):
        lo, cge = carry
        cand = lo + jnp.left_shift(jnp.int32(1), 31 - i)
        cnt = count_ge(cand)
        take = cnt >= topk
        return jnp.where(take, cand, lo), jnp.where(take, cnt, cge)

    lo0 = jnp.full((tq, 1), INT_MIN, I32)
    cge0 = jnp.zeros((tq, 1), I32) + n_cols * LANES
    thr, cge = lax.fori_loop(0, 32, bit_step, (lo0, cge0))
    need_tie = (cge > topk) & (thr > NEG_INF_KEY)
    thr = jnp.maximum(thr, NEG_INF_KEY + 1)

    @pl.when(jnp.max(need_tie.astype(I32)) > 0)
    def _():
        want = topk - count_ge(thr + 1)
        thr_b = jnp.broadcast_to(thr, (tq, LANES))
        col_iota = lax.broadcasted_iota(I32, (tq, LANES), 1)

        def count_tied_below(m):
            mb = jnp.broadcast_to(m, (tq, LANES))

            def body(c, acc):
                c0 = pl.multiple_of(c * LANES, LANES)
                hit = (key_scr[:, pl.ds(c0, LANES)] == thr_b) & ((c0 + col_iota) < mb)
                return acc + jnp.where(hit, 1, 0)

            acc = lax.fori_loop(0, n_cols, body, jnp.zeros((tq, LANES), I32))
            return jnp.sum(acc, axis=1, keepdims=True)

        n_bits = int(math.ceil(math.log2(n_kb_total * tk + 1)))

        def idx_step(i, m):
            cand = m + jnp.left_shift(jnp.int32(1), n_bits - 1 - i)
            return jnp.where(count_tied_below(cand) <= want, cand, m)

        m_keep = lax.fori_loop(0, n_bits, idx_step, jnp.zeros((tq, 1), I32))
        mk_b = jnp.broadcast_to(jnp.where(need_tie, m_keep, jnp.int32(2 ** 30)), (tq, LANES))

        def demote(c, carry):
            c0 = pl.multiple_of(c * LANES, LANES)
            kv = key_scr[:, pl.ds(c0, LANES)]
            drop = (kv == thr_b) & ((c0 + col_iota) >= mk_b)
            key_scr[:, pl.ds(c0, LANES)] = jnp.where(drop, kv - 1, kv)
            return carry

        lax.fori_loop(0, n_cols, demote, 0)

    qd = qd_ref[...]
    q_groups = []
    for g in range(DSA_KV_HEADS):
        q_groups.append(jnp.concatenate(
            [qd[:, (r * DSA_KV_HEADS + g) * DSA_HEAD_DIM:(r * DSA_KV_HEADS + g + 1) * DSA_HEAD_DIM]
             for r in range(DSA_GROUP)], axis=0))
    m_scr[...] = jnp.full(m_scr.shape, NEG_BIG, F32)
    l_scr[...] = jnp.zeros(l_scr.shape, F32)
    acc_scr[...] = jnp.zeros(acc_scr.shape, F32)
    thr_t = jnp.broadcast_to(thr, (tq, LANES))

    def attend_block(kb, carry):
        k0 = pl.multiple_of(kb * tk, tk)
        sel = key_scr[:, pl.ds(k0, tk)] >= jnp.concatenate([thr_t] * (tk // LANES), axis=1)
        bias = jnp.where(sel, 0.0, NEG_BIG)
        bias_g = jnp.concatenate([bias] * DSA_GROUP, axis=0)
        for g in range(DSA_KV_HEADS):
            s = _dot(q_groups[g], kt_ref[0, g, :, pl.ds(k0, tk)]) + bias_g
            m_old = m_scr[g]
            m_new = jnp.maximum(m_old, jnp.max(s, axis=1, keepdims=True))
            alpha = jnp.exp2(m_old - m_new)
            p = jnp.exp2(s - m_new)
            p = jnp.where(bias_g < 0.0, 0.0, p)
            l_scr[g] = alpha * l_scr[g] + jnp.sum(p, axis=1, keepdims=True)
            acc_scr[g] = alpha * acc_scr[g] + _dot(p.astype(BF16), v_ref[0, g, pl.ds(k0, tk), :])
            m_scr[g] = m_new
        return carry

    lax.fori_loop(0, n_kb, attend_block, 0)

    for g in range(DSA_KV_HEADS):
        og = acc_scr[g] / l_scr[g]
        for r in range(DSA_GROUP):
            c0 = (g * DSA_GROUP + r) * DSA_HEAD_DIM
            o_ref[:, c0:c0 + DSA_HEAD_DIM] = og[r * tq:(r + 1) * tq, :].astype(BF16)


def _dsa(pf, pb, kit, kt, v, *, n_seq, row0, rows_per_seq, tq, tk, n_keys, topk, causal):
    nk_pad = kit.shape[-1]
    n_kb_total = nk_pad // tk
    steps = rows_per_seq // tq
    base = row0 // tq

    def rmap(c):
        return lambda s, t: (base + s * steps + t, c)

    kern = functools.partial(_dsa_kernel, tq=tq, tk=tk, n_keys=n_keys, topk=topk, causal=causal,
                             n_kb_total=n_kb_total)
    single = pl.Buffered(1)
    return pl.pallas_call(
        kern,
        grid=(n_seq, steps),
        in_specs=[
            pl.BlockSpec((tq, IDX_Q), rmap(B_IQ // IDX_Q)),
            pl.BlockSpec((tq, LANES), rmap(F_SMALL // LANES)),
            pl.BlockSpec((tq, DSA_Q), rmap(B_DQ // DSA_Q)),
            pl.BlockSpec((1, IDX_DIM, nk_pad), lambda s, t: (s, 0, 0), pipeline_mode=single),
            pl.BlockSpec((1, DSA_KV_HEADS, DSA_HEAD_DIM, nk_pad), lambda s, t: (s, 0, 0, 0),
                         pipeline_mode=single),
            pl.BlockSpec((1, DSA_KV_HEADS, nk_pad, DSA_HEAD_DIM), lambda s, t: (s, 0, 0, 0),
                         pipeline_mode=single),
        ],
        out_specs=pl.BlockSpec((tq, DSA_Q), lambda s, t: (s * steps + t, 0)),
        out_shape=jax.ShapeDtypeStruct((n_seq * rows_per_seq, DSA_Q), BF16),
        scratch_shapes=[
            pltpu.VMEM((tq, nk_pad), I32),
            pltpu.VMEM((DSA_KV_HEADS, DSA_GROUP * tq, 1), F32),
            pltpu.VMEM((DSA_KV_HEADS, DSA_GROUP * tq, 1), F32),
            pltpu.VMEM((DSA_KV_HEADS, DSA_GROUP * tq, DSA_HEAD_DIM), F32),
        ],
        compiler_params=pltpu.CompilerParams(
            dimension_semantics=("parallel", "arbitrary"), vmem_limit_bytes=VMEM_LIMIT_BYTES),
        name="dsa",
    )(pb, pf, pb, kit, kt, v)


V_ROWS = 80
V_ONES_ROW = DSA_HEAD_DIM


def _dsa_t_kernel(iq_ref, sm_ref, qd_ref, ki_ref, k_ref, vt_ref, o_ref,
                  key_scr, m_scr, acc_scr, s_scr, hi_scr, lo_scr, *, tq, tk, n_keys, topk):
    qb = pl.program_id(1)
    q0 = qb * tq
    q_chunk_max = (q0 + tq - 1 - N_META) >> 6
    limit = jnp.minimum(N_META + CHUNK * (q_chunk_max + 1), n_keys)
    n_kb = (limit + tk - 1) // tk
    n_full = jnp.minimum(jnp.minimum((q0 + N_META) // tk, n_keys // tk), n_kb)

    q_chunk = (q0 + lax.broadcasted_iota(I32, (1, tq), 1) - N_META) >> 6
    row_iota = lax.broadcasted_iota(I32, (tk, tq), 0)

    iq = iq_ref[...]
    iq_hm = jnp.concatenate([iq[:, h * IDX_DIM:(h + 1) * IDX_DIM] for h in range(IDX_HEADS)], axis=0)
    w_t = sm_ref[...].T[SM_IW:SM_IW + IDX_HEADS, :] * IDX_W_SCALE

    def score_block(kb, masked):
        k0 = pl.multiple_of(kb * tk, tk)
        r = _dot_nt(ki_ref[0, pl.ds(k0, tk), :], iq_hm)
        sc = None
        for h in range(IDX_HEADS):
            t = jnp.maximum(r[:, h * tq:(h + 1) * tq], 0.0) * w_t[h:h + 1, :]
            sc = t if sc is None else sc + t
        if masked:
            key_pos = k0 + row_iota
            adm = (key_pos < n_keys) & (((key_pos - N_META) >> 6) <= q_chunk)
            sc = jnp.where(adm, sc, -jnp.inf)
        bits = pltpu.bitcast(sc, I32)
        keys = bits ^ ((bits >> 31) & 0x7FFFFFFF)
        key_scr[pl.ds(k0, tk), :] = keys
        hi_scr[pl.ds(k0, tk), :] = (keys >> 16).astype(jnp.int16)
        lo_scr[pl.ds(k0, tk), :] = ((keys & 0xFFFF) - 2 ** 15).astype(jnp.int16)

    def score_full(kb, carry):
        score_block(kb, False)
        return carry

    def score_edge(kb, carry):
        score_block(kb, True)
        return carry

    lax.fori_loop(0, n_full, score_full, 0)
    lax.fori_loop(n_full, n_kb, score_edge, 0)

    n_chunks = n_kb * (tk // LANES)

    sub_per_blk = tk // LANES

    def count_rows(pred):
        def body(kb, acc):
            for u in range(sub_per_blk):
                c0 = pl.multiple_of(kb * tk + u * LANES, LANES)
                acc = acc + jnp.where(pred(key_scr[pl.ds(c0, LANES), :], c0), 1, 0)
            return acc

        acc = lax.fori_loop(0, n_kb, body, jnp.zeros((LANES, tq), I32))
        return jnp.sum(acc, axis=0, keepdims=True)

    def count_ge(cand):
        return count_rows(lambda kv, c0: kv >= cand)

    half_rows = 2 * LANES
    halves_per_blk = tk // half_rows
    I16_MIN = -(2 ** 15)

    def count16(ref, pred):
        def body(kb, acc):
            for u in range(halves_per_blk):
                c0 = pl.multiple_of(kb * tk + u * half_rows, half_rows)
                acc = acc + jnp.where(pred(ref[pl.ds(c0, half_rows), :]), jnp.int16(1), jnp.int16(0))
            return acc

        acc = lax.fori_loop(0, n_kb, body, jnp.zeros((half_rows, tq), jnp.int16))
        return jnp.sum(acc.astype(I32), axis=0, keepdims=True)

    bits_per_round = 4

    def half_search(ref, want, lo0, cnt0, settled):
        def cond(carry):
            i, lo, cnt = carry
            return (i < 16) & (jnp.max(jnp.where(settled | (cnt == want), 0, 1)) > 0)

        def rnd(carry):
            i, lo, cnt = carry
            for u in range(bits_per_round):
                cand = lo + jnp.left_shift(jnp.int32(1), 15 - u - i)
                cand16 = cand.astype(jnp.int16)
                c = count16(ref, lambda kv: kv >= cand16)
                take = c >= want
                lo = jnp.where(take, cand, lo)
                cnt = jnp.where(take, c, cnt)
            return i + bits_per_round, lo, cnt

        _, lo, cnt = lax.while_loop(cond, rnd, (jnp.int32(0), lo0, cnt0))
        return lo, cnt

    never = jnp.zeros((1, tq), jnp.bool_)
    lo_init = jnp.full((1, tq), I16_MIN, I32)
    n_rows = jnp.zeros((1, tq), I32) + n_chunks * LANES
    want_hi = jnp.full((1, tq), topk, I32)
    p_hi, c_ge_hi = half_search(hi_scr, want_hi, lo_init, n_rows, never)
    p_hi16 = p_hi.astype(jnp.int16)
    c_above = count16(hi_scr, lambda kv: kv > p_hi16)

    def keep_bucket(kb, carry):
        for u in range(halves_per_blk):
            c0 = pl.multiple_of(kb * tk + u * half_rows, half_rows)
            rows = pl.ds(c0, half_rows)
            lo_scr[rows, :] = jnp.where(hi_scr[rows, :] == p_hi16, lo_scr[rows, :], jnp.int16(I16_MIN))
        return carry

    lax.fori_loop(0, n_kb, keep_bucket, 0)
    p_lo, c_in = half_search(lo_scr, topk - c_above, lo_init, c_ge_hi - c_above, c_ge_hi == topk)
    thr = p_hi * 65536 + (p_lo - I16_MIN)
    cge = c_above + c_in
    need_tie = (cge > topk) & (thr > NEG_INF_KEY)
    thr = jnp.maximum(thr, NEG_INF_KEY + 1)

    @pl.when(jnp.max(need_tie.astype(I32)) > 0)
    def _():
        want = topk - count_ge(thr + 1)
        sub_iota = lax.broadcasted_iota(I32, (LANES, tq), 0)
        n_bits = int(math.ceil(math.log2(key_scr.shape[0] + 1)))

        def idx_step(i, m):
            cand = m + jnp.left_shift(jnp.int32(1), n_bits - 1 - i)
            tied_below = count_rows(lambda kv, c0: (kv == thr) & ((c0 + sub_iota) < cand))
            return jnp.where(tied_below <= want, cand, m)

        m_keep = lax.fori_loop(0, n_bits, idx_step, jnp.zeros((1, tq), I32))
        m_keep = jnp.where(need_tie, m_keep, jnp.int32(2 ** 30))

        def demote(c, carry):
            c0 = pl.multiple_of(c * LANES, LANES)
            kv = key_scr[pl.ds(c0, LANES), :]
            drop = (kv == thr) & ((c0 + sub_iota) >= m_keep)
            key_scr[pl.ds(c0, LANES), :] = jnp.where(drop, kv - 1, kv)
            return carry

        lax.fori_loop(0, n_chunks, demote, 0)

    qd = qd_ref[...]
    win = lax.broadcasted_iota(I32, (tq, DSA_KV), 1) >> 6
    q_pad = []
    for g in range(DSA_KV_HEADS):
        q_pad.append(jnp.concatenate(
            [jnp.where(win == g, qd[:, r * DSA_KV:(r + 1) * DSA_KV].astype(F32), 0.0).astype(BF16)
             for r in range(DSA_GROUP)], axis=0))
    m_scr[...] = jnp.full(m_scr.shape, NEG_BIG, F32)
    acc_scr[...] = jnp.zeros(acc_scr.shape, F32)
    s_scr[0] = _dot_nt(k_ref[0, pl.ds(0, tk), :], q_pad[0])

    def attend_block(kb, carry):
        k0 = pl.multiple_of(kb * tk, tk)
        bias = jnp.where(key_scr[pl.ds(k0, tk), :] >= thr, 0.0, NEG_BIG)
        bias_g = jnp.concatenate([bias] * DSA_GROUP, axis=1)
        kblk = k_ref[0, pl.ds(k0, tk), :]
        k1 = pl.multiple_of(jnp.minimum(kb + 1, n_kb - 1) * tk, tk)
        for g in range(DSA_KV_HEADS):
            if g + 1 < DSA_KV_HEADS:
                s_scr[(g + 1) % 2] = _dot_nt(kblk, q_pad[g + 1])
            else:
                s_scr[0] = _dot_nt(k_ref[0, pl.ds(k1, tk), :], q_pad[0])
            s = s_scr[g % 2] + bias_g
            m_old = m_scr[g]
            m_new = jnp.maximum(m_old, jnp.max(s, axis=0, keepdims=True))
            alpha = jnp.exp2(m_old[0:1] - m_new[0:1])
            p = jnp.exp2(s - m_new[0:1]).astype(BF16)
            acc_scr[g] = acc_scr[g] * alpha + _dot(vt_ref[0, g, :, pl.ds(k0, tk)], p)
            m_scr[g] = m_new
        return carry

    lax.fori_loop(0, n_kb, attend_block, 0)

    for g in range(DSA_KV_HEADS):
        acc = acc_scr[g]
        og = acc[0:DSA_HEAD_DIM] / acc[V_ONES_ROW:V_ONES_ROW + 1]
        for r in range(DSA_GROUP):
            c0 = (g * DSA_GROUP + r) * DSA_HEAD_DIM
            o_ref[:, c0:c0 + DSA_HEAD_DIM] = og[:, r * tq:(r + 1) * tq].T.astype(BF16)


def _dsa_t(pf, pb, ki, k, vt, *, n_seq, rows_per_seq, tk, n_keys, topk):
    tq = LANES
    nk_pad = k.shape[1]
    steps = rows_per_seq // tq

    def rmap(c):
        return lambda s, t: (s * steps + t, c)

    kern = functools.partial(_dsa_t_kernel, tq=tq, tk=tk, n_keys=n_keys, topk=topk)
    single = pl.Buffered(1)
    return pl.pallas_call(
        kern,
        grid=(n_seq, steps),
        in_specs=[
            pl.BlockSpec((tq, IDX_Q), rmap(B_IQ // IDX_Q)),
            pl.BlockSpec((tq, LANES), rmap(F_SMALL // LANES)),
            pl.BlockSpec((tq, DSA_Q), rmap(B_DQ // DSA_Q)),
            pl.BlockSpec((1, nk_pad, IDX_DIM), lambda s, t: (s, 0, 0), pipeline_mode=single),
            pl.BlockSpec((1, nk_pad, DSA_KV), lambda s, t: (s, 0, 0), pipeline_mode=single),
            pl.BlockSpec((1, DSA_KV_HEADS, V_ROWS, nk_pad), lambda s, t: (s, 0, 0, 0), pipeline_mode=single),
        ],
        out_specs=pl.BlockSpec((tq, DSA_Q), lambda s, t: (s * steps + t, 0)),
        out_shape=jax.ShapeDtypeStruct((n_seq * rows_per_seq, DSA_Q), BF16),
        scratch_shapes=[
            pltpu.VMEM((nk_pad, tq), I32),
            pltpu.VMEM((DSA_KV_HEADS, 8, DSA_GROUP * tq), F32),
            pltpu.VMEM((DSA_KV_HEADS, V_ROWS, DSA_GROUP * tq), F32),
            pltpu.VMEM((2, tk, DSA_GROUP * tq), F32),
            pltpu.VMEM((nk_pad, tq), jnp.int16),
            pltpu.VMEM((nk_pad, tq), jnp.int16),
        ],
        compiler_params=pltpu.CompilerParams(
            dimension_semantics=("parallel", "arbitrary"), vmem_limit_bytes=VMEM_LIMIT_BYTES),
        name="dsa_t",
    )(pb, pf, pb, ki, k, vt)


def _out_kernel(x_ref, ya_ref, ob_ref, z_ref, ma_ref, mb_ref, wg_ref, wd_ref, wo_ref,
                gate_ref, lig_ref, lib_ref, lg_ref, lb_ref, o_ref, *, alpha):
    hn = _layer_norm_rows(x_ref[...], lig_ref[...], lib_ref[...])
    y_a = _dot(ya_ref[...], wg_ref[...])
    z = z_ref[...].astype(F32)
    yb_in = ob_ref[...].astype(F32) * (z * jax.nn.sigmoid(z))
    y_b = _dot(yb_in.astype(BF16), wd_ref[...])
    ga = jax.nn.sigmoid(ma_ref[...].astype(F32) + gate_ref[0:1, :])
    gb = jax.nn.sigmoid(mb_ref[...].astype(F32) + gate_ref[1:2, :])
    merged = ga * y_a + gb * y_b
    y = alpha * hn + _dot(merged.astype(BF16), wo_ref[...])
    o_ref[...] = _layer_norm_rows(y, lg_ref[...], lb_ref[...])


def _out(x, ya, ob, pb, wg, wd, wo, gate, lig, lib, lg, lb, tm, alpha):
    rows = x.shape[0]
    row = lambda c: pl.BlockSpec((tm, D_MODEL), lambda i, c=c: (i, c))
    full = lambda a: pl.BlockSpec(a.shape, lambda i: (0, 0))
    return pl.pallas_call(
        functools.partial(_out_kernel, alpha=alpha),
        grid=(rows // tm,),
        in_specs=[row(0), row(0), row(0), row(B_DZ // D_MODEL), row(B_MA // D_MODEL), row(B_MB // D_MODEL),
                  full(wg), full(wd), full(wo), full(gate), full(lig), full(lib), full(lg), full(lb)],
        out_specs=pl.BlockSpec((tm, D_MODEL), lambda i: (i, 0)),
        out_shape=jax.ShapeDtypeStruct((rows, D_MODEL), F32),
        compiler_params=pltpu.CompilerParams(
            dimension_semantics=("parallel",), vmem_limit_bytes=VMEM_LIMIT_BYTES),
        name="out_proj",
    )(x, ya, ob, pb, pb, pb, wg, wd, wo, gate, lig, lib, lg, lb)


def _round_up(a, b):
    return -(-a // b) * b


def _forward(x_prompt, x_sample, cache_k, cache_v, cache_idx_k, state_gla, meta, ln_in_g, ln_in_b,
             w_in, gla_w2, gla_gate_b, gla_norm_g, idx_kn_g, idx_kn_b, w_gla, w_dsa, gate_b,
             w_out, ln_g, ln_b):
    depth = w_in.shape[0]
    assert depth == 1, "single-layer trunk"
    bsz, seq, _ = x_prompt.shape
    dbsz, dseq, _ = x_sample.shape
    past = cache_k.shape[2]
    t_p = N_META + seq
    tq_p = 128
    tp = _round_up(t_p, KEY_TILE)
    assert tp % tq_p == 0 and tp % GLA_BLOCK == 0
    row_s = bsz * tp
    rows = _round_up(row_s + dbsz * dseq, ROW_TILE)
    alpha = (2.0 * depth) ** 0.25

    meta_rows = jnp.broadcast_to(meta.astype(F32)[None], (bsz, N_META, D_MODEL))
    xp = jnp.concatenate([meta_rows, x_prompt, jnp.zeros((bsz, tp - t_p, D_MODEL), F32)], axis=1)
    x_all = jnp.concatenate([xp.reshape(bsz * tp, D_MODEL), x_sample.reshape(dbsz * dseq, D_MODEL),
                             jnp.zeros((rows - row_s - dbsz * dseq, D_MODEL), F32)], axis=0)

    w = w_in[0]
    w_f = jnp.concatenate([_col(w, _GQ), _col(w, _GK), _col(w, _DK), _col(w, _DV), _col(w, _IK),
                           _col(w, _GLOW), _col(w, _IW),
                           jnp.zeros((D_MODEL, LANES - IDX_DIM - GLA_RANK - IDX_HEADS), F32)], axis=1).astype(BF16)
    w_dq = _col(w, _DQ).reshape(D_MODEL, DSA_KV_HEADS, DSA_GROUP, DSA_HEAD_DIM)
    w_dq = (jnp.swapaxes(w_dq, 1, 2) * (DSA_SCALE * math.log2(math.e))).reshape(D_MODEL, DSA_Q)
    w_b = jnp.concatenate([_col(w, _GV), _col(w, _GR), w_dq, _col(w, _DZ), _col(w, _MA), _col(w, _MB),
                           _col(w, _IQ), jnp.zeros((D_MODEL, B_COLS - B_PAD), F32)], axis=1).astype(BF16)
    lig = ln_in_g.reshape(1, D_MODEL)
    lib = ln_in_b.reshape(1, D_MODEL)

    pf = _proj_f32(x_all, lig, lib, w_f, idx_kn_g[0].reshape(1, IDX_DIM), idx_kn_b[0].reshape(1, IDX_DIM),
                   ROW_TILE // 4)
    pb = _proj_bf16(x_all, lig, lib, w_b, ROW_TILE // 2, 1024)

    w2pad = jnp.zeros((LANES, GLA_QK), F32).at[SM_GLOW:SM_GLOW + GLA_RANK].set(gla_w2[0]).astype(BF16)
    gbias = gla_gate_b[0].reshape(1, GLA_QK)
    ng = gla_norm_g[0].reshape(1, GLA_DV)
    blk_p = min(GLA_BLOCK, t_p)
    cm_p = jnp.asarray(_gla_decay_matrix(blk_p), BF16)
    s0_p = jnp.zeros((bsz, GLA_HEADS, GLA_DV, GLA_DK), F32)
    ya_p, st_p = _gla(pf, pb, cm_p, w2pad, gbias, ng, s0_p, n_seq=bsz, row0=0, rows_per_seq=tp,
                      t_valid=t_p, blk=blk_p, n_blk=KEY_TILE // blk_p)
    blk_s = min(GLA_BLOCK, dseq)
    assert dseq % blk_s == 0 and row_s % dseq == 0
    cm_s = jnp.asarray(_gla_decay_matrix(blk_s), BF16)
    s0_s = jnp.swapaxes(state_gla[0], -1, -2)
    ya_s, st_s = _gla(pf, pb, cm_s, w2pad, gbias, ng, s0_s, n_seq=dbsz, row0=row_s, rows_per_seq=dseq,
                      t_valid=dseq, blk=blk_s, n_blk=dseq // blk_s)

    dk = pf[:, F_DK:F_DK + DSA_KV]
    dv = pf[:, F_DV:F_DV + DSA_KV]
    ki = pf[:, F_SMALL:F_SMALL + IDX_DIM]
    dk_p = dk[:row_s].reshape(bsz, tp, DSA_KV_HEADS, DSA_HEAD_DIM)
    dv_p = dv[:row_s].reshape(bsz, tp, DSA_KV_HEADS, DSA_HEAD_DIM)
    ki_p = ki[:row_s].reshape(bsz, tp, IDX_DIM)
    vt_p = jnp.concatenate([jnp.transpose(dv_p, (0, 2, 3, 1)),
                            jnp.ones((bsz, DSA_KV_HEADS, 1, tp), F32),
                            jnp.zeros((bsz, DSA_KV_HEADS, V_ROWS - DSA_HEAD_DIM - 1, tp), F32)], axis=2).astype(BF16)
    topk_p = min(TOPK_MAX, (t_p - N_META) // 4)
    ob_p = _dsa_t(pf, pb, ki_p.astype(BF16), dk[:row_s].reshape(bsz, tp, DSA_KV).astype(BF16), vt_p,
                  n_seq=bsz, rows_per_seq=tp, tk=KEY_TILE, n_keys=t_p, topk=topk_p)

    n_keys_s = past + dseq
    nk_s = _round_up(n_keys_s, KEY_TILE)
    dk_s = dk[row_s:row_s + dbsz * dseq].reshape(dbsz, dseq, DSA_KV_HEADS, DSA_HEAD_DIM)
    dv_s = dv[row_s:row_s + dbsz * dseq].reshape(dbsz, dseq, DSA_KV_HEADS, DSA_HEAD_DIM)
    ki_s = ki[row_s:row_s + dbsz * dseq].reshape(dbsz, dseq, IDX_DIM)
    kpad = jnp.zeros((dbsz, nk_s - n_keys_s, DSA_KV_HEADS, DSA_HEAD_DIM), F32)
    k_all = jnp.concatenate([cache_k[0], dk_s, kpad], axis=1)
    v_all = jnp.concatenate([cache_v[0], dv_s, kpad], axis=1)
    ki_all = jnp.concatenate([cache_idx_k[0], ki_s, jnp.zeros((dbsz, nk_s - n_keys_s, IDX_DIM), F32)], axis=1)
    kt_s = jnp.transpose(k_all, (0, 2, 3, 1)).astype(BF16)
    v_s = jnp.transpose(v_all, (0, 2, 1, 3)).astype(BF16)
    kit_s = jnp.transpose(ki_all, (0, 2, 1)).astype(BF16)
    topk_s = min(TOPK_MAX, n_keys_s // 4)
    ob_s = _dsa(pf, pb, kit_s, kt_s, v_s, n_seq=dbsz, row0=row_s, rows_per_seq=dseq, tq=dseq, tk=KEY_TILE,
                n_keys=n_keys_s, topk=topk_s, causal=False)

    tail = jnp.zeros((rows - row_s - dbsz * dseq, D_MODEL), BF16)
    ya = jnp.concatenate([ya_p, ya_s, tail], axis=0)
    ob = jnp.concatenate([ob_p, ob_s, tail], axis=0)
    y = _out(x_all, ya, ob, pb, w_gla[0].astype(BF16), w_dsa[0].astype(BF16), w_out[0].astype(BF16),
             gate_b[0], lig, lib, ln_g[0].reshape(1, D_MODEL), ln_b[0].reshape(1, D_MODEL),
             ROW_TILE // 4, alpha)

    y_prompt = y[:row_s].reshape(bsz, tp, D_MODEL)[:, N_META:t_p]
    y_sample = y[row_s:row_s + dbsz * dseq].reshape(dbsz, dseq, D_MODEL)
    k_prompt = dk_p[:, :t_p][None]
    v_prompt = dv_p[:, :t_p][None]
    idx_k_prompt = ki_p[:, :t_p][None]
    gla_prompt = jnp.swapaxes(st_p, -1, -2)[None]
    k_sample = dk_s[None]
    v_sample = dv_s[None]
    idx_k_sample = ki_s[None]
    gla_sample = jnp.swapaxes(st_s, -1, -2)[None]
    return (y_prompt, y_sample, k_prompt, v_prompt, idx_k_prompt, gla_prompt,
            k_sample, v_sample, idx_k_sample, gla_sample)


def kernel(x_prompt, x_sample, cache_k, cache_v, cache_idx_k, state_gla, meta, ln_in_g, ln_in_b,
           w_in, gla_w2, gla_gate_b, gla_norm_g, idx_kn_g, idx_kn_b, w_gla, w_dsa, gate_b,
           w_out, ln_g, ln_b):
    return _forward(x_prompt, x_sample, cache_k, cache_v, cache_idx_k, state_gla, meta, ln_in_g, ln_in_b,
                    w_in, gla_w2, gla_gate_b, gla_norm_g, idx_kn_g, idx_kn_b, w_gla, w_dsa, gate_b,
                    w_out, ln_g, ln_b)
```

```python
import functools
import math

import numpy as np
import jax
import jax.numpy as jnp
from jax import lax
from jax.experimental import pallas as pl
from jax.experimental.pallas import tpu as pltpu

F32 = jnp.float32
BF16 = jnp.bfloat16
I32 = jnp.int32

D_MODEL = 1024
CHUNK = 64
N_META = 16
GLA_HEADS = 4
GLA_DK = 128
GLA_DV = 256
GLA_RANK = 16
GLA_TAU = 16.0
GLA_BLOCK = 64
DSA_HEADS = 16
DSA_KV_HEADS = 4
DSA_HEAD_DIM = 64
DSA_GROUP = DSA_HEADS // DSA_KV_HEADS
DSA_SCALE = DSA_HEAD_DIM ** -0.5
IDX_HEADS = 8
IDX_DIM = 64
IDX_W_SCALE = (IDX_HEADS ** -0.5) * (IDX_DIM ** -0.5)
TOPK_MAX = 256
NORM_EPS = 1e-5
GLA_QK = GLA_HEADS * GLA_DK
GLA_V = GLA_HEADS * GLA_DV
DSA_Q = DSA_HEADS * DSA_HEAD_DIM
DSA_KV = DSA_KV_HEADS * DSA_HEAD_DIM
IDX_Q = IDX_HEADS * IDX_DIM
SPLITS = (GLA_QK, GLA_QK, GLA_V, GLA_RANK, GLA_V,
          DSA_Q, DSA_KV, DSA_KV, IDX_Q, IDX_DIM, IDX_HEADS, DSA_Q,
          D_MODEL, D_MODEL)
_OFF = tuple(int(o) for o in np.cumsum((0,) + SPLITS))
(_GQ, _GK, _GV, _GLOW, _GR, _DQ, _DK, _DV, _IQ, _IK, _IW, _DZ, _MA, _MB) = range(14)

LANES = 128
VMEM_LIMIT_BYTES = 56 * 1024 * 1024

F_GQ, F_GK, F_DK, F_DV, F_SMALL = 0, 512, 1024, 1280, 1536
F_COLS = F_SMALL + LANES
SM_IK, SM_GLOW, SM_IW = 0, IDX_DIM, IDX_DIM + GLA_RANK
B_GV, B_GR, B_DQ, B_DZ, B_MA, B_MB, B_IQ, B_PAD = 0, 1024, 2048, 3072, 4096, 5120, 6144, 6656
B_COLS = 7168

ROW_TILE = 2048
KEY_TILE = 512
NEG_BIG = -1e30
INT_MIN = -(2 ** 31)
NEG_INF_KEY = int(np.array(-np.inf, np.float32).view(np.int32)) ^ 0x7FFFFFFF


def _col(w, idx):
    return w[:, _OFF[idx]:_OFF[idx + 1]]


def _layer_norm_rows(x, g, b):
    mu = jnp.mean(x, axis=-1, keepdims=True)
    xc = x - mu
    var = jnp.mean(xc * xc, axis=-1, keepdims=True)
    return xc * lax.rsqrt(var + NORM_EPS) * g + b


def _dot(a, b):
    return jnp.dot(a, b, preferred_element_type=F32)


def _dot_nt(a, b):
    return lax.dot_general(a, b, (((1,), (1,)), ((), ())), preferred_element_type=F32)


def _dot_tn(a, b):
    return lax.dot_general(a, b, (((0,), (0,)), ((), ())), preferred_element_type=F32)


def _proj_f32_kernel(x_ref, g_ref, b_ref, w_ref, kg_ref, kb_ref, o_ref):
    hn = _layer_norm_rows(x_ref[...], g_ref[...], b_ref[...])
    y = _dot(hn.astype(BF16), w_ref[...])
    o_ref[...] = y
    ik = y[:, F_SMALL:F_SMALL + IDX_DIM]
    o_ref[:, F_SMALL:F_SMALL + IDX_DIM] = _layer_norm_rows(ik, kg_ref[...], kb_ref[...])


def _proj_f32(x, g, b, w, kg, kb, tm):
    rows = x.shape[0]
    return pl.pallas_call(
        _proj_f32_kernel,
        grid=(rows // tm,),
        in_specs=[
            pl.BlockSpec((tm, D_MODEL), lambda i: (i, 0)),
            pl.BlockSpec((1, D_MODEL), lambda i: (0, 0)),
            pl.BlockSpec((1, D_MODEL), lambda i: (0, 0)),
            pl.BlockSpec((D_MODEL, F_COLS), lambda i: (0, 0)),
            pl.BlockSpec((1, IDX_DIM), lambda i: (0, 0)),
            pl.BlockSpec((1, IDX_DIM), lambda i: (0, 0)),
        ],
        out_specs=pl.BlockSpec((tm, F_COLS), lambda i: (i, 0)),
        out_shape=jax.ShapeDtypeStruct((rows, F_COLS), F32),
        compiler_params=pltpu.CompilerParams(
            dimension_semantics=("parallel",), vmem_limit_bytes=VMEM_LIMIT_BYTES),
        name="proj_f32",
    )(x, g, b, w, kg, kb)


def _proj_bf16_kernel(x_ref, g_ref, b_ref, w_ref, o_ref, hn_ref):
    @pl.when(pl.program_id(1) == 0)
    def _():
        hn_ref[...] = _layer_norm_rows(x_ref[...], g_ref[...], b_ref[...]).astype(BF16)

    o_ref[...] = _dot(hn_ref[...], w_ref[...]).astype(BF16)


def _proj_bf16(x, g, b, w, tm, tn):
    rows = x.shape[0]
    return pl.pallas_call(
        _proj_bf16_kernel,
        grid=(rows // tm, B_COLS // tn),
        in_specs=[
            pl.BlockSpec((tm, D_MODEL), lambda i, j: (i, 0)),
            pl.BlockSpec((1, D_MODEL), lambda i, j: (0, 0)),
            pl.BlockSpec((1, D_MODEL), lambda i, j: (0, 0)),
            pl.BlockSpec((D_MODEL, tn), lambda i, j: (0, j)),
        ],
        out_specs=pl.BlockSpec((tm, tn), lambda i, j: (i, j)),
        out_shape=jax.ShapeDtypeStruct((rows, B_COLS), BF16),
        scratch_shapes=[pltpu.VMEM((tm, D_MODEL), BF16)],
        compiler_params=pltpu.CompilerParams(
            dimension_semantics=("parallel", "arbitrary"), vmem_limit_bytes=VMEM_LIMIT_BYTES),
        name="proj_bf16",
    )(x, g, b, w)


def _gla_levels(blk):
    levels = []
    s = blk // 2
    while s >= 1:
        levels.append(s)
        s //= 2
    return levels


def _gla_decay_matrix(blk):
    i = np.arange(blk)[:, None]
    t = np.arange(blk)[None, :]
    mats = [(t <= i).astype(np.float32), (t > i).astype(np.float32)]
    for s in _gla_levels(blk):
        mid = (i // (2 * s)) * (2 * s) + s - 1
        lower = (i // s) % 2 == 1
        m = np.where(lower, ((t > mid) & (t <= i)), False).astype(np.float32)
        n = np.where(~lower, ((t > i) & (t <= mid)), False).astype(np.float32)
        mats.append(m + n)
    return np.concatenate(mats, axis=0)


def _gla_kernel(q_ref, k_ref, sm_ref, v_ref, r_ref, c_ref, w2_ref, gb_ref, ng_ref, s0_ref,
                o_ref, st_ref, s_scr, *, blk, n_blk, t_valid):
    ti = pl.program_id(1)
    levels = _gla_levels(blk)

    @pl.when(ti == 0)
    def _():
        s_scr[...] = s0_ref[0]

    ri = lax.broadcasted_iota(I32, (blk, blk), 0)
    ci = lax.broadcasted_iota(I32, (blk, blk), 1)
    eye = ri == ci
    lvl_masks = []
    for s in levels:
        sh = s.bit_length() - 1
        same = (ri >> (sh + 1)) == (ci >> (sh + 1))
        lvl_masks.append(same & (((ri >> sh) & 1) == 1) & (((ci >> sh) & 1) == 0))
    row_iota = lax.broadcasted_iota(I32, (blk, 1), 0)
    cmat = c_ref[...]
    w2 = w2_ref[...]
    gbias = gb_ref[...]
    ng = ng_ref[...]

    def block(j):
        r0 = pl.multiple_of(j * blk, blk)
        rows = pl.ds(r0, blk)
        valid = (ti * (n_blk * blk) + r0 + row_iota) < t_valid
        gq = q_ref[rows, :] * (GLA_DK ** -0.5)
        gk = jnp.where(valid, k_ref[rows, :], 0.0)
        x = _dot(sm_ref[rows, :].astype(BF16), w2) + gbias
        logf = (jnp.minimum(x, 0.0) - jnp.log1p(jnp.exp(-jnp.abs(x)))) * (1.0 / GLA_TAU)
        logf = jnp.where(valid, logf, 0.0)
        hi = logf.astype(BF16)
        r1 = logf - hi.astype(F32)
        mid = r1.astype(BF16)
        lo = (r1 - mid.astype(F32)).astype(BF16)
        e_all = _dot(cmat, hi) + _dot(cmat, mid) + _dot(cmat, lo)
        for h in range(GLA_HEADS):
            ksl = slice(h * GLA_DK, (h + 1) * GLA_DK)
            vsl = slice(h * GLA_DV, (h + 1) * GLA_DV)
            qh = gq[:, ksl]
            kh = gk[:, ksl]
            vh = v_ref[rows, vsl]
            b_h = e_all[0:blk, ksl]
            rev_h = e_all[blk:2 * blk, ksl]
            st = s_scr[h]
            o = _dot_nt((qh * jnp.exp(b_h)).astype(BF16), st.astype(BF16))
            a = jnp.where(eye, _dot_nt(qh.astype(BF16), kh.astype(BF16)), 0.0)
            for li in range(len(levels)):
                e = jnp.exp(e_all[(2 + li) * blk:(3 + li) * blk, ksl])
                p = _dot_nt((qh * e).astype(BF16), (kh * e).astype(BF16))
                a = a + jnp.where(lvl_masks[li], p, 0.0)
            o = o + _dot(a.astype(BF16), vh)
            kd = (kh * jnp.exp(rev_h)).astype(BF16)
            dec = jnp.exp(b_h[blk - 1:blk, :])
            s_scr[h] = st * dec + _dot_tn(vh, kd)
            on = o * lax.rsqrt(jnp.mean(o * o, axis=-1, keepdims=True) + NORM_EPS) * ng
            gr = r_ref[rows, vsl].astype(F32)
            o_ref[rows, vsl] = (on * (gr * jax.nn.sigmoid(gr))).astype(BF16)

    n_live = jnp.clip((t_valid - ti * (n_blk * blk) + blk - 1) // blk, 0, n_blk)

    @pl.when(n_live < n_blk)
    def _():
        o_ref[...] = jnp.zeros(o_ref.shape, o_ref.dtype)

    def block_pair(j, carry):
        block(2 * j)
        block(2 * j + 1)
        return carry

    lax.fori_loop(0, n_live // 2, block_pair, 0)

    @pl.when(n_live % 2 == 1)
    def _():
        block(n_live - 1)

    @pl.when(ti == pl.num_programs(1) - 1)
    def _():
        st_ref[0] = s_scr[...]


def _gla(pf, pb, cmat, w2pad, gbias, ng, s0t, *, n_seq, row0, rows_per_seq, t_valid, blk, n_blk):
    rb = blk * n_blk
    steps = rows_per_seq // rb
    base = row0 // rb

    def rmap(c):
        return lambda s, t: (base + s * steps + t, c)

    kern = functools.partial(_gla_kernel, blk=blk, n_blk=n_blk, t_valid=t_valid)
    return pl.pallas_call(
        kern,
        grid=(n_seq, steps),
        in_specs=[
            pl.BlockSpec((rb, GLA_QK), rmap(F_GQ // GLA_QK)),
            pl.BlockSpec((rb, GLA_QK), rmap(F_GK // GLA_QK)),
            pl.BlockSpec((rb, LANES), rmap(F_SMALL // LANES)),
            pl.BlockSpec((rb, GLA_V), rmap(B_GV // GLA_V)),
            pl.BlockSpec((rb, GLA_V), rmap(B_GR // GLA_V)),
            pl.BlockSpec(cmat.shape, lambda s, t: (0, 0)),
            pl.BlockSpec(w2pad.shape, lambda s, t: (0, 0)),
            pl.BlockSpec((1, GLA_QK), lambda s, t: (0, 0)),
            pl.BlockSpec((1, GLA_DV), lambda s, t: (0, 0)),
            pl.BlockSpec((1, GLA_HEADS, GLA_DV, GLA_DK), lambda s, t: (s, 0, 0, 0)),
        ],
        out_specs=[
            pl.BlockSpec((rb, GLA_V), lambda s, t: (s * steps + t, 0)),
            pl.BlockSpec((1, GLA_HEADS, GLA_DV, GLA_DK), lambda s, t: (s, 0, 0, 0)),
        ],
        out_shape=[
            jax.ShapeDtypeStruct((n_seq * rows_per_seq, GLA_V), BF16),
            jax.ShapeDtypeStruct((n_seq, GLA_HEADS, GLA_DV, GLA_DK), F32),
        ],
        scratch_shapes=[pltpu.VMEM((GLA_HEADS, GLA_DV, GLA_DK), F32)],
        compiler_params=pltpu.CompilerParams(
            dimension_semantics=("parallel", "arbitrary"), vmem_limit_bytes=VMEM_LIMIT_BYTES),
        name="gla",
    )(pf, pf, pf, pb, pb, cmat, w2pad, gbias, ng, s0t)


def _dsa_kernel(iq_ref, sm_ref, qd_ref, kit_ref, kt_ref, v_ref, o_ref,
                key_scr, m_scr, l_scr, acc_scr, *, tq, tk, n_keys, topk, causal, n_kb_total):
    qb = pl.program_id(1)
    n_rows_g = DSA_GROUP * tq

    if causal:
        q_chunk_max = (qb * tq + tq - 1 - N_META) >> 6
        limit = jnp.minimum(N_META + CHUNK * (q_chunk_max + 1), n_keys)
        n_kb = (limit + tk - 1) // tk
    else:
        n_kb = n_kb_total

    q_pos = qb * tq + lax.broadcasted_iota(I32, (tq, 1), 0)
    q_chunk = (q_pos - N_META) >> 6
    lane_iota = lax.broadcasted_iota(I32, (1, tk), 1)

    iq = iq_ref[...]
    iq_hm = jnp.concatenate([iq[:, h * IDX_DIM:(h + 1) * IDX_DIM] for h in range(IDX_HEADS)], axis=0)
    w_i = sm_ref[:, SM_IW:SM_IW + IDX_HEADS] * IDX_W_SCALE
    w_cols = [jnp.broadcast_to(w_i[:, h:h + 1], (tq, LANES)) for h in range(IDX_HEADS)]

    def score_block(kb, carry):
        k0 = pl.multiple_of(kb * tk, tk)
        r = _dot(iq_hm, kit_ref[0, :, pl.ds(k0, tk)])
        sc = None
        for h in range(IDX_HEADS):
            rh = jnp.maximum(r[h * tq:(h + 1) * tq, :], 0.0)
            wh = jnp.concatenate([w_cols[h]] * (tk // LANES), axis=1)
            sc = rh * wh if sc is None else sc + rh * wh
        key_pos = k0 + lane_iota
        adm = key_pos < n_keys
        if causal:
            adm = adm & (((key_pos - N_META) >> 6) <= q_chunk)
        sc = jnp.where(adm, sc, -jnp.inf)
        bits = pltpu.bitcast(sc, I32)
        key_scr[:, pl.ds(k0, tk)] = bits ^ ((bits >> 31) & 0x7FFFFFFF)
        return carry

    lax.fori_loop(0, n_kb, score_block, 0)

    n_cols = n_kb * (tk // LANES)

    def count_ge(cand):
        cb = jnp.broadcast_to(cand, (tq, LANES))

        def body(c, acc):
            c0 = pl.multiple_of(c * LANES, LANES)
            return acc + jnp.where(key_scr[:, pl.ds(c0, LANES)] >= cb, 1, 0)

        acc = lax.fori_loop(0, n_cols, body, jnp.zeros((tq, LANES), I32))
        return jnp.sum(acc, axis=1, keepdims=True)

    def bit_step(i, carry):
        lo, cge = carry
        cand = lo + jnp.left_shift(jnp.int32(1), 31 - i)
        cnt = count_ge(cand)
        take = cnt >= topk
        return jnp.where(take, cand, lo), jnp.where(take, cnt, cge)

    lo0 = jnp.full((tq, 1), INT_MIN, I32)
    cge0 = jnp.zeros((tq, 1), I32) + n_cols * LANES
    thr, cge = lax.fori_loop(0, 32, bit_step, (lo0, cge0))
    need_tie = (cge > topk) & (thr > NEG_INF_KEY)
    thr = jnp.maximum(thr, NEG_INF_KEY + 1)

    @pl.when(jnp.max(need_tie.astype(I32)) > 0)
    def _():
        want = topk - count_ge(thr + 1)
        thr_b = jnp.broadcast_to(thr, (tq, LANES))
        col_iota = lax.broadcasted_iota(I32, (tq, LANES), 1)

        def count_tied_below(m):
            mb = jnp.broadcast_to(m, (tq, LANES))

            def body(c, acc):
                c0 = pl.multiple_of(c * LANES, LANES)
                hit = (key_scr[:, pl.ds(c0, LANES)] == thr_b) & ((c0 + col_iota) < mb)
                return acc + jnp.where(hit, 1, 0)

            acc = lax.fori_loop(0, n_cols, body, jnp.zeros((tq, LANES), I32))
            return jnp.sum(acc, axis=1, keepdims=True)

        n_bits = int(math.ceil(math.log2(n_kb_total * tk + 1)))

        def idx_step(i, m):
            cand = m + jnp.left_shift(jnp.int32(1), n_bits - 1 - i)
            return jnp.where(count_tied_below(cand) <= want, cand, m)

        m_keep = lax.fori_loop(0, n_bits, idx_step, jnp.zeros((tq, 1), I32))
        mk_b = jnp.broadcast_to(jnp.where(need_tie, m_keep, jnp.int32(2 ** 30)), (tq, LANES))

        def demote(c, carry):
            c0 = pl.multiple_of(c * LANES, LANES)
            kv = key_scr[:, pl.ds(c0, LANES)]
            drop = (kv == thr_b) & ((c0 + col_iota) >= mk_b)
            key_scr[:, pl.ds(c0, LANES)] = jnp.where(drop, kv - 1, kv)
            return carry

        lax.fori_loop(0, n_cols, demote, 0)

    qd = qd_ref[...]
    q_groups = []
    for g in range(DSA_KV_HEADS):
        q_groups.append(jnp.concatenate(
            [qd[:, (r * DSA_KV_HEADS + g) * DSA_HEAD_DIM:(r * DSA_KV_HEADS + g + 1) * DSA_HEAD_DIM]
             for r in range(DSA_GROUP)], axis=0))
    m_scr[...] = jnp.full(m_scr.shape, NEG_BIG, F32)
    l_scr[...] = jnp.zeros(l_scr.shape, F32)
    acc_scr[...] = jnp.zeros(acc_scr.shape, F32)
    thr_t = jnp.broadcast_to(thr, (tq, LANES))

    def attend_block(kb, carry):
        k0 = pl.multiple_of(kb * tk, tk)
        sel = key_scr[:, pl.ds(k0, tk)] >= jnp.concatenate([thr_t] * (tk // LANES), axis=1)
        bias = jnp.where(sel, 0.0, NEG_BIG)
        bias_g = jnp.concatenate([bias] * DSA_GROUP, axis=0)
        for g in range(DSA_KV_HEADS):
            s = _dot(q_groups[g], kt_ref[0, g, :, pl.ds(k0, tk)]) + bias_g
            m_old = m_scr[g]
            m_new = jnp.maximum(m_old, jnp.max(s, axis=1, keepdims=True))
            alpha = jnp.exp2(m_old - m_new)
            p = jnp.exp2(s - m_new)
            p = jnp.where(bias_g < 0.0, 0.0, p)
            l_scr[g] = alpha * l_scr[g] + jnp.sum(p, axis=1, keepdims=True)
            acc_scr[g] = alpha * acc_scr[g] + _dot(p.astype(BF16), v_ref[0, g, pl.ds(k0, tk), :])
            m_scr[g] = m_new
        return carry

    lax.fori_loop(0, n_kb, attend_block, 0)

    for g in range(DSA_KV_HEADS):
        og = acc_scr[g] / l_scr[g]
        for r in range(DSA_GROUP):
            c0 = (g * DSA_GROUP + r) * DSA_HEAD_DIM
            o_ref[:, c0:c0 + DSA_HEAD_DIM] = og[r * tq:(r + 1) * tq, :].astype(BF16)


def _dsa(pf, pb, kit, kt, v, *, n_seq, row0, rows_per_seq, tq, tk, n_keys, topk, causal):
    nk_pad = kit.shape[-1]
    n_kb_total = nk_pad // tk
    steps = rows_per_seq // tq
    base = row0 // tq

    def rmap(c):
        return lambda s, t: (base + s * steps + t, c)

    kern = functools.partial(_dsa_kernel, tq=tq, tk=tk, n_keys=n_keys, topk=topk, causal=causal,
                             n_kb_total=n_kb_total)
    single = pl.Buffered(1)
    return pl.pallas_call(
        kern,
        grid=(n_seq, steps),
        in_specs=[
            pl.BlockSpec((tq, IDX_Q), rmap(B_IQ // IDX_Q)),
            pl.BlockSpec((tq, LANES), rmap(F_SMALL // LANES)),
            pl.BlockSpec((tq, DSA_Q), rmap(B_DQ // DSA_Q)),
            pl.BlockSpec((1, IDX_DIM, nk_pad), lambda s, t: (s, 0, 0), pipeline_mode=single),
            pl.BlockSpec((1, DSA_KV_HEADS, DSA_HEAD_DIM, nk_pad), lambda s, t: (s, 0, 0, 0),
                         pipeline_mode=single),
            pl.BlockSpec((1, DSA_KV_HEADS, nk_pad, DSA_HEAD_DIM), lambda s, t: (s, 0, 0, 0),
                         pipeline_mode=single),
        ],
        out_specs=pl.BlockSpec((tq, DSA_Q), lambda s, t: (s * steps + t, 0)),
        out_shape=jax.ShapeDtypeStruct((n_seq * rows_per_seq, DSA_Q), BF16),
        scratch_shapes=[
            pltpu.VMEM((tq, nk_pad), I32),
            pltpu.VMEM((DSA_KV_HEADS, DSA_GROUP * tq, 1), F32),
            pltpu.VMEM((DSA_KV_HEADS, DSA_GROUP * tq, 1), F32),
            pltpu.VMEM((DSA_KV_HEADS, DSA_GROUP * tq, DSA_HEAD_DIM), F32),
        ],
        compiler_params=pltpu.CompilerParams(
            dimension_semantics=("parallel", "arbitrary"), vmem_limit_bytes=VMEM_LIMIT_BYTES),
        name="dsa",
    )(pb, pf, pb, kit, kt, v)


V_ROWS = 80
V_ONES_ROW = DSA_HEAD_DIM


def _dsa_t_kernel(iq_ref, sm_ref, qd_ref, ki_ref, k_ref, vt_ref, o_ref,
                  key_scr, m_scr, acc_scr, s_scr, *, tq, tk, n_keys, topk):
    live = pl.program_id(1) * tq < n_keys

    @pl.when(live)
    def _():
        _dsa_t_body(iq_ref, sm_ref, qd_ref, ki_ref, k_ref, vt_ref, o_ref, key_scr, m_scr, acc_scr, s_scr,
                    tq=tq, tk=tk, n_keys=n_keys, topk=topk)

    @pl.when(jnp.logical_not(live))
    def _():
        o_ref[...] = jnp.zeros(o_ref.shape, o_ref.dtype)


def _dsa_t_body(iq_ref, sm_ref, qd_ref, ki_ref, k_ref, vt_ref, o_ref,
                key_scr, m_scr, acc_scr, s_scr, *, tq, tk, n_keys, topk):
    qb = pl.program_id(1)
    q0 = qb * tq
    q_chunk_max = (q0 + tq - 1 - N_META) >> 6
    limit = jnp.minimum(N_META + CHUNK * (q_chunk_max + 1), n_keys)
    n_kb = (limit + tk - 1) // tk
    n_full = jnp.minimum(jnp.minimum((q0 + N_META) // tk, n_keys // tk), n_kb)

    q_chunk = (q0 + lax.broadcasted_iota(I32, (1, tq), 1) - N_META) >> 6
    row_iota = lax.broadcasted_iota(I32, (tk, tq), 0)

    iq = iq_ref[...]
    iq_hm = jnp.concatenate([iq[:, h * IDX_DIM:(h + 1) * IDX_DIM] for h in range(IDX_HEADS)], axis=0)
    w_t = sm_ref[...].T[SM_IW:SM_IW + IDX_HEADS, :] * IDX_W_SCALE

    def score_block(kb, masked):
        k0 = pl.multiple_of(kb * tk, tk)
        r = _dot_nt(ki_ref[0, pl.ds(k0, tk), :], iq_hm)
        sc = None
        for h in range(IDX_HEADS):
            t = jnp.maximum(r[:, h * tq:(h + 1) * tq], 0.0) * w_t[h:h + 1, :]
            sc = t if sc is None else sc + t
        if masked:
            key_pos = k0 + row_iota
            adm = (key_pos < n_keys) & (((key_pos - N_META) >> 6) <= q_chunk)
            sc = jnp.where(adm, sc, -jnp.inf)
        bits = pltpu.bitcast(sc, I32)
        key_scr[pl.ds(k0, tk), :] = bits ^ ((bits >> 31) & 0x7FFFFFFF)

    def score_full(kb, carry):
        score_block(kb, False)
        return carry

    def score_edge(kb, carry):
        score_block(kb, True)
        return carry

    lax.fori_loop(0, n_full, score_full, 0)
    lax.fori_loop(n_full, n_kb, score_edge, 0)

    n_chunks = n_kb * (tk // LANES)

    sub_per_blk = tk // LANES

    def count_rows(pred):
        def body(kb, acc):
            for u in range(sub_per_blk):
                c0 = pl.multiple_of(kb * tk + u * LANES, LANES)
                acc = acc + jnp.where(pred(key_scr[pl.ds(c0, LANES), :], c0), 1, 0)
            return acc

        acc = lax.fori_loop(0, n_kb, body, jnp.zeros((LANES, tq), I32))
        return jnp.sum(acc, axis=0, keepdims=True)

    def count_ge(cand):
        return count_rows(lambda kv, c0: kv >= cand)

    bits_per_round = 4

    def search_cond(carry):
        i, lo, cge = carry
        return (i < 32) & (jnp.max(jnp.abs(cge - topk)) > 0)

    def search_round(carry):
        i, lo, cge = carry
        for u in range(bits_per_round):
            cand = lo + jnp.left_shift(jnp.int32(1), 31 - u - i)
            cnt = count_ge(cand)
            take = cnt >= topk
            lo = jnp.where(take, cand, lo)
            cge = jnp.where(take, cnt, cge)
        return i + bits_per_round, lo, cge

    lo0 = jnp.full((1, tq), INT_MIN, I32)
    cge0 = jnp.zeros((1, tq), I32) + n_chunks * LANES
    _, thr, cge = lax.while_loop(search_cond, search_round, (jnp.int32(0), lo0, cge0))
    need_tie = (cge > topk) & (thr > NEG_INF_KEY)
    thr = jnp.maximum(thr, NEG_INF_KEY + 1)

    @pl.when(jnp.max(need_tie.astype(I32)) > 0)
    def _():
        want = topk - count_ge(thr + 1)
        sub_iota = lax.broadcasted_iota(I32, (LANES, tq), 0)
        n_bits = int(math.ceil(math.log2(key_scr.shape[0] + 1)))

        def idx_step(i, m):
            cand = m + jnp.left_shift(jnp.int32(1), n_bits - 1 - i)
            tied_below = count_rows(lambda kv, c0: (kv == thr) & ((c0 + sub_iota) < cand))
            return jnp.where(tied_below <= want, cand, m)

        m_keep = lax.fori_loop(0, n_bits, idx_step, jnp.zeros((1, tq), I32))
        m_keep = jnp.where(need_tie, m_keep, jnp.int32(2 ** 30))

        def demote(c, carry):
            c0 = pl.multiple_of(c * LANES, LANES)
            kv = key_scr[pl.ds(c0, LANES), :]
            drop = (kv == thr) & ((c0 + sub_iota) >= m_keep)
            key_scr[pl.ds(c0, LANES), :] = jnp.where(drop, kv - 1, kv)
            return carry

        lax.fori_loop(0, n_chunks, demote, 0)

    qd = qd_ref[...]
    win = lax.broadcasted_iota(I32, (tq, DSA_KV), 1) >> 6
    q_pad = []
    for g in range(DSA_KV_HEADS):
        q_pad.append(jnp.concatenate(
            [jnp.where(win == g, qd[:, r * DSA_KV:(r + 1) * DSA_KV].astype(F32), 0.0).astype(BF16)
             for r in range(DSA_GROUP)], axis=0))
    m_scr[...] = jnp.full(m_scr.shape, NEG_BIG, F32)
    acc_scr[...] = jnp.zeros(acc_scr.shape, F32)
    s_scr[0] = _dot_nt(k_ref[0, pl.ds(0, tk), :], q_pad[0])

    def attend_block(kb):
        k0 = pl.multiple_of(kb * tk, tk)
        bias = jnp.where(key_scr[pl.ds(k0, tk), :] >= thr, 0.0, NEG_BIG)
        bias_g = jnp.concatenate([bias] * DSA_GROUP, axis=1)
        kblk = k_ref[0, pl.ds(k0, tk), :]
        k1 = pl.multiple_of(jnp.minimum(kb + 1, n_kb - 1) * tk, tk)
        for g in range(DSA_KV_HEADS):
            if g + 1 < DSA_KV_HEADS:
                s_scr[(g + 1) % 2] = _dot_nt(kblk, q_pad[g + 1])
            else:
                s_scr[0] = _dot_nt(k_ref[0, pl.ds(k1, tk), :], q_pad[0])
            s = s_scr[g % 2] + bias_g
            m_old = m_scr[g]
            m_new = jnp.maximum(m_old, jnp.max(s, axis=0, keepdims=True))
            alpha = jnp.exp2(m_old[0:1] - m_new[0:1])
            p = jnp.exp2(s - m_new[0:1]).astype(BF16)
            acc_scr[g] = acc_scr[g] * alpha + _dot(vt_ref[0, g, :, pl.ds(k0, tk)], p)
            m_scr[g] = m_new

    def attend_pair(j, carry):
        attend_block(2 * j)
        attend_block(2 * j + 1)
        return carry

    lax.fori_loop(0, n_kb // 2, attend_pair, 0)

    @pl.when(n_kb % 2 == 1)
    def _():
        attend_block(n_kb - 1)

    for g in range(DSA_KV_HEADS):
        acc = acc_scr[g]
        og = acc[0:DSA_HEAD_DIM] / acc[V_ONES_ROW:V_ONES_ROW + 1]
        for r in range(DSA_GROUP):
            c0 = (g * DSA_GROUP + r) * DSA_HEAD_DIM
            o_ref[:, c0:c0 + DSA_HEAD_DIM] = og[:, r * tq:(r + 1) * tq].T.astype(BF16)


def _dsa_t(pf, pb, ki, k, vt, *, n_seq, rows_per_seq, tk, n_keys, topk):
    tq = LANES
    nk_pad = k.shape[1]
    steps = rows_per_seq // tq

    def rmap(c):
        return lambda s, t: (s * steps + t, c)

    kern = functools.partial(_dsa_t_kernel, tq=tq, tk=tk, n_keys=n_keys, topk=topk)
    single = pl.Buffered(1)
    return pl.pallas_call(
        kern,
        grid=(n_seq, steps),
        in_specs=[
            pl.BlockSpec((tq, IDX_Q), rmap(B_IQ // IDX_Q)),
            pl.BlockSpec((tq, LANES), rmap(F_SMALL // LANES)),
            pl.BlockSpec((tq, DSA_Q), rmap(B_DQ // DSA_Q)),
            pl.BlockSpec((1, nk_pad, IDX_DIM), lambda s, t: (s, 0, 0), pipeline_mode=single),
            pl.BlockSpec((1, nk_pad, DSA_KV), lambda s, t: (s, 0, 0), pipeline_mode=single),
            pl.BlockSpec((1, DSA_KV_HEADS, V_ROWS, nk_pad), lambda s, t: (s, 0, 0, 0), pipeline_mode=single),
        ],
        out_specs=pl.BlockSpec((tq, DSA_Q), lambda s, t: (s * steps + t, 0)),
        out_shape=jax.ShapeDtypeStruct((n_seq * rows_per_seq, DSA_Q), BF16),
        scratch_shapes=[
            pltpu.VMEM((nk_pad, tq), I32),
            pltpu.VMEM((DSA_KV_HEADS, 8, DSA_GROUP * tq), F32),
            pltpu.VMEM((DSA_KV_HEADS, V_ROWS, DSA_GROUP * tq), F32),
            pltpu.VMEM((2, tk, DSA_GROUP * tq), F32),
        ],
        compiler_params=pltpu.CompilerParams(
            dimension_semantics=("parallel", "arbitrary"), vmem_limit_bytes=VMEM_LIMIT_BYTES),
        name="dsa_t",
    )(pb, pf, pb, ki, k, vt)


def _out_kernel(x_ref, ya_ref, ob_ref, z_ref, ma_ref, mb_ref, wg_ref, wd_ref, wo_ref,
                gate_ref, lig_ref, lib_ref, lg_ref, lb_ref, o_ref, *, alpha):
    hn = _layer_norm_rows(x_ref[...], lig_ref[...], lib_ref[...])
    y_a = _dot(ya_ref[...], wg_ref[...])
    z = z_ref[...].astype(F32)
    yb_in = ob_ref[...].astype(F32) * (z * jax.nn.sigmoid(z))
    y_b = _dot(yb_in.astype(BF16), wd_ref[...])
    ga = jax.nn.sigmoid(ma_ref[...].astype(F32) + gate_ref[0:1, :])
    gb = jax.nn.sigmoid(mb_ref[...].astype(F32) + gate_ref[1:2, :])
    merged = ga * y_a + gb * y_b
    y = alpha * hn + _dot(merged.astype(BF16), wo_ref[...])
    o_ref[...] = _layer_norm_rows(y, lg_ref[...], lb_ref[...])


def _out(x, ya, ob, pb, wg, wd, wo, gate, lig, lib, lg, lb, tm, alpha, row0):
    rows = ya.shape[0]
    base = row0 // tm
    row = lambda c: pl.BlockSpec((tm, D_MODEL), lambda i, c=c: (i, c))
    off = lambda c: pl.BlockSpec((tm, D_MODEL), lambda i, c=c: (base + i, c))
    full = lambda a: pl.BlockSpec(a.shape, lambda i: (0, 0))
    return pl.pallas_call(
        functools.partial(_out_kernel, alpha=alpha),
        grid=(rows // tm,),
        in_specs=[off(0), row(0), row(0), off(B_DZ // D_MODEL), off(B_MA // D_MODEL), off(B_MB // D_MODEL),
                  full(wg), full(wd), full(wo), full(gate), full(lig), full(lib), full(lg), full(lb)],
        out_specs=pl.BlockSpec((tm, D_MODEL), lambda i: (i, 0)),
        out_shape=jax.ShapeDtypeStruct((rows, D_MODEL), F32),
        compiler_params=pltpu.CompilerParams(
            dimension_semantics=("parallel",), vmem_limit_bytes=VMEM_LIMIT_BYTES),
        name="out_proj",
    )(x, ya, ob, pb, pb, pb, wg, wd, wo, gate, lig, lib, lg, lb)


def _round_up(a, b):
    return -(-a // b) * b


def _forward(x_prompt, x_sample, cache_k, cache_v, cache_idx_k, state_gla, meta, ln_in_g, ln_in_b,
             w_in, gla_w2, gla_gate_b, gla_norm_g, idx_kn_g, idx_kn_b, w_gla, w_dsa, gate_b,
             w_out, ln_g, ln_b):
    depth = w_in.shape[0]
    assert depth == 1, "single-layer trunk"
    bsz, seq, _ = x_prompt.shape
    dbsz, dseq, _ = x_sample.shape
    past = cache_k.shape[2]
    t_p = N_META + seq
    tq_p = 128
    tp = _round_up(t_p, KEY_TILE)
    assert tp % tq_p == 0 and tp % GLA_BLOCK == 0
    row_s = bsz * tp
    rows = _round_up(row_s + dbsz * dseq, ROW_TILE)
    alpha = (2.0 * depth) ** 0.25

    meta_rows = jnp.broadcast_to(meta.astype(F32)[None], (bsz, N_META, D_MODEL))
    xp = jnp.concatenate([meta_rows, x_prompt, jnp.zeros((bsz, tp - t_p, D_MODEL), F32)], axis=1)
    x_all = jnp.concatenate([xp.reshape(bsz * tp, D_MODEL), x_sample.reshape(dbsz * dseq, D_MODEL),
                             jnp.zeros((rows - row_s - dbsz * dseq, D_MODEL), F32)], axis=0)

    w = w_in[0]
    w_f = jnp.concatenate([_col(w, _GQ), _col(w, _GK), _col(w, _DK), _col(w, _DV), _col(w, _IK),
                           _col(w, _GLOW), _col(w, _IW),
                           jnp.zeros((D_MODEL, LANES - IDX_DIM - GLA_RANK - IDX_HEADS), F32)], axis=1).astype(BF16)
    w_dq = _col(w, _DQ).reshape(D_MODEL, DSA_KV_HEADS, DSA_GROUP, DSA_HEAD_DIM)
    w_dq = (jnp.swapaxes(w_dq, 1, 2) * (DSA_SCALE * math.log2(math.e))).reshape(D_MODEL, DSA_Q)
    w_b = jnp.concatenate([_col(w, _GV), _col(w, _GR), w_dq, _col(w, _DZ), _col(w, _MA), _col(w, _MB),
                           _col(w, _IQ), jnp.zeros((D_MODEL, B_COLS - B_PAD), F32)], axis=1).astype(BF16)
    lig = ln_in_g.reshape(1, D_MODEL)
    lib = ln_in_b.reshape(1, D_MODEL)

    pf = _proj_f32(x_all, lig, lib, w_f, idx_kn_g[0].reshape(1, IDX_DIM), idx_kn_b[0].reshape(1, IDX_DIM),
                   ROW_TILE // 4)
    pb = _proj_bf16(x_all, lig, lib, w_b, ROW_TILE // 2, 1024)

    w2pad = jnp.zeros((LANES, GLA_QK), F32).at[SM_GLOW:SM_GLOW + GLA_RANK].set(gla_w2[0]).astype(BF16)
    gbias = gla_gate_b[0].reshape(1, GLA_QK)
    ng = gla_norm_g[0].reshape(1, GLA_DV)
    blk_p = min(GLA_BLOCK, t_p)
    cm_p = jnp.asarray(_gla_decay_matrix(blk_p), BF16)
    s0_p = jnp.zeros((bsz, GLA_HEADS, GLA_DV, GLA_DK), F32)
    ya_p, st_p = _gla(pf, pb, cm_p, w2pad, gbias, ng, s0_p, n_seq=bsz, row0=0, rows_per_seq=tp,
                      t_valid=t_p, blk=blk_p, n_blk=KEY_TILE // blk_p)
    blk_s = min(GLA_BLOCK, dseq)
    assert dseq % blk_s == 0 and row_s % dseq == 0
    cm_s = jnp.asarray(_gla_decay_matrix(blk_s), BF16)
    s0_s = jnp.swapaxes(state_gla[0], -1, -2)
    ya_s, st_s = _gla(pf, pb, cm_s, w2pad, gbias, ng, s0_s, n_seq=dbsz, row0=row_s, rows_per_seq=dseq,
                      t_valid=dseq, blk=blk_s, n_blk=dseq // blk_s)

    dk = pf[:, F_DK:F_DK + DSA_KV]
    dv = pf[:, F_DV:F_DV + DSA_KV]
    ki = pf[:, F_SMALL:F_SMALL + IDX_DIM]
    dk_p = dk[:row_s].reshape(bsz, tp, DSA_KV_HEADS, DSA_HEAD_DIM)
    dv_p = dv[:row_s].reshape(bsz, tp, DSA_KV_HEADS, DSA_HEAD_DIM)
    ki_p = ki[:row_s].reshape(bsz, tp, IDX_DIM)
    vt_p = jnp.concatenate([jnp.transpose(dv_p, (0, 2, 3, 1)),
                            jnp.ones((bsz, DSA_KV_HEADS, 1, tp), F32),
                            jnp.zeros((bsz, DSA_KV_HEADS, V_ROWS - DSA_HEAD_DIM - 1, tp), F32)], axis=2).astype(BF16)
    topk_p = min(TOPK_MAX, (t_p - N_META) // 4)
    ob_p = _dsa_t(pf, pb, ki_p.astype(BF16), dk[:row_s].reshape(bsz, tp, DSA_KV).astype(BF16), vt_p,
                  n_seq=bsz, rows_per_seq=tp, tk=KEY_TILE, n_keys=t_p, topk=topk_p)

    n_keys_s = past + dseq
    nk_s = _round_up(n_keys_s, KEY_TILE)
    dk_s = dk[row_s:row_s + dbsz * dseq].reshape(dbsz, dseq, DSA_KV_HEADS, DSA_HEAD_DIM)
    dv_s = dv[row_s:row_s + dbsz * dseq].reshape(dbsz, dseq, DSA_KV_HEADS, DSA_HEAD_DIM)
    ki_s = ki[row_s:row_s + dbsz * dseq].reshape(dbsz, dseq, IDX_DIM)
    kpad = jnp.zeros((dbsz, nk_s - n_keys_s, DSA_KV_HEADS, DSA_HEAD_DIM), F32)
    k_all = jnp.concatenate([cache_k[0], dk_s, kpad], axis=1)
    v_all = jnp.concatenate([cache_v[0], dv_s, kpad], axis=1)
    ki_all = jnp.concatenate([cache_idx_k[0], ki_s, jnp.zeros((dbsz, nk_s - n_keys_s, IDX_DIM), F32)], axis=1)
    kt_s = jnp.transpose(k_all, (0, 2, 3, 1)).astype(BF16)
    v_s = jnp.transpose(v_all, (0, 2, 1, 3)).astype(BF16)
    kit_s = jnp.transpose(ki_all, (0, 2, 1)).astype(BF16)
    topk_s = min(TOPK_MAX, n_keys_s // 4)
    ob_s = _dsa(pf, pb, kit_s, kt_s, v_s, n_seq=dbsz, row0=row_s, rows_per_seq=dseq, tq=dseq, tk=KEY_TILE,
                n_keys=n_keys_s, topk=topk_s, causal=False)

    out_w = (w_gla[0].astype(BF16), w_dsa[0].astype(BF16), w_out[0].astype(BF16), gate_b[0], lig, lib,
             ln_g[0].reshape(1, D_MODEL), ln_b[0].reshape(1, D_MODEL))
    tm_out = ROW_TILE // 4
    tm_s = math.gcd(tm_out, dbsz * dseq)
    assert row_s % tm_out == 0 and tm_s % 16 == 0
    y_p = _out(x_all, ya_p, ob_p, pb, *out_w, tm_out, alpha, 0)
    y_s = _out(x_all, ya_s, ob_s, pb, *out_w, tm_s, alpha, row_s)

    y_prompt = y_p.reshape(bsz, tp, D_MODEL)[:, N_META:t_p]
    y_sample = y_s.reshape(dbsz, dseq, D_MODEL)
    k_prompt = dk_p[:, :t_p][None]
    v_prompt = dv_p[:, :t_p][None]
    idx_k_prompt = ki_p[:, :t_p][None]
    gla_prompt = jnp.swapaxes(st_p, -1, -2)[None]
    k_sample = dk_s[None]
    v_sample = dv_s[None]
    idx_k_sample = ki_s[None]
    gla_sample = jnp.swapaxes(st_s, -1, -2)[None]
    return (y_prompt, y_sample, k_prompt, v_prompt, idx_k_prompt, gla_prompt,
            k_sample, v_sample, idx_k_sample, gla_sample)


def kernel(x_prompt, x_sample, cache_k, cache_v, cache_idx_k, state_gla, meta, ln_in_g, ln_in_b,
           w_in, gla_w2, gla_gate_b, gla_norm_g, idx_kn_g, idx_kn_b, w_gla, w_dsa, gate_b,
           w_out, ln_g, ln_b):
    return _forward(x_prompt, x_sample, cache_k, cache_v, cache_idx_k, state_gla, meta, ln_in_g, ln_in_b,
                    w_in, gla_w2, gla_gate_b, gla_norm_g, idx_kn_g, idx_kn_b, w_gla, w_dsa, gate_b,
                    w_out, ln_g, ln_b)
```

```python
import functools
import math

import numpy as np
import jax
import jax.numpy as jnp
from jax import lax
from jax.experimental import pallas as pl
from jax.experimental.pallas import tpu as pltpu

F32 = jnp.float32
BF16 = jnp.bfloat16
I32 = jnp.int32

D_MODEL = 1024
CHUNK = 64
N_META = 16
GLA_HEADS = 4
GLA_DK = 128
GLA_DV = 256
GLA_RANK = 16
GLA_TAU = 16.0
GLA_BLOCK = 64
DSA_HEADS = 16
DSA_KV_HEADS = 4
DSA_HEAD_DIM = 64
DSA_GROUP = DSA_HEADS // DSA_KV_HEADS
DSA_SCALE = DSA_HEAD_DIM ** -0.5
IDX_HEADS = 8
IDX_DIM = 64
IDX_W_SCALE = (IDX_HEADS ** -0.5) * (IDX_DIM ** -0.5)
TOPK_MAX = 256
NORM_EPS = 1e-5
GLA_QK = GLA_HEADS * GLA_DK
GLA_V = GLA_HEADS * GLA_DV
DSA_Q = DSA_HEADS * DSA_HEAD_DIM
DSA_KV = DSA_KV_HEADS * DSA_HEAD_DIM
IDX_Q = IDX_HEADS * IDX_DIM
SPLITS = (GLA_QK, GLA_QK, GLA_V, GLA_RANK, GLA_V,
          DSA_Q, DSA_KV, DSA_KV, IDX_Q, IDX_DIM, IDX_HEADS, DSA_Q,
          D_MODEL, D_MODEL)
_OFF = tuple(int(o) for o in np.cumsum((0,) + SPLITS))
(_GQ, _GK, _GV, _GLOW, _GR, _DQ, _DK, _DV, _IQ, _IK, _IW, _DZ, _MA, _MB) = range(14)

LANES = 128
VMEM_LIMIT_BYTES = 56 * 1024 * 1024

F_GQ, F_GK, F_DK, F_DV, F_SMALL = 0, 512, 1024, 1280, 1536
F_COLS = F_SMALL + LANES
SM_IK, SM_GLOW, SM_IW = 0, IDX_DIM, IDX_DIM + GLA_RANK
B_GV, B_GR, B_DQ, B_DZ, B_MA, B_MB, B_IQ, B_PAD = 0, 1024, 2048, 3072, 4096, 5120, 6144, 6656
B_COLS = 7168

ROW_TILE = 2048
KEY_TILE = 512
NEG_BIG = -1e30
INT_MIN = -(2 ** 31)
NEG_INF_KEY = int(np.array(-np.inf, np.float32).view(np.int32)) ^ 0x7FFFFFFF


def _col(w, idx):
    return w[:, _OFF[idx]:_OFF[idx + 1]]


def _layer_norm_rows(x, g, b):
    mu = jnp.mean(x, axis=-1, keepdims=True)
    xc = x - mu
    var = jnp.mean(xc * xc, axis=-1, keepdims=True)
    return xc * lax.rsqrt(var + NORM_EPS) * g + b


def _dot(a, b):
    return jnp.dot(a, b, preferred_element_type=F32)


def _dot_nt(a, b):
    return lax.dot_general(a, b, (((1,), (1,)), ((), ())), preferred_element_type=F32)


def _dot_tn(a, b):
    return lax.dot_general(a, b, (((0,), (0,)), ((), ())), preferred_element_type=F32)


def _proj_f32_kernel(x_ref, g_ref, b_ref, w_ref, kg_ref, kb_ref, o_ref):
    hn = _layer_norm_rows(x_ref[...], g_ref[...], b_ref[...])
    y = _dot(hn.astype(BF16), w_ref[...])
    o_ref[...] = y
    ik = y[:, F_SMALL:F_SMALL + IDX_DIM]
    o_ref[:, F_SMALL:F_SMALL + IDX_DIM] = _layer_norm_rows(ik, kg_ref[...], kb_ref[...])


def _proj_f32(x, g, b, w, kg, kb, tm):
    rows = x.shape[0]
    return pl.pallas_call(
        _proj_f32_kernel,
        grid=(rows // tm,),
        in_specs=[
            pl.BlockSpec((tm, D_MODEL), lambda i: (i, 0)),
            pl.BlockSpec((1, D_MODEL), lambda i: (0, 0)),
            pl.BlockSpec((1, D_MODEL), lambda i: (0, 0)),
            pl.BlockSpec((D_MODEL, F_COLS), lambda i: (0, 0)),
            pl.BlockSpec((1, IDX_DIM), lambda i: (0, 0)),
            pl.BlockSpec((1, IDX_DIM), lambda i: (0, 0)),
        ],
        out_specs=pl.BlockSpec((tm, F_COLS), lambda i: (i, 0)),
        out_shape=jax.ShapeDtypeStruct((rows, F_COLS), F32),
        compiler_params=pltpu.CompilerParams(
            dimension_semantics=("parallel",), vmem_limit_bytes=VMEM_LIMIT_BYTES),
        name="proj_f32",
    )(x, g, b, w, kg, kb)


def _proj_bf16_kernel(x_ref, g_ref, b_ref, w_ref, o_ref, hn_ref):
    @pl.when(pl.program_id(1) == 0)
    def _():
        hn_ref[...] = _layer_norm_rows(x_ref[...], g_ref[...], b_ref[...]).astype(BF16)

    o_ref[...] = _dot(hn_ref[...], w_ref[...]).astype(BF16)


def _proj_bf16(x, g, b, w, tm, tn):
    rows = x.shape[0]
    return pl.pallas_call(
        _proj_bf16_kernel,
        grid=(rows // tm, B_COLS // tn),
        in_specs=[
            pl.BlockSpec((tm, D_MODEL), lambda i, j: (i, 0)),
            pl.BlockSpec((1, D_MODEL), lambda i, j: (0, 0)),
            pl.BlockSpec((1, D_MODEL), lambda i, j: (0, 0)),
            pl.BlockSpec((D_MODEL, tn), lambda i, j: (0, j)),
        ],
        out_specs=pl.BlockSpec((tm, tn), lambda i, j: (i, j)),
        out_shape=jax.ShapeDtypeStruct((rows, B_COLS), BF16),
        scratch_shapes=[pltpu.VMEM((tm, D_MODEL), BF16)],
        compiler_params=pltpu.CompilerParams(
            dimension_semantics=("parallel", "arbitrary"), vmem_limit_bytes=VMEM_LIMIT_BYTES),
        name="proj_bf16",
    )(x, g, b, w)


def _gla_levels(blk):
    levels = []
    s = blk // 2
    while s >= 1:
        levels.append(s)
        s //= 2
    return levels


def _gla_decay_matrix(blk):
    i = np.arange(blk)[:, None]
    t = np.arange(blk)[None, :]
    mats = [(t <= i).astype(np.float32), (t > i).astype(np.float32)]
    for s in _gla_levels(blk):
        mid = (i // (2 * s)) * (2 * s) + s - 1
        lower = (i // s) % 2 == 1
        m = np.where(lower, ((t > mid) & (t <= i)), False).astype(np.float32)
        n = np.where(~lower, ((t > i) & (t <= mid)), False).astype(np.float32)
        mats.append(m + n)
    return np.concatenate(mats, axis=0)


def _gla_kernel(q_ref, k_ref, sm_ref, v_ref, r_ref, c_ref, w2_ref, gb_ref, ng_ref, s0_ref,
                o_ref, st_ref, s_scr, *, blk, n_blk, t_valid):
    ti = pl.program_id(1)
    levels = _gla_levels(blk)

    @pl.when(ti == 0)
    def _():
        s_scr[...] = s0_ref[0]

    ri = lax.broadcasted_iota(I32, (blk, blk), 0)
    ci = lax.broadcasted_iota(I32, (blk, blk), 1)
    eye = ri == ci
    lvl_masks = []
    for s in levels:
        sh = s.bit_length() - 1
        same = (ri >> (sh + 1)) == (ci >> (sh + 1))
        lvl_masks.append(same & (((ri >> sh) & 1) == 1) & (((ci >> sh) & 1) == 0))
    row_iota = lax.broadcasted_iota(I32, (blk, 1), 0)
    cmat = c_ref[...]
    w2 = w2_ref[...]
    gbias = gb_ref[...]
    ng = ng_ref[...]

    def block(j):
        r0 = pl.multiple_of(j * blk, blk)
        rows = pl.ds(r0, blk)
        valid = (ti * (n_blk * blk) + r0 + row_iota) < t_valid
        gq = q_ref[rows, :] * (GLA_DK ** -0.5)
        gk = jnp.where(valid, k_ref[rows, :], 0.0)
        x = _dot(sm_ref[rows, :].astype(BF16), w2) + gbias
        logf = (jnp.minimum(x, 0.0) - jnp.log1p(jnp.exp(-jnp.abs(x)))) * (1.0 / GLA_TAU)
        logf = jnp.where(valid, logf, 0.0)
        hi = logf.astype(BF16)
        r1 = logf - hi.astype(F32)
        mid = r1.astype(BF16)
        lo = (r1 - mid.astype(F32)).astype(BF16)
        e_all = _dot(cmat, hi) + _dot(cmat, mid) + _dot(cmat, lo)
        for h in range(GLA_HEADS):
            ksl = slice(h * GLA_DK, (h + 1) * GLA_DK)
            vsl = slice(h * GLA_DV, (h + 1) * GLA_DV)
            qh = gq[:, ksl]
            kh = gk[:, ksl]
            vh = v_ref[rows, vsl]
            b_h = e_all[0:blk, ksl]
            rev_h = e_all[blk:2 * blk, ksl]
            st = s_scr[h]
            o = _dot_nt((qh * jnp.exp(b_h)).astype(BF16), st.astype(BF16))
            a = jnp.where(eye, _dot_nt(qh.astype(BF16), kh.astype(BF16)), 0.0)
            for li in range(len(levels)):
                e = jnp.exp(e_all[(2 + li) * blk:(3 + li) * blk, ksl])
                p = _dot_nt((qh * e).astype(BF16), (kh * e).astype(BF16))
                a = a + jnp.where(lvl_masks[li], p, 0.0)
            o = o + _dot(a.astype(BF16), vh)
            kd = (kh * jnp.exp(rev_h)).astype(BF16)
            dec = jnp.exp(b_h[blk - 1:blk, :])
            s_scr[h] = st * dec + _dot_tn(vh, kd)
            on = o * lax.rsqrt(jnp.mean(o * o, axis=-1, keepdims=True) + NORM_EPS) * ng
            gr = r_ref[rows, vsl].astype(F32)
            o_ref[rows, vsl] = (on * (gr * jax.nn.sigmoid(gr))).astype(BF16)

    n_live = jnp.clip((t_valid - ti * (n_blk * blk) + blk - 1) // blk, 0, n_blk)

    @pl.when(n_live < n_blk)
    def _():
        o_ref[...] = jnp.zeros(o_ref.shape, o_ref.dtype)

    def block_pair(j, carry):
        block(2 * j)
        block(2 * j + 1)
        return carry

    lax.fori_loop(0, n_live // 2, block_pair, 0)

    @pl.when(n_live % 2 == 1)
    def _():
        block(n_live - 1)

    @pl.when(ti == pl.num_programs(1) - 1)
    def _():
        st_ref[0] = s_scr[...]


def _gla(pf, pb, cmat, w2pad, gbias, ng, s0t, *, n_seq, row0, rows_per_seq, t_valid, blk, n_blk):
    rb = blk * n_blk
    steps = rows_per_seq // rb
    base = row0 // rb

    def rmap(c):
        return lambda s, t: (base + s * steps + t, c)

    kern = functools.partial(_gla_kernel, blk=blk, n_blk=n_blk, t_valid=t_valid)
    return pl.pallas_call(
        kern,
        grid=(n_seq, steps),
        in_specs=[
            pl.BlockSpec((rb, GLA_QK), rmap(F_GQ // GLA_QK)),
            pl.BlockSpec((rb, GLA_QK), rmap(F_GK // GLA_QK)),
            pl.BlockSpec((rb, LANES), rmap(F_SMALL // LANES)),
            pl.BlockSpec((rb, GLA_V), rmap(B_GV // GLA_V)),
            pl.BlockSpec((rb, GLA_V), rmap(B_GR // GLA_V)),
            pl.BlockSpec(cmat.shape, lambda s, t: (0, 0)),
            pl.BlockSpec(w2pad.shape, lambda s, t: (0, 0)),
            pl.BlockSpec((1, GLA_QK), lambda s, t: (0, 0)),
            pl.BlockSpec((1, GLA_DV), lambda s, t: (0, 0)),
            pl.BlockSpec((1, GLA_HEADS, GLA_DV, GLA_DK), lambda s, t: (s, 0, 0, 0)),
        ],
        out_specs=[
            pl.BlockSpec((rb, GLA_V), lambda s, t: (s * steps + t, 0)),
            pl.BlockSpec((1, GLA_HEADS, GLA_DV, GLA_DK), lambda s, t: (s, 0, 0, 0)),
        ],
        out_shape=[
            jax.ShapeDtypeStruct((n_seq * rows_per_seq, GLA_V), BF16),
            jax.ShapeDtypeStruct((n_seq, GLA_HEADS, GLA_DV, GLA_DK), F32),
        ],
        scratch_shapes=[pltpu.VMEM((GLA_HEADS, GLA_DV, GLA_DK), F32)],
        compiler_params=pltpu.CompilerParams(
            dimension_semantics=("parallel", "arbitrary"), vmem_limit_bytes=VMEM_LIMIT_BYTES),
        name="gla",
    )(pf, pf, pf, pb, pb, cmat, w2pad, gbias, ng, s0t)


def _dsa_kernel(iq_ref, sm_ref, qd_ref, kit_ref, kt_ref, v_ref, o_ref,
                key_scr, m_scr, l_scr, acc_scr, *, tq, tk, n_keys, topk, causal, n_kb_total):
    qb = pl.program_id(1)
    n_rows_g = DSA_GROUP * tq

    if causal:
        q_chunk_max = (qb * tq + tq - 1 - N_META) >> 6
        limit = jnp.minimum(N_META + CHUNK * (q_chunk_max + 1), n_keys)
        n_kb = (limit + tk - 1) // tk
    else:
        n_kb = n_kb_total

    q_pos = qb * tq + lax.broadcasted_iota(I32, (tq, 1), 0)
    q_chunk = (q_pos - N_META) >> 6
    lane_iota = lax.broadcasted_iota(I32, (1, tk), 1)

    iq = iq_ref[...]
    iq_hm = jnp.concatenate([iq[:, h * IDX_DIM:(h + 1) * IDX_DIM] for h in range(IDX_HEADS)], axis=0)
    w_i = sm_ref[:, SM_IW:SM_IW + IDX_HEADS] * IDX_W_SCALE
    w_cols = [jnp.broadcast_to(w_i[:, h:h + 1], (tq, LANES)) for h in range(IDX_HEADS)]

    def score_block(kb, carry):
        k0 = pl.multiple_of(kb * tk, tk)
        r = _dot(iq_hm, kit_ref[0, :, pl.ds(k0, tk)])
        sc = None
        for h in range(IDX_HEADS):
            rh = jnp.maximum(r[h * tq:(h + 1) * tq, :], 0.0)
            wh = jnp.concatenate([w_cols[h]] * (tk // LANES), axis=1)
            sc = rh * wh if sc is None else sc + rh * wh
        key_pos = k0 + lane_iota
        adm = key_pos < n_keys
        if causal:
            adm = adm & (((key_pos - N_META) >> 6) <= q_chunk)
        sc = jnp.where(adm, sc, -jnp.inf)
        bits = pltpu.bitcast(sc, I32)
        key_scr[:, pl.ds(k0, tk)] = bits ^ ((bits >> 31) & 0x7FFFFFFF)
        return carry

    lax.fori_loop(0, n_kb, score_block, 0)

    n_cols = n_kb * (tk // LANES)

    def count_ge(cand):
        cb = jnp.broadcast_to(cand, (tq, LANES))

        def body(c, acc):
            c0 = pl.multiple_of(c * LANES, LANES)
            return acc + jnp.where(key_scr[:, pl.ds(c0, LANES)] >= cb, 1, 0)

        acc = lax.fori_loop(0, n_cols, body, jnp.zeros((tq, LANES), I32))
        return jnp.sum(acc, axis=1, keepdims=True)

    def bit_step(i, carry):
        lo, cge = carry
        cand = lo + jnp.left_shift(jnp.int32(1), 31 - i)
        cnt = count_ge(cand)
        take = cnt >= topk
        return jnp.where(take, cand, lo), jnp.where(take, cnt, cge)

    lo0 = jnp.full((tq, 1), INT_MIN, I32)
    cge0 = jnp.zeros((tq, 1), I32) + n_cols * LANES
    thr, cge = lax.fori_loop(0, 32, bit_step, (lo0, cge0))
    need_tie = (cge > topk) & (thr > NEG_INF_KEY)
    thr = jnp.maximum(thr, NEG_INF_KEY + 1)

    @pl.when(jnp.max(need_tie.astype(I32)) > 0)
    def _():
        want = topk - count_ge(thr + 1)
        thr_b = jnp.broadcast_to(thr, (tq, LANES))
        col_iota = lax.broadcasted_iota(I32, (tq, LANES), 1)

        def count_tied_below(m):
            mb = jnp.broadcast_to(m, (tq, LANES))

            def body(c, acc):
                c0 = pl.multiple_of(c * LANES, LANES)
                hit = (key_scr[:, pl.ds(c0, LANES)] == thr_b) & ((c0 + col_iota) < mb)
                return acc + jnp.where(hit, 1, 0)

            acc = lax.fori_loop(0, n_cols, body, jnp.zeros((tq, LANES), I32))
            return jnp.sum(acc, axis=1, keepdims=True)

        n_bits = int(math.ceil(math.log2(n_kb_total * tk + 1)))

        def idx_step(i, m):
            cand = m + jnp.left_shift(jnp.int32(1), n_bits - 1 - i)
            return jnp.where(count_tied_below(cand) <= want, cand, m)

        m_keep = lax.fori_loop(0, n_bits, idx_step, jnp.zeros((tq, 1), I32))
        mk_b = jnp.broadcast_to(jnp.where(need_tie, m_keep, jnp.int32(2 ** 30)), (tq, LANES))

        def demote(c, carry):
            c0 = pl.multiple_of(c * LANES, LANES)
            kv = key_scr[:, pl.ds(c0, LANES)]
            drop = (kv == thr_b) & ((c0 + col_iota) >= mk_b)
            key_scr[:, pl.ds(c0, LANES)] = jnp.where(drop, kv - 1, kv)
            return carry

        lax.fori_loop(0, n_cols, demote, 0)

    qd = qd_ref[...]
    q_groups = []
    for g in range(DSA_KV_HEADS):
        q_groups.append(jnp.concatenate(
            [qd[:, (r * DSA_KV_HEADS + g) * DSA_HEAD_DIM:(r * DSA_KV_HEADS + g + 1) * DSA_HEAD_DIM]
             for r in range(DSA_GROUP)], axis=0))
    m_scr[...] = jnp.full(m_scr.shape, NEG_BIG, F32)
    l_scr[...] = jnp.zeros(l_scr.shape, F32)
    acc_scr[...] = jnp.zeros(acc_scr.shape, F32)
    thr_t = jnp.broadcast_to(thr, (tq, LANES))

    def attend_block(kb, carry):
        k0 = pl.multiple_of(kb * tk, tk)
        sel = key_scr[:, pl.ds(k0, tk)] >= jnp.concatenate([thr_t] * (tk // LANES), axis=1)
        bias = jnp.where(sel, 0.0, NEG_BIG)
        bias_g = jnp.concatenate([bias] * DSA_GROUP, axis=0)
        for g in range(DSA_KV_HEADS):
            s = _dot(q_groups[g], kt_ref[0, g, :, pl.ds(k0, tk)]) + bias_g
            m_old = m_scr[g]
            m_new = jnp.maximum(m_old, jnp.max(s, axis=1, keepdims=True))
            alpha = jnp.exp2(m_old - m_new)
            p = jnp.exp2(s - m_new)
            p = jnp.where(bias_g < 0.0, 0.0, p)
            l_scr[g] = alpha * l_scr[g] + jnp.sum(p, axis=1, keepdims=True)
            acc_scr[g] = alpha * acc_scr[g] + _dot(p.astype(BF16), v_ref[0, g, pl.ds(k0, tk), :])
            m_scr[g] = m_new
        return carry

    lax.fori_loop(0, n_kb, attend_block, 0)

    for g in range(DSA_KV_HEADS):
        og = acc_scr[g] / l_scr[g]
        for r in range(DSA_GROUP):
            c0 = (g * DSA_GROUP + r) * DSA_HEAD_DIM
            o_ref[:, c0:c0 + DSA_HEAD_DIM] = og[r * tq:(r + 1) * tq, :].astype(BF16)


def _dsa(pf, pb, kit, kt, v, *, n_seq, row0, rows_per_seq, tq, tk, n_keys, topk, causal):
    nk_pad = kit.shape[-1]
    n_kb_total = nk_pad // tk
    steps = rows_per_seq // tq
    base = row0 // tq

    def rmap(c):
        return lambda s, t: (base + s * steps + t, c)

    kern = functools.partial(_dsa_kernel, tq=tq, tk=tk, n_keys=n_keys, topk=topk, causal=causal,
                             n_kb_total=n_kb_total)
    single = pl.Buffered(1)
    return pl.pallas_call(
        kern,
        grid=(n_seq, steps),
        in_specs=[
            pl.BlockSpec((tq, IDX_Q), rmap(B_IQ // IDX_Q)),
            pl.BlockSpec((tq, LANES), rmap(F_SMALL // LANES)),
            pl.BlockSpec((tq, DSA_Q), rmap(B_DQ // DSA_Q)),
            pl.BlockSpec((1, IDX_DIM, nk_pad), lambda s, t: (s, 0, 0), pipeline_mode=single),
            pl.BlockSpec((1, DSA_KV_HEADS, DSA_HEAD_DIM, nk_pad), lambda s, t: (s, 0, 0, 0),
                         pipeline_mode=single),
            pl.BlockSpec((1, DSA_KV_HEADS, nk_pad, DSA_HEAD_DIM), lambda s, t: (s, 0, 0, 0),
                         pipeline_mode=single),
        ],
        out_specs=pl.BlockSpec((tq, DSA_Q), lambda s, t: (s * steps + t, 0)),
        out_shape=jax.ShapeDtypeStruct((n_seq * rows_per_seq, DSA_Q), BF16),
        scratch_shapes=[
            pltpu.VMEM((tq, nk_pad), I32),
            pltpu.VMEM((DSA_KV_HEADS, DSA_GROUP * tq, 1), F32),
            pltpu.VMEM((DSA_KV_HEADS, DSA_GROUP * tq, 1), F32),
            pltpu.VMEM((DSA_KV_HEADS, DSA_GROUP * tq, DSA_HEAD_DIM), F32),
        ],
        compiler_params=pltpu.CompilerParams(
            dimension_semantics=("parallel", "arbitrary"), vmem_limit_bytes=VMEM_LIMIT_BYTES),
        name="dsa",
    )(pb, pf, pb, kit, kt, v)


BIT_GROUP = 256
SLAB_ROWS = 128


def _bit_transpose32(words):
    a = list(words)
    mask, j = 0x0000FFFF, 16
    while j:
        k = 0
        while k < 32:
            t = (a[k] ^ lax.shift_right_logical(a[k + j], jnp.int32(j))) & mask
            a[k] = a[k] ^ t
            a[k + j] = a[k + j] ^ (t << j)
            k = (k + j + 1) & ~j
        j >>= 1
        mask ^= (mask << j) & 0xFFFFFFFF
    return a


V_ROWS = 80
V_ONES_ROW = DSA_HEAD_DIM


def _dsa_t_kernel(iq_ref, sm_ref, qd_ref, ki_ref, k_ref, vt_ref, o_ref,
                  key_scr, m_scr, acc_scr, s_scr, plane_scr, active_scr, *, tq, tk, n_keys, topk):
    live = pl.program_id(1) * tq < n_keys

    @pl.when(live)
    def _():
        _dsa_t_body(iq_ref, sm_ref, qd_ref, ki_ref, k_ref, vt_ref, o_ref, key_scr, m_scr, acc_scr, s_scr,
                    plane_scr, active_scr,
                    tq=tq, tk=tk, n_keys=n_keys, topk=topk)

    @pl.when(jnp.logical_not(live))
    def _():
        o_ref[...] = jnp.zeros(o_ref.shape, o_ref.dtype)


def _dsa_t_body(iq_ref, sm_ref, qd_ref, ki_ref, k_ref, vt_ref, o_ref,
                key_scr, m_scr, acc_scr, s_scr, plane_scr, active_scr, *, tq, tk, n_keys, topk):
    qb = pl.program_id(1)
    q0 = qb * tq
    q_chunk_max = (q0 + tq - 1 - N_META) >> 6
    limit = jnp.minimum(N_META + CHUNK * (q_chunk_max + 1), n_keys)
    n_kb = (limit + tk - 1) // tk
    n_full = jnp.minimum(jnp.minimum((q0 + N_META) // tk, n_keys // tk), n_kb)

    q_chunk = (q0 + lax.broadcasted_iota(I32, (1, tq), 1) - N_META) >> 6
    row_iota = lax.broadcasted_iota(I32, (tk, tq), 0)
    plane_rows = key_scr.shape[0] // BIT_GROUP * 8

    @pl.when(qb == 0)
    def _():
        plane_scr[...] = jnp.zeros(plane_scr.shape, I32)

    iq = iq_ref[...]
    iq_hm = jnp.concatenate([iq[:, h * IDX_DIM:(h + 1) * IDX_DIM] for h in range(IDX_HEADS)], axis=0)
    w_t = sm_ref[...].T[SM_IW:SM_IW + IDX_HEADS, :] * IDX_W_SCALE

    def score_block(kb, masked):
        k0 = pl.multiple_of(kb * tk, tk)
        r = _dot_nt(ki_ref[0, pl.ds(k0, tk), :], iq_hm)
        sc = None
        for h in range(IDX_HEADS):
            t = jnp.maximum(r[:, h * tq:(h + 1) * tq], 0.0) * w_t[h:h + 1, :]
            sc = t if sc is None else sc + t
        if masked:
            key_pos = k0 + row_iota
            adm = (key_pos < n_keys) & (((key_pos - N_META) >> 6) <= q_chunk)
            sc = jnp.where(adm, sc, -jnp.inf)
        bits = pltpu.bitcast(sc, I32)
        keys = bits ^ ((bits >> 31) & 0x7FFFFFFF)
        key_scr[pl.ds(k0, tk), :] = keys
        ukeys = keys ^ INT_MIN
        for u in range(tk // BIT_GROUP):
            words = [ukeys[u * BIT_GROUP + 8 * w:u * BIT_GROUP + 8 * w + 8, :] for w in range(32)]
            planes = _bit_transpose32(words)
            g_row = pl.multiple_of((kb * (tk // BIT_GROUP) + u) * 8, 8)
            for w in range(32):
                plane_scr[pl.ds(w * plane_rows + g_row, 8), :] = planes[w]

    def score_full(kb, carry):
        score_block(kb, False)
        return carry

    def score_edge(kb, carry):
        score_block(kb, True)
        return carry

    lax.fori_loop(0, n_full, score_full, 0)
    lax.fori_loop(n_full, n_kb, score_edge, 0)

    n_chunks = n_kb * (tk // LANES)

    sub_per_blk = tk // LANES

    def count_rows(pred):
        def body(kb, acc):
            for u in range(sub_per_blk):
                c0 = pl.multiple_of(kb * tk + u * LANES, LANES)
                acc = acc + jnp.where(pred(key_scr[pl.ds(c0, LANES), :], c0), 1, 0)
            return acc

        acc = lax.fori_loop(0, n_kb, body, jnp.zeros((LANES, tq), I32))
        return jnp.sum(acc, axis=0, keepdims=True)

    def count_ge(cand):
        return count_rows(lambda kv, c0: kv >= cand)

    n_groups = n_kb * (tk // BIT_GROUP)
    n_slabs = (n_groups * 8 + SLAB_ROWS - 1) // SLAB_ROWS
    slab_iota = lax.broadcasted_iota(I32, (SLAB_ROWS, tq), 0)

    def init_active(sl, carry):
        r0 = pl.multiple_of(sl * SLAB_ROWS, SLAB_ROWS)
        active_scr[pl.ds(r0, SLAB_ROWS), :] = jnp.where(r0 + slab_iota < n_groups * 8, -1, 0)
        return carry

    lax.fori_loop(0, n_slabs, init_active, 0)

    def select_pass(i, carry):
        thr_u, c_above, n_act, flip = carry
        cur = i * plane_rows
        prev = cur - plane_rows

        def body(sl, acc):
            r0 = pl.multiple_of(sl * SLAB_ROWS, SLAB_ROWS)
            rows = pl.ds(r0, SLAB_ROWS)
            act = active_scr[rows, :] & (plane_scr[pl.ds(pl.multiple_of(prev + r0, 8), SLAB_ROWS), :] ^ flip)
            active_scr[rows, :] = act
            return acc + lax.population_count(act & plane_scr[pl.ds(pl.multiple_of(cur + r0, 8), SLAB_ROWS), :])

        acc = lax.fori_loop(0, n_slabs, body, jnp.zeros((SLAB_ROWS, tq), I32))
        ones = jnp.sum(acc, axis=0, keepdims=True)
        take = c_above + ones >= topk
        thr_u = thr_u | jnp.where(take, jnp.left_shift(jnp.int32(1), 31 - i), 0)
        c_above = c_above + jnp.where(take, 0, ones)
        n_act = jnp.where(take, ones, n_act - ones)
        return thr_u, c_above, n_act, jnp.where(take, 0, -1)

    n_act = jnp.zeros((1, tq), I32) + n_groups * BIT_GROUP

    def count_top(sl, acc):
        rows = pl.ds(pl.multiple_of(sl * SLAB_ROWS, SLAB_ROWS), SLAB_ROWS)
        return acc + lax.population_count(active_scr[rows, :] & plane_scr[rows, :])

    acc0 = lax.fori_loop(0, n_slabs, count_top, jnp.zeros((SLAB_ROWS, tq), I32))
    ones0 = jnp.sum(acc0, axis=0, keepdims=True)
    take0 = ones0 >= topk
    carry0 = (jnp.where(take0, INT_MIN, 0), jnp.where(take0, 0, ones0),
              jnp.where(take0, ones0, n_act - ones0), jnp.where(take0, 0, -1))
    thr_u, c_above, n_act, _ = lax.fori_loop(1, 32, select_pass, carry0)
    thr = thr_u ^ INT_MIN
    cge = c_above + n_act
    need_tie = (cge > topk) & (thr > NEG_INF_KEY)
    thr = jnp.maximum(thr, NEG_INF_KEY + 1)

    @pl.when(jnp.max(need_tie.astype(I32)) > 0)
    def _():
        want = topk - count_ge(thr + 1)
        sub_iota = lax.broadcasted_iota(I32, (LANES, tq), 0)
        n_bits = int(math.ceil(math.log2(key_scr.shape[0] + 1)))

        def idx_step(i, m):
            cand = m + jnp.left_shift(jnp.int32(1), n_bits - 1 - i)
            tied_below = count_rows(lambda kv, c0: (kv == thr) & ((c0 + sub_iota) < cand))
            return jnp.where(tied_below <= want, cand, m)

        m_keep = lax.fori_loop(0, n_bits, idx_step, jnp.zeros((1, tq), I32))
        m_keep = jnp.where(need_tie, m_keep, jnp.int32(2 ** 30))

        def demote(c, carry):
            c0 = pl.multiple_of(c * LANES, LANES)
            kv = key_scr[pl.ds(c0, LANES), :]
            drop = (kv == thr) & ((c0 + sub_iota) >= m_keep)
            key_scr[pl.ds(c0, LANES), :] = jnp.where(drop, kv - 1, kv)
            return carry

        lax.fori_loop(0, n_chunks, demote, 0)

    qd = qd_ref[...]
    win = lax.broadcasted_iota(I32, (tq, DSA_KV), 1) >> 6
    q_pad = []
    for g in range(DSA_KV_HEADS):
        q_pad.append(jnp.concatenate(
            [jnp.where(win == g, qd[:, r * DSA_KV:(r + 1) * DSA_KV].astype(F32), 0.0).astype(BF16)
             for r in range(DSA_GROUP)], axis=0))
    m_scr[...] = jnp.full(m_scr.shape, NEG_BIG, F32)
    acc_scr[...] = jnp.zeros(acc_scr.shape, F32)
    s_scr[0] = _dot_nt(k_ref[0, pl.ds(0, tk), :], q_pad[0])

    def attend_block(kb):
        k0 = pl.multiple_of(kb * tk, tk)
        bias = jnp.where(key_scr[pl.ds(k0, tk), :] >= thr, 0.0, NEG_BIG)
        bias_g = jnp.concatenate([bias] * DSA_GROUP, axis=1)
        kblk = k_ref[0, pl.ds(k0, tk), :]
        k1 = pl.multiple_of(jnp.minimum(kb + 1, n_kb - 1) * tk, tk)
        for g in range(DSA_KV_HEADS):
            if g + 1 < DSA_KV_HEADS:
                s_scr[(g + 1) % 2] = _dot_nt(kblk, q_pad[g + 1])
            else:
                s_scr[0] = _dot_nt(k_ref[0, pl.ds(k1, tk), :], q_pad[0])
            s = s_scr[g % 2] + bias_g
            m_old = m_scr[g]
            m_new = jnp.maximum(m_old, jnp.max(s, axis=0, keepdims=True))
            alpha = jnp.exp2(m_old[0:1] - m_new[0:1])
            p = jnp.exp2(s - m_new[0:1]).astype(BF16)
            acc_scr[g] = acc_scr[g] * alpha + _dot(vt_ref[0, g, :, pl.ds(k0, tk)], p)
            m_scr[g] = m_new

    def attend_pair(j, carry):
        attend_block(2 * j)
        attend_block(2 * j + 1)
        return carry

    lax.fori_loop(0, n_kb // 2, attend_pair, 0)

    @pl.when(n_kb % 2 == 1)
    def _():
        attend_block(n_kb - 1)

    for g in range(DSA_KV_HEADS):
        acc = acc_scr[g]
        og = acc[0:DSA_HEAD_DIM] / acc[V_ONES_ROW:V_ONES_ROW + 1]
        for r in range(DSA_GROUP):
            c0 = (g * DSA_GROUP + r) * DSA_HEAD_DIM
            o_ref[:, c0:c0 + DSA_HEAD_DIM] = og[:, r * tq:(r + 1) * tq].T.astype(BF16)


def _dsa_t(pf, pb, ki, k, vt, *, n_seq, rows_per_seq, tk, n_keys, topk):
    tq = LANES
    nk_pad = k.shape[1]
    steps = rows_per_seq // tq

    def rmap(c):
        return lambda s, t: (s * steps + t, c)

    kern = functools.partial(_dsa_t_kernel, tq=tq, tk=tk, n_keys=n_keys, topk=topk)
    single = pl.Buffered(1)
    return pl.pallas_call(
        kern,
        grid=(n_seq, steps),
        in_specs=[
            pl.BlockSpec((tq, IDX_Q), rmap(B_IQ // IDX_Q)),
            pl.BlockSpec((tq, LANES), rmap(F_SMALL // LANES)),
            pl.BlockSpec((tq, DSA_Q), rmap(B_DQ // DSA_Q)),
            pl.BlockSpec((1, nk_pad, IDX_DIM), lambda s, t: (s, 0, 0), pipeline_mode=single),
            pl.BlockSpec((1, nk_pad, DSA_KV), lambda s, t: (s, 0, 0), pipeline_mode=single),
            pl.BlockSpec((1, DSA_KV_HEADS, V_ROWS, nk_pad), lambda s, t: (s, 0, 0, 0), pipeline_mode=single),
        ],
        out_specs=pl.BlockSpec((tq, DSA_Q), lambda s, t: (s * steps + t, 0)),
        out_shape=jax.ShapeDtypeStruct((n_seq * rows_per_seq, DSA_Q), BF16),
        scratch_shapes=[
            pltpu.VMEM((nk_pad, tq), I32),
            pltpu.VMEM((DSA_KV_HEADS, 8, DSA_GROUP * tq), F32),
            pltpu.VMEM((DSA_KV_HEADS, V_ROWS, DSA_GROUP * tq), F32),
            pltpu.VMEM((2, tk, DSA_GROUP * tq), F32),
            pltpu.VMEM((nk_pad + SLAB_ROWS, tq), I32),
            pltpu.VMEM((_round_up(nk_pad // BIT_GROUP * 8, SLAB_ROWS), tq), I32),
        ],
        compiler_params=pltpu.CompilerParams(
            dimension_semantics=("parallel", "arbitrary"), vmem_limit_bytes=VMEM_LIMIT_BYTES),
        name="dsa_t",
    )(pb, pf, pb, ki, k, vt)


def _out_kernel(x_ref, ya_ref, ob_ref, z_ref, ma_ref, mb_ref, wg_ref, wd_ref, wo_ref,
                gate_ref, lig_ref, lib_ref, lg_ref, lb_ref, o_ref, *, alpha):
    hn = _layer_norm_rows(x_ref[...], lig_ref[...], lib_ref[...])
    y_a = _dot(ya_ref[...], wg_ref[...])
    z = z_ref[...].astype(F32)
    yb_in = ob_ref[...].astype(F32) * (z * jax.nn.sigmoid(z))
    y_b = _dot(yb_in.astype(BF16), wd_ref[...])
    ga = jax.nn.sigmoid(ma_ref[...].astype(F32) + gate_ref[0:1, :])
    gb = jax.nn.sigmoid(mb_ref[...].astype(F32) + gate_ref[1:2, :])
    merged = ga * y_a + gb * y_b
    y = alpha * hn + _dot(merged.astype(BF16), wo_ref[...])
    o_ref[...] = _layer_norm_rows(y, lg_ref[...], lb_ref[...])


def _out(x, ya, ob, pb, wg, wd, wo, gate, lig, lib, lg, lb, tm, alpha, row0):
    rows = ya.shape[0]
    base = row0 // tm
    row = lambda c: pl.BlockSpec((tm, D_MODEL), lambda i, c=c: (i, c))
    off = lambda c: pl.BlockSpec((tm, D_MODEL), lambda i, c=c: (base + i, c))
    full = lambda a: pl.BlockSpec(a.shape, lambda i: (0, 0))
    return pl.pallas_call(
        functools.partial(_out_kernel, alpha=alpha),
        grid=(rows // tm,),
        in_specs=[off(0), row(0), row(0), off(B_DZ // D_MODEL), off(B_MA // D_MODEL), off(B_MB // D_MODEL),
                  full(wg), full(wd), full(wo), full(gate), full(lig), full(lib), full(lg), full(lb)],
        out_specs=pl.BlockSpec((tm, D_MODEL), lambda i: (i, 0)),
        out_shape=jax.ShapeDtypeStruct((rows, D_MODEL), F32),
        compiler_params=pltpu.CompilerParams(
            dimension_semantics=("parallel",), vmem_limit_bytes=VMEM_LIMIT_BYTES),
        name="out_proj",
    )(x, ya, ob, pb, pb, pb, wg, wd, wo, gate, lig, lib, lg, lb)


def _round_up(a, b):
    return -(-a // b) * b


def _forward(x_prompt, x_sample, cache_k, cache_v, cache_idx_k, state_gla, meta, ln_in_g, ln_in_b,
             w_in, gla_w2, gla_gate_b, gla_norm_g, idx_kn_g, idx_kn_b, w_gla, w_dsa, gate_b,
             w_out, ln_g, ln_b):
    depth = w_in.shape[0]
    assert depth == 1, "single-layer trunk"
    bsz, seq, _ = x_prompt.shape
    dbsz, dseq, _ = x_sample.shape
    past = cache_k.shape[2]
    t_p = N_META + seq
    tq_p = 128
    tp = _round_up(t_p, KEY_TILE)
    assert tp % tq_p == 0 and tp % GLA_BLOCK == 0
    row_s = bsz * tp
    rows = _round_up(row_s + dbsz * dseq, ROW_TILE)
    alpha = (2.0 * depth) ** 0.25

    meta_rows = jnp.broadcast_to(meta.astype(F32)[None], (bsz, N_META, D_MODEL))
    xp = jnp.concatenate([meta_rows, x_prompt, jnp.zeros((bsz, tp - t_p, D_MODEL), F32)], axis=1)
    x_all = jnp.concatenate([xp.reshape(bsz * tp, D_MODEL), x_sample.reshape(dbsz * dseq, D_MODEL),
                             jnp.zeros((rows - row_s - dbsz * dseq, D_MODEL), F32)], axis=0)

    w = w_in[0]
    w_f = jnp.concatenate([_col(w, _GQ), _col(w, _GK), _col(w, _DK), _col(w, _DV), _col(w, _IK),
                           _col(w, _GLOW), _col(w, _IW),
                           jnp.zeros((D_MODEL, LANES - IDX_DIM - GLA_RANK - IDX_HEADS), F32)], axis=1).astype(BF16)
    w_dq = _col(w, _DQ).reshape(D_MODEL, DSA_KV_HEADS, DSA_GROUP, DSA_HEAD_DIM)
    w_dq = (jnp.swapaxes(w_dq, 1, 2) * (DSA_SCALE * math.log2(math.e))).reshape(D_MODEL, DSA_Q)
    w_b = jnp.concatenate([_col(w, _GV), _col(w, _GR), w_dq, _col(w, _DZ), _col(w, _MA), _col(w, _MB),
                           _col(w, _IQ), jnp.zeros((D_MODEL, B_COLS - B_PAD), F32)], axis=1).astype(BF16)
    lig = ln_in_g.reshape(1, D_MODEL)
    lib = ln_in_b.reshape(1, D_MODEL)

    pf = _proj_f32(x_all, lig, lib, w_f, idx_kn_g[0].reshape(1, IDX_DIM), idx_kn_b[0].reshape(1, IDX_DIM),
                   ROW_TILE // 4)
    pb = _proj_bf16(x_all, lig, lib, w_b, ROW_TILE // 2, 1024)

    w2pad = jnp.zeros((LANES, GLA_QK), F32).at[SM_GLOW:SM_GLOW + GLA_RANK].set(gla_w2[0]).astype(BF16)
    gbias = gla_gate_b[0].reshape(1, GLA_QK)
    ng = gla_norm_g[0].reshape(1, GLA_DV)
    blk_p = min(GLA_BLOCK, t_p)
    cm_p = jnp.asarray(_gla_decay_matrix(blk_p), BF16)
    s0_p = jnp.zeros((bsz, GLA_HEADS, GLA_DV, GLA_DK), F32)
    ya_p, st_p = _gla(pf, pb, cm_p, w2pad, gbias, ng, s0_p, n_seq=bsz, row0=0, rows_per_seq=tp,
                      t_valid=t_p, blk=blk_p, n_blk=KEY_TILE // blk_p)
    blk_s = min(GLA_BLOCK, dseq)
    assert dseq % blk_s == 0 and row_s % dseq == 0
    cm_s = jnp.asarray(_gla_decay_matrix(blk_s), BF16)
    s0_s = jnp.swapaxes(state_gla[0], -1, -2)
    ya_s, st_s = _gla(pf, pb, cm_s, w2pad, gbias, ng, s0_s, n_seq=dbsz, row0=row_s, rows_per_seq=dseq,
                      t_valid=dseq, blk=blk_s, n_blk=dseq // blk_s)

    dk = pf[:, F_DK:F_DK + DSA_KV]
    dv = pf[:, F_DV:F_DV + DSA_KV]
    ki = pf[:, F_SMALL:F_SMALL + IDX_DIM]
    dk_p = dk[:row_s].reshape(bsz, tp, DSA_KV_HEADS, DSA_HEAD_DIM)
    dv_p = dv[:row_s].reshape(bsz, tp, DSA_KV_HEADS, DSA_HEAD_DIM)
    ki_p = ki[:row_s].reshape(bsz, tp, IDX_DIM)
    vt_p = jnp.concatenate([jnp.transpose(dv_p, (0, 2, 3, 1)),
                            jnp.ones((bsz, DSA_KV_HEADS, 1, tp), F32),
                            jnp.zeros((bsz, DSA_KV_HEADS, V_ROWS - DSA_HEAD_DIM - 1, tp), F32)], axis=2).astype(BF16)
    topk_p = min(TOPK_MAX, (t_p - N_META) // 4)
    ob_p = _dsa_t(pf, pb, ki_p.astype(BF16), dk[:row_s].reshape(bsz, tp, DSA_KV).astype(BF16), vt_p,
                  n_seq=bsz, rows_per_seq=tp, tk=KEY_TILE, n_keys=t_p, topk=topk_p)

    n_keys_s = past + dseq
    nk_s = _round_up(n_keys_s, KEY_TILE)
    dk_s = dk[row_s:row_s + dbsz * dseq].reshape(dbsz, dseq, DSA_KV_HEADS, DSA_HEAD_DIM)
    dv_s = dv[row_s:row_s + dbsz * dseq].reshape(dbsz, dseq, DSA_KV_HEADS, DSA_HEAD_DIM)
    ki_s = ki[row_s:row_s + dbsz * dseq].reshape(dbsz, dseq, IDX_DIM)
    kpad = jnp.zeros((dbsz, nk_s - n_keys_s, DSA_KV_HEADS, DSA_HEAD_DIM), F32)
    k_all = jnp.concatenate([cache_k[0], dk_s, kpad], axis=1)
    v_all = jnp.concatenate([cache_v[0], dv_s, kpad], axis=1)
    ki_all = jnp.concatenate([cache_idx_k[0], ki_s, jnp.zeros((dbsz, nk_s - n_keys_s, IDX_DIM), F32)], axis=1)
    kt_s = jnp.transpose(k_all, (0, 2, 3, 1)).astype(BF16)
    v_s = jnp.transpose(v_all, (0, 2, 1, 3)).astype(BF16)
    kit_s = jnp.transpose(ki_all, (0, 2, 1)).astype(BF16)
    topk_s = min(TOPK_MAX, n_keys_s // 4)
    ob_s = _dsa(pf, pb, kit_s, kt_s, v_s, n_seq=dbsz, row0=row_s, rows_per_seq=dseq, tq=dseq, tk=KEY_TILE,
                n_keys=n_keys_s, topk=topk_s, causal=False)

    out_w = (w_gla[0].astype(BF16), w_dsa[0].astype(BF16), w_out[0].astype(BF16), gate_b[0], lig, lib,
             ln_g[0].reshape(1, D_MODEL), ln_b[0].reshape(1, D_MODEL))
    tm_out = ROW_TILE // 4
    tm_s = math.gcd(tm_out, dbsz * dseq)
    assert row_s % tm_out == 0 and tm_s % 16 == 0
    y_p = _out(x_all, ya_p, ob_p, pb, *out_w, tm_out, alpha, 0)
    y_s = _out(x_all, ya_s, ob_s, pb, *out_w, tm_s, alpha, row_s)

    y_prompt = y_p.reshape(bsz, tp, D_MODEL)[:, N_META:t_p]
    y_sample = y_s.reshape(dbsz, dseq, D_MODEL)
    k_prompt = dk_p[:, :t_p][None]
    v_prompt = dv_p[:, :t_p][None]
    idx_k_prompt = ki_p[:, :t_p][None]
    gla_prompt = jnp.swapaxes(st_p, -1, -2)[None]
    k_sample = dk_s[None]
    v_sample = dv_s[None]
    idx_k_sample = ki_s[None]
    gla_sample = jnp.swapaxes(st_s, -1, -2)[None]
    return (y_prompt, y_sample, k_prompt, v_prompt, idx_k_prompt, gla_prompt,
            k_sample, v_sample, idx_k_sample, gla_sample)


def kernel(x_prompt, x_sample, cache_k, cache_v, cache_idx_k, state_gla, meta, ln_in_g, ln_in_b,
           w_in, gla_w2, gla_gate_b, gla_norm_g, idx_kn_g, idx_kn_b, w_gla, w_dsa, gate_b,
           w_out, ln_g, ln_b):
    return _forward(x_prompt, x_sample, cache_k, cache_v, cache_idx_k, state_gla, meta, ln_in_g, ln_in_b,
                    w_in, gla_w2, gla_gate_b, gla_norm_g, idx_kn_g, idx_kn_b, w_gla, w_dsa, gate_b,
                    w_out, ln_g, ln_b)
```

```python
import functools
import math

import numpy as np
import jax
import jax.numpy as jnp
from jax import lax
from jax.experimental import pallas as pl
from jax.experimental.pallas import tpu as pltpu

F32 = jnp.float32
BF16 = jnp.bfloat16
I32 = jnp.int32

D_MODEL = 1024
CHUNK = 64
N_META = 16
GLA_HEADS = 4
GLA_DK = 128
GLA_DV = 256
GLA_RANK = 16
GLA_TAU = 16.0
GLA_BLOCK = 64
DSA_HEADS = 16
DSA_KV_HEADS = 4
DSA_HEAD_DIM = 64
DSA_GROUP = DSA_HEADS // DSA_KV_HEADS
DSA_SCALE = DSA_HEAD_DIM ** -0.5
IDX_HEADS = 8
IDX_DIM = 64
IDX_W_SCALE = (IDX_HEADS ** -0.5) * (IDX_DIM ** -0.5)
TOPK_MAX = 256
NORM_EPS = 1e-5
GLA_QK = GLA_HEADS * GLA_DK
GLA_V = GLA_HEADS * GLA_DV
DSA_Q = DSA_HEADS * DSA_HEAD_DIM
DSA_KV = DSA_KV_HEADS * DSA_HEAD_DIM
IDX_Q = IDX_HEADS * IDX_DIM
SPLITS = (GLA_QK, GLA_QK, GLA_V, GLA_RANK, GLA_V,
          DSA_Q, DSA_KV, DSA_KV, IDX_Q, IDX_DIM, IDX_HEADS, DSA_Q,
          D_MODEL, D_MODEL)
_OFF = tuple(int(o) for o in np.cumsum((0,) + SPLITS))
(_GQ, _GK, _GV, _GLOW, _GR, _DQ, _DK, _DV, _IQ, _IK, _IW, _DZ, _MA, _MB) = range(14)

LANES = 128
VMEM_LIMIT_BYTES = 56 * 1024 * 1024

F_GQ, F_GK, F_DK, F_DV, F_SMALL = 0, 512, 1024, 1280, 1536
F_COLS = F_SMALL + LANES
SM_IK, SM_GLOW, SM_IW = 0, IDX_DIM, IDX_DIM + GLA_RANK
B_GV, B_GR, B_DQ, B_DZ, B_MA, B_MB, B_IQ, B_PAD = 0, 1024, 2048, 3072, 4096, 5120, 6144, 6656
B_COLS = 7168

ROW_TILE = 2048
KEY_TILE = 512
NEG_BIG = -1e30
INT_MIN = -(2 ** 31)
NEG_INF_KEY = int(np.array(-np.inf, np.float32).view(np.int32)) ^ 0x7FFFFFFF


def _col(w, idx):
    return w[:, _OFF[idx]:_OFF[idx + 1]]


def _layer_norm_rows(x, g, b):
    mu = jnp.mean(x, axis=-1, keepdims=True)
    xc = x - mu
    var = jnp.mean(xc * xc, axis=-1, keepdims=True)
    return xc * lax.rsqrt(var + NORM_EPS) * g + b


def _dot(a, b):
    return jnp.dot(a, b, preferred_element_type=F32)


def _dot_nt(a, b):
    return lax.dot_general(a, b, (((1,), (1,)), ((), ())), preferred_element_type=F32)


def _dot_tn(a, b):
    return lax.dot_general(a, b, (((0,), (0,)), ((), ())), preferred_element_type=F32)


def _proj_f32_kernel(x_ref, g_ref, b_ref, w_ref, kg_ref, kb_ref, o_ref):
    hn = _layer_norm_rows(x_ref[...], g_ref[...], b_ref[...])
    y = _dot(hn.astype(BF16), w_ref[...])
    o_ref[...] = y
    ik = y[:, F_SMALL:F_SMALL + IDX_DIM]
    o_ref[:, F_SMALL:F_SMALL + IDX_DIM] = _layer_norm_rows(ik, kg_ref[...], kb_ref[...])


def _proj_f32(x, g, b, w, kg, kb, tm):
    rows = x.shape[0]
    return pl.pallas_call(
        _proj_f32_kernel,
        grid=(rows // tm,),
        in_specs=[
            pl.BlockSpec((tm, D_MODEL), lambda i: (i, 0)),
            pl.BlockSpec((1, D_MODEL), lambda i: (0, 0)),
            pl.BlockSpec((1, D_MODEL), lambda i: (0, 0)),
            pl.BlockSpec((D_MODEL, F_COLS), lambda i: (0, 0)),
            pl.BlockSpec((1, IDX_DIM), lambda i: (0, 0)),
            pl.BlockSpec((1, IDX_DIM), lambda i: (0, 0)),
        ],
        out_specs=pl.BlockSpec((tm, F_COLS), lambda i: (i, 0)),
        out_shape=jax.ShapeDtypeStruct((rows, F_COLS), F32),
        compiler_params=pltpu.CompilerParams(
            dimension_semantics=("parallel",), vmem_limit_bytes=VMEM_LIMIT_BYTES),
        name="proj_f32",
    )(x, g, b, w, kg, kb)


def _proj_bf16_kernel(x_ref, g_ref, b_ref, w_ref, o_ref, hn_ref):
    @pl.when(pl.program_id(1) == 0)
    def _():
        hn_ref[...] = _layer_norm_rows(x_ref[...], g_ref[...], b_ref[...]).astype(BF16)

    o_ref[...] = _dot(hn_ref[...], w_ref[...]).astype(BF16)


def _proj_bf16(x, g, b, w, tm, tn):
    rows = x.shape[0]
    return pl.pallas_call(
        _proj_bf16_kernel,
        grid=(rows // tm, B_COLS // tn),
        in_specs=[
            pl.BlockSpec((tm, D_MODEL), lambda i, j: (i, 0)),
            pl.BlockSpec((1, D_MODEL), lambda i, j: (0, 0)),
            pl.BlockSpec((1, D_MODEL), lambda i, j: (0, 0)),
            pl.BlockSpec((D_MODEL, tn), lambda i, j: (0, j)),
        ],
        out_specs=pl.BlockSpec((tm, tn), lambda i, j: (i, j)),
        out_shape=jax.ShapeDtypeStruct((rows, B_COLS), BF16),
        scratch_shapes=[pltpu.VMEM((tm, D_MODEL), BF16)],
        compiler_params=pltpu.CompilerParams(
            dimension_semantics=("parallel", "arbitrary"), vmem_limit_bytes=VMEM_LIMIT_BYTES),
        name="proj_bf16",
    )(x, g, b, w)


def _gla_levels(blk):
    levels = []
    s = blk // 2
    while s >= 1:
        levels.append(s)
        s //= 2
    return levels


def _gla_decay_matrix(blk):
    i = np.arange(blk)[:, None]
    t = np.arange(blk)[None, :]
    mats = [(t <= i).astype(np.float32), (t > i).astype(np.float32)]
    for s in _gla_levels(blk):
        mid = (i // (2 * s)) * (2 * s) + s - 1
        lower = (i // s) % 2 == 1
        m = np.where(lower, ((t > mid) & (t <= i)), False).astype(np.float32)
        n = np.where(~lower, ((t > i) & (t <= mid)), False).astype(np.float32)
        mats.append(m + n)
    return np.concatenate(mats, axis=0)


def _gla_kernel(q_ref, k_ref, sm_ref, v_ref, r_ref, c_ref, w2_ref, gb_ref, ng_ref, s0_ref,
                o_ref, st_ref, s_scr, *, blk, n_blk, t_valid):
    ti = pl.program_id(1)
    levels = _gla_levels(blk)

    @pl.when(ti == 0)
    def _():
        s_scr[...] = s0_ref[0]

    hb = GLA_HEADS * blk
    ri = lax.broadcasted_iota(I32, (hb, hb), 0)
    ci = lax.broadcasted_iota(I32, (hb, hb), 1)
    level_id = jnp.where(ri == ci, 0, -1)
    for li, s in enumerate(levels):
        sh = s.bit_length() - 1
        same = (ri >> (sh + 1)) == (ci >> (sh + 1))
        level_id = jnp.where(same & (((ri >> sh) & 1) == 1) & (((ci >> sh) & 1) == 0), 1 + li, level_id)

    def stack(x, width):
        return jnp.concatenate([x[:, h * width:(h + 1) * width] for h in range(GLA_HEADS)], axis=0)
    row_iota = lax.broadcasted_iota(I32, (blk, 1), 0)
    cmat = c_ref[...]
    w2 = w2_ref[...]
    gbias = gb_ref[...]
    ng = ng_ref[...]

    def block(j):
        r0 = pl.multiple_of(j * blk, blk)
        rows = pl.ds(r0, blk)
        valid = (ti * (n_blk * blk) + r0 + row_iota) < t_valid
        gq = q_ref[rows, :] * (GLA_DK ** -0.5)
        gk = jnp.where(valid, k_ref[rows, :], 0.0)
        x = _dot(sm_ref[rows, :].astype(BF16), w2) + gbias
        logf = (jnp.minimum(x, 0.0) - jnp.log1p(jnp.exp(-jnp.abs(x)))) * (1.0 / GLA_TAU)
        logf = jnp.where(valid, logf, 0.0)
        hi = logf.astype(BF16)
        r1 = logf - hi.astype(F32)
        mid = r1.astype(BF16)
        lo = (r1 - mid.astype(F32)).astype(BF16)
        e_all = _dot(cmat, hi) + _dot(cmat, mid) + _dot(cmat, lo)
        qs = stack(gq, GLA_DK)
        ks = stack(gk, GLA_DK)
        vs = stack(v_ref[rows, :], GLA_DV)
        b_s = stack(e_all[0:blk], GLA_DK)
        a = jnp.where(level_id == 0, _dot_nt(qs.astype(BF16), ks.astype(BF16)), 0.0)
        for li in range(len(levels)):
            e = jnp.exp(stack(e_all[(2 + li) * blk:(3 + li) * blk], GLA_DK))
            p = _dot_nt((qs * e).astype(BF16), (ks * e).astype(BF16))
            a = a + jnp.where(level_id == 1 + li, p, 0.0)
        o_intra = _dot(a.astype(BF16), vs)
        qe = (qs * jnp.exp(b_s)).astype(BF16)
        kd = (ks * jnp.exp(stack(e_all[blk:2 * blk], GLA_DK))).astype(BF16)
        for h in range(GLA_HEADS):
            hs = slice(h * blk, (h + 1) * blk)
            vsl = slice(h * GLA_DV, (h + 1) * GLA_DV)
            st = s_scr[h]
            o = _dot_nt(qe[hs], st.astype(BF16)) + o_intra[hs]
            dec = jnp.exp(b_s[(h + 1) * blk - 1:(h + 1) * blk, :])
            s_scr[h] = st * dec + _dot_tn(vs[hs], kd[hs])
            on = o * lax.rsqrt(jnp.mean(o * o, axis=-1, keepdims=True) + NORM_EPS) * ng
            gr = r_ref[rows, vsl].astype(F32)
            o_ref[rows, vsl] = (on * (gr * jax.nn.sigmoid(gr))).astype(BF16)

    n_live = jnp.clip((t_valid - ti * (n_blk * blk) + blk - 1) // blk, 0, n_blk)

    @pl.when(n_live < n_blk)
    def _():
        o_ref[...] = jnp.zeros(o_ref.shape, o_ref.dtype)

    def block_pair(j, carry):
        block(2 * j)
        block(2 * j + 1)
        return carry

    lax.fori_loop(0, n_live // 2, block_pair, 0)

    @pl.when(n_live % 2 == 1)
    def _():
        block(n_live - 1)

    @pl.when(ti == pl.num_programs(1) - 1)
    def _():
        st_ref[0] = s_scr[...]


def _gla(pf, pb, cmat, w2pad, gbias, ng, s0t, *, n_seq, row0, rows_per_seq, t_valid, blk, n_blk):
    rb = blk * n_blk
    steps = rows_per_seq // rb
    base = row0 // rb

    def rmap(c):
        return lambda s, t: (base + s * steps + t, c)

    kern = functools.partial(_gla_kernel, blk=blk, n_blk=n_blk, t_valid=t_valid)
    return pl.pallas_call(
        kern,
        grid=(n_seq, steps),
        in_specs=[
            pl.BlockSpec((rb, GLA_QK), rmap(F_GQ // GLA_QK)),
            pl.BlockSpec((rb, GLA_QK), rmap(F_GK // GLA_QK)),
            pl.BlockSpec((rb, LANES), rmap(F_SMALL // LANES)),
            pl.BlockSpec((rb, GLA_V), rmap(B_GV // GLA_V)),
            pl.BlockSpec((rb, GLA_V), rmap(B_GR // GLA_V)),
            pl.BlockSpec(cmat.shape, lambda s, t: (0, 0)),
            pl.BlockSpec(w2pad.shape, lambda s, t: (0, 0)),
            pl.BlockSpec((1, GLA_QK), lambda s, t: (0, 0)),
            pl.BlockSpec((1, GLA_DV), lambda s, t: (0, 0)),
            pl.BlockSpec((1, GLA_HEADS, GLA_DV, GLA_DK), lambda s, t: (s, 0, 0, 0)),
        ],
        out_specs=[
            pl.BlockSpec((rb, GLA_V), lambda s, t: (s * steps + t, 0)),
            pl.BlockSpec((1, GLA_HEADS, GLA_DV, GLA_DK), lambda s, t: (s, 0, 0, 0)),
        ],
        out_shape=[
            jax.ShapeDtypeStruct((n_seq * rows_per_seq, GLA_V), BF16),
            jax.ShapeDtypeStruct((n_seq, GLA_HEADS, GLA_DV, GLA_DK), F32),
        ],
        scratch_shapes=[pltpu.VMEM((GLA_HEADS, GLA_DV, GLA_DK), F32)],
        compiler_params=pltpu.CompilerParams(
            dimension_semantics=("parallel", "arbitrary"), vmem_limit_bytes=VMEM_LIMIT_BYTES),
        name="gla",
    )(pf, pf, pf, pb, pb, cmat, w2pad, gbias, ng, s0t)


def _dsa_kernel(iq_ref, sm_ref, qd_ref, kit_ref, kt_ref, v_ref, o_ref,
                key_scr, m_scr, l_scr, acc_scr, *, tq, tk, n_keys, topk, causal, n_kb_total):
    qb = pl.program_id(1)
    n_rows_g = DSA_GROUP * tq

    if causal:
        q_chunk_max = (qb * tq + tq - 1 - N_META) >> 6
        limit = jnp.minimum(N_META + CHUNK * (q_chunk_max + 1), n_keys)
        n_kb = (limit + tk - 1) // tk
    else:
        n_kb = n_kb_total

    q_pos = qb * tq + lax.broadcasted_iota(I32, (tq, 1), 0)
    q_chunk = (q_pos - N_META) >> 6
    lane_iota = lax.broadcasted_iota(I32, (1, tk), 1)

    iq = iq_ref[...]
    iq_hm = jnp.concatenate([iq[:, h * IDX_DIM:(h + 1) * IDX_DIM] for h in range(IDX_HEADS)], axis=0)
    w_i = sm_ref[:, SM_IW:SM_IW + IDX_HEADS] * IDX_W_SCALE
    w_cols = [jnp.broadcast_to(w_i[:, h:h + 1], (tq, LANES)) for h in range(IDX_HEADS)]

    def score_block(kb, carry):
        k0 = pl.multiple_of(kb * tk, tk)
        r = _dot(iq_hm, kit_ref[0, :, pl.ds(k0, tk)])
        sc = None
        for h in range(IDX_HEADS):
            rh = jnp.maximum(r[h * tq:(h + 1) * tq, :], 0.0)
            wh = jnp.concatenate([w_cols[h]] * (tk // LANES), axis=1)
            sc = rh * wh if sc is None else sc + rh * wh
        key_pos = k0 + lane_iota
        adm = key_pos < n_keys
        if causal:
            adm = adm & (((key_pos - N_META) >> 6) <= q_chunk)
        sc = jnp.where(adm, sc, -jnp.inf)
        bits = pltpu.bitcast(sc, I32)
        key_scr[:, pl.ds(k0, tk)] = bits ^ ((bits >> 31) & 0x7FFFFFFF)
        return carry

    lax.fori_loop(0, n_kb, score_block, 0)

    n_cols = n_kb * (tk // LANES)

    def count_ge(cand):
        cb = jnp.broadcast_to(cand, (tq, LANES))

        def body(kb, acc):
            for u in range(tk // LANES):
                c0 = pl.multiple_of(kb * tk + u * LANES, LANES)
                acc = acc + jnp.where(key_scr[:, pl.ds(c0, LANES)] >= cb, 1, 0)
            return acc

        acc = lax.fori_loop(0, n_kb, body, jnp.zeros((tq, LANES), I32))
        return jnp.sum(acc, axis=1, keepdims=True)

    bits_per_round = 4

    def search_cond(carry):
        i, lo, cge = carry
        return (i < 32) & (jnp.max(jnp.abs(cge - topk)) > 0)

    def search_round(carry):
        i, lo, cge = carry
        for u in range(bits_per_round):
            cand = lo + jnp.left_shift(jnp.int32(1), 31 - u - i)
            cnt = count_ge(cand)
            take = cnt >= topk
            lo = jnp.where(take, cand, lo)
            cge = jnp.where(take, cnt, cge)
        return i + bits_per_round, lo, cge

    lo0 = jnp.full((tq, 1), INT_MIN, I32)
    cge0 = jnp.zeros((tq, 1), I32) + n_cols * LANES
    _, thr, cge = lax.while_loop(search_cond, search_round, (jnp.int32(0), lo0, cge0))
    need_tie = (cge > topk) & (thr > NEG_INF_KEY)
    thr = jnp.maximum(thr, NEG_INF_KEY + 1)

    @pl.when(jnp.max(need_tie.astype(I32)) > 0)
    def _():
        want = topk - count_ge(thr + 1)
        thr_b = jnp.broadcast_to(thr, (tq, LANES))
        col_iota = lax.broadcasted_iota(I32, (tq, LANES), 1)

        def count_tied_below(m):
            mb = jnp.broadcast_to(m, (tq, LANES))

            def body(c, acc):
                c0 = pl.multiple_of(c * LANES, LANES)
                hit = (key_scr[:, pl.ds(c0, LANES)] == thr_b) & ((c0 + col_iota) < mb)
                return acc + jnp.where(hit, 1, 0)

            acc = lax.fori_loop(0, n_cols, body, jnp.zeros((tq, LANES), I32))
            return jnp.sum(acc, axis=1, keepdims=True)

        n_bits = int(math.ceil(math.log2(n_kb_total * tk + 1)))

        def idx_step(i, m):
            cand = m + jnp.left_shift(jnp.int32(1), n_bits - 1 - i)
            return jnp.where(count_tied_below(cand) <= want, cand, m)

        m_keep = lax.fori_loop(0, n_bits, idx_step, jnp.zeros((tq, 1), I32))
        mk_b = jnp.broadcast_to(jnp.where(need_tie, m_keep, jnp.int32(2 ** 30)), (tq, LANES))

        def demote(c, carry):
            c0 = pl.multiple_of(c * LANES, LANES)
            kv = key_scr[:, pl.ds(c0, LANES)]
            drop = (kv == thr_b) & ((c0 + col_iota) >= mk_b)
            key_scr[:, pl.ds(c0, LANES)] = jnp.where(drop, kv - 1, kv)
            return carry

        lax.fori_loop(0, n_cols, demote, 0)

    qd = qd_ref[...]
    q_groups = []
    for g in range(DSA_KV_HEADS):
        q_groups.append(jnp.concatenate(
            [qd[:, (r * DSA_KV_HEADS + g) * DSA_HEAD_DIM:(r * DSA_KV_HEADS + g + 1) * DSA_HEAD_DIM]
             for r in range(DSA_GROUP)], axis=0))
    m_scr[...] = jnp.full(m_scr.shape, NEG_BIG, F32)
    l_scr[...] = jnp.zeros(l_scr.shape, F32)
    acc_scr[...] = jnp.zeros(acc_scr.shape, F32)
    thr_t = jnp.broadcast_to(thr, (tq, LANES))

    def attend_block(kb, carry):
        k0 = pl.multiple_of(kb * tk, tk)
        sel = key_scr[:, pl.ds(k0, tk)] >= jnp.concatenate([thr_t] * (tk // LANES), axis=1)
        bias = jnp.where(sel, 0.0, NEG_BIG)
        bias_g = jnp.concatenate([bias] * DSA_GROUP, axis=0)
        for g in range(DSA_KV_HEADS):
            s = _dot(q_groups[g], kt_ref[0, g, :, pl.ds(k0, tk)]) + bias_g
            m_old = m_scr[g]
            m_new = jnp.maximum(m_old, jnp.max(s, axis=1, keepdims=True))
            alpha = jnp.exp2(m_old - m_new)
            p = jnp.exp2(s - m_new)
            p = jnp.where(bias_g < 0.0, 0.0, p)
            l_scr[g] = alpha * l_scr[g] + jnp.sum(p, axis=1, keepdims=True)
            acc_scr[g] = alpha * acc_scr[g] + _dot(p.astype(BF16), v_ref[0, g, pl.ds(k0, tk), :])
            m_scr[g] = m_new
        return carry

    lax.fori_loop(0, n_kb, attend_block, 0)

    for g in range(DSA_KV_HEADS):
        og = acc_scr[g] / l_scr[g]
        for r in range(DSA_GROUP):
            c0 = (g * DSA_GROUP + r) * DSA_HEAD_DIM
            o_ref[:, c0:c0 + DSA_HEAD_DIM] = og[r * tq:(r + 1) * tq, :].astype(BF16)


def _dsa(pf, pb, kit, kt, v, *, n_seq, row0, rows_per_seq, tq, tk, n_keys, topk, causal):
    nk_pad = kit.shape[-1]
    n_kb_total = nk_pad // tk
    steps = rows_per_seq // tq
    base = row0 // tq

    def rmap(c):
        return lambda s, t: (base + s * steps + t, c)

    kern = functools.partial(_dsa_kernel, tq=tq, tk=tk, n_keys=n_keys, topk=topk, causal=causal,
                             n_kb_total=n_kb_total)
    single = pl.Buffered(1)
    return pl.pallas_call(
        kern,
        grid=(n_seq, steps),
        in_specs=[
            pl.BlockSpec((tq, IDX_Q), rmap(B_IQ // IDX_Q)),
            pl.BlockSpec((tq, LANES), rmap(F_SMALL // LANES)),
            pl.BlockSpec((tq, DSA_Q), rmap(B_DQ // DSA_Q)),
            pl.BlockSpec((1, IDX_DIM, nk_pad), lambda s, t: (s, 0, 0), pipeline_mode=single),
            pl.BlockSpec((1, DSA_KV_HEADS, DSA_HEAD_DIM, nk_pad), lambda s, t: (s, 0, 0, 0),
                         pipeline_mode=single),
            pl.BlockSpec((1, DSA_KV_HEADS, nk_pad, DSA_HEAD_DIM), lambda s, t: (s, 0, 0, 0),
                         pipeline_mode=single),
        ],
        out_specs=pl.BlockSpec((tq, DSA_Q), lambda s, t: (s * steps + t, 0)),
        out_shape=jax.ShapeDtypeStruct((n_seq * rows_per_seq, DSA_Q), BF16),
        scratch_shapes=[
            pltpu.VMEM((tq, nk_pad), I32),
            pltpu.VMEM((DSA_KV_HEADS, DSA_GROUP * tq, 1), F32),
            pltpu.VMEM((DSA_KV_HEADS, DSA_GROUP * tq, 1), F32),
            pltpu.VMEM((DSA_KV_HEADS, DSA_GROUP * tq, DSA_HEAD_DIM), F32),
        ],
        compiler_params=pltpu.CompilerParams(
            dimension_semantics=("parallel", "arbitrary"), vmem_limit_bytes=VMEM_LIMIT_BYTES),
        name="dsa",
    )(pb, pf, pb, kit, kt, v)


BIT_GROUP = 256
SLAB_ROWS = 128
ATTEND_UNROLL = 4


def _bit_transpose32(words):
    a = list(words)
    mask, j = 0x0000FFFF, 16
    while j:
        k = 0
        while k < 32:
            t = (a[k] ^ lax.shift_right_logical(a[k + j], jnp.int32(j))) & mask
            a[k] = a[k] ^ t
            a[k + j] = a[k + j] ^ (t << j)
            k = (k + j + 1) & ~j
        j >>= 1
        mask ^= (mask << j) & 0xFFFFFFFF
    return a


V_ROWS = 80
V_ONES_ROW = DSA_HEAD_DIM


def _dsa_t_kernel(iq_ref, sm_ref, qd_ref, ki_ref, k_ref, vt_ref, o_ref,
                  key_scr, m_scr, acc_scr, s_scr, plane_scr, active_scr, *, tq, tk, n_keys, topk):
    live = pl.program_id(1) * tq < n_keys

    @pl.when(live)
    def _():
        _dsa_t_body(iq_ref, sm_ref, qd_ref, ki_ref, k_ref, vt_ref, o_ref, key_scr, m_scr, acc_scr, s_scr,
                    plane_scr, active_scr,
                    tq=tq, tk=tk, n_keys=n_keys, topk=topk)

    @pl.when(jnp.logical_not(live))
    def _():
        o_ref[...] = jnp.zeros(o_ref.shape, o_ref.dtype)


def _dsa_t_body(iq_ref, sm_ref, qd_ref, ki_ref, k_ref, vt_ref, o_ref,
                key_scr, m_scr, acc_scr, s_scr, plane_scr, active_scr, *, tq, tk, n_keys, topk):
    qb = pl.program_id(1)
    q0 = qb * tq
    q_chunk_max = (q0 + tq - 1 - N_META) >> 6
    limit = jnp.minimum(N_META + CHUNK * (q_chunk_max + 1), n_keys)
    n_kb = (limit + tk - 1) // tk
    n_full = jnp.minimum(jnp.minimum((q0 + N_META) // tk, n_keys // tk), n_kb)

    q_chunk = (q0 + lax.broadcasted_iota(I32, (1, tq), 1) - N_META) >> 6
    row_iota = lax.broadcasted_iota(I32, (tk, tq), 0)
    plane_rows = key_scr.shape[0] // BIT_GROUP * 8

    @pl.when(qb == 0)
    def _():
        plane_scr[...] = jnp.zeros(plane_scr.shape, I32)

    iq = iq_ref[...]
    iq_hm = jnp.concatenate([iq[:, h * IDX_DIM:(h + 1) * IDX_DIM] for h in range(IDX_HEADS)], axis=0)
    w_t = sm_ref[...].T[SM_IW:SM_IW + IDX_HEADS, :] * IDX_W_SCALE

    def score_block(kb, masked):
        k0 = pl.multiple_of(kb * tk, tk)
        r = _dot_nt(ki_ref[0, pl.ds(k0, tk), :], iq_hm)
        sc = None
        for h in range(IDX_HEADS):
            t = jnp.maximum(r[:, h * tq:(h + 1) * tq], 0.0) * w_t[h:h + 1, :]
            sc = t if sc is None else sc + t
        if masked:
            key_pos = k0 + row_iota
            adm = (key_pos < n_keys) & (((key_pos - N_META) >> 6) <= q_chunk)
            sc = jnp.where(adm, sc, -jnp.inf)
        bits = pltpu.bitcast(sc, I32)
        keys = bits ^ ((bits >> 31) & 0x7FFFFFFF)
        key_scr[pl.ds(k0, tk), :] = keys
        ukeys = keys ^ INT_MIN
        for u in range(tk // BIT_GROUP):
            words = [ukeys[u * BIT_GROUP + 8 * w:u * BIT_GROUP + 8 * w + 8, :] for w in range(32)]
            planes = _bit_transpose32(words)
            g_row = pl.multiple_of((kb * (tk // BIT_GROUP) + u) * 8, 8)
            for w in range(32):
                plane_scr[pl.ds(w * plane_rows + g_row, 8), :] = planes[w]

    def score_full(kb, carry):
        score_block(kb, False)
        return carry

    def score_edge(kb, carry):
        score_block(kb, True)
        return carry

    def score_pair(j, carry):
        score_block(2 * j, False)
        score_block(2 * j + 1, False)
        return carry

    lax.fori_loop(0, n_full // 2, score_pair, 0)
    lax.fori_loop(n_full // 2 * 2, n_full, score_full, 0)
    lax.fori_loop(n_full, n_kb, score_edge, 0)

    n_chunks = n_kb * (tk // LANES)

    sub_per_blk = tk // LANES

    def count_rows(pred):
        def body(kb, acc):
            for u in range(sub_per_blk):
                c0 = pl.multiple_of(kb * tk + u * LANES, LANES)
                acc = acc + jnp.where(pred(key_scr[pl.ds(c0, LANES), :], c0), 1, 0)
            return acc

        acc = lax.fori_loop(0, n_kb, body, jnp.zeros((LANES, tq), I32))
        return jnp.sum(acc, axis=0, keepdims=True)

    def count_ge(cand):
        return count_rows(lambda kv, c0: kv >= cand)

    n_groups = n_kb * (tk // BIT_GROUP)
    n_slabs = (n_groups * 8 + SLAB_ROWS - 1) // SLAB_ROWS
    slab_iota = lax.broadcasted_iota(I32, (SLAB_ROWS, tq), 0)

    def init_active(sl, carry):
        r0 = pl.multiple_of(sl * SLAB_ROWS, SLAB_ROWS)
        active_scr[pl.ds(r0, SLAB_ROWS), :] = jnp.where(r0 + slab_iota < n_groups * 8, -1, 0)
        return carry

    lax.fori_loop(0, n_slabs, init_active, 0)

    def select_pass(i, carry):
        thr_u, c_above, n_act, flip = carry
        cur = i * plane_rows
        prev = cur - plane_rows

        def body(sl, acc):
            r0 = pl.multiple_of(sl * SLAB_ROWS, SLAB_ROWS)
            rows = pl.ds(r0, SLAB_ROWS)
            act = active_scr[rows, :] & (plane_scr[pl.ds(pl.multiple_of(prev + r0, 8), SLAB_ROWS), :] ^ flip)
            active_scr[rows, :] = act
            return acc + lax.population_count(act & plane_scr[pl.ds(pl.multiple_of(cur + r0, 8), SLAB_ROWS), :])

        acc = lax.fori_loop(0, n_slabs, body, jnp.zeros((SLAB_ROWS, tq), I32))
        ones = jnp.sum(acc, axis=0, keepdims=True)
        take = c_above + ones >= topk
        thr_u = thr_u | jnp.where(take, jnp.left_shift(jnp.int32(1), 31 - i), 0)
        c_above = c_above + jnp.where(take, 0, ones)
        n_act = jnp.where(take, ones, n_act - ones)
        return thr_u, c_above, n_act, jnp.where(take, 0, -1)

    n_act = jnp.zeros((1, tq), I32) + n_groups * BIT_GROUP

    def count_top(sl, acc):
        rows = pl.ds(pl.multiple_of(sl * SLAB_ROWS, SLAB_ROWS), SLAB_ROWS)
        return acc + lax.population_count(active_scr[rows, :] & plane_scr[rows, :])

    acc0 = lax.fori_loop(0, n_slabs, count_top, jnp.zeros((SLAB_ROWS, tq), I32))
    ones0 = jnp.sum(acc0, axis=0, keepdims=True)
    take0 = ones0 >= topk
    carry0 = (jnp.where(take0, INT_MIN, 0), jnp.where(take0, 0, ones0),
              jnp.where(take0, ones0, n_act - ones0), jnp.where(take0, 0, -1))
    thr_u, c_above, n_act, _ = lax.fori_loop(1, 32, select_pass, carry0)
    thr = thr_u ^ INT_MIN
    cge = c_above + n_act
    need_tie = (cge > topk) & (thr > NEG_INF_KEY)
    thr = jnp.maximum(thr, NEG_INF_KEY + 1)

    @pl.when(jnp.max(need_tie.astype(I32)) > 0)
    def _():
        want = topk - count_ge(thr + 1)
        sub_iota = lax.broadcasted_iota(I32, (LANES, tq), 0)
        n_bits = int(math.ceil(math.log2(key_scr.shape[0] + 1)))

        def idx_step(i, m):
            cand = m + jnp.left_shift(jnp.int32(1), n_bits - 1 - i)
            tied_below = count_rows(lambda kv, c0: (kv == thr) & ((c0 + sub_iota) < cand))
            return jnp.where(tied_below <= want, cand, m)

        m_keep = lax.fori_loop(0, n_bits, idx_step, jnp.zeros((1, tq), I32))
        m_keep = jnp.where(need_tie, m_keep, jnp.int32(2 ** 30))

        def demote(c, carry):
            c0 = pl.multiple_of(c * LANES, LANES)
            kv = key_scr[pl.ds(c0, LANES), :]
            drop = (kv == thr) & ((c0 + sub_iota) >= m_keep)
            key_scr[pl.ds(c0, LANES), :] = jnp.where(drop, kv - 1, kv)
            return carry

        lax.fori_loop(0, n_chunks, demote, 0)

    qd = qd_ref[...]
    win = lax.broadcasted_iota(I32, (tq, DSA_KV), 1) >> 6
    q_pad = []
    for g in range(DSA_KV_HEADS):
        q_pad.append(jnp.concatenate(
            [jnp.where(win == g, qd[:, r * DSA_KV:(r + 1) * DSA_KV].astype(F32), 0.0).astype(BF16)
             for r in range(DSA_GROUP)], axis=0))
    m_scr[...] = jnp.full(m_scr.shape, NEG_BIG, F32)
    acc_scr[...] = jnp.zeros(acc_scr.shape, F32)
    s_scr[0] = _dot_nt(k_ref[0, pl.ds(0, tk), :], q_pad[0])

    def attend_block(kb):
        k0 = pl.multiple_of(kb * tk, tk)
        bias = jnp.where(key_scr[pl.ds(k0, tk), :] >= thr, 0.0, NEG_BIG)
        bias_g = jnp.concatenate([bias] * DSA_GROUP, axis=1)
        kblk = k_ref[0, pl.ds(k0, tk), :]
        k1 = pl.multiple_of(jnp.minimum(kb + 1, n_kb - 1) * tk, tk)
        for g in range(DSA_KV_HEADS):
            if g + 1 < DSA_KV_HEADS:
                s_scr[(g + 1) % 2] = _dot_nt(kblk, q_pad[g + 1])
            else:
                s_scr[0] = _dot_nt(k_ref[0, pl.ds(k1, tk), :], q_pad[0])
            s = s_scr[g % 2] + bias_g
            m_old = m_scr[g]
            m_new = jnp.maximum(m_old, jnp.max(s, axis=0, keepdims=True))
            alpha = jnp.exp2(m_old[0:1] - m_new[0:1])
            p = jnp.exp2(s - m_new[0:1]).astype(BF16)
            acc_scr[g] = acc_scr[g] * alpha + _dot(vt_ref[0, g, :, pl.ds(k0, tk)], p)
            m_scr[g] = m_new

    def attend_many(j, carry):
        for u in range(ATTEND_UNROLL):
            attend_block(ATTEND_UNROLL * j + u)
        return carry

    def attend_one(kb, carry):
        attend_block(kb)
        return carry

    n_many = n_kb // ATTEND_UNROLL
    lax.fori_loop(0, n_many, attend_many, 0)
    lax.fori_loop(n_many * ATTEND_UNROLL, n_kb, attend_one, 0)

    for g in range(DSA_KV_HEADS):
        acc = acc_scr[g]
        og = acc[0:DSA_HEAD_DIM] / acc[V_ONES_ROW:V_ONES_ROW + 1]
        for r in range(DSA_GROUP):
            c0 = (g * DSA_GROUP + r) * DSA_HEAD_DIM
            o_ref[:, c0:c0 + DSA_HEAD_DIM] = og[:, r * tq:(r + 1) * tq].T.astype(BF16)


def _dsa_t(pf, pb, ki, k, vt, *, n_seq, rows_per_seq, tk, n_keys, topk):
    tq = LANES
    nk_pad = k.shape[1]
    steps = rows_per_seq // tq

    def rmap(c):
        return lambda s, t: (s * steps + t, c)

    kern = functools.partial(_dsa_t_kernel, tq=tq, tk=tk, n_keys=n_keys, topk=topk)
    single = pl.Buffered(1)
    return pl.pallas_call(
        kern,
        grid=(n_seq, steps),
        in_specs=[
            pl.BlockSpec((tq, IDX_Q), rmap(B_IQ // IDX_Q)),
            pl.BlockSpec((tq, LANES), rmap(F_SMALL // LANES)),
            pl.BlockSpec((tq, DSA_Q), rmap(B_DQ // DSA_Q)),
            pl.BlockSpec((1, nk_pad, IDX_DIM), lambda s, t: (s, 0, 0), pipeline_mode=single),
            pl.BlockSpec((1, nk_pad, DSA_KV), lambda s, t: (s, 0, 0), pipeline_mode=single),
            pl.BlockSpec((1, DSA_KV_HEADS, V_ROWS, nk_pad), lambda s, t: (s, 0, 0, 0), pipeline_mode=single),
        ],
        out_specs=pl.BlockSpec((tq, DSA_Q), lambda s, t: (s * steps + t, 0)),
        out_shape=jax.ShapeDtypeStruct((n_seq * rows_per_seq, DSA_Q), BF16),
        scratch_shapes=[
            pltpu.VMEM((nk_pad, tq), I32),
            pltpu.VMEM((DSA_KV_HEADS, 8, DSA_GROUP * tq), F32),
            pltpu.VMEM((DSA_KV_HEADS, V_ROWS, DSA_GROUP * tq), F32),
            pltpu.VMEM((2, tk, DSA_GROUP * tq), F32),
            pltpu.VMEM((nk_pad + SLAB_ROWS, tq), I32),
            pltpu.VMEM((_round_up(nk_pad // BIT_GROUP * 8, SLAB_ROWS), tq), I32),
        ],
        compiler_params=pltpu.CompilerParams(
            dimension_semantics=("parallel", "arbitrary"), vmem_limit_bytes=VMEM_LIMIT_BYTES),
        name="dsa_t",
    )(pb, pf, pb, ki, k, vt)


def _out_kernel(x_ref, ya_ref, ob_ref, z_ref, ma_ref, mb_ref, wg_ref, wd_ref, wo_ref,
                gate_ref, lig_ref, lib_ref, lg_ref, lb_ref, o_ref, *, alpha):
    hn = _layer_norm_rows(x_ref[...], lig_ref[...], lib_ref[...])
    y_a = _dot(ya_ref[...], wg_ref[...])
    z = z_ref[...].astype(F32)
    yb_in = ob_ref[...].astype(F32) * (z * jax.nn.sigmoid(z))
    y_b = _dot(yb_in.astype(BF16), wd_ref[...])
    ga = jax.nn.sigmoid(ma_ref[...].astype(F32) + gate_ref[0:1, :])
    gb = jax.nn.sigmoid(mb_ref[...].astype(F32) + gate_ref[1:2, :])
    merged = ga * y_a + gb * y_b
    y = alpha * hn + _dot(merged.astype(BF16), wo_ref[...])
    o_ref[...] = _layer_norm_rows(y, lg_ref[...], lb_ref[...])


def _out(x, ya, ob, pb, wg, wd, wo, gate, lig, lib, lg, lb, tm, alpha, row0):
    rows = ya.shape[0]
    base = row0 // tm
    row = lambda c: pl.BlockSpec((tm, D_MODEL), lambda i, c=c: (i, c))
    off = lambda c: pl.BlockSpec((tm, D_MODEL), lambda i, c=c: (base + i, c))
    full = lambda a: pl.BlockSpec(a.shape, lambda i: (0, 0))
    return pl.pallas_call(
        functools.partial(_out_kernel, alpha=alpha),
        grid=(rows // tm,),
        in_specs=[off(0), row(0), row(0), off(B_DZ // D_MODEL), off(B_MA // D_MODEL), off(B_MB // D_MODEL),
                  full(wg), full(wd), full(wo), full(gate), full(lig), full(lib), full(lg), full(lb)],
        out_specs=pl.BlockSpec((tm, D_MODEL), lambda i: (i, 0)),
        out_shape=jax.ShapeDtypeStruct((rows, D_MODEL), F32),
        compiler_params=pltpu.CompilerParams(
            dimension_semantics=("parallel",), vmem_limit_bytes=VMEM_LIMIT_BYTES),
        name="out_proj",
    )(x, ya, ob, pb, pb, pb, wg, wd, wo, gate, lig, lib, lg, lb)


def _round_up(a, b):
    return -(-a // b) * b


def _forward(x_prompt, x_sample, cache_k, cache_v, cache_idx_k, state_gla, meta, ln_in_g, ln_in_b,
             w_in, gla_w2, gla_gate_b, gla_norm_g, idx_kn_g, idx_kn_b, w_gla, w_dsa, gate_b,
             w_out, ln_g, ln_b):
    depth = w_in.shape[0]
    assert depth == 1, "single-layer trunk"
    bsz, seq, _ = x_prompt.shape
    dbsz, dseq, _ = x_sample.shape
    past = cache_k.shape[2]
    t_p = N_META + seq
    tq_p = 128
    tp = _round_up(t_p, KEY_TILE)
    assert tp % tq_p == 0 and tp % GLA_BLOCK == 0
    row_s = bsz * tp
    rows = _round_up(row_s + dbsz * dseq, ROW_TILE)
    alpha = (2.0 * depth) ** 0.25

    pieces = []
    for b in range(bsz):
        pieces += [meta.astype(F32), x_prompt[b], jnp.zeros((tp - t_p, D_MODEL), F32)]
    pieces += [x_sample.reshape(dbsz * dseq, D_MODEL), jnp.zeros((rows - row_s - dbsz * dseq, D_MODEL), F32)]
    x_all = jnp.concatenate(pieces, axis=0)

    w = w_in[0]
    w_f = jnp.concatenate([_col(w, _GQ), _col(w, _GK), _col(w, _DK), _col(w, _DV), _col(w, _IK),
                           _col(w, _GLOW), _col(w, _IW),
                           jnp.zeros((D_MODEL, LANES - IDX_DIM - GLA_RANK - IDX_HEADS), F32)], axis=1).astype(BF16)
    w_dq = _col(w, _DQ).reshape(D_MODEL, DSA_KV_HEADS, DSA_GROUP, DSA_HEAD_DIM)
    w_dq = (jnp.swapaxes(w_dq, 1, 2) * (DSA_SCALE * math.log2(math.e))).reshape(D_MODEL, DSA_Q)
    w_b = jnp.concatenate([_col(w, _GV), _col(w, _GR), w_dq, _col(w, _DZ), _col(w, _MA), _col(w, _MB),
                           _col(w, _IQ), jnp.zeros((D_MODEL, B_COLS - B_PAD), F32)], axis=1).astype(BF16)
    lig = ln_in_g.reshape(1, D_MODEL)
    lib = ln_in_b.reshape(1, D_MODEL)

    pf = _proj_f32(x_all, lig, lib, w_f, idx_kn_g[0].reshape(1, IDX_DIM), idx_kn_b[0].reshape(1, IDX_DIM),
                   ROW_TILE // 4)
    pb = _proj_bf16(x_all, lig, lib, w_b, ROW_TILE // 2, 1024)

    w2pad = jnp.zeros((LANES, GLA_QK), F32).at[SM_GLOW:SM_GLOW + GLA_RANK].set(gla_w2[0]).astype(BF16)
    gbias = gla_gate_b[0].reshape(1, GLA_QK)
    ng = gla_norm_g[0].reshape(1, GLA_DV)
    blk_p = min(GLA_BLOCK, t_p)
    cm_p = jnp.asarray(_gla_decay_matrix(blk_p), BF16)
    s0_p = jnp.zeros((bsz, GLA_HEADS, GLA_DV, GLA_DK), F32)
    ya_p, st_p = _gla(pf, pb, cm_p, w2pad, gbias, ng, s0_p, n_seq=bsz, row0=0, rows_per_seq=tp,
                      t_valid=t_p, blk=blk_p, n_blk=KEY_TILE // blk_p)
    blk_s = min(GLA_BLOCK, dseq)
    assert dseq % blk_s == 0 and row_s % dseq == 0
    cm_s = jnp.asarray(_gla_decay_matrix(blk_s), BF16)
    s0_s = jnp.swapaxes(state_gla[0], -1, -2)
    ya_s, st_s = _gla(pf, pb, cm_s, w2pad, gbias, ng, s0_s, n_seq=dbsz, row0=row_s, rows_per_seq=dseq,
                      t_valid=dseq, blk=blk_s, n_blk=dseq // blk_s)

    dk = pf[:, F_DK:F_DK + DSA_KV]
    dv = pf[:, F_DV:F_DV + DSA_KV]
    ki = pf[:, F_SMALL:F_SMALL + IDX_DIM]
    dk_p = dk[:row_s].reshape(bsz, tp, DSA_KV_HEADS, DSA_HEAD_DIM)
    dv_p = dv[:row_s].reshape(bsz, tp, DSA_KV_HEADS, DSA_HEAD_DIM)
    ki_p = ki[:row_s].reshape(bsz, tp, IDX_DIM)
    vt_p = jnp.concatenate([jnp.transpose(dv_p, (0, 2, 3, 1)),
                            jnp.ones((bsz, DSA_KV_HEADS, 1, tp), F32),
                            jnp.zeros((bsz, DSA_KV_HEADS, V_ROWS - DSA_HEAD_DIM - 1, tp), F32)], axis=2).astype(BF16)
    topk_p = min(TOPK_MAX, (t_p - N_META) // 4)
    ob_p = _dsa_t(pf, pb, ki_p.astype(BF16), dk[:row_s].reshape(bsz, tp, DSA_KV).astype(BF16), vt_p,
                  n_seq=bsz, rows_per_seq=tp, tk=KEY_TILE, n_keys=t_p, topk=topk_p)

    n_keys_s = past + dseq
    nk_s = _round_up(n_keys_s, KEY_TILE)
    dk_s = dk[row_s:row_s + dbsz * dseq].reshape(dbsz, dseq, DSA_KV_HEADS, DSA_HEAD_DIM)
    dv_s = dv[row_s:row_s + dbsz * dseq].reshape(dbsz, dseq, DSA_KV_HEADS, DSA_HEAD_DIM)
    ki_s = ki[row_s:row_s + dbsz * dseq].reshape(dbsz, dseq, IDX_DIM)
    kpad = jnp.zeros((dbsz, nk_s - n_keys_s, DSA_KV_HEADS, DSA_HEAD_DIM), F32)
    k_all = jnp.concatenate([cache_k[0], dk_s, kpad], axis=1)
    v_all = jnp.concatenate([cache_v[0], dv_s, kpad], axis=1)
    ki_all = jnp.concatenate([cache_idx_k[0], ki_s, jnp.zeros((dbsz, nk_s - n_keys_s, IDX_DIM), F32)], axis=1)
    kt_s = jnp.transpose(k_all, (0, 2, 3, 1)).astype(BF16)
    v_s = jnp.transpose(v_all, (0, 2, 1, 3)).astype(BF16)
    kit_s = jnp.transpose(ki_all, (0, 2, 1)).astype(BF16)
    topk_s = min(TOPK_MAX, n_keys_s // 4)
    ob_s = _dsa(pf, pb, kit_s, kt_s, v_s, n_seq=dbsz, row0=row_s, rows_per_seq=dseq, tq=dseq, tk=KEY_TILE,
                n_keys=n_keys_s, topk=topk_s, causal=False)

    out_w = (w_gla[0].astype(BF16), w_dsa[0].astype(BF16), w_out[0].astype(BF16), gate_b[0], lig, lib,
             ln_g[0].reshape(1, D_MODEL), ln_b[0].reshape(1, D_MODEL))
    tm_out = ROW_TILE // 4
    tm_s = math.gcd(tm_out, dbsz * dseq)
    assert row_s % tm_out == 0 and tm_s % 16 == 0
    y_p = _out(x_all, ya_p, ob_p, pb, *out_w, tm_out, alpha, 0)
    y_s = _out(x_all, ya_s, ob_s, pb, *out_w, tm_s, alpha, row_s)

    y_prompt = y_p.reshape(bsz, tp, D_MODEL)[:, N_META:t_p]
    y_sample = y_s.reshape(dbsz, dseq, D_MODEL)
    k_prompt = dk_p[:, :t_p][None]
    v_prompt = dv_p[:, :t_p][None]
    idx_k_prompt = ki_p[:, :t_p][None]
    gla_prompt = jnp.swapaxes(st_p, -1, -2)[None]
    k_sample = dk_s[None]
    v_sample = dv_s[None]
    idx_k_sample = ki_s[None]
    gla_sample = jnp.swapaxes(st_s, -1, -2)[None]
    return (y_prompt, y_sample, k_prompt, v_prompt, idx_k_prompt, gla_prompt,
            k_sample, v_sample, idx_k_sample, gla_sample)


def kernel(x_prompt, x_sample, cache_k, cache_v, cache_idx_k, state_gla, meta, ln_in_g, ln_in_b,
           w_in, gla_w2, gla_gate_b, gla_norm_g, idx_kn_g, idx_kn_b, w_gla, w_dsa, gate_b,
           w_out, ln_g, ln_b):
    return _forward(x_prompt, x_sample, cache_k, cache_v, cache_idx_k, state_gla, meta, ln_in_g, ln_in_b,
                    w_in, gla_w2, gla_gate_b, gla_norm_g, idx_kn_g, idx_kn_b, w_gla, w_dsa, gate_b,
                    w_out, ln_g, ln_b)
```

```python
import functools
import math

import numpy as np
import jax
import jax.numpy as jnp
from jax import lax
from jax.experimental import pallas as pl
from jax.experimental.pallas import tpu as pltpu

F32 = jnp.float32
BF16 = jnp.bfloat16
I32 = jnp.int32

D_MODEL = 1024
CHUNK = 64
N_META = 16
GLA_HEADS = 4
GLA_DK = 128
GLA_DV = 256
GLA_RANK = 16
GLA_TAU = 16.0
GLA_BLOCK = 64
DSA_HEADS = 16
DSA_KV_HEADS = 4
DSA_HEAD_DIM = 64
DSA_GROUP = DSA_HEADS // DSA_KV_HEADS
DSA_SCALE = DSA_HEAD_DIM ** -0.5
IDX_HEADS = 8
IDX_DIM = 64
IDX_W_SCALE = (IDX_HEADS ** -0.5) * (IDX_DIM ** -0.5)
TOPK_MAX = 256
NORM_EPS = 1e-5
GLA_QK = GLA_HEADS * GLA_DK
GLA_V = GLA_HEADS * GLA_DV
DSA_Q = DSA_HEADS * DSA_HEAD_DIM
DSA_KV = DSA_KV_HEADS * DSA_HEAD_DIM
IDX_Q = IDX_HEADS * IDX_DIM
SPLITS = (GLA_QK, GLA_QK, GLA_V, GLA_RANK, GLA_V,
          DSA_Q, DSA_KV, DSA_KV, IDX_Q, IDX_DIM, IDX_HEADS, DSA_Q,
          D_MODEL, D_MODEL)
_OFF = tuple(int(o) for o in np.cumsum((0,) + SPLITS))
(_GQ, _GK, _GV, _GLOW, _GR, _DQ, _DK, _DV, _IQ, _IK, _IW, _DZ, _MA, _MB) = range(14)

LANES = 128
VMEM_LIMIT_BYTES = 56 * 1024 * 1024

F_GQ, F_GK, F_DK, F_DV, F_SMALL = 0, 512, 1024, 1280, 1536
F_COLS = F_SMALL + LANES
SM_IK, SM_GLOW, SM_IW = 0, IDX_DIM, IDX_DIM + GLA_RANK
B_GV, B_GR, B_DQ, B_DZ, B_MA, B_MB, B_IQ, B_PAD = 0, 1024, 2048, 3072, 4096, 5120, 6144, 6656
B_COLS = 7168

ROW_TILE = 2048
KEY_TILE = 512
NEG_BIG = -1e30
INT_MIN = -(2 ** 31)
NEG_INF_KEY = int(np.array(-np.inf, np.float32).view(np.int32)) ^ 0x7FFFFFFF


def _col(w, idx):
    return w[:, _OFF[idx]:_OFF[idx + 1]]


def _layer_norm_rows(x, g, b):
    mu = jnp.mean(x, axis=-1, keepdims=True)
    xc = x - mu
    var = jnp.mean(xc * xc, axis=-1, keepdims=True)
    return xc * lax.rsqrt(var + NORM_EPS) * g + b


def _dot(a, b):
    return jnp.dot(a, b, preferred_element_type=F32)


def _dot_nt(a, b):
    return lax.dot_general(a, b, (((1,), (1,)), ((), ())), preferred_element_type=F32)


def _dot_tn(a, b):
    return lax.dot_general(a, b, (((0,), (0,)), ((), ())), preferred_element_type=F32)


def _proj_f32_kernel(x_ref, g_ref, b_ref, w_ref, kg_ref, kb_ref, o_ref):
    hn = _layer_norm_rows(x_ref[...], g_ref[...], b_ref[...])
    y = _dot(hn.astype(BF16), w_ref[...])
    o_ref[...] = y
    ik = y[:, F_SMALL:F_SMALL + IDX_DIM]
    o_ref[:, F_SMALL:F_SMALL + IDX_DIM] = _layer_norm_rows(ik, kg_ref[...], kb_ref[...])


def _proj_f32(x, g, b, w, kg, kb, tm):
    rows = x.shape[0]
    return pl.pallas_call(
        _proj_f32_kernel,
        grid=(rows // tm,),
        in_specs=[
            pl.BlockSpec((tm, D_MODEL), lambda i: (i, 0)),
            pl.BlockSpec((1, D_MODEL), lambda i: (0, 0)),
            pl.BlockSpec((1, D_MODEL), lambda i: (0, 0)),
            pl.BlockSpec((D_MODEL, F_COLS), lambda i: (0, 0)),
            pl.BlockSpec((1, IDX_DIM), lambda i: (0, 0)),
            pl.BlockSpec((1, IDX_DIM), lambda i: (0, 0)),
        ],
        out_specs=pl.BlockSpec((tm, F_COLS), lambda i: (i, 0)),
        out_shape=jax.ShapeDtypeStruct((rows, F_COLS), F32),
        compiler_params=pltpu.CompilerParams(
            dimension_semantics=("parallel",), vmem_limit_bytes=VMEM_LIMIT_BYTES),
        name="proj_f32",
    )(x, g, b, w, kg, kb)


def _proj_bf16_kernel(x_ref, g_ref, b_ref, w_ref, o_ref, hn_ref):
    @pl.when(pl.program_id(1) == 0)
    def _():
        hn_ref[...] = _layer_norm_rows(x_ref[...], g_ref[...], b_ref[...]).astype(BF16)

    o_ref[...] = _dot(hn_ref[...], w_ref[...]).astype(BF16)


def _proj_bf16(x, g, b, w, tm, tn):
    rows = x.shape[0]
    return pl.pallas_call(
        _proj_bf16_kernel,
        grid=(rows // tm, B_COLS // tn),
        in_specs=[
            pl.BlockSpec((tm, D_MODEL), lambda i, j: (i, 0)),
            pl.BlockSpec((1, D_MODEL), lambda i, j: (0, 0)),
            pl.BlockSpec((1, D_MODEL), lambda i, j: (0, 0)),
            pl.BlockSpec((D_MODEL, tn), lambda i, j: (0, j)),
        ],
        out_specs=pl.BlockSpec((tm, tn), lambda i, j: (i, j)),
        out_shape=jax.ShapeDtypeStruct((rows, B_COLS), BF16),
        scratch_shapes=[pltpu.VMEM((tm, D_MODEL), BF16)],
        compiler_params=pltpu.CompilerParams(
            dimension_semantics=("parallel", "arbitrary"), vmem_limit_bytes=VMEM_LIMIT_BYTES),
        name="proj_bf16",
    )(x, g, b, w)


def _gla_levels(blk):
    levels = []
    s = blk // 2
    while s >= 1:
        levels.append(s)
        s //= 2
    return levels


def _gla_decay_matrix(blk):
    i = np.arange(blk)[:, None]
    t = np.arange(blk)[None, :]
    mats = [(t <= i).astype(np.float32), (t > i).astype(np.float32)]
    for s in _gla_levels(blk):
        mid = (i // (2 * s)) * (2 * s) + s - 1
        lower = (i // s) % 2 == 1
        m = np.where(lower, ((t > mid) & (t <= i)), False).astype(np.float32)
        n = np.where(~lower, ((t > i) & (t <= mid)), False).astype(np.float32)
        mats.append(m + n)
    return np.concatenate(mats, axis=0)


def _gla_kernel(q_ref, k_ref, sm_ref, v_ref, r_ref, c_ref, w2_ref, gb_ref, ng_ref, s0_ref,
                o_ref, st_ref, s_scr, *, blk, n_blk, t_valid):
    ti = pl.program_id(1)
    levels = _gla_levels(blk)

    @pl.when(ti == 0)
    def _():
        s_scr[...] = s0_ref[0]

    hb = GLA_HEADS * blk
    ri = lax.broadcasted_iota(I32, (hb, hb), 0)
    ci = lax.broadcasted_iota(I32, (hb, hb), 1)
    level_id = jnp.where(ri == ci, 0, -1)
    for li, s in enumerate(levels):
        sh = s.bit_length() - 1
        same = (ri >> (sh + 1)) == (ci >> (sh + 1))
        level_id = jnp.where(same & (((ri >> sh) & 1) == 1) & (((ci >> sh) & 1) == 0), 1 + li, level_id)

    def stack(x, width):
        return jnp.concatenate([x[:, h * width:(h + 1) * width] for h in range(GLA_HEADS)], axis=0)
    row_iota = lax.broadcasted_iota(I32, (blk, 1), 0)
    cmat = c_ref[...]
    w2 = w2_ref[...]
    gbias = gb_ref[...]
    ng = ng_ref[...]

    def block(j):
        r0 = pl.multiple_of(j * blk, blk)
        rows = pl.ds(r0, blk)
        valid = (ti * (n_blk * blk) + r0 + row_iota) < t_valid
        gq = q_ref[rows, :] * (GLA_DK ** -0.5)
        gk = jnp.where(valid, k_ref[rows, :], 0.0)
        x = _dot(sm_ref[rows, :].astype(BF16), w2) + gbias
        logf = (jnp.minimum(x, 0.0) - jnp.log1p(jnp.exp(-jnp.abs(x)))) * (1.0 / GLA_TAU)
        logf = jnp.where(valid, logf, 0.0)
        hi = logf.astype(BF16)
        r1 = logf - hi.astype(F32)
        mid = r1.astype(BF16)
        lo = (r1 - mid.astype(F32)).astype(BF16)
        e_all = _dot(cmat, jnp.concatenate([hi, mid, lo], axis=0))
        qs = stack(gq, GLA_DK)
        ks = stack(gk, GLA_DK)
        vs = stack(v_ref[rows, :], GLA_DV)
        b_s = stack(e_all[0:blk], GLA_DK)
        a = jnp.where(level_id == 0, _dot_nt(qs.astype(BF16), ks.astype(BF16)), 0.0)
        for li in range(len(levels)):
            e = jnp.exp(stack(e_all[(2 + li) * blk:(3 + li) * blk], GLA_DK))
            p = _dot_nt((qs * e).astype(BF16), (ks * e).astype(BF16))
            a = a + jnp.where(level_id == 1 + li, p, 0.0)
        o_intra = _dot(a.astype(BF16), vs)
        qe = (qs * jnp.exp(b_s)).astype(BF16)
        kd = (ks * jnp.exp(stack(e_all[blk:2 * blk], GLA_DK))).astype(BF16)
        for h in range(GLA_HEADS):
            hs = slice(h * blk, (h + 1) * blk)
            vsl = slice(h * GLA_DV, (h + 1) * GLA_DV)
            st = s_scr[h]
            o = _dot_nt(qe[hs], st.astype(BF16)) + o_intra[hs]
            dec = jnp.exp(b_s[(h + 1) * blk - 1:(h + 1) * blk, :])
            s_scr[h] = st * dec + _dot_tn(vs[hs], kd[hs])
            on = o * lax.rsqrt(jnp.mean(o * o, axis=-1, keepdims=True) + NORM_EPS) * ng
            gr = r_ref[rows, vsl].astype(F32)
            o_ref[rows, vsl] = (on * (gr * jax.nn.sigmoid(gr))).astype(BF16)

    n_live = jnp.clip((t_valid - ti * (n_blk * blk) + blk - 1) // blk, 0, n_blk)

    @pl.when(n_live < n_blk)
    def _():
        o_ref[...] = jnp.zeros(o_ref.shape, o_ref.dtype)

    def block_pair(j, carry):
        block(2 * j)
        block(2 * j + 1)
        return carry

    lax.fori_loop(0, n_live // 2, block_pair, 0)

    @pl.when(n_live % 2 == 1)
    def _():
        block(n_live - 1)

    @pl.when(ti == pl.num_programs(1) - 1)
    def _():
        st_ref[0] = s_scr[...]


def _gla(pf, pb, cmat, w2pad, gbias, ng, s0t, *, n_seq, row0, rows_per_seq, t_valid, blk, n_blk):
    rb = blk * n_blk
    steps = rows_per_seq // rb
    base = row0 // rb

    def rmap(c):
        return lambda s, t: (base + s * steps + t, c)

    kern = functools.partial(_gla_kernel, blk=blk, n_blk=n_blk, t_valid=t_valid)
    return pl.pallas_call(
        kern,
        grid=(n_seq, steps),
        in_specs=[
            pl.BlockSpec((rb, GLA_QK), rmap(F_GQ // GLA_QK)),
            pl.BlockSpec((rb, GLA_QK), rmap(F_GK // GLA_QK)),
            pl.BlockSpec((rb, LANES), rmap(F_SMALL // LANES)),
            pl.BlockSpec((rb, GLA_V), rmap(B_GV // GLA_V)),
            pl.BlockSpec((rb, GLA_V), rmap(B_GR // GLA_V)),
            pl.BlockSpec(cmat.shape, lambda s, t: (0, 0)),
            pl.BlockSpec(w2pad.shape, lambda s, t: (0, 0)),
            pl.BlockSpec((1, GLA_QK), lambda s, t: (0, 0)),
            pl.BlockSpec((1, GLA_DV), lambda s, t: (0, 0)),
            pl.BlockSpec((1, GLA_HEADS, GLA_DV, GLA_DK), lambda s, t: (s, 0, 0, 0)),
        ],
        out_specs=[
            pl.BlockSpec((rb, GLA_V), lambda s, t: (s * steps + t, 0)),
            pl.BlockSpec((1, GLA_HEADS, GLA_DV, GLA_DK), lambda s, t: (s, 0, 0, 0)),
        ],
        out_shape=[
            jax.ShapeDtypeStruct((n_seq * rows_per_seq, GLA_V), BF16),
            jax.ShapeDtypeStruct((n_seq, GLA_HEADS, GLA_DV, GLA_DK), F32),
        ],
        scratch_shapes=[pltpu.VMEM((GLA_HEADS, GLA_DV, GLA_DK), F32)],
        compiler_params=pltpu.CompilerParams(
            dimension_semantics=("parallel", "arbitrary"), vmem_limit_bytes=VMEM_LIMIT_BYTES),
        name="gla",
    )(pf, pf, pf, pb, pb, cmat, w2pad, gbias, ng, s0t)


def _dsa_kernel(iq_ref, sm_ref, qd_ref, kit_ref, kt_ref, v_ref, o_ref,
                key_scr, m_scr, l_scr, acc_scr, *, tq, tk, n_keys, topk, n_kb_total):
    n_kb = n_kb_total
    lane_iota = lax.broadcasted_iota(I32, (1, tk), 1)

    iq = iq_ref[...]
    iq_hm = jnp.concatenate([iq[:, h * IDX_DIM:(h + 1) * IDX_DIM] for h in range(IDX_HEADS)], axis=0)
    w_i = sm_ref[:, SM_IW:SM_IW + IDX_HEADS] * IDX_W_SCALE
    w_cols = [jnp.broadcast_to(w_i[:, h:h + 1], (tq, LANES)) for h in range(IDX_HEADS)]

    def score_block(kb, carry):
        k0 = pl.multiple_of(kb * tk, tk)
        r = _dot(iq_hm, kit_ref[0, :, pl.ds(k0, tk)])
        sc = None
        for h in range(IDX_HEADS):
            rh = jnp.maximum(r[h * tq:(h + 1) * tq, :], 0.0)
            wh = jnp.concatenate([w_cols[h]] * (tk // LANES), axis=1)
            sc = rh * wh if sc is None else sc + rh * wh
        sc = jnp.where(k0 + lane_iota < n_keys, sc, -jnp.inf)
        bits = pltpu.bitcast(sc, I32)
        key_scr[:, pl.ds(k0, tk)] = bits ^ ((bits >> 31) & 0x7FFFFFFF)
        return carry

    lax.fori_loop(0, n_kb, score_block, 0)

    n_cols = n_kb * (tk // LANES)

    def count_ge(cand):
        cb = jnp.broadcast_to(cand, (tq, LANES))

        def body(kb, acc):
            for u in range(tk // LANES):
                c0 = pl.multiple_of(kb * tk + u * LANES, LANES)
                acc = acc + jnp.where(key_scr[:, pl.ds(c0, LANES)] >= cb, 1, 0)
            return acc

        acc = lax.fori_loop(0, n_kb, body, jnp.zeros((tq, LANES), I32))
        return jnp.sum(acc, axis=1, keepdims=True)

    bits_per_round = 4

    def search_cond(carry):
        i, lo, cge = carry
        return (i < 32) & (jnp.max(jnp.abs(cge - topk)) > 0)

    def search_round(carry):
        i, lo, cge = carry
        for u in range(bits_per_round):
            cand = lo + jnp.left_shift(jnp.int32(1), 31 - u - i)
            cnt = count_ge(cand)
            take = cnt >= topk
            lo = jnp.where(take, cand, lo)
            cge = jnp.where(take, cnt, cge)
        return i + bits_per_round, lo, cge

    lo0 = jnp.full((tq, 1), INT_MIN, I32)
    cge0 = jnp.zeros((tq, 1), I32) + n_cols * LANES
    _, thr, cge = lax.while_loop(search_cond, search_round, (jnp.int32(0), lo0, cge0))
    need_tie = (cge > topk) & (thr > NEG_INF_KEY)
    thr = jnp.maximum(thr, NEG_INF_KEY + 1)

    @pl.when(jnp.max(need_tie.astype(I32)) > 0)
    def _():
        want = topk - count_ge(thr + 1)
        thr_b = jnp.broadcast_to(thr, (tq, LANES))
        col_iota = lax.broadcasted_iota(I32, (tq, LANES), 1)

        def count_tied_below(m):
            mb = jnp.broadcast_to(m, (tq, LANES))

            def body(c, acc):
                c0 = pl.multiple_of(c * LANES, LANES)
                hit = (key_scr[:, pl.ds(c0, LANES)] == thr_b) & ((c0 + col_iota) < mb)
                return acc + jnp.where(hit, 1, 0)

            acc = lax.fori_loop(0, n_cols, body, jnp.zeros((tq, LANES), I32))
            return jnp.sum(acc, axis=1, keepdims=True)

        n_bits = int(math.ceil(math.log2(n_kb_total * tk + 1)))

        def idx_step(i, m):
            cand = m + jnp.left_shift(jnp.int32(1), n_bits - 1 - i)
            return jnp.where(count_tied_below(cand) <= want, cand, m)

        m_keep = lax.fori_loop(0, n_bits, idx_step, jnp.zeros((tq, 1), I32))
        mk_b = jnp.broadcast_to(jnp.where(need_tie, m_keep, jnp.int32(2 ** 30)), (tq, LANES))

        def demote(c, carry):
            c0 = pl.multiple_of(c * LANES, LANES)
            kv = key_scr[:, pl.ds(c0, LANES)]
            drop = (kv == thr_b) & ((c0 + col_iota) >= mk_b)
            key_scr[:, pl.ds(c0, LANES)] = jnp.where(drop, kv - 1, kv)
            return carry

        lax.fori_loop(0, n_cols, demote, 0)

    qd = qd_ref[...]
    win = lax.broadcasted_iota(I32, (tq, DSA_KV), 1) >> 6
    q_all = jnp.concatenate(
        [jnp.where(win == g, qd[:, r * DSA_KV:(r + 1) * DSA_KV].astype(F32), 0.0).astype(BF16)
         for g in range(DSA_KV_HEADS) for r in range(DSA_GROUP)], axis=0)
    m_scr[...] = jnp.full(m_scr.shape, NEG_BIG, F32)
    l_scr[...] = jnp.zeros(l_scr.shape, F32)
    acc_scr[...] = jnp.zeros(acc_scr.shape, F32)
    thr_t = jnp.broadcast_to(thr, (tq, LANES))

    def attend_block(kb, carry):
        k0 = pl.multiple_of(kb * tk, tk)
        sel = key_scr[:, pl.ds(k0, tk)] >= jnp.concatenate([thr_t] * (tk // LANES), axis=1)
        bias = jnp.where(sel, 0.0, NEG_BIG)
        s = _dot(q_all, kt_ref[0, :, pl.ds(k0, tk)]) + jnp.concatenate([bias] * DSA_HEADS, axis=0)
        m_old = m_scr[...]
        m_new = jnp.maximum(m_old, jnp.max(s, axis=1, keepdims=True))
        alpha = jnp.exp2(m_old - m_new)
        p = jnp.exp2(s - m_new)
        l_scr[...] = alpha * l_scr[...] + jnp.sum(p, axis=1, keepdims=True)
        acc_scr[...] = alpha * acc_scr[...] + _dot(p.astype(BF16), v_ref[0, pl.ds(k0, tk), :])
        m_scr[...] = m_new
        return carry

    lax.fori_loop(0, n_kb, attend_block, 0)

    o_all = acc_scr[...] / l_scr[...]
    for g in range(DSA_KV_HEADS):
        for r in range(DSA_GROUP):
            h = g * DSA_GROUP + r
            o_ref[:, h * DSA_HEAD_DIM:(h + 1) * DSA_HEAD_DIM] = (
                o_all[h * tq:(h + 1) * tq, g * DSA_HEAD_DIM:(g + 1) * DSA_HEAD_DIM].astype(BF16))


def _dsa(pf, pb, kit, kt, v, *, n_seq, row0, rows_per_seq, tq, tk, n_keys, topk):
    nk_pad = kit.shape[-1]
    n_kb_total = nk_pad // tk
    steps = rows_per_seq // tq
    base = row0 // tq

    def rmap(c):
        return lambda s, t: (base + s * steps + t, c)

    kern = functools.partial(_dsa_kernel, tq=tq, tk=tk, n_keys=n_keys, topk=topk, n_kb_total=n_kb_total)
    single = pl.Buffered(1)
    return pl.pallas_call(
        kern,
        grid=(n_seq, steps),
        in_specs=[
            pl.BlockSpec((tq, IDX_Q), rmap(B_IQ // IDX_Q)),
            pl.BlockSpec((tq, LANES), rmap(F_SMALL // LANES)),
            pl.BlockSpec((tq, DSA_Q), rmap(B_DQ // DSA_Q)),
            pl.BlockSpec((1, IDX_DIM, nk_pad), lambda s, t: (s, 0, 0), pipeline_mode=single),
            pl.BlockSpec((1, DSA_KV, nk_pad), lambda s, t: (s, 0, 0), pipeline_mode=single),
            pl.BlockSpec((1, nk_pad, DSA_KV), lambda s, t: (s, 0, 0), pipeline_mode=single),
        ],
        out_specs=pl.BlockSpec((tq, DSA_Q), lambda s, t: (s * steps + t, 0)),
        out_shape=jax.ShapeDtypeStruct((n_seq * rows_per_seq, DSA_Q), BF16),
        scratch_shapes=[
            pltpu.VMEM((tq, nk_pad), I32),
            pltpu.VMEM((DSA_HEADS * tq, 1), F32),
            pltpu.VMEM((DSA_HEADS * tq, 1), F32),
            pltpu.VMEM((DSA_HEADS * tq, DSA_KV), F32),
        ],
        compiler_params=pltpu.CompilerParams(
            dimension_semantics=("parallel", "arbitrary"), vmem_limit_bytes=VMEM_LIMIT_BYTES),
        name="dsa",
    )(pb, pf, pb, kit, kt, v)


BIT_GROUP = 256
SLAB_ROWS = 128
ATTEND_UNROLL = 4


def _bit_transpose32(words):
    a = list(words)
    mask, j = 0x0000FFFF, 16
    while j:
        k = 0
        while k < 32:
            t = (a[k] ^ lax.shift_right_logical(a[k + j], jnp.int32(j))) & mask
            a[k] = a[k] ^ t
            a[k + j] = a[k + j] ^ (t << j)
            k = (k + j + 1) & ~j
        j >>= 1
        mask ^= (mask << j) & 0xFFFFFFFF
    return a


V_ROWS = 80
V_ONES_ROW = DSA_HEAD_DIM


def _dsa_t_kernel(iq_ref, sm_ref, qd_ref, ki_ref, k_ref, vt_ref, o_ref,
                  key_scr, m_scr, acc_scr, s_scr, plane_scr, active_scr, *, tq, tk, n_keys, topk):
    live = pl.program_id(1) * tq < n_keys

    @pl.when(live)
    def _():
        _dsa_t_body(iq_ref, sm_ref, qd_ref, ki_ref, k_ref, vt_ref, o_ref, key_scr, m_scr, acc_scr, s_scr,
                    plane_scr, active_scr,
                    tq=tq, tk=tk, n_keys=n_keys, topk=topk)

    @pl.when(jnp.logical_not(live))
    def _():
        o_ref[...] = jnp.zeros(o_ref.shape, o_ref.dtype)


def _dsa_t_body(iq_ref, sm_ref, qd_ref, ki_ref, k_ref, vt_ref, o_ref,
                key_scr, m_scr, acc_scr, s_scr, plane_scr, active_scr, *, tq, tk, n_keys, topk):
    qb = pl.program_id(1)
    q0 = qb * tq
    q_chunk_max = (q0 + tq - 1 - N_META) >> 6
    limit = jnp.minimum(N_META + CHUNK * (q_chunk_max + 1), n_keys)
    n_kb = (limit + tk - 1) // tk
    n_full = jnp.minimum(jnp.minimum((q0 + N_META) // tk, n_keys // tk), n_kb)

    q_chunk = (q0 + lax.broadcasted_iota(I32, (1, tq), 1) - N_META) >> 6
    row_iota = lax.broadcasted_iota(I32, (tk, tq), 0)
    plane_rows = key_scr.shape[0] // BIT_GROUP * 8

    @pl.when(qb == 0)
    def _():
        plane_scr[...] = jnp.zeros(plane_scr.shape, I32)

    iq = iq_ref[...]
    iq_hm = jnp.concatenate([iq[:, h * IDX_DIM:(h + 1) * IDX_DIM] for h in range(IDX_HEADS)], axis=0)
    w_t = sm_ref[...].T[SM_IW:SM_IW + IDX_HEADS, :] * IDX_W_SCALE

    def score_block(kb, masked):
        k0 = pl.multiple_of(kb * tk, tk)
        r = _dot_nt(ki_ref[0, pl.ds(k0, tk), :], iq_hm)
        sc = None
        for h in range(IDX_HEADS):
            t = jnp.maximum(r[:, h * tq:(h + 1) * tq], 0.0) * w_t[h:h + 1, :]
            sc = t if sc is None else sc + t
        if masked:
            key_pos = k0 + row_iota
            adm = (key_pos < n_keys) & (((key_pos - N_META) >> 6) <= q_chunk)
            sc = jnp.where(adm, sc, -jnp.inf)
        bits = pltpu.bitcast(sc, I32)
        keys = bits ^ ((bits >> 31) & 0x7FFFFFFF)
        key_scr[pl.ds(k0, tk), :] = keys
        ukeys = keys ^ INT_MIN
        for u in range(tk // BIT_GROUP):
            words = [ukeys[u * BIT_GROUP + 8 * w:u * BIT_GROUP + 8 * w + 8, :] for w in range(32)]
            planes = _bit_transpose32(words)
            g_row = pl.multiple_of((kb * (tk // BIT_GROUP) + u) * 8, 8)
            for w in range(32):
                plane_scr[pl.ds(w * plane_rows + g_row, 8), :] = planes[w]

    def score_full(kb, carry):
        score_block(kb, False)
        return carry

    def score_edge(kb, carry):
        score_block(kb, True)
        return carry

    def score_pair(j, carry):
        score_block(2 * j, False)
        score_block(2 * j + 1, False)
        return carry

    lax.fori_loop(0, n_full // 2, score_pair, 0)
    lax.fori_loop(n_full // 2 * 2, n_full, score_full, 0)
    lax.fori_loop(n_full, n_kb, score_edge, 0)

    n_chunks = n_kb * (tk // LANES)

    sub_per_blk = tk // LANES

    def count_rows(pred):
        def body(kb, acc):
            for u in range(sub_per_blk):
                c0 = pl.multiple_of(kb * tk + u * LANES, LANES)
                acc = acc + jnp.where(pred(key_scr[pl.ds(c0, LANES), :], c0), 1, 0)
            return acc

        acc = lax.fori_loop(0, n_kb, body, jnp.zeros((LANES, tq), I32))
        return jnp.sum(acc, axis=0, keepdims=True)

    def count_ge(cand):
        return count_rows(lambda kv, c0: kv >= cand)

    n_groups = n_kb * (tk // BIT_GROUP)
    n_slabs = (n_groups * 8 + SLAB_ROWS - 1) // SLAB_ROWS
    slab_iota = lax.broadcasted_iota(I32, (SLAB_ROWS, tq), 0)

    def init_active(sl, carry):
        r0 = pl.multiple_of(sl * SLAB_ROWS, SLAB_ROWS)
        active_scr[pl.ds(r0, SLAB_ROWS), :] = jnp.where(r0 + slab_iota < n_groups * 8, -1, 0)
        return carry

    lax.fori_loop(0, n_slabs, init_active, 0)

    def select_pass(i, carry):
        thr_u, c_above, n_act, flip = carry
        cur = i * plane_rows
        prev = cur - plane_rows

        def body(sl, acc):
            r0 = pl.multiple_of(sl * SLAB_ROWS, SLAB_ROWS)
            rows = pl.ds(r0, SLAB_ROWS)
            act = active_scr[rows, :] & (plane_scr[pl.ds(pl.multiple_of(prev + r0, 8), SLAB_ROWS), :] ^ flip)
            active_scr[rows, :] = act
            return acc + lax.population_count(act & plane_scr[pl.ds(pl.multiple_of(cur + r0, 8), SLAB_ROWS), :])

        acc = lax.fori_loop(0, n_slabs, body, jnp.zeros((SLAB_ROWS, tq), I32))
        ones = jnp.sum(acc, axis=0, keepdims=True)
        take = c_above + ones >= topk
        thr_u = thr_u | jnp.where(take, jnp.left_shift(jnp.int32(1), 31 - i), 0)
        c_above = c_above + jnp.where(take, 0, ones)
        n_act = jnp.where(take, ones, n_act - ones)
        return thr_u, c_above, n_act, jnp.where(take, 0, -1)

    n_act = jnp.zeros((1, tq), I32) + n_groups * BIT_GROUP

    def count_top(sl, acc):
        rows = pl.ds(pl.multiple_of(sl * SLAB_ROWS, SLAB_ROWS), SLAB_ROWS)
        return acc + lax.population_count(active_scr[rows, :] & plane_scr[rows, :])

    acc0 = lax.fori_loop(0, n_slabs, count_top, jnp.zeros((SLAB_ROWS, tq), I32))
    ones0 = jnp.sum(acc0, axis=0, keepdims=True)
    take0 = ones0 >= topk
    carry0 = (jnp.where(take0, INT_MIN, 0), jnp.where(take0, 0, ones0),
              jnp.where(take0, ones0, n_act - ones0), jnp.where(take0, 0, -1))
    thr_u, c_above, n_act, _ = lax.fori_loop(1, 32, select_pass, carry0)
    thr = thr_u ^ INT_MIN
    cge = c_above + n_act
    need_tie = (cge > topk) & (thr > NEG_INF_KEY)
    thr = jnp.maximum(thr, NEG_INF_KEY + 1)

    @pl.when(jnp.max(need_tie.astype(I32)) > 0)
    def _():
        want = topk - count_ge(thr + 1)
        sub_iota = lax.broadcasted_iota(I32, (LANES, tq), 0)
        n_bits = int(math.ceil(math.log2(key_scr.shape[0] + 1)))

        def idx_step(i, m):
            cand = m + jnp.left_shift(jnp.int32(1), n_bits - 1 - i)
            tied_below = count_rows(lambda kv, c0: (kv == thr) & ((c0 + sub_iota) < cand))
            return jnp.where(tied_below <= want, cand, m)

        m_keep = lax.fori_loop(0, n_bits, idx_step, jnp.zeros((1, tq), I32))
        m_keep = jnp.where(need_tie, m_keep, jnp.int32(2 ** 30))

        def demote(c, carry):
            c0 = pl.multiple_of(c * LANES, LANES)
            kv = key_scr[pl.ds(c0, LANES), :]
            drop = (kv == thr) & ((c0 + sub_iota) >= m_keep)
            key_scr[pl.ds(c0, LANES), :] = jnp.where(drop, kv - 1, kv)
            return carry

        lax.fori_loop(0, n_chunks, demote, 0)

    qd = qd_ref[...]
    win = lax.broadcasted_iota(I32, (tq, DSA_KV), 1) >> 6
    q_pad = []
    for g in range(DSA_KV_HEADS):
        q_pad.append(jnp.concatenate(
            [jnp.where(win == g, qd[:, r * DSA_KV:(r + 1) * DSA_KV].astype(F32), 0.0).astype(BF16)
             for r in range(DSA_GROUP)], axis=0))
    m_scr[...] = jnp.full(m_scr.shape, NEG_BIG, F32)
    acc_scr[...] = jnp.zeros(acc_scr.shape, F32)
    s_scr[0] = _dot_nt(k_ref[0, pl.ds(0, tk), :], q_pad[0])

    def attend_block(kb):
        k0 = pl.multiple_of(kb * tk, tk)
        bias = jnp.where(key_scr[pl.ds(k0, tk), :] >= thr, 0.0, NEG_BIG)
        bias_g = jnp.concatenate([bias] * DSA_GROUP, axis=1)
        kblk = k_ref[0, pl.ds(k0, tk), :]
        k1 = pl.multiple_of(jnp.minimum(kb + 1, n_kb - 1) * tk, tk)
        for g in range(DSA_KV_HEADS):
            if g + 1 < DSA_KV_HEADS:
                s_scr[(g + 1) % 2] = _dot_nt(kblk, q_pad[g + 1])
            else:
                s_scr[0] = _dot_nt(k_ref[0, pl.ds(k1, tk), :], q_pad[0])
            s = s_scr[g % 2] + bias_g
            m_old = m_scr[g]
            m_new = jnp.maximum(m_old, jnp.max(s, axis=0, keepdims=True))
            alpha = jnp.exp2(m_old[0:1] - m_new[0:1])
            p = jnp.exp2(s - m_new[0:1]).astype(BF16)
            acc_scr[g] = acc_scr[g] * alpha + _dot(vt_ref[0, g, :, pl.ds(k0, tk)], p)
            m_scr[g] = m_new

    def attend_many(j, carry):
        for u in range(ATTEND_UNROLL):
            attend_block(ATTEND_UNROLL * j + u)
        return carry

    def attend_one(kb, carry):
        attend_block(kb)
        return carry

    n_many = n_kb // ATTEND_UNROLL
    lax.fori_loop(0, n_many, attend_many, 0)
    lax.fori_loop(n_many * ATTEND_UNROLL, n_kb, attend_one, 0)

    for g in range(DSA_KV_HEADS):
        acc = acc_scr[g]
        og = acc[0:DSA_HEAD_DIM] / acc[V_ONES_ROW:V_ONES_ROW + 1]
        for r in range(DSA_GROUP):
            c0 = (g * DSA_GROUP + r) * DSA_HEAD_DIM
            o_ref[:, c0:c0 + DSA_HEAD_DIM] = og[:, r * tq:(r + 1) * tq].T.astype(BF16)


def _dsa_t(pf, pb, ki, k, vt, *, n_seq, rows_per_seq, tk, n_keys, topk):
    tq = LANES
    nk_pad = k.shape[1]
    steps = rows_per_seq // tq

    def rmap(c):
        return lambda s, t: (s * steps + t, c)

    kern = functools.partial(_dsa_t_kernel, tq=tq, tk=tk, n_keys=n_keys, topk=topk)
    single = pl.Buffered(1)
    return pl.pallas_call(
        kern,
        grid=(n_seq, steps),
        in_specs=[
            pl.BlockSpec((tq, IDX_Q), rmap(B_IQ // IDX_Q)),
            pl.BlockSpec((tq, LANES), rmap(F_SMALL // LANES)),
            pl.BlockSpec((tq, DSA_Q), rmap(B_DQ // DSA_Q)),
            pl.BlockSpec((1, nk_pad, IDX_DIM), lambda s, t: (s, 0, 0), pipeline_mode=single),
            pl.BlockSpec((1, nk_pad, DSA_KV), lambda s, t: (s, 0, 0), pipeline_mode=single),
            pl.BlockSpec((1, DSA_KV_HEADS, V_ROWS, nk_pad), lambda s, t: (s, 0, 0, 0), pipeline_mode=single),
        ],
        out_specs=pl.BlockSpec((tq, DSA_Q), lambda s, t: (s * steps + t, 0)),
        out_shape=jax.ShapeDtypeStruct((n_seq * rows_per_seq, DSA_Q), BF16),
        scratch_shapes=[
            pltpu.VMEM((nk_pad, tq), I32),
            pltpu.VMEM((DSA_KV_HEADS, 8, DSA_GROUP * tq), F32),
            pltpu.VMEM((DSA_KV_HEADS, V_ROWS, DSA_GROUP * tq), F32),
            pltpu.VMEM((2, tk, DSA_GROUP * tq), F32),
            pltpu.VMEM((nk_pad + SLAB_ROWS, tq), I32),
            pltpu.VMEM((_round_up(nk_pad // BIT_GROUP * 8, SLAB_ROWS), tq), I32),
        ],
        compiler_params=pltpu.CompilerParams(
            dimension_semantics=("parallel", "arbitrary"), vmem_limit_bytes=VMEM_LIMIT_BYTES),
        name="dsa_t",
    )(pb, pf, pb, ki, k, vt)


def _out_kernel(x_ref, ya_ref, ob_ref, z_ref, ma_ref, mb_ref, wg_ref, wd_ref, wo_ref,
                gate_ref, lig_ref, lib_ref, lg_ref, lb_ref, o_ref, *, alpha):
    hn = _layer_norm_rows(x_ref[...], lig_ref[...], lib_ref[...])
    y_a = _dot(ya_ref[...], wg_ref[...])
    z = z_ref[...].astype(F32)
    yb_in = ob_ref[...].astype(F32) * (z * jax.nn.sigmoid(z))
    y_b = _dot(yb_in.astype(BF16), wd_ref[...])
    ga = jax.nn.sigmoid(ma_ref[...].astype(F32) + gate_ref[0:1, :])
    gb = jax.nn.sigmoid(mb_ref[...].astype(F32) + gate_ref[1:2, :])
    merged = ga * y_a + gb * y_b
    y = alpha * hn + _dot(merged.astype(BF16), wo_ref[...])
    o_ref[...] = _layer_norm_rows(y, lg_ref[...], lb_ref[...])


def _out(x, ya, ob, pb, wg, wd, wo, gate, lig, lib, lg, lb, tm, alpha, row0):
    rows = ya.shape[0]
    base = row0 // tm
    row = lambda c: pl.BlockSpec((tm, D_MODEL), lambda i, c=c: (i, c))
    off = lambda c: pl.BlockSpec((tm, D_MODEL), lambda i, c=c: (base + i, c))
    full = lambda a: pl.BlockSpec(a.shape, lambda i: (0, 0))
    return pl.pallas_call(
        functools.partial(_out_kernel, alpha=alpha),
        grid=(rows // tm,),
        in_specs=[off(0), row(0), row(0), off(B_DZ // D_MODEL), off(B_MA // D_MODEL), off(B_MB // D_MODEL),
                  full(wg), full(wd), full(wo), full(gate), full(lig), full(lib), full(lg), full(lb)],
        out_specs=pl.BlockSpec((tm, D_MODEL), lambda i: (i, 0)),
        out_shape=jax.ShapeDtypeStruct((rows, D_MODEL), F32),
        compiler_params=pltpu.CompilerParams(
            dimension_semantics=("parallel",), vmem_limit_bytes=VMEM_LIMIT_BYTES),
        name="out_proj",
    )(x, ya, ob, pb, pb, pb, wg, wd, wo, gate, lig, lib, lg, lb)


def _round_up(a, b):
    return -(-a // b) * b


def _forward(x_prompt, x_sample, cache_k, cache_v, cache_idx_k, state_gla, meta, ln_in_g, ln_in_b,
             w_in, gla_w2, gla_gate_b, gla_norm_g, idx_kn_g, idx_kn_b, w_gla, w_dsa, gate_b,
             w_out, ln_g, ln_b):
    depth = w_in.shape[0]
    assert depth == 1, "single-layer trunk"
    bsz, seq, _ = x_prompt.shape
    dbsz, dseq, _ = x_sample.shape
    past = cache_k.shape[2]
    t_p = N_META + seq
    tq_p = 128
    tp = _round_up(t_p, KEY_TILE)
    assert tp % tq_p == 0 and tp % GLA_BLOCK == 0
    row_s = bsz * tp
    rows = _round_up(row_s + dbsz * dseq, ROW_TILE)
    alpha = (2.0 * depth) ** 0.25

    pieces = []
    for b in range(bsz):
        pieces += [meta.astype(F32), x_prompt[b], jnp.zeros((tp - t_p, D_MODEL), F32)]
    pieces += [x_sample.reshape(dbsz * dseq, D_MODEL), jnp.zeros((rows - row_s - dbsz * dseq, D_MODEL), F32)]
    x_all = jnp.concatenate(pieces, axis=0)

    w = w_in[0]
    w_f = jnp.concatenate([_col(w, _GQ), _col(w, _GK), _col(w, _DK), _col(w, _DV), _col(w, _IK),
                           _col(w, _GLOW), _col(w, _IW),
                           jnp.zeros((D_MODEL, LANES - IDX_DIM - GLA_RANK - IDX_HEADS), F32)], axis=1).astype(BF16)
    w_dq = _col(w, _DQ).reshape(D_MODEL, DSA_KV_HEADS, DSA_GROUP, DSA_HEAD_DIM)
    w_dq = (jnp.swapaxes(w_dq, 1, 2) * (DSA_SCALE * math.log2(math.e))).reshape(D_MODEL, DSA_Q)
    w_b = jnp.concatenate([_col(w, _GV), _col(w, _GR), w_dq, _col(w, _DZ), _col(w, _MA), _col(w, _MB),
                           _col(w, _IQ), jnp.zeros((D_MODEL, B_COLS - B_PAD), F32)], axis=1).astype(BF16)
    lig = ln_in_g.reshape(1, D_MODEL)
    lib = ln_in_b.reshape(1, D_MODEL)

    pf = _proj_f32(x_all, lig, lib, w_f, idx_kn_g[0].reshape(1, IDX_DIM), idx_kn_b[0].reshape(1, IDX_DIM),
                   ROW_TILE // 4)
    pb = _proj_bf16(x_all, lig, lib, w_b, ROW_TILE // 2, 1024)

    w2pad = jnp.zeros((LANES, GLA_QK), F32).at[SM_GLOW:SM_GLOW + GLA_RANK].set(gla_w2[0]).astype(BF16)
    gbias = gla_gate_b[0].reshape(1, GLA_QK)
    ng = gla_norm_g[0].reshape(1, GLA_DV)
    blk_p = min(GLA_BLOCK, t_p)
    cm_p = jnp.asarray(np.tile(_gla_decay_matrix(blk_p), (1, 3)), BF16)
    s0_p = jnp.zeros((bsz, GLA_HEADS, GLA_DV, GLA_DK), F32)
    ya_p, st_p = _gla(pf, pb, cm_p, w2pad, gbias, ng, s0_p, n_seq=bsz, row0=0, rows_per_seq=tp,
                      t_valid=t_p, blk=blk_p, n_blk=KEY_TILE // blk_p)
    blk_s = min(GLA_BLOCK, dseq)
    assert dseq % blk_s == 0 and row_s % dseq == 0
    cm_s = jnp.asarray(np.tile(_gla_decay_matrix(blk_s), (1, 3)), BF16)
    s0_s = jnp.swapaxes(state_gla[0], -1, -2)
    ya_s, st_s = _gla(pf, pb, cm_s, w2pad, gbias, ng, s0_s, n_seq=dbsz, row0=row_s, rows_per_seq=dseq,
                      t_valid=dseq, blk=blk_s, n_blk=dseq // blk_s)

    dk = pf[:, F_DK:F_DK + DSA_KV]
    dv = pf[:, F_DV:F_DV + DSA_KV]
    ki = pf[:, F_SMALL:F_SMALL + IDX_DIM]
    dk_p = dk[:row_s].reshape(bsz, tp, DSA_KV_HEADS, DSA_HEAD_DIM)
    dv_p = dv[:row_s].reshape(bsz, tp, DSA_KV_HEADS, DSA_HEAD_DIM)
    ki_p = ki[:row_s].reshape(bsz, tp, IDX_DIM)
    vt_p = jnp.concatenate([jnp.transpose(dv_p, (0, 2, 3, 1)),
                            jnp.ones((bsz, DSA_KV_HEADS, 1, tp), F32),
                            jnp.zeros((bsz, DSA_KV_HEADS, V_ROWS - DSA_HEAD_DIM - 1, tp), F32)], axis=2).astype(BF16)
    topk_p = min(TOPK_MAX, (t_p - N_META) // 4)
    ob_p = _dsa_t(pf, pb, ki_p.astype(BF16), dk[:row_s].reshape(bsz, tp, DSA_KV).astype(BF16), vt_p,
                  n_seq=bsz, rows_per_seq=tp, tk=KEY_TILE, n_keys=t_p, topk=topk_p)

    n_keys_s = past + dseq
    nk_s = _round_up(n_keys_s, KEY_TILE)
    dk_s = dk[row_s:row_s + dbsz * dseq].reshape(dbsz, dseq, DSA_KV_HEADS, DSA_HEAD_DIM)
    dv_s = dv[row_s:row_s + dbsz * dseq].reshape(dbsz, dseq, DSA_KV_HEADS, DSA_HEAD_DIM)
    ki_s = ki[row_s:row_s + dbsz * dseq].reshape(dbsz, dseq, IDX_DIM)
    kpad = jnp.zeros((dbsz, nk_s - n_keys_s, DSA_KV_HEADS, DSA_HEAD_DIM), F32)
    k_all = jnp.concatenate([cache_k[0], dk_s, kpad], axis=1)
    v_all = jnp.concatenate([cache_v[0], dv_s, kpad], axis=1)
    ki_all = jnp.concatenate([cache_idx_k[0], ki_s, jnp.zeros((dbsz, nk_s - n_keys_s, IDX_DIM), F32)], axis=1)
    kt_s = jnp.transpose(k_all, (0, 2, 3, 1)).reshape(dbsz, DSA_KV, nk_s).astype(BF16)
    v_s = v_all.reshape(dbsz, nk_s, DSA_KV).astype(BF16)
    kit_s = jnp.transpose(ki_all, (0, 2, 1)).astype(BF16)
    topk_s = min(TOPK_MAX, n_keys_s // 4)
    ob_s = _dsa(pf, pb, kit_s, kt_s, v_s, n_seq=dbsz, row0=row_s, rows_per_seq=dseq, tq=dseq, tk=KEY_TILE,
                n_keys=n_keys_s, topk=topk_s)

    out_w = (w_gla[0].astype(BF16), w_dsa[0].astype(BF16), w_out[0].astype(BF16), gate_b[0], lig, lib,
             ln_g[0].reshape(1, D_MODEL), ln_b[0].reshape(1, D_MODEL))
    tm_out = ROW_TILE // 4
    tm_s = math.gcd(tm_out, dbsz * dseq)
    assert row_s % tm_out == 0 and tm_s % 16 == 0
    y_p = _out(x_all, ya_p, ob_p, pb, *out_w, tm_out, alpha, 0)
    y_s = _out(x_all, ya_s, ob_s, pb, *out_w, tm_s, alpha, row_s)

    y_prompt = y_p.reshape(bsz, tp, D_MODEL)[:, N_META:t_p]
    y_sample = y_s.reshape(dbsz, dseq, D_MODEL)
    k_prompt = dk_p[:, :t_p][None]
    v_prompt = dv_p[:, :t_p][None]
    idx_k_prompt = ki_p[:, :t_p][None]
    gla_prompt = jnp.swapaxes(st_p, -1, -2)[None]
    k_sample = dk_s[None]
    v_sample = dv_s[None]
    idx_k_sample = ki_s[None]
    gla_sample = jnp.swapaxes(st_s, -1, -2)[None]
    return (y_prompt, y_sample, k_prompt, v_prompt, idx_k_prompt, gla_prompt,
            k_sample, v_sample, idx_k_sample, gla_sample)


def kernel(x_prompt, x_sample, cache_k, cache_v, cache_idx_k, state_gla, meta, ln_in_g, ln_in_b,
           w_in, gla_w2, gla_gate_b, gla_norm_g, idx_kn_g, idx_kn_b, w_gla, w_dsa, gate_b,
           w_out, ln_g, ln_b):
    return _forward(x_prompt, x_sample, cache_k, cache_v, cache_idx_k, state_gla, meta, ln_in_g, ln_in_b,
                    w_in, gla_w2, gla_gate_b, gla_norm_g, idx_kn_g, idx_kn_b, w_gla, w_dsa, gate_b,
                    w_out, ln_g, ln_b)
```

```python
import functools
import math

import numpy as np
import jax
import jax.numpy as jnp
from jax import lax
from jax.experimental import pallas as pl
from jax.experimental.pallas import tpu as pltpu

F32 = jnp.float32
BF16 = jnp.bfloat16
I32 = jnp.int32

D_MODEL = 1024
CHUNK = 64
N_META = 16
GLA_HEADS = 4
GLA_DK = 128
GLA_DV = 256
GLA_RANK = 16
GLA_TAU = 16.0
GLA_BLOCK = 64
DSA_HEADS = 16
DSA_KV_HEADS = 4
DSA_HEAD_DIM = 64
DSA_GROUP = DSA_HEADS // DSA_KV_HEADS
DSA_SCALE = DSA_HEAD_DIM ** -0.5
IDX_HEADS = 8
IDX_DIM = 64
IDX_W_SCALE = (IDX_HEADS ** -0.5) * (IDX_DIM ** -0.5)
TOPK_MAX = 256
NORM_EPS = 1e-5
GLA_QK = GLA_HEADS * GLA_DK
GLA_V = GLA_HEADS * GLA_DV
DSA_Q = DSA_HEADS * DSA_HEAD_DIM
DSA_KV = DSA_KV_HEADS * DSA_HEAD_DIM
IDX_Q = IDX_HEADS * IDX_DIM
SPLITS = (GLA_QK, GLA_QK, GLA_V, GLA_RANK, GLA_V,
          DSA_Q, DSA_KV, DSA_KV, IDX_Q, IDX_DIM, IDX_HEADS, DSA_Q,
          D_MODEL, D_MODEL)
_OFF = tuple(int(o) for o in np.cumsum((0,) + SPLITS))
(_GQ, _GK, _GV, _GLOW, _GR, _DQ, _DK, _DV, _IQ, _IK, _IW, _DZ, _MA, _MB) = range(14)

LANES = 128
VMEM_LIMIT_BYTES = 56 * 1024 * 1024

F_GQ, F_GK, F_DK, F_DV, F_SMALL = 0, 512, 1024, 1280, 1536
F_COLS = F_SMALL + LANES
SM_IK, SM_GLOW, SM_IW = 0, IDX_DIM, IDX_DIM + GLA_RANK
B_GV, B_GR, B_DQ, B_DZ, B_MA, B_MB, B_IQ = 0, 1024, 2048, 3072, 4096, 5120, 6144
B_COLS = B_IQ + IDX_Q
B_COL_TILE = B_COLS // 2

ROW_TILE = 2048
KEY_TILE = 512
NEG_BIG = -1e30
INT_MIN = -(2 ** 31)
NEG_INF_KEY = int(np.array(-np.inf, np.float32).view(np.int32)) ^ 0x7FFFFFFF


def _col(w, idx):
    return w[:, _OFF[idx]:_OFF[idx + 1]]


def _layer_norm_rows(x, g, b):
    mu = jnp.mean(x, axis=-1, keepdims=True)
    xc = x - mu
    var = jnp.mean(xc * xc, axis=-1, keepdims=True)
    return xc * lax.rsqrt(var + NORM_EPS) * g + b


def _dot(a, b):
    return jnp.dot(a, b, preferred_element_type=F32)


def _dot_nt(a, b):
    return lax.dot_general(a, b, (((1,), (1,)), ((), ())), preferred_element_type=F32)


def _dot_tn(a, b):
    return lax.dot_general(a, b, (((0,), (0,)), ((), ())), preferred_element_type=F32)


def _proj_f32_kernel(x_ref, g_ref, b_ref, w_ref, kg_ref, kb_ref, o_ref):
    hn = _layer_norm_rows(x_ref[...], g_ref[...], b_ref[...])
    y = _dot(hn.astype(BF16), w_ref[...])
    o_ref[...] = y
    ik = y[:, F_SMALL:F_SMALL + IDX_DIM]
    o_ref[:, F_SMALL:F_SMALL + IDX_DIM] = _layer_norm_rows(ik, kg_ref[...], kb_ref[...])


def _proj_f32(x, g, b, w, kg, kb, tm):
    rows = x.shape[0]
    return pl.pallas_call(
        _proj_f32_kernel,
        grid=(rows // tm,),
        in_specs=[
            pl.BlockSpec((tm, D_MODEL), lambda i: (i, 0)),
            pl.BlockSpec((1, D_MODEL), lambda i: (0, 0)),
            pl.BlockSpec((1, D_MODEL), lambda i: (0, 0)),
            pl.BlockSpec((D_MODEL, F_COLS), lambda i: (0, 0)),
            pl.BlockSpec((1, IDX_DIM), lambda i: (0, 0)),
            pl.BlockSpec((1, IDX_DIM), lambda i: (0, 0)),
        ],
        out_specs=pl.BlockSpec((tm, F_COLS), lambda i: (i, 0)),
        out_shape=jax.ShapeDtypeStruct((rows, F_COLS), F32),
        compiler_params=pltpu.CompilerParams(
            dimension_semantics=("parallel",), vmem_limit_bytes=VMEM_LIMIT_BYTES),
        name="proj_f32",
    )(x, g, b, w, kg, kb)


def _proj_bf16_kernel(x_ref, g_ref, b_ref, w_ref, o_ref, hn_ref):
    @pl.when(pl.program_id(1) == 0)
    def _():
        hn_ref[...] = _layer_norm_rows(x_ref[...], g_ref[...], b_ref[...]).astype(BF16)

    o_ref[...] = _dot(hn_ref[...], w_ref[...]).astype(BF16)


def _proj_bf16(x, g, b, w, tm, tn):
    rows = x.shape[0]
    return pl.pallas_call(
        _proj_bf16_kernel,
        grid=(rows // tm, B_COLS // tn),
        in_specs=[
            pl.BlockSpec((tm, D_MODEL), lambda i, j: (i, 0)),
            pl.BlockSpec((1, D_MODEL), lambda i, j: (0, 0)),
            pl.BlockSpec((1, D_MODEL), lambda i, j: (0, 0)),
            pl.BlockSpec((D_MODEL, tn), lambda i, j: (0, j)),
        ],
        out_specs=pl.BlockSpec((tm, tn), lambda i, j: (i, j)),
        out_shape=jax.ShapeDtypeStruct((rows, B_COLS), BF16),
        scratch_shapes=[pltpu.VMEM((tm, D_MODEL), BF16)],
        compiler_params=pltpu.CompilerParams(
            dimension_semantics=("parallel", "arbitrary"), vmem_limit_bytes=VMEM_LIMIT_BYTES),
        name="proj_bf16",
    )(x, g, b, w)


def _gla_levels(blk):
    levels = []
    s = blk // 2
    while s >= 1:
        levels.append(s)
        s //= 2
    return levels


def _gla_decay_matrix(blk):
    i = np.arange(blk)[:, None]
    t = np.arange(blk)[None, :]
    mats = [(t <= i).astype(np.float32), (t > i).astype(np.float32)]
    for s in _gla_levels(blk):
        mid = (i // (2 * s)) * (2 * s) + s - 1
        lower = (i // s) % 2 == 1
        m = np.where(lower, ((t > mid) & (t <= i)), False).astype(np.float32)
        n = np.where(~lower, ((t > i) & (t <= mid)), False).astype(np.float32)
        mats.append(m + n)
    return np.concatenate(mats, axis=0)


def _gla_kernel(q_ref, k_ref, sm_ref, v_ref, r_ref, c_ref, w2_ref, gb_ref, ng_ref, s0_ref,
                o_ref, st_ref, s_scr, *, blk, n_blk, t_valid):
    ti = pl.program_id(1)
    levels = _gla_levels(blk)

    @pl.when(ti == 0)
    def _():
        s_scr[...] = s0_ref[0]

    hb = GLA_HEADS * blk
    ri = lax.broadcasted_iota(I32, (hb, hb), 0)
    ci = lax.broadcasted_iota(I32, (hb, hb), 1)
    level_id = jnp.where(ri == ci, 0, -1)
    for li, s in enumerate(levels):
        sh = s.bit_length() - 1
        same = (ri >> (sh + 1)) == (ci >> (sh + 1))
        level_id = jnp.where(same & (((ri >> sh) & 1) == 1) & (((ci >> sh) & 1) == 0), 1 + li, level_id)

    def stack(x, width):
        return jnp.concatenate([x[:, h * width:(h + 1) * width] for h in range(GLA_HEADS)], axis=0)
    row_iota = lax.broadcasted_iota(I32, (blk, 1), 0)
    cmat = c_ref[...]
    w2 = w2_ref[...]
    gbias = gb_ref[...]
    ng = ng_ref[...]

    def block(j):
        r0 = pl.multiple_of(j * blk, blk)
        rows = pl.ds(r0, blk)
        valid = (ti * (n_blk * blk) + r0 + row_iota) < t_valid
        gq = q_ref[rows, :] * (GLA_DK ** -0.5)
        gk = jnp.where(valid, k_ref[rows, :], 0.0)
        x = _dot(sm_ref[rows, :].astype(BF16), w2) + gbias
        logf = (jnp.minimum(x, 0.0) - jnp.log1p(jnp.exp(-jnp.abs(x)))) * (1.0 / GLA_TAU)
        logf = jnp.where(valid, logf, 0.0)
        hi = logf.astype(BF16)
        r1 = logf - hi.astype(F32)
        mid = r1.astype(BF16)
        lo = (r1 - mid.astype(F32)).astype(BF16)
        e_all = _dot(cmat, jnp.concatenate([hi, mid, lo], axis=0))
        qs = stack(gq, GLA_DK)
        ks = stack(gk, GLA_DK)
        vs = stack(v_ref[rows, :], GLA_DV)
        b_s = stack(e_all[0:blk], GLA_DK)
        a = jnp.where(level_id == 0, _dot_nt(qs.astype(BF16), ks.astype(BF16)), 0.0)
        for li in range(len(levels)):
            e = jnp.exp(stack(e_all[(2 + li) * blk:(3 + li) * blk], GLA_DK))
            p = _dot_nt((qs * e).astype(BF16), (ks * e).astype(BF16))
            a = a + jnp.where(level_id == 1 + li, p, 0.0)
        o_intra = _dot(a.astype(BF16), vs)
        qe = (qs * jnp.exp(b_s)).astype(BF16)
        kd = (ks * jnp.exp(stack(e_all[blk:2 * blk], GLA_DK))).astype(BF16)
        for h in range(GLA_HEADS):
            hs = slice(h * blk, (h + 1) * blk)
            vsl = slice(h * GLA_DV, (h + 1) * GLA_DV)
            st = s_scr[h]
            o = _dot_nt(qe[hs], st.astype(BF16)) + o_intra[hs]
            dec = jnp.exp(b_s[(h + 1) * blk - 1:(h + 1) * blk, :])
            s_scr[h] = st * dec + _dot_tn(vs[hs], kd[hs])
            on = o * lax.rsqrt(jnp.mean(o * o, axis=-1, keepdims=True) + NORM_EPS) * ng
            gr = r_ref[rows, vsl].astype(F32)
            o_ref[rows, vsl] = (on * (gr * jax.nn.sigmoid(gr))).astype(BF16)

    n_live = jnp.clip((t_valid - ti * (n_blk * blk) + blk - 1) // blk, 0, n_blk)

    @pl.when(n_live < n_blk)
    def _():
        o_ref[...] = jnp.zeros(o_ref.shape, o_ref.dtype)

    def block_pair(j, carry):
        block(2 * j)
        block(2 * j + 1)
        return carry

    lax.fori_loop(0, n_live // 2, block_pair, 0)

    @pl.when(n_live % 2 == 1)
    def _():
        block(n_live - 1)

    @pl.when(ti == pl.num_programs(1) - 1)
    def _():
        st_ref[0] = s_scr[...]


def _gla(pf, pb, cmat, w2pad, gbias, ng, s0t, *, n_seq, row0, rows_per_seq, t_valid, blk, n_blk):
    rb = blk * n_blk
    steps = rows_per_seq // rb
    base = row0 // rb

    def rmap(c):
        return lambda s, t: (base + s * steps + t, c)

    kern = functools.partial(_gla_kernel, blk=blk, n_blk=n_blk, t_valid=t_valid)
    return pl.pallas_call(
        kern,
        grid=(n_seq, steps),
        in_specs=[
            pl.BlockSpec((rb, GLA_QK), rmap(F_GQ // GLA_QK)),
            pl.BlockSpec((rb, GLA_QK), rmap(F_GK // GLA_QK)),
            pl.BlockSpec((rb, LANES), rmap(F_SMALL // LANES)),
            pl.BlockSpec((rb, GLA_V), rmap(B_GV // GLA_V)),
            pl.BlockSpec((rb, GLA_V), rmap(B_GR // GLA_V)),
            pl.BlockSpec(cmat.shape, lambda s, t: (0, 0)),
            pl.BlockSpec(w2pad.shape, lambda s, t: (0, 0)),
            pl.BlockSpec((1, GLA_QK), lambda s, t: (0, 0)),
            pl.BlockSpec((1, GLA_DV), lambda s, t: (0, 0)),
            pl.BlockSpec((1, GLA_HEADS, GLA_DV, GLA_DK), lambda s, t: (s, 0, 0, 0)),
        ],
        out_specs=[
            pl.BlockSpec((rb, GLA_V), lambda s, t: (s * steps + t, 0)),
            pl.BlockSpec((1, GLA_HEADS, GLA_DV, GLA_DK), lambda s, t: (s, 0, 0, 0)),
        ],
        out_shape=[
            jax.ShapeDtypeStruct((n_seq * rows_per_seq, GLA_V), BF16),
            jax.ShapeDtypeStruct((n_seq, GLA_HEADS, GLA_DV, GLA_DK), F32),
        ],
        scratch_shapes=[pltpu.VMEM((GLA_HEADS, GLA_DV, GLA_DK), F32)],
        compiler_params=pltpu.CompilerParams(
            dimension_semantics=("parallel", "arbitrary"), vmem_limit_bytes=VMEM_LIMIT_BYTES),
        name="gla",
    )(pf, pf, pf, pb, pb, cmat, w2pad, gbias, ng, s0t)


def _dsa_kernel(iq_ref, sm_ref, qd_ref, kit_ref, kt_ref, v_ref, o_ref,
                key_scr, m_scr, l_scr, acc_scr, *, tq, tk, n_keys, topk, n_kb_total):
    n_kb = n_kb_total
    lane_iota = lax.broadcasted_iota(I32, (1, tk), 1)

    iq = iq_ref[...]
    iq_hm = jnp.concatenate([iq[:, h * IDX_DIM:(h + 1) * IDX_DIM] for h in range(IDX_HEADS)], axis=0)
    w_i = sm_ref[:, SM_IW:SM_IW + IDX_HEADS] * IDX_W_SCALE
    w_cols = [jnp.broadcast_to(w_i[:, h:h + 1], (tq, LANES)) for h in range(IDX_HEADS)]

    def score_block(kb, carry):
        k0 = pl.multiple_of(kb * tk, tk)
        r = _dot(iq_hm, kit_ref[0, :, pl.ds(k0, tk)])
        sc = None
        for h in range(IDX_HEADS):
            rh = jnp.maximum(r[h * tq:(h + 1) * tq, :], 0.0)
            wh = jnp.concatenate([w_cols[h]] * (tk // LANES), axis=1)
            sc = rh * wh if sc is None else sc + rh * wh
        sc = jnp.where(k0 + lane_iota < n_keys, sc, -jnp.inf)
        bits = pltpu.bitcast(sc, I32)
        key_scr[:, pl.ds(k0, tk)] = bits ^ ((bits >> 31) & 0x7FFFFFFF)
        return carry

    lax.fori_loop(0, n_kb, score_block, 0)

    n_cols = n_kb * (tk // LANES)

    def count_ge(cand):
        cb = jnp.broadcast_to(cand, (tq, LANES))

        def body(kb, acc):
            for u in range(tk // LANES):
                c0 = pl.multiple_of(kb * tk + u * LANES, LANES)
                acc = acc + jnp.where(key_scr[:, pl.ds(c0, LANES)] >= cb, 1, 0)
            return acc

        acc = lax.fori_loop(0, n_kb, body, jnp.zeros((tq, LANES), I32))
        return jnp.sum(acc, axis=1, keepdims=True)

    bits_per_round = 4

    def search_cond(carry):
        i, lo, cge = carry
        return (i < 32) & (jnp.max(jnp.abs(cge - topk)) > 0)

    def search_round(carry):
        i, lo, cge = carry
        for u in range(bits_per_round):
            cand = lo + jnp.left_shift(jnp.int32(1), 31 - u - i)
            cnt = count_ge(cand)
            take = cnt >= topk
            lo = jnp.where(take, cand, lo)
            cge = jnp.where(take, cnt, cge)
        return i + bits_per_round, lo, cge

    lo0 = jnp.full((tq, 1), INT_MIN, I32)
    cge0 = jnp.zeros((tq, 1), I32) + n_cols * LANES
    _, thr, cge = lax.while_loop(search_cond, search_round, (jnp.int32(0), lo0, cge0))
    need_tie = (cge > topk) & (thr > NEG_INF_KEY)
    thr = jnp.maximum(thr, NEG_INF_KEY + 1)

    @pl.when(jnp.max(need_tie.astype(I32)) > 0)
    def _():
        want = topk - count_ge(thr + 1)
        thr_b = jnp.broadcast_to(thr, (tq, LANES))
        col_iota = lax.broadcasted_iota(I32, (tq, LANES), 1)

        def count_tied_below(m):
            mb = jnp.broadcast_to(m, (tq, LANES))

            def body(c, acc):
                c0 = pl.multiple_of(c * LANES, LANES)
                hit = (key_scr[:, pl.ds(c0, LANES)] == thr_b) & ((c0 + col_iota) < mb)
                return acc + jnp.where(hit, 1, 0)

            acc = lax.fori_loop(0, n_cols, body, jnp.zeros((tq, LANES), I32))
            return jnp.sum(acc, axis=1, keepdims=True)

        n_bits = int(math.ceil(math.log2(n_kb_total * tk + 1)))

        def idx_step(i, m):
            cand = m + jnp.left_shift(jnp.int32(1), n_bits - 1 - i)
            return jnp.where(count_tied_below(cand) <= want, cand, m)

        m_keep = lax.fori_loop(0, n_bits, idx_step, jnp.zeros((tq, 1), I32))
        mk_b = jnp.broadcast_to(jnp.where(need_tie, m_keep, jnp.int32(2 ** 30)), (tq, LANES))

        def demote(c, carry):
            c0 = pl.multiple_of(c * LANES, LANES)
            kv = key_scr[:, pl.ds(c0, LANES)]
            drop = (kv == thr_b) & ((c0 + col_iota) >= mk_b)
            key_scr[:, pl.ds(c0, LANES)] = jnp.where(drop, kv - 1, kv)
            return carry

        lax.fori_loop(0, n_cols, demote, 0)

    qd = qd_ref[...]
    win = lax.broadcasted_iota(I32, (tq, DSA_KV), 1) >> 6
    q_all = jnp.concatenate(
        [jnp.where(win == g, qd[:, r * DSA_KV:(r + 1) * DSA_KV].astype(F32), 0.0).astype(BF16)
         for g in range(DSA_KV_HEADS) for r in range(DSA_GROUP)], axis=0)
    m_scr[...] = jnp.full(m_scr.shape, NEG_BIG, F32)
    l_scr[...] = jnp.zeros(l_scr.shape, F32)
    acc_scr[...] = jnp.zeros(acc_scr.shape, F32)
    thr_t = jnp.broadcast_to(thr, (tq, LANES))

    def attend_block(kb, carry):
        k0 = pl.multiple_of(kb * tk, tk)
        sel = key_scr[:, pl.ds(k0, tk)] >= jnp.concatenate([thr_t] * (tk // LANES), axis=1)
        bias = jnp.where(sel, 0.0, NEG_BIG)
        s = _dot(q_all, kt_ref[0, :, pl.ds(k0, tk)]) + jnp.concatenate([bias] * DSA_HEADS, axis=0)
        m_old = m_scr[...]
        m_new = jnp.maximum(m_old, jnp.max(s, axis=1, keepdims=True))
        alpha = jnp.exp2(m_old - m_new)
        p = jnp.exp2(s - m_new)
        l_scr[...] = alpha * l_scr[...] + jnp.sum(p, axis=1, keepdims=True)
        acc_scr[...] = alpha * acc_scr[...] + _dot(p.astype(BF16), v_ref[0, pl.ds(k0, tk), :])
        m_scr[...] = m_new
        return carry

    lax.fori_loop(0, n_kb, attend_block, 0)

    o_all = acc_scr[...] / l_scr[...]
    for g in range(DSA_KV_HEADS):
        for r in range(DSA_GROUP):
            h = g * DSA_GROUP + r
            o_ref[:, h * DSA_HEAD_DIM:(h + 1) * DSA_HEAD_DIM] = (
                o_all[h * tq:(h + 1) * tq, g * DSA_HEAD_DIM:(g + 1) * DSA_HEAD_DIM].astype(BF16))


def _dsa(pf, pb, kit, kt, v, *, n_seq, row0, rows_per_seq, tq, tk, n_keys, topk):
    nk_pad = kit.shape[-1]
    n_kb_total = nk_pad // tk
    steps = rows_per_seq // tq
    base = row0 // tq

    def rmap(c):
        return lambda s, t: (base + s * steps + t, c)

    kern = functools.partial(_dsa_kernel, tq=tq, tk=tk, n_keys=n_keys, topk=topk, n_kb_total=n_kb_total)
    single = pl.Buffered(1)
    return pl.pallas_call(
        kern,
        grid=(n_seq, steps),
        in_specs=[
            pl.BlockSpec((tq, IDX_Q), rmap(B_IQ // IDX_Q)),
            pl.BlockSpec((tq, LANES), rmap(F_SMALL // LANES)),
            pl.BlockSpec((tq, DSA_Q), rmap(B_DQ // DSA_Q)),
            pl.BlockSpec((1, IDX_DIM, nk_pad), lambda s, t: (s, 0, 0), pipeline_mode=single),
            pl.BlockSpec((1, DSA_KV, nk_pad), lambda s, t: (s, 0, 0), pipeline_mode=single),
            pl.BlockSpec((1, nk_pad, DSA_KV), lambda s, t: (s, 0, 0), pipeline_mode=single),
        ],
        out_specs=pl.BlockSpec((tq, DSA_Q), lambda s, t: (s * steps + t, 0)),
        out_shape=jax.ShapeDtypeStruct((n_seq * rows_per_seq, DSA_Q), BF16),
        scratch_shapes=[
            pltpu.VMEM((tq, nk_pad), I32),
            pltpu.VMEM((DSA_HEADS * tq, 1), F32),
            pltpu.VMEM((DSA_HEADS * tq, 1), F32),
            pltpu.VMEM((DSA_HEADS * tq, DSA_KV), F32),
        ],
        compiler_params=pltpu.CompilerParams(
            dimension_semantics=("parallel", "arbitrary"), vmem_limit_bytes=VMEM_LIMIT_BYTES),
        name="dsa",
    )(pb, pf, pb, kit, kt, v)


BIT_GROUP = 256
SLAB_ROWS = 128
ATTEND_UNROLL = 4


def _bit_transpose32(words):
    a = list(words)
    mask, j = 0x0000FFFF, 16
    while j:
        k = 0
        while k < 32:
            t = (a[k] ^ lax.shift_right_logical(a[k + j], jnp.int32(j))) & mask
            a[k] = a[k] ^ t
            a[k + j] = a[k + j] ^ (t << j)
            k = (k + j + 1) & ~j
        j >>= 1
        mask ^= (mask << j) & 0xFFFFFFFF
    return a


V_ROWS = 80
V_ONES_ROW = DSA_HEAD_DIM


def _dsa_t_kernel(iq_ref, sm_ref, qd_ref, ki_ref, k_ref, vt_ref, o_ref,
                  key_scr, m_scr, acc_scr, s_scr, plane_scr, active_scr, *, tq, tk, n_keys, topk):
    live = pl.program_id(1) * tq < n_keys

    @pl.when(live)
    def _():
        _dsa_t_body(iq_ref, sm_ref, qd_ref, ki_ref, k_ref, vt_ref, o_ref, key_scr, m_scr, acc_scr, s_scr,
                    plane_scr, active_scr,
                    tq=tq, tk=tk, n_keys=n_keys, topk=topk)

    @pl.when(jnp.logical_not(live))
    def _():
        o_ref[...] = jnp.zeros(o_ref.shape, o_ref.dtype)


def _dsa_t_body(iq_ref, sm_ref, qd_ref, ki_ref, k_ref, vt_ref, o_ref,
                key_scr, m_scr, acc_scr, s_scr, plane_scr, active_scr, *, tq, tk, n_keys, topk):
    qb = pl.program_id(1)
    q0 = qb * tq
    q_chunk_max = (q0 + tq - 1 - N_META) >> 6
    limit = jnp.minimum(N_META + CHUNK * (q_chunk_max + 1), n_keys)
    n_kb = (limit + tk - 1) // tk
    n_full = jnp.minimum(jnp.minimum((q0 + N_META) // tk, n_keys // tk), n_kb)

    q_chunk = (q0 + lax.broadcasted_iota(I32, (1, tq), 1) - N_META) >> 6
    row_iota = lax.broadcasted_iota(I32, (tk, tq), 0)
    plane_rows = key_scr.shape[0] // BIT_GROUP * 8

    @pl.when(qb == 0)
    def _():
        plane_scr[...] = jnp.zeros(plane_scr.shape, I32)

    iq = iq_ref[...]
    iq_hm = jnp.concatenate([iq[:, h * IDX_DIM:(h + 1) * IDX_DIM] for h in range(IDX_HEADS)], axis=0)
    w_t = sm_ref[...].T[SM_IW:SM_IW + IDX_HEADS, :] * IDX_W_SCALE

    def score_block(kb, masked):
        k0 = pl.multiple_of(kb * tk, tk)
        r = _dot_nt(ki_ref[0, pl.ds(k0, tk), :], iq_hm)
        sc = None
        for h in range(IDX_HEADS):
            t = jnp.maximum(r[:, h * tq:(h + 1) * tq], 0.0) * w_t[h:h + 1, :]
            sc = t if sc is None else sc + t
        if masked:
            key_pos = k0 + row_iota
            adm = (key_pos < n_keys) & (((key_pos - N_META) >> 6) <= q_chunk)
            sc = jnp.where(adm, sc, -jnp.inf)
        bits = pltpu.bitcast(sc, I32)
        keys = bits ^ ((bits >> 31) & 0x7FFFFFFF)
        key_scr[pl.ds(k0, tk), :] = keys
        ukeys = keys ^ INT_MIN
        for u in range(tk // BIT_GROUP):
            words = [ukeys[u * BIT_GROUP + 8 * w:u * BIT_GROUP + 8 * w + 8, :] for w in range(32)]
            planes = _bit_transpose32(words)
            g_row = pl.multiple_of((kb * (tk // BIT_GROUP) + u) * 8, 8)
            for w in range(32):
                plane_scr[pl.ds(w * plane_rows + g_row, 8), :] = planes[w]

    def score_full(kb, carry):
        score_block(kb, False)
        return carry

    def score_edge(kb, carry):
        score_block(kb, True)
        return carry

    def score_pair(j, carry):
        score_block(2 * j, False)
        score_block(2 * j + 1, False)
        return carry

    lax.fori_loop(0, n_full // 2, score_pair, 0)
    lax.fori_loop(n_full // 2 * 2, n_full, score_full, 0)
    lax.fori_loop(n_full, n_kb, score_edge, 0)

    n_chunks = n_kb * (tk // LANES)

    sub_per_blk = tk // LANES

    def count_rows(pred):
        def body(kb, acc):
            for u in range(sub_per_blk):
                c0 = pl.multiple_of(kb * tk + u * LANES, LANES)
                acc = acc + jnp.where(pred(key_scr[pl.ds(c0, LANES), :], c0), 1, 0)
            return acc

        acc = lax.fori_loop(0, n_kb, body, jnp.zeros((LANES, tq), I32))
        return jnp.sum(acc, axis=0, keepdims=True)

    def count_ge(cand):
        return count_rows(lambda kv, c0: kv >= cand)

    n_groups = n_kb * (tk // BIT_GROUP)
    n_slabs = (n_groups * 8 + SLAB_ROWS - 1) // SLAB_ROWS
    slab_iota = lax.broadcasted_iota(I32, (SLAB_ROWS, tq), 0)

    def init_active(sl, carry):
        r0 = pl.multiple_of(sl * SLAB_ROWS, SLAB_ROWS)
        active_scr[pl.ds(r0, SLAB_ROWS), :] = jnp.where(r0 + slab_iota < n_groups * 8, -1, 0)
        return carry

    lax.fori_loop(0, n_slabs, init_active, 0)

    def select_pass(i, carry):
        thr_u, c_above, n_act, flip = carry
        cur = i * plane_rows
        prev = cur - plane_rows

        def body(sl, acc):
            r0 = pl.multiple_of(sl * SLAB_ROWS, SLAB_ROWS)
            rows = pl.ds(r0, SLAB_ROWS)
            act = active_scr[rows, :] & (plane_scr[pl.ds(pl.multiple_of(prev + r0, 8), SLAB_ROWS), :] ^ flip)
            active_scr[rows, :] = act
            return acc + lax.population_count(act & plane_scr[pl.ds(pl.multiple_of(cur + r0, 8), SLAB_ROWS), :])

        acc = lax.fori_loop(0, n_slabs, body, jnp.zeros((SLAB_ROWS, tq), I32))
        ones = jnp.sum(acc, axis=0, keepdims=True)
        take = c_above + ones >= topk
        thr_u = thr_u | jnp.where(take, jnp.left_shift(jnp.int32(1), 31 - i), 0)
        c_above = c_above + jnp.where(take, 0, ones)
        n_act = jnp.where(take, ones, n_act - ones)
        return thr_u, c_above, n_act, jnp.where(take, 0, -1)

    n_act = jnp.zeros((1, tq), I32) + n_groups * BIT_GROUP

    def count_top(sl, acc):
        rows = pl.ds(pl.multiple_of(sl * SLAB_ROWS, SLAB_ROWS), SLAB_ROWS)
        return acc + lax.population_count(active_scr[rows, :] & plane_scr[rows, :])

    acc0 = lax.fori_loop(0, n_slabs, count_top, jnp.zeros((SLAB_ROWS, tq), I32))
    ones0 = jnp.sum(acc0, axis=0, keepdims=True)
    take0 = ones0 >= topk
    carry0 = (jnp.where(take0, INT_MIN, 0), jnp.where(take0, 0, ones0),
              jnp.where(take0, ones0, n_act - ones0), jnp.where(take0, 0, -1))
    thr_u, c_above, n_act, _ = lax.fori_loop(1, 32, select_pass, carry0)
    thr = thr_u ^ INT_MIN
    cge = c_above + n_act
    need_tie = (cge > topk) & (thr > NEG_INF_KEY)
    thr = jnp.maximum(thr, NEG_INF_KEY + 1)

    @pl.when(jnp.max(need_tie.astype(I32)) > 0)
    def _():
        want = topk - count_ge(thr + 1)
        sub_iota = lax.broadcasted_iota(I32, (LANES, tq), 0)
        n_bits = int(math.ceil(math.log2(key_scr.shape[0] + 1)))

        def idx_step(i, m):
            cand = m + jnp.left_shift(jnp.int32(1), n_bits - 1 - i)
            tied_below = count_rows(lambda kv, c0: (kv == thr) & ((c0 + sub_iota) < cand))
            return jnp.where(tied_below <= want, cand, m)

        m_keep = lax.fori_loop(0, n_bits, idx_step, jnp.zeros((1, tq), I32))
        m_keep = jnp.where(need_tie, m_keep, jnp.int32(2 ** 30))

        def demote(c, carry):
            c0 = pl.multiple_of(c * LANES, LANES)
            kv = key_scr[pl.ds(c0, LANES), :]
            drop = (kv == thr) & ((c0 + sub_iota) >= m_keep)
            key_scr[pl.ds(c0, LANES), :] = jnp.where(drop, kv - 1, kv)
            return carry

        lax.fori_loop(0, n_chunks, demote, 0)

    qd = qd_ref[...]
    win = lax.broadcasted_iota(I32, (tq, DSA_KV), 1) >> 6
    q_pad = []
    for g in range(DSA_KV_HEADS):
        q_pad.append(jnp.concatenate(
            [jnp.where(win == g, qd[:, r * DSA_KV:(r + 1) * DSA_KV].astype(F32), 0.0).astype(BF16)
             for r in range(DSA_GROUP)], axis=0))
    m_scr[...] = jnp.full(m_scr.shape, NEG_BIG, F32)
    acc_scr[...] = jnp.zeros(acc_scr.shape, F32)
    s_scr[0] = _dot_nt(k_ref[0, pl.ds(0, tk), :], q_pad[0])

    def attend_block(kb):
        k0 = pl.multiple_of(kb * tk, tk)
        bias = jnp.where(key_scr[pl.ds(k0, tk), :] >= thr, 0.0, NEG_BIG)
        bias_g = jnp.concatenate([bias] * DSA_GROUP, axis=1)
        kblk = k_ref[0, pl.ds(k0, tk), :]
        k1 = pl.multiple_of(jnp.minimum(kb + 1, n_kb - 1) * tk, tk)
        for g in range(DSA_KV_HEADS):
            if g + 1 < DSA_KV_HEADS:
                s_scr[(g + 1) % 2] = _dot_nt(kblk, q_pad[g + 1])
            else:
                s_scr[0] = _dot_nt(k_ref[0, pl.ds(k1, tk), :], q_pad[0])
            s = s_scr[g % 2] + bias_g
            m_old = m_scr[g]
            m_new = jnp.maximum(m_old, jnp.max(s, axis=0, keepdims=True))
            alpha = jnp.exp2(m_old[0:1] - m_new[0:1])
            p = jnp.exp2(s - m_new[0:1]).astype(BF16)
            acc_scr[g] = acc_scr[g] * alpha + _dot(vt_ref[0, g, :, pl.ds(k0, tk)], p)
            m_scr[g] = m_new

    def attend_many(j, carry):
        for u in range(ATTEND_UNROLL):
            attend_block(ATTEND_UNROLL * j + u)
        return carry

    def attend_one(kb, carry):
        attend_block(kb)
        return carry

    n_many = n_kb // ATTEND_UNROLL
    lax.fori_loop(0, n_many, attend_many, 0)
    lax.fori_loop(n_many * ATTEND_UNROLL, n_kb, attend_one, 0)

    for g in range(DSA_KV_HEADS):
        acc = acc_scr[g]
        og = acc[0:DSA_HEAD_DIM] / acc[V_ONES_ROW:V_ONES_ROW + 1]
        for r in range(DSA_GROUP):
            c0 = (g * DSA_GROUP + r) * DSA_HEAD_DIM
            o_ref[:, c0:c0 + DSA_HEAD_DIM] = og[:, r * tq:(r + 1) * tq].T.astype(BF16)


def _dsa_t(pf, pb, ki, k, vt, *, n_seq, rows_per_seq, tk, n_keys, topk):
    tq = LANES
    nk_pad = k.shape[1]
    steps = rows_per_seq // tq

    def rmap(c):
        return lambda s, t: (s * steps + t, c)

    kern = functools.partial(_dsa_t_kernel, tq=tq, tk=tk, n_keys=n_keys, topk=topk)
    single = pl.Buffered(1)
    return pl.pallas_call(
        kern,
        grid=(n_seq, steps),
        in_specs=[
            pl.BlockSpec((tq, IDX_Q), rmap(B_IQ // IDX_Q)),
            pl.BlockSpec((tq, LANES), rmap(F_SMALL // LANES)),
            pl.BlockSpec((tq, DSA_Q), rmap(B_DQ // DSA_Q)),
            pl.BlockSpec((1, nk_pad, IDX_DIM), lambda s, t: (s, 0, 0), pipeline_mode=single),
            pl.BlockSpec((1, nk_pad, DSA_KV), lambda s, t: (s, 0, 0), pipeline_mode=single),
            pl.BlockSpec((1, DSA_KV_HEADS, V_ROWS, nk_pad), lambda s, t: (s, 0, 0, 0), pipeline_mode=single),
        ],
        out_specs=pl.BlockSpec((tq, DSA_Q), lambda s, t: (s * steps + t, 0)),
        out_shape=jax.ShapeDtypeStruct((n_seq * rows_per_seq, DSA_Q), BF16),
        scratch_shapes=[
            pltpu.VMEM((nk_pad, tq), I32),
            pltpu.VMEM((DSA_KV_HEADS, 8, DSA_GROUP * tq), F32),
            pltpu.VMEM((DSA_KV_HEADS, V_ROWS, DSA_GROUP * tq), F32),
            pltpu.VMEM((2, tk, DSA_GROUP * tq), F32),
            pltpu.VMEM((nk_pad + SLAB_ROWS, tq), I32),
            pltpu.VMEM((_round_up(nk_pad // BIT_GROUP * 8, SLAB_ROWS), tq), I32),
        ],
        compiler_params=pltpu.CompilerParams(
            dimension_semantics=("parallel", "arbitrary"), vmem_limit_bytes=VMEM_LIMIT_BYTES),
        name="dsa_t",
    )(pb, pf, pb, ki, k, vt)


def _out_kernel(x_ref, ya_ref, ob_ref, z_ref, ma_ref, mb_ref, wg_ref, wd_ref, wo_ref,
                gate_ref, lig_ref, lib_ref, lg_ref, lb_ref, o_ref, *, alpha):
    hn = _layer_norm_rows(x_ref[...], lig_ref[...], lib_ref[...])
    y_a = _dot(ya_ref[...], wg_ref[...])
    z = z_ref[...].astype(F32)
    yb_in = ob_ref[...].astype(F32) * (z * jax.nn.sigmoid(z))
    y_b = _dot(yb_in.astype(BF16), wd_ref[...])
    ga = jax.nn.sigmoid(ma_ref[...].astype(F32) + gate_ref[0:1, :])
    gb = jax.nn.sigmoid(mb_ref[...].astype(F32) + gate_ref[1:2, :])
    merged = ga * y_a + gb * y_b
    y = alpha * hn + _dot(merged.astype(BF16), wo_ref[...])
    o_ref[...] = _layer_norm_rows(y, lg_ref[...], lb_ref[...])


def _out(x, ya, ob, pb, wg, wd, wo, gate, lig, lib, lg, lb, tm, alpha, row0):
    rows = ya.shape[0]
    base = row0 // tm
    row = lambda c: pl.BlockSpec((tm, D_MODEL), lambda i, c=c: (i, c))
    off = lambda c: pl.BlockSpec((tm, D_MODEL), lambda i, c=c: (base + i, c))
    full = lambda a: pl.BlockSpec(a.shape, lambda i: (0, 0))
    return pl.pallas_call(
        functools.partial(_out_kernel, alpha=alpha),
        grid=(rows // tm,),
        in_specs=[off(0), row(0), row(0), off(B_DZ // D_MODEL), off(B_MA // D_MODEL), off(B_MB // D_MODEL),
                  full(wg), full(wd), full(wo), full(gate), full(lig), full(lib), full(lg), full(lb)],
        out_specs=pl.BlockSpec((tm, D_MODEL), lambda i: (i, 0)),
        out_shape=jax.ShapeDtypeStruct((rows, D_MODEL), F32),
        compiler_params=pltpu.CompilerParams(
            dimension_semantics=("parallel",), vmem_limit_bytes=VMEM_LIMIT_BYTES),
        name="out_proj",
    )(x, ya, ob, pb, pb, pb, wg, wd, wo, gate, lig, lib, lg, lb)


def _round_up(a, b):
    return -(-a // b) * b


def _forward(x_prompt, x_sample, cache_k, cache_v, cache_idx_k, state_gla, meta, ln_in_g, ln_in_b,
             w_in, gla_w2, gla_gate_b, gla_norm_g, idx_kn_g, idx_kn_b, w_gla, w_dsa, gate_b,
             w_out, ln_g, ln_b):
    depth = w_in.shape[0]
    assert depth == 1, "single-layer trunk"
    bsz, seq, _ = x_prompt.shape
    dbsz, dseq, _ = x_sample.shape
    past = cache_k.shape[2]
    t_p = N_META + seq
    tq_p = 128
    tp = _round_up(t_p, KEY_TILE)
    assert tp % tq_p == 0 and tp % GLA_BLOCK == 0
    row_s = bsz * tp
    rows = _round_up(row_s + dbsz * dseq, ROW_TILE)
    alpha = (2.0 * depth) ** 0.25

    pieces = []
    for b in range(bsz):
        pieces += [meta.astype(F32), x_prompt[b], jnp.zeros((tp - t_p, D_MODEL), F32)]
    pieces += [x_sample.reshape(dbsz * dseq, D_MODEL), jnp.zeros((rows - row_s - dbsz * dseq, D_MODEL), F32)]
    x_all = jnp.concatenate(pieces, axis=0)

    w = w_in[0]
    w_f = jnp.concatenate([_col(w, _GQ), _col(w, _GK), _col(w, _DK), _col(w, _DV), _col(w, _IK),
                           _col(w, _GLOW), _col(w, _IW),
                           jnp.zeros((D_MODEL, LANES - IDX_DIM - GLA_RANK - IDX_HEADS), F32)], axis=1).astype(BF16)
    w_dq = _col(w, _DQ).reshape(D_MODEL, DSA_KV_HEADS, DSA_GROUP, DSA_HEAD_DIM)
    w_dq = (jnp.swapaxes(w_dq, 1, 2) * (DSA_SCALE * math.log2(math.e))).reshape(D_MODEL, DSA_Q)
    w_b = jnp.concatenate([_col(w, _GV), _col(w, _GR), w_dq, _col(w, _DZ), _col(w, _MA), _col(w, _MB),
                           _col(w, _IQ)], axis=1).astype(BF16)
    lig = ln_in_g.reshape(1, D_MODEL)
    lib = ln_in_b.reshape(1, D_MODEL)

    pf = _proj_f32(x_all, lig, lib, w_f, idx_kn_g[0].reshape(1, IDX_DIM), idx_kn_b[0].reshape(1, IDX_DIM),
                   ROW_TILE // 4)
    pb = _proj_bf16(x_all, lig, lib, w_b, ROW_TILE // 4, B_COL_TILE)

    w2pad = jnp.zeros((LANES, GLA_QK), F32).at[SM_GLOW:SM_GLOW + GLA_RANK].set(gla_w2[0]).astype(BF16)
    gbias = gla_gate_b[0].reshape(1, GLA_QK)
    ng = gla_norm_g[0].reshape(1, GLA_DV)
    blk_p = min(GLA_BLOCK, t_p)
    cm_p = jnp.asarray(np.tile(_gla_decay_matrix(blk_p), (1, 3)), BF16)
    s0_p = jnp.zeros((bsz, GLA_HEADS, GLA_DV, GLA_DK), F32)
    ya_p, st_p = _gla(pf, pb, cm_p, w2pad, gbias, ng, s0_p, n_seq=bsz, row0=0, rows_per_seq=tp,
                      t_valid=t_p, blk=blk_p, n_blk=KEY_TILE // blk_p)
    blk_s = min(GLA_BLOCK, dseq)
    assert dseq % blk_s == 0 and row_s % dseq == 0
    cm_s = jnp.asarray(np.tile(_gla_decay_matrix(blk_s), (1, 3)), BF16)
    s0_s = jnp.swapaxes(state_gla[0], -1, -2)
    ya_s, st_s = _gla(pf, pb, cm_s, w2pad, gbias, ng, s0_s, n_seq=dbsz, row0=row_s, rows_per_seq=dseq,
                      t_valid=dseq, blk=blk_s, n_blk=dseq // blk_s)

    dk = pf[:, F_DK:F_DK + DSA_KV]
    dv = pf[:, F_DV:F_DV + DSA_KV]
    ki = pf[:, F_SMALL:F_SMALL + IDX_DIM]
    dk_p = dk[:row_s].reshape(bsz, tp, DSA_KV_HEADS, DSA_HEAD_DIM)
    dv_p = dv[:row_s].reshape(bsz, tp, DSA_KV_HEADS, DSA_HEAD_DIM)
    ki_p = ki[:row_s].reshape(bsz, tp, IDX_DIM)
    vt_p = jnp.concatenate([jnp.transpose(dv_p, (0, 2, 3, 1)),
                            jnp.ones((bsz, DSA_KV_HEADS, 1, tp), F32),
                            jnp.zeros((bsz, DSA_KV_HEADS, V_ROWS - DSA_HEAD_DIM - 1, tp), F32)], axis=2).astype(BF16)
    topk_p = min(TOPK_MAX, (t_p - N_META) // 4)
    ob_p = _dsa_t(pf, pb, ki_p.astype(BF16), dk[:row_s].reshape(bsz, tp, DSA_KV).astype(BF16), vt_p,
                  n_seq=bsz, rows_per_seq=tp, tk=KEY_TILE, n_keys=t_p, topk=topk_p)

    n_keys_s = past + dseq
    nk_s = _round_up(n_keys_s, KEY_TILE)
    dk_s = dk[row_s:row_s + dbsz * dseq].reshape(dbsz, dseq, DSA_KV_HEADS, DSA_HEAD_DIM)
    dv_s = dv[row_s:row_s + dbsz * dseq].reshape(dbsz, dseq, DSA_KV_HEADS, DSA_HEAD_DIM)
    ki_s = ki[row_s:row_s + dbsz * dseq].reshape(dbsz, dseq, IDX_DIM)
    kpad = jnp.zeros((dbsz, nk_s - n_keys_s, DSA_KV_HEADS, DSA_HEAD_DIM), F32)
    k_all = jnp.concatenate([cache_k[0], dk_s, kpad], axis=1)
    v_all = jnp.concatenate([cache_v[0], dv_s, kpad], axis=1)
    ki_all = jnp.concatenate([cache_idx_k[0], ki_s, jnp.zeros((dbsz, nk_s - n_keys_s, IDX_DIM), F32)], axis=1)
    kt_s = jnp.transpose(k_all, (0, 2, 3, 1)).reshape(dbsz, DSA_KV, nk_s).astype(BF16)
    v_s = v_all.reshape(dbsz, nk_s, DSA_KV).astype(BF16)
    kit_s = jnp.transpose(ki_all, (0, 2, 1)).astype(BF16)
    topk_s = min(TOPK_MAX, n_keys_s // 4)
    ob_s = _dsa(pf, pb, kit_s, kt_s, v_s, n_seq=dbsz, row0=row_s, rows_per_seq=dseq, tq=dseq, tk=KEY_TILE,
                n_keys=n_keys_s, topk=topk_s)

    out_w = (w_gla[0].astype(BF16), w_dsa[0].astype(BF16), w_out[0].astype(BF16), gate_b[0], lig, lib,
             ln_g[0].reshape(1, D_MODEL), ln_b[0].reshape(1, D_MODEL))
    tm_out = ROW_TILE // 4
    tm_s = math.gcd(tm_out, dbsz * dseq)
    assert row_s % tm_out == 0 and tm_s % 16 == 0
    y_p = _out(x_all, ya_p, ob_p, pb, *out_w, tm_out, alpha, 0)
    y_s = _out(x_all, ya_s, ob_s, pb, *out_w, tm_s, alpha, row_s)

    y_prompt = y_p.reshape(bsz, tp, D_MODEL)[:, N_META:t_p]
    y_sample = y_s.reshape(dbsz, dseq, D_MODEL)
    k_prompt = dk_p[:, :t_p][None]
    v_prompt = dv_p[:, :t_p][None]
    idx_k_prompt = ki_p[:, :t_p][None]
    gla_prompt = jnp.swapaxes(st_p, -1, -2)[None]
    k_sample = dk_s[None]
    v_sample = dv_s[None]
    idx_k_sample = ki_s[None]
    gla_sample = jnp.swapaxes(st_s, -1, -2)[None]
    return (y_prompt, y_sample, k_prompt, v_prompt, idx_k_prompt, gla_prompt,
            k_sample, v_sample, idx_k_sample, gla_sample)


def kernel(x_prompt, x_sample, cache_k, cache_v, cache_idx_k, state_gla, meta, ln_in_g, ln_in_b,
           w_in, gla_w2, gla_gate_b, gla_norm_g, idx_kn_g, idx_kn_b, w_gla, w_dsa, gate_b,
           w_out, ln_g, ln_b):
    return _forward(x_prompt, x_sample, cache_k, cache_v, cache_idx_k, state_gla, meta, ln_in_g, ln_in_b,
                    w_in, gla_w2, gla_gate_b, gla_norm_g, idx_kn_g, idx_kn_b, w_gla, w_dsa, gate_b,
                    w_out, ln_g, ln_b)
```

```python
import functools
import math

import numpy as np
import jax
import jax.numpy as jnp
from jax import lax
from jax.experimental import pallas as pl
from jax.experimental.pallas import tpu as pltpu

F32 = jnp.float32
BF16 = jnp.bfloat16
I32 = jnp.int32

D_MODEL = 1024
CHUNK = 64
N_META = 16
GLA_HEADS = 4
GLA_DK = 128
GLA_DV = 256
GLA_RANK = 16
GLA_TAU = 16.0
GLA_BLOCK = 64
DSA_HEADS = 16
DSA_KV_HEADS = 4
DSA_HEAD_DIM = 64
DSA_GROUP = DSA_HEADS // DSA_KV_HEADS
DSA_SCALE = DSA_HEAD_DIM ** -0.5
IDX_HEADS = 8
IDX_DIM = 64
IDX_W_SCALE = (IDX_HEADS ** -0.5) * (IDX_DIM ** -0.5)
TOPK_MAX = 256
NORM_EPS = 1e-5
GLA_QK = GLA_HEADS * GLA_DK
GLA_V = GLA_HEADS * GLA_DV
DSA_Q = DSA_HEADS * DSA_HEAD_DIM
DSA_KV = DSA_KV_HEADS * DSA_HEAD_DIM
IDX_Q = IDX_HEADS * IDX_DIM
SPLITS = (GLA_QK, GLA_QK, GLA_V, GLA_RANK, GLA_V,
          DSA_Q, DSA_KV, DSA_KV, IDX_Q, IDX_DIM, IDX_HEADS, DSA_Q,
          D_MODEL, D_MODEL)
_OFF = tuple(int(o) for o in np.cumsum((0,) + SPLITS))
(_GQ, _GK, _GV, _GLOW, _GR, _DQ, _DK, _DV, _IQ, _IK, _IW, _DZ, _MA, _MB) = range(14)

LANES = 128
VMEM_LIMIT_BYTES = 56 * 1024 * 1024

F_GQ, F_GK, F_DK, F_DV, F_SMALL = 0, 512, 1024, 1280, 1536
F_COLS = F_SMALL + LANES
SM_IK, SM_GLOW, SM_IW = 0, IDX_DIM, IDX_DIM + GLA_RANK
B_GV, B_GR, B_DQ, B_DZ, B_MA, B_MB, B_IQ = 0, 1024, 2048, 3072, 4096, 5120, 6144
B_COLS = B_IQ + IDX_Q
B_COL_TILE = B_COLS // 2

ROW_TILE = 2048
KEY_TILE = 512
NEG_BIG = -1e30
INT_MIN = -(2 ** 31)
NEG_INF_KEY = int(np.array(-np.inf, np.float32).view(np.int32)) ^ 0x7FFFFFFF


def _col(w, idx):
    return w[:, _OFF[idx]:_OFF[idx + 1]]


def _layer_norm_rows(x, g, b):
    mu = jnp.mean(x, axis=-1, keepdims=True)
    xc = x - mu
    var = jnp.mean(xc * xc, axis=-1, keepdims=True)
    return xc * lax.rsqrt(var + NORM_EPS) * g + b


def _dot(a, b):
    return jnp.dot(a, b, preferred_element_type=F32)


def _dot_nt(a, b):
    return lax.dot_general(a, b, (((1,), (1,)), ((), ())), preferred_element_type=F32)


def _dot_tn(a, b):
    return lax.dot_general(a, b, (((0,), (0,)), ((), ())), preferred_element_type=F32)


def _proj_f32_kernel(x_ref, g_ref, b_ref, w_ref, kg_ref, kb_ref, o_ref):
    hn = _layer_norm_rows(x_ref[...], g_ref[...], b_ref[...])
    y = _dot(hn.astype(BF16), w_ref[...])
    o_ref[...] = y
    ik = y[:, F_SMALL:F_SMALL + IDX_DIM]
    o_ref[:, F_SMALL:F_SMALL + IDX_DIM] = _layer_norm_rows(ik, kg_ref[...], kb_ref[...])


def _proj_f32(x, g, b, w, kg, kb, tm):
    rows = x.shape[0]
    return pl.pallas_call(
        _proj_f32_kernel,
        grid=(rows // tm,),
        in_specs=[
            pl.BlockSpec((tm, D_MODEL), lambda i: (i, 0)),
            pl.BlockSpec((1, D_MODEL), lambda i: (0, 0)),
            pl.BlockSpec((1, D_MODEL), lambda i: (0, 0)),
            pl.BlockSpec((D_MODEL, F_COLS), lambda i: (0, 0)),
            pl.BlockSpec((1, IDX_DIM), lambda i: (0, 0)),
            pl.BlockSpec((1, IDX_DIM), lambda i: (0, 0)),
        ],
        out_specs=pl.BlockSpec((tm, F_COLS), lambda i: (i, 0)),
        out_shape=jax.ShapeDtypeStruct((rows, F_COLS), F32),
        compiler_params=pltpu.CompilerParams(
            dimension_semantics=("parallel",), vmem_limit_bytes=VMEM_LIMIT_BYTES),
        name="proj_f32",
    )(x, g, b, w, kg, kb)


def _proj_bf16_kernel(x_ref, g_ref, b_ref, w_ref, o_ref, hn_ref):
    @pl.when(pl.program_id(1) == 0)
    def _():
        hn_ref[...] = _layer_norm_rows(x_ref[...], g_ref[...], b_ref[...]).astype(BF16)

    o_ref[...] = _dot(hn_ref[...], w_ref[...]).astype(BF16)


def _proj_bf16(x, g, b, w, tm, tn):
    rows = x.shape[0]
    return pl.pallas_call(
        _proj_bf16_kernel,
        grid=(rows // tm, B_COLS // tn),
        in_specs=[
            pl.BlockSpec((tm, D_MODEL), lambda i, j: (i, 0)),
            pl.BlockSpec((1, D_MODEL), lambda i, j: (0, 0)),
            pl.BlockSpec((1, D_MODEL), lambda i, j: (0, 0)),
            pl.BlockSpec((D_MODEL, tn), lambda i, j: (0, j)),
        ],
        out_specs=pl.BlockSpec((tm, tn), lambda i, j: (i, j)),
        out_shape=jax.ShapeDtypeStruct((rows, B_COLS), BF16),
        scratch_shapes=[pltpu.VMEM((tm, D_MODEL), BF16)],
        compiler_params=pltpu.CompilerParams(
            dimension_semantics=("parallel", "arbitrary"), vmem_limit_bytes=VMEM_LIMIT_BYTES),
        name="proj_bf16",
    )(x, g, b, w)


def _gla_levels(blk):
    levels = []
    s = blk // 2
    while s >= 1:
        levels.append(s)
        s //= 2
    return levels


def _gla_decay_matrix(blk):
    i = np.arange(blk)[:, None]
    t = np.arange(blk)[None, :]
    mats = [(t <= i).astype(np.float32), (t > i).astype(np.float32)]
    for s in _gla_levels(blk):
        mid = (i // (2 * s)) * (2 * s) + s - 1
        lower = (i // s) % 2 == 1
        m = np.where(lower, ((t > mid) & (t <= i)), False).astype(np.float32)
        n = np.where(~lower, ((t > i) & (t <= mid)), False).astype(np.float32)
        mats.append(m + n)
    return np.concatenate(mats, axis=0)


def _gla_kernel(q_ref, k_ref, sm_ref, v_ref, r_ref, c_ref, w2_ref, gb_ref, ng_ref, s0_ref,
                o_ref, st_ref, s_scr, *, blk, n_blk, t_valid):
    ti = pl.program_id(1)
    levels = _gla_levels(blk)

    @pl.when(ti == 0)
    def _():
        s_scr[...] = s0_ref[0]

    hb = GLA_HEADS * blk
    ri = lax.broadcasted_iota(I32, (hb, hb), 0)
    ci = lax.broadcasted_iota(I32, (hb, hb), 1)
    level_id = jnp.where(ri == ci, 0, -1)
    for li, s in enumerate(levels):
        sh = s.bit_length() - 1
        same = (ri >> (sh + 1)) == (ci >> (sh + 1))
        level_id = jnp.where(same & (((ri >> sh) & 1) == 1) & (((ci >> sh) & 1) == 0), 1 + li, level_id)

    def stack(x, width):
        return jnp.concatenate([x[:, h * width:(h + 1) * width] for h in range(GLA_HEADS)], axis=0)
    row_iota = lax.broadcasted_iota(I32, (blk, 1), 0)
    cmat = c_ref[...]
    w2 = w2_ref[...]
    gbias = gb_ref[...]
    ng = ng_ref[...]

    def block(j):
        r0 = pl.multiple_of(j * blk, blk)
        rows = pl.ds(r0, blk)
        valid = (ti * (n_blk * blk) + r0 + row_iota) < t_valid
        gq = q_ref[rows, :] * (GLA_DK ** -0.5)
        gk = jnp.where(valid, k_ref[rows, :], 0.0)
        x = _dot(sm_ref[rows, :].astype(BF16), w2) + gbias
        logf = (jnp.minimum(x, 0.0) - jnp.log1p(jnp.exp(-jnp.abs(x)))) * (1.0 / GLA_TAU)
        logf = jnp.where(valid, logf, 0.0)
        hi = logf.astype(BF16)
        r1 = logf - hi.astype(F32)
        mid = r1.astype(BF16)
        lo = (r1 - mid.astype(F32)).astype(BF16)
        e_all = _dot(cmat, jnp.concatenate([hi, mid, lo], axis=0))
        qs = stack(gq, GLA_DK)
        ks = stack(gk, GLA_DK)
        vs = stack(v_ref[rows, :], GLA_DV)
        b_s = stack(e_all[0:blk], GLA_DK)
        a = jnp.where(level_id == 0, _dot_nt(qs.astype(BF16), ks.astype(BF16)), 0.0)
        for li in range(len(levels)):
            e = jnp.exp(stack(e_all[(2 + li) * blk:(3 + li) * blk], GLA_DK))
            p = _dot_nt((qs * e).astype(BF16), (ks * e).astype(BF16))
            a = a + jnp.where(level_id == 1 + li, p, 0.0)
        o_intra = _dot(a.astype(BF16), vs)
        qe = (qs * jnp.exp(b_s)).astype(BF16)
        kd = (ks * jnp.exp(stack(e_all[blk:2 * blk], GLA_DK))).astype(BF16)
        for h in range(GLA_HEADS):
            hs = slice(h * blk, (h + 1) * blk)
            vsl = slice(h * GLA_DV, (h + 1) * GLA_DV)
            st = s_scr[h]
            o = _dot_nt(qe[hs], st.astype(BF16)) + o_intra[hs]
            dec = jnp.exp(b_s[(h + 1) * blk - 1:(h + 1) * blk, :])
            s_scr[h] = st * dec + _dot_tn(vs[hs], kd[hs])
            on = o * lax.rsqrt(jnp.mean(o * o, axis=-1, keepdims=True) + NORM_EPS) * ng
            gr = r_ref[rows, vsl].astype(F32)
            o_ref[rows, vsl] = (on * (gr * jax.nn.sigmoid(gr))).astype(BF16)

    n_live = jnp.clip((t_valid - ti * (n_blk * blk) + blk - 1) // blk, 0, n_blk)

    @pl.when(n_live < n_blk)
    def _():
        o_ref[...] = jnp.zeros(o_ref.shape, o_ref.dtype)

    def block_pair(j, carry):
        block(2 * j)
        block(2 * j + 1)
        return carry

    lax.fori_loop(0, n_live // 2, block_pair, 0)

    @pl.when(n_live % 2 == 1)
    def _():
        block(n_live - 1)

    @pl.when(ti == pl.num_programs(1) - 1)
    def _():
        st_ref[0] = s_scr[...]


def _gla(pf, pb, cmat, w2pad, gbias, ng, s0t, *, n_seq, row0, rows_per_seq, t_valid, blk, n_blk):
    rb = blk * n_blk
    steps = rows_per_seq // rb
    base = row0 // rb

    def rmap(c):
        return lambda s, t: (base + s * steps + t, c)

    kern = functools.partial(_gla_kernel, blk=blk, n_blk=n_blk, t_valid=t_valid)
    return pl.pallas_call(
        kern,
        grid=(n_seq, steps),
        in_specs=[
            pl.BlockSpec((rb, GLA_QK), rmap(F_GQ // GLA_QK)),
            pl.BlockSpec((rb, GLA_QK), rmap(F_GK // GLA_QK)),
            pl.BlockSpec((rb, LANES), rmap(F_SMALL // LANES)),
            pl.BlockSpec((rb, GLA_V), rmap(B_GV // GLA_V)),
            pl.BlockSpec((rb, GLA_V), rmap(B_GR // GLA_V)),
            pl.BlockSpec(cmat.shape, lambda s, t: (0, 0)),
            pl.BlockSpec(w2pad.shape, lambda s, t: (0, 0)),
            pl.BlockSpec((1, GLA_QK), lambda s, t: (0, 0)),
            pl.BlockSpec((1, GLA_DV), lambda s, t: (0, 0)),
            pl.BlockSpec((1, GLA_HEADS, GLA_DV, GLA_DK), lambda s, t: (s, 0, 0, 0)),
        ],
        out_specs=[
            pl.BlockSpec((rb, GLA_V), lambda s, t: (s * steps + t, 0)),
            pl.BlockSpec((1, GLA_HEADS, GLA_DV, GLA_DK), lambda s, t: (s, 0, 0, 0)),
        ],
        out_shape=[
            jax.ShapeDtypeStruct((n_seq * rows_per_seq, GLA_V), BF16),
            jax.ShapeDtypeStruct((n_seq, GLA_HEADS, GLA_DV, GLA_DK), F32),
        ],
        scratch_shapes=[pltpu.VMEM((GLA_HEADS, GLA_DV, GLA_DK), F32)],
        compiler_params=pltpu.CompilerParams(
            dimension_semantics=("parallel", "arbitrary"), vmem_limit_bytes=VMEM_LIMIT_BYTES),
        name="gla",
    )(pf, pf, pf, pb, pb, cmat, w2pad, gbias, ng, s0t)


def _dsa_kernel(iq_ref, sm_ref, qd_ref, kit_ref, kt_ref, v_ref, o_ref,
                key_scr, m_scr, l_scr, acc_scr, *, tq, tk, n_keys, topk, n_kb_total):
    n_kb = n_kb_total
    lane_iota = lax.broadcasted_iota(I32, (1, tk), 1)

    iq = iq_ref[...]
    iq_hm = jnp.concatenate([iq[:, h * IDX_DIM:(h + 1) * IDX_DIM] for h in range(IDX_HEADS)], axis=0)
    w_i = sm_ref[:, SM_IW:SM_IW + IDX_HEADS] * IDX_W_SCALE
    w_cols = [jnp.broadcast_to(w_i[:, h:h + 1], (tq, LANES)) for h in range(IDX_HEADS)]

    def score_block(kb, carry):
        k0 = pl.multiple_of(kb * tk, tk)
        r = _dot(iq_hm, kit_ref[0, :, pl.ds(k0, tk)])
        sc = None
        for h in range(IDX_HEADS):
            rh = jnp.maximum(r[h * tq:(h + 1) * tq, :], 0.0)
            wh = jnp.concatenate([w_cols[h]] * (tk // LANES), axis=1)
            sc = rh * wh if sc is None else sc + rh * wh
        sc = jnp.where(k0 + lane_iota < n_keys, sc, -jnp.inf)
        bits = pltpu.bitcast(sc, I32)
        key_scr[:, pl.ds(k0, tk)] = bits ^ ((bits >> 31) & 0x7FFFFFFF)
        return carry

    lax.fori_loop(0, n_kb, score_block, 0)

    n_cols = n_kb * (tk // LANES)

    def count_ge(cand):
        cb = jnp.broadcast_to(cand, (tq, LANES))

        def body(kb, acc):
            for u in range(tk // LANES):
                c0 = pl.multiple_of(kb * tk + u * LANES, LANES)
                acc = acc + jnp.where(key_scr[:, pl.ds(c0, LANES)] >= cb, 1, 0)
            return acc

        acc = lax.fori_loop(0, n_kb, body, jnp.zeros((tq, LANES), I32))
        return jnp.sum(acc, axis=1, keepdims=True)

    bits_per_round = 4

    def search_cond(carry):
        i, lo, cge = carry
        return (i < 32) & (jnp.max(jnp.abs(cge - topk)) > 0)

    def search_round(carry):
        i, lo, cge = carry
        for u in range(bits_per_round):
            cand = lo + jnp.left_shift(jnp.int32(1), 31 - u - i)
            cnt = count_ge(cand)
            take = cnt >= topk
            lo = jnp.where(take, cand, lo)
            cge = jnp.where(take, cnt, cge)
        return i + bits_per_round, lo, cge

    lo0 = jnp.full((tq, 1), INT_MIN, I32)
    cge0 = jnp.zeros((tq, 1), I32) + n_cols * LANES
    _, thr, cge = lax.while_loop(search_cond, search_round, (jnp.int32(0), lo0, cge0))
    need_tie = (cge > topk) & (thr > NEG_INF_KEY)
    thr = jnp.maximum(thr, NEG_INF_KEY + 1)

    @pl.when(jnp.max(need_tie.astype(I32)) > 0)
    def _():
        want = topk - count_ge(thr + 1)
        thr_b = jnp.broadcast_to(thr, (tq, LANES))
        col_iota = lax.broadcasted_iota(I32, (tq, LANES), 1)

        def count_tied_below(m):
            mb = jnp.broadcast_to(m, (tq, LANES))

            def body(c, acc):
                c0 = pl.multiple_of(c * LANES, LANES)
                hit = (key_scr[:, pl.ds(c0, LANES)] == thr_b) & ((c0 + col_iota) < mb)
                return acc + jnp.where(hit, 1, 0)

            acc = lax.fori_loop(0, n_cols, body, jnp.zeros((tq, LANES), I32))
            return jnp.sum(acc, axis=1, keepdims=True)

        n_bits = int(math.ceil(math.log2(n_kb_total * tk + 1)))

        def idx_step(i, m):
            cand = m + jnp.left_shift(jnp.int32(1), n_bits - 1 - i)
            return jnp.where(count_tied_below(cand) <= want, cand, m)

        m_keep = lax.fori_loop(0, n_bits, idx_step, jnp.zeros((tq, 1), I32))
        mk_b = jnp.broadcast_to(jnp.where(need_tie, m_keep, jnp.int32(2 ** 30)), (tq, LANES))

        def demote(c, carry):
            c0 = pl.multiple_of(c * LANES, LANES)
            kv = key_scr[:, pl.ds(c0, LANES)]
            drop = (kv == thr_b) & ((c0 + col_iota) >= mk_b)
            key_scr[:, pl.ds(c0, LANES)] = jnp.where(drop, kv - 1, kv)
            return carry

        lax.fori_loop(0, n_cols, demote, 0)

    qd = qd_ref[...]
    win = lax.broadcasted_iota(I32, (tq, DSA_KV), 1) >> 6
    q_all = jnp.concatenate(
        [jnp.where(win == g, qd[:, r * DSA_KV:(r + 1) * DSA_KV].astype(F32), 0.0).astype(BF16)
         for g in range(DSA_KV_HEADS) for r in range(DSA_GROUP)], axis=0)
    m_scr[...] = jnp.full(m_scr.shape, NEG_BIG, F32)
    l_scr[...] = jnp.zeros(l_scr.shape, F32)
    acc_scr[...] = jnp.zeros(acc_scr.shape, F32)
    thr_t = jnp.broadcast_to(thr, (tq, LANES))

    def attend_block(kb, carry):
        k0 = pl.multiple_of(kb * tk, tk)
        sel = key_scr[:, pl.ds(k0, tk)] >= jnp.concatenate([thr_t] * (tk // LANES), axis=1)
        bias = jnp.where(sel, 0.0, NEG_BIG)
        s = _dot(q_all, kt_ref[0, :, pl.ds(k0, tk)]) + jnp.concatenate([bias] * DSA_HEADS, axis=0)
        m_old = m_scr[...]
        m_new = jnp.maximum(m_old, jnp.max(s, axis=1, keepdims=True))
        alpha = jnp.exp2(m_old - m_new)
        p = jnp.exp2(s - m_new)
        l_scr[...] = alpha * l_scr[...] + jnp.sum(p, axis=1, keepdims=True)
        acc_scr[...] = alpha * acc_scr[...] + _dot(p.astype(BF16), v_ref[0, pl.ds(k0, tk), :])
        m_scr[...] = m_new
        return carry

    lax.fori_loop(0, n_kb, attend_block, 0)

    o_all = acc_scr[...] / l_scr[...]
    for g in range(DSA_KV_HEADS):
        for r in range(DSA_GROUP):
            h = g * DSA_GROUP + r
            o_ref[:, h * DSA_HEAD_DIM:(h + 1) * DSA_HEAD_DIM] = (
                o_all[h * tq:(h + 1) * tq, g * DSA_HEAD_DIM:(g + 1) * DSA_HEAD_DIM].astype(BF16))


def _dsa(pf, pb, kit, kt, v, *, n_seq, row0, rows_per_seq, tq, tk, n_keys, topk):
    nk_pad = kit.shape[-1]
    n_kb_total = nk_pad // tk
    steps = rows_per_seq // tq
    base = row0 // tq

    def rmap(c):
        return lambda s, t: (base + s * steps + t, c)

    kern = functools.partial(_dsa_kernel, tq=tq, tk=tk, n_keys=n_keys, topk=topk, n_kb_total=n_kb_total)
    single = pl.Buffered(1)
    return pl.pallas_call(
        kern,
        grid=(n_seq, steps),
        in_specs=[
            pl.BlockSpec((tq, IDX_Q), rmap(B_IQ // IDX_Q)),
            pl.BlockSpec((tq, LANES), rmap(F_SMALL // LANES)),
            pl.BlockSpec((tq, DSA_Q), rmap(B_DQ // DSA_Q)),
            pl.BlockSpec((1, IDX_DIM, nk_pad), lambda s, t: (s, 0, 0), pipeline_mode=single),
            pl.BlockSpec((1, DSA_KV, nk_pad), lambda s, t: (s, 0, 0), pipeline_mode=single),
            pl.BlockSpec((1, nk_pad, DSA_KV), lambda s, t: (s, 0, 0), pipeline_mode=single),
        ],
        out_specs=pl.BlockSpec((tq, DSA_Q), lambda s, t: (s * steps + t, 0)),
        out_shape=jax.ShapeDtypeStruct((n_seq * rows_per_seq, DSA_Q), BF16),
        scratch_shapes=[
            pltpu.VMEM((tq, nk_pad), I32),
            pltpu.VMEM((DSA_HEADS * tq, 1), F32),
            pltpu.VMEM((DSA_HEADS * tq, 1), F32),
            pltpu.VMEM((DSA_HEADS * tq, DSA_KV), F32),
        ],
        compiler_params=pltpu.CompilerParams(
            dimension_semantics=("parallel", "arbitrary"), vmem_limit_bytes=VMEM_LIMIT_BYTES),
        name="dsa",
    )(pb, pf, pb, kit, kt, v)


BIT_GROUP = 256
SLAB_ROWS = 128
ATTEND_UNROLL = 4


def _slab(g):
    return g * DSA_HEAD_DIM // LANES * LANES


def _bit_transpose32(words):
    a = list(words)
    mask, j = 0x0000FFFF, 16
    while j:
        k = 0
        while k < 32:
            t = (a[k] ^ lax.shift_right_logical(a[k + j], jnp.int32(j))) & mask
            a[k] = a[k] ^ t
            a[k + j] = a[k + j] ^ (t << j)
            k = (k + j + 1) & ~j
        j >>= 1
        mask ^= (mask << j) & 0xFFFFFFFF
    return a


V_ROWS = 80
V_ONES_ROW = DSA_HEAD_DIM


def _dsa_t_kernel(iq_ref, sm_ref, qd_ref, ki_ref, k_ref, vt_ref, o_ref,
                  key_scr, m_scr, acc_scr, s_scr, plane_scr, active_scr, *, tq, tk, n_keys, topk):
    live = pl.program_id(1) * tq < n_keys

    @pl.when(live)
    def _():
        _dsa_t_body(iq_ref, sm_ref, qd_ref, ki_ref, k_ref, vt_ref, o_ref, key_scr, m_scr, acc_scr, s_scr,
                    plane_scr, active_scr,
                    tq=tq, tk=tk, n_keys=n_keys, topk=topk)

    @pl.when(jnp.logical_not(live))
    def _():
        o_ref[...] = jnp.zeros(o_ref.shape, o_ref.dtype)


def _dsa_t_body(iq_ref, sm_ref, qd_ref, ki_ref, k_ref, vt_ref, o_ref,
                key_scr, m_scr, acc_scr, s_scr, plane_scr, active_scr, *, tq, tk, n_keys, topk):
    qb = pl.program_id(1)
    q0 = qb * tq
    q_chunk_max = (q0 + tq - 1 - N_META) >> 6
    limit = jnp.minimum(N_META + CHUNK * (q_chunk_max + 1), n_keys)
    n_kb = (limit + tk - 1) // tk
    n_full = jnp.minimum(jnp.minimum((q0 + N_META) // tk, n_keys // tk), n_kb)

    q_chunk = (q0 + lax.broadcasted_iota(I32, (1, tq), 1) - N_META) >> 6
    row_iota = lax.broadcasted_iota(I32, (tk, tq), 0)
    plane_rows = key_scr.shape[0] // BIT_GROUP * 8

    @pl.when(qb == 0)
    def _():
        plane_scr[...] = jnp.zeros(plane_scr.shape, I32)

    iq = iq_ref[...]
    iq_hm = jnp.concatenate([iq[:, h * IDX_DIM:(h + 1) * IDX_DIM] for h in range(IDX_HEADS)], axis=0)
    w_t = sm_ref[...].T[SM_IW:SM_IW + IDX_HEADS, :] * IDX_W_SCALE
    iq_hm = iq_hm.astype(F32).T.astype(BF16)

    def score_block(kb, masked):
        k0 = pl.multiple_of(kb * tk, tk)
        r = _dot(ki_ref[0, pl.ds(k0, tk), :], iq_hm)
        sc = None
        for h in range(IDX_HEADS):
            t = jnp.maximum(r[:, h * tq:(h + 1) * tq], 0.0) * w_t[h:h + 1, :]
            sc = t if sc is None else sc + t
        if masked:
            key_pos = k0 + row_iota
            adm = (key_pos < n_keys) & (((key_pos - N_META) >> 6) <= q_chunk)
            sc = jnp.where(adm, sc, -jnp.inf)
        bits = pltpu.bitcast(sc, I32)
        keys = bits ^ ((bits >> 31) & 0x7FFFFFFF)
        key_scr[pl.ds(k0, tk), :] = keys
        ukeys = keys ^ INT_MIN
        for u in range(tk // BIT_GROUP):
            words = [ukeys[u * BIT_GROUP + 8 * w:u * BIT_GROUP + 8 * w + 8, :] for w in range(32)]
            planes = _bit_transpose32(words)
            g_row = pl.multiple_of((kb * (tk // BIT_GROUP) + u) * 8, 8)
            for w in range(32):
                plane_scr[pl.ds(w * plane_rows + g_row, 8), :] = planes[w]

    def score_full(kb, carry):
        score_block(kb, False)
        return carry

    def score_edge(kb, carry):
        score_block(kb, True)
        return carry

    def score_pair(j, carry):
        score_block(2 * j, False)
        score_block(2 * j + 1, False)
        return carry

    lax.fori_loop(0, n_full // 2, score_pair, 0)
    lax.fori_loop(n_full // 2 * 2, n_full, score_full, 0)
    lax.fori_loop(n_full, n_kb, score_edge, 0)

    n_chunks = n_kb * (tk // LANES)

    sub_per_blk = tk // LANES

    def count_rows(pred):
        def body(kb, acc):
            for u in range(sub_per_blk):
                c0 = pl.multiple_of(kb * tk + u * LANES, LANES)
                acc = acc + jnp.where(pred(key_scr[pl.ds(c0, LANES), :], c0), 1, 0)
            return acc

        acc = lax.fori_loop(0, n_kb, body, jnp.zeros((LANES, tq), I32))
        return jnp.sum(acc, axis=0, keepdims=True)

    def count_ge(cand):
        return count_rows(lambda kv, c0: kv >= cand)

    n_groups = n_kb * (tk // BIT_GROUP)
    n_slabs = (n_groups * 8 + SLAB_ROWS - 1) // SLAB_ROWS
    slab_iota = lax.broadcasted_iota(I32, (SLAB_ROWS, tq), 0)

    def init_active(sl, carry):
        r0 = pl.multiple_of(sl * SLAB_ROWS, SLAB_ROWS)
        active_scr[pl.ds(r0, SLAB_ROWS), :] = jnp.where(r0 + slab_iota < n_groups * 8, -1, 0)
        return carry

    lax.fori_loop(0, n_slabs, init_active, 0)

    def select_pass(i, carry):
        thr_u, c_above, n_act, flip = carry
        cur = i * plane_rows
        prev = cur - plane_rows

        def body(sl, acc):
            r0 = pl.multiple_of(sl * SLAB_ROWS, SLAB_ROWS)
            rows = pl.ds(r0, SLAB_ROWS)
            act = active_scr[rows, :] & (plane_scr[pl.ds(pl.multiple_of(prev + r0, 8), SLAB_ROWS), :] ^ flip)
            active_scr[rows, :] = act
            return acc + lax.population_count(act & plane_scr[pl.ds(pl.multiple_of(cur + r0, 8), SLAB_ROWS), :])

        acc = lax.fori_loop(0, n_slabs, body, jnp.zeros((SLAB_ROWS, tq), I32))
        ones = jnp.sum(acc, axis=0, keepdims=True)
        take = c_above + ones >= topk
        thr_u = thr_u | jnp.where(take, jnp.left_shift(jnp.int32(1), 31 - i), 0)
        c_above = c_above + jnp.where(take, 0, ones)
        n_act = jnp.where(take, ones, n_act - ones)
        return thr_u, c_above, n_act, jnp.where(take, 0, -1)

    n_act = jnp.zeros((1, tq), I32) + n_groups * BIT_GROUP

    def count_top(sl, acc):
        rows = pl.ds(pl.multiple_of(sl * SLAB_ROWS, SLAB_ROWS), SLAB_ROWS)
        return acc + lax.population_count(active_scr[rows, :] & plane_scr[rows, :])

    acc0 = lax.fori_loop(0, n_slabs, count_top, jnp.zeros((SLAB_ROWS, tq), I32))
    ones0 = jnp.sum(acc0, axis=0, keepdims=True)
    take0 = ones0 >= topk
    carry0 = (jnp.where(take0, INT_MIN, 0), jnp.where(take0, 0, ones0),
              jnp.where(take0, ones0, n_act - ones0), jnp.where(take0, 0, -1))
    thr_u, c_above, n_act, _ = lax.fori_loop(1, 32, select_pass, carry0)
    thr = thr_u ^ INT_MIN
    cge = c_above + n_act
    need_tie = (cge > topk) & (thr > NEG_INF_KEY)
    thr = jnp.maximum(thr, NEG_INF_KEY + 1)

    @pl.when(jnp.max(need_tie.astype(I32)) > 0)
    def _():
        want = topk - count_ge(thr + 1)
        sub_iota = lax.broadcasted_iota(I32, (LANES, tq), 0)
        n_bits = int(math.ceil(math.log2(key_scr.shape[0] + 1)))

        def idx_step(i, m):
            cand = m + jnp.left_shift(jnp.int32(1), n_bits - 1 - i)
            tied_below = count_rows(lambda kv, c0: (kv == thr) & ((c0 + sub_iota) < cand))
            return jnp.where(tied_below <= want, cand, m)

        m_keep = lax.fori_loop(0, n_bits, idx_step, jnp.zeros((1, tq), I32))
        m_keep = jnp.where(need_tie, m_keep, jnp.int32(2 ** 30))

        def demote(c, carry):
            c0 = pl.multiple_of(c * LANES, LANES)
            kv = key_scr[pl.ds(c0, LANES), :]
            drop = (kv == thr) & ((c0 + sub_iota) >= m_keep)
            key_scr[pl.ds(c0, LANES), :] = jnp.where(drop, kv - 1, kv)
            return carry

        lax.fori_loop(0, n_chunks, demote, 0)

    qd = qd_ref[...]
    win = lax.broadcasted_iota(I32, (tq, LANES), 1) >> 6
    q_pad = []
    for g in range(DSA_KV_HEADS):
        lo = _slab(g)
        q_pad.append(jnp.concatenate(
            [jnp.where(win == g % 2, qd[:, r * DSA_KV + lo:r * DSA_KV + lo + LANES].astype(F32), 0.0)
             for r in range(DSA_GROUP)], axis=0))
    q_pad = [x.T.astype(BF16) for x in q_pad]
    m_scr[...] = jnp.full(m_scr.shape, NEG_BIG, F32)
    acc_scr[...] = jnp.zeros(acc_scr.shape, F32)
    s_scr[0] = _dot(k_ref[0, pl.ds(0, tk), 0:LANES], q_pad[0])

    def attend_block(kb):
        k0 = pl.multiple_of(kb * tk, tk)
        bias = jnp.where(key_scr[pl.ds(k0, tk), :] >= thr, 0.0, NEG_BIG)
        bias_g = jnp.concatenate([bias] * DSA_GROUP, axis=1)
        kblk = k_ref[0, pl.ds(k0, tk), :]
        k1 = pl.multiple_of(jnp.minimum(kb + 1, n_kb - 1) * tk, tk)
        for g in range(DSA_KV_HEADS):
            if g + 1 < DSA_KV_HEADS:
                s_scr[(g + 1) % 2] = _dot(kblk[:, _slab(g + 1):_slab(g + 1) + LANES], q_pad[g + 1])
            else:
                s_scr[0] = _dot(k_ref[0, pl.ds(k1, tk), 0:LANES], q_pad[0])
            s = s_scr[g % 2] + bias_g
            m_old = m_scr[g]
            m_new = jnp.maximum(m_old, jnp.max(s, axis=0, keepdims=True))
            alpha = jnp.exp2(m_old[0:1] - m_new[0:1])
            p = jnp.exp2(s - m_new[0:1]).astype(BF16)
            acc_scr[g] = acc_scr[g] * alpha + _dot(vt_ref[0, g, :, pl.ds(k0, tk)], p)
            m_scr[g] = m_new

    def attend_many(j, carry):
        for u in range(ATTEND_UNROLL):
            attend_block(ATTEND_UNROLL * j + u)
        return carry

    def attend_one(kb, carry):
        attend_block(kb)
        return carry

    n_many = n_kb // ATTEND_UNROLL
    lax.fori_loop(0, n_many, attend_many, 0)
    lax.fori_loop(n_many * ATTEND_UNROLL, n_kb, attend_one, 0)

    for g in range(DSA_KV_HEADS):
        acc = acc_scr[g]
        og = acc[0:DSA_HEAD_DIM] / acc[V_ONES_ROW:V_ONES_ROW + 1]
        for r in range(DSA_GROUP):
            c0 = (g * DSA_GROUP + r) * DSA_HEAD_DIM
            o_ref[:, c0:c0 + DSA_HEAD_DIM] = og[:, r * tq:(r + 1) * tq].T.astype(BF16)


def _dsa_t(pf, pb, ki, k, vt, *, n_seq, rows_per_seq, tk, n_keys, topk):
    tq = LANES
    nk_pad = k.shape[1]
    steps = rows_per_seq // tq

    def rmap(c):
        return lambda s, t: (s * steps + t, c)

    kern = functools.partial(_dsa_t_kernel, tq=tq, tk=tk, n_keys=n_keys, topk=topk)
    single = pl.Buffered(1)
    return pl.pallas_call(
        kern,
        grid=(n_seq, steps),
        in_specs=[
            pl.BlockSpec((tq, IDX_Q), rmap(B_IQ // IDX_Q)),
            pl.BlockSpec((tq, LANES), rmap(F_SMALL // LANES)),
            pl.BlockSpec((tq, DSA_Q), rmap(B_DQ // DSA_Q)),
            pl.BlockSpec((1, nk_pad, IDX_DIM), lambda s, t: (s, 0, 0), pipeline_mode=single),
            pl.BlockSpec((1, nk_pad, DSA_KV), lambda s, t: (s, 0, 0), pipeline_mode=single),
            pl.BlockSpec((1, DSA_KV_HEADS, V_ROWS, nk_pad), lambda s, t: (s, 0, 0, 0), pipeline_mode=single),
        ],
        out_specs=pl.BlockSpec((tq, DSA_Q), lambda s, t: (s * steps + t, 0)),
        out_shape=jax.ShapeDtypeStruct((n_seq * rows_per_seq, DSA_Q), BF16),
        scratch_shapes=[
            pltpu.VMEM((nk_pad, tq), I32),
            pltpu.VMEM((DSA_KV_HEADS, 8, DSA_GROUP * tq), F32),
            pltpu.VMEM((DSA_KV_HEADS, V_ROWS, DSA_GROUP * tq), F32),
            pltpu.VMEM((2, tk, DSA_GROUP * tq), F32),
            pltpu.VMEM((nk_pad + SLAB_ROWS, tq), I32),
            pltpu.VMEM((_round_up(nk_pad // BIT_GROUP * 8, SLAB_ROWS), tq), I32),
        ],
        compiler_params=pltpu.CompilerParams(
            dimension_semantics=("parallel", "arbitrary"), vmem_limit_bytes=VMEM_LIMIT_BYTES),
        name="dsa_t",
    )(pb, pf, pb, ki, k, vt)


def _out_kernel(x_ref, ya_ref, ob_ref, z_ref, ma_ref, mb_ref, wg_ref, wd_ref, wo_ref,
                gate_ref, lig_ref, lib_ref, lg_ref, lb_ref, o_ref, *, alpha):
    hn = _layer_norm_rows(x_ref[...], lig_ref[...], lib_ref[...])
    y_a = _dot(ya_ref[...], wg_ref[...])
    z = z_ref[...].astype(F32)
    yb_in = ob_ref[...].astype(F32) * (z * jax.nn.sigmoid(z))
    y_b = _dot(yb_in.astype(BF16), wd_ref[...])
    ga = jax.nn.sigmoid(ma_ref[...].astype(F32) + gate_ref[0:1, :])
    gb = jax.nn.sigmoid(mb_ref[...].astype(F32) + gate_ref[1:2, :])
    merged = ga * y_a + gb * y_b
    y = alpha * hn + _dot(merged.astype(BF16), wo_ref[...])
    o_ref[...] = _layer_norm_rows(y, lg_ref[...], lb_ref[...])


def _out(x, ya, ob, pb, wg, wd, wo, gate, lig, lib, lg, lb, tm, alpha, row0):
    rows = ya.shape[0]
    base = row0 // tm
    row = lambda c: pl.BlockSpec((tm, D_MODEL), lambda i, c=c: (i, c))
    off = lambda c: pl.BlockSpec((tm, D_MODEL), lambda i, c=c: (base + i, c))
    full = lambda a: pl.BlockSpec(a.shape, lambda i: (0, 0))
    return pl.pallas_call(
        functools.partial(_out_kernel, alpha=alpha),
        grid=(rows // tm,),
        in_specs=[off(0), row(0), row(0), off(B_DZ // D_MODEL), off(B_MA // D_MODEL), off(B_MB // D_MODEL),
                  full(wg), full(wd), full(wo), full(gate), full(lig), full(lib), full(lg), full(lb)],
        out_specs=pl.BlockSpec((tm, D_MODEL), lambda i: (i, 0)),
        out_shape=jax.ShapeDtypeStruct((rows, D_MODEL), F32),
        compiler_params=pltpu.CompilerParams(
            dimension_semantics=("parallel",), vmem_limit_bytes=VMEM_LIMIT_BYTES),
        name="out_proj",
    )(x, ya, ob, pb, pb, pb, wg, wd, wo, gate, lig, lib, lg, lb)


def _round_up(a, b):
    return -(-a // b) * b


def _forward(x_prompt, x_sample, cache_k, cache_v, cache_idx_k, state_gla, meta, ln_in_g, ln_in_b,
             w_in, gla_w2, gla_gate_b, gla_norm_g, idx_kn_g, idx_kn_b, w_gla, w_dsa, gate_b,
             w_out, ln_g, ln_b):
    depth = w_in.shape[0]
    assert depth == 1, "single-layer trunk"
    bsz, seq, _ = x_prompt.shape
    dbsz, dseq, _ = x_sample.shape
    past = cache_k.shape[2]
    t_p = N_META + seq
    tp = _round_up(t_p, KEY_TILE)
    assert tp % LANES == 0 and tp % GLA_BLOCK == 0
    row_s = bsz * tp
    rows = _round_up(row_s + dbsz * dseq, ROW_TILE)
    alpha = (2.0 * depth) ** 0.25

    pieces = []
    for b in range(bsz):
        pieces += [meta.astype(F32), x_prompt[b], jnp.zeros((tp - t_p, D_MODEL), F32)]
    pieces += [x_sample.reshape(dbsz * dseq, D_MODEL), jnp.zeros((rows - row_s - dbsz * dseq, D_MODEL), F32)]
    x_all = jnp.concatenate(pieces, axis=0)

    w = w_in[0]
    w_f = jnp.concatenate([_col(w, _GQ), _col(w, _GK), _col(w, _DK), _col(w, _DV), _col(w, _IK),
                           _col(w, _GLOW), _col(w, _IW),
                           jnp.zeros((D_MODEL, LANES - IDX_DIM - GLA_RANK - IDX_HEADS), F32)], axis=1).astype(BF16)
    w_dq = _col(w, _DQ).reshape(D_MODEL, DSA_KV_HEADS, DSA_GROUP, DSA_HEAD_DIM)
    w_dq = (jnp.swapaxes(w_dq, 1, 2) * (DSA_SCALE * math.log2(math.e))).reshape(D_MODEL, DSA_Q)
    w_b = jnp.concatenate([_col(w, _GV), _col(w, _GR), w_dq, _col(w, _DZ), _col(w, _MA), _col(w, _MB),
                           _col(w, _IQ)], axis=1).astype(BF16)
    lig = ln_in_g.reshape(1, D_MODEL)
    lib = ln_in_b.reshape(1, D_MODEL)

    pf = _proj_f32(x_all, lig, lib, w_f, idx_kn_g[0].reshape(1, IDX_DIM), idx_kn_b[0].reshape(1, IDX_DIM),
                   ROW_TILE // 4)
    pb = _proj_bf16(x_all, lig, lib, w_b, ROW_TILE // 4, B_COL_TILE)

    w2pad = jnp.zeros((LANES, GLA_QK), F32).at[SM_GLOW:SM_GLOW + GLA_RANK].set(gla_w2[0]).astype(BF16)
    gbias = gla_gate_b[0].reshape(1, GLA_QK)
    ng = gla_norm_g[0].reshape(1, GLA_DV)
    blk_p = min(GLA_BLOCK, t_p)
    cm_p = jnp.asarray(np.tile(_gla_decay_matrix(blk_p), (1, 3)), BF16)
    s0_p = jnp.zeros((bsz, GLA_HEADS, GLA_DV, GLA_DK), F32)
    ya_p, st_p = _gla(pf, pb, cm_p, w2pad, gbias, ng, s0_p, n_seq=bsz, row0=0, rows_per_seq=tp,
                      t_valid=t_p, blk=blk_p, n_blk=KEY_TILE // blk_p)
    blk_s = min(GLA_BLOCK, dseq)
    assert dseq % blk_s == 0 and row_s % dseq == 0
    cm_s = jnp.asarray(np.tile(_gla_decay_matrix(blk_s), (1, 3)), BF16)
    s0_s = jnp.swapaxes(state_gla[0], -1, -2)
    ya_s, st_s = _gla(pf, pb, cm_s, w2pad, gbias, ng, s0_s, n_seq=dbsz, row0=row_s, rows_per_seq=dseq,
                      t_valid=dseq, blk=blk_s, n_blk=dseq // blk_s)

    dk = pf[:, F_DK:F_DK + DSA_KV]
    dv = pf[:, F_DV:F_DV + DSA_KV]
    ki = pf[:, F_SMALL:F_SMALL + IDX_DIM]
    dk_p = dk[:row_s].reshape(bsz, tp, DSA_KV_HEADS, DSA_HEAD_DIM)
    dv_p = dv[:row_s].reshape(bsz, tp, DSA_KV_HEADS, DSA_HEAD_DIM)
    ki_p = ki[:row_s].reshape(bsz, tp, IDX_DIM)
    vt_p = jnp.concatenate([jnp.transpose(dv_p, (0, 2, 3, 1)),
                            jnp.ones((bsz, DSA_KV_HEADS, 1, tp), F32),
                            jnp.zeros((bsz, DSA_KV_HEADS, V_ROWS - DSA_HEAD_DIM - 1, tp), F32)], axis=2).astype(BF16)
    topk_p = min(TOPK_MAX, (t_p - N_META) // 4)
    ob_p = _dsa_t(pf, pb, ki_p.astype(BF16), dk[:row_s].reshape(bsz, tp, DSA_KV).astype(BF16), vt_p,
                  n_seq=bsz, rows_per_seq=tp, tk=KEY_TILE, n_keys=t_p, topk=topk_p)

    n_keys_s = past + dseq
    nk_s = _round_up(n_keys_s, KEY_TILE)
    dk_s = dk[row_s:row_s + dbsz * dseq].reshape(dbsz, dseq, DSA_KV_HEADS, DSA_HEAD_DIM)
    dv_s = dv[row_s:row_s + dbsz * dseq].reshape(dbsz, dseq, DSA_KV_HEADS, DSA_HEAD_DIM)
    ki_s = ki[row_s:row_s + dbsz * dseq].reshape(dbsz, dseq, IDX_DIM)
    kpad = jnp.zeros((dbsz, nk_s - n_keys_s, DSA_KV_HEADS, DSA_HEAD_DIM), F32)
    k_all = jnp.concatenate([cache_k[0], dk_s, kpad], axis=1)
    v_all = jnp.concatenate([cache_v[0], dv_s, kpad], axis=1)
    ki_all = jnp.concatenate([cache_idx_k[0], ki_s, jnp.zeros((dbsz, nk_s - n_keys_s, IDX_DIM), F32)], axis=1)
    kt_s = jnp.transpose(k_all, (0, 2, 3, 1)).reshape(dbsz, DSA_KV, nk_s).astype(BF16)
    v_s = v_all.reshape(dbsz, nk_s, DSA_KV).astype(BF16)
    kit_s = jnp.transpose(ki_all, (0, 2, 1)).astype(BF16)
    topk_s = min(TOPK_MAX, n_keys_s // 4)
    ob_s = _dsa(pf, pb, kit_s, kt_s, v_s, n_seq=dbsz, row0=row_s, rows_per_seq=dseq, tq=dseq, tk=KEY_TILE,
                n_keys=n_keys_s, topk=topk_s)

    out_w = (w_gla[0].astype(BF16), w_dsa[0].astype(BF16), w_out[0].astype(BF16), gate_b[0], lig, lib,
             ln_g[0].reshape(1, D_MODEL), ln_b[0].reshape(1, D_MODEL))
    tm_out = ROW_TILE // 4
    tm_s = math.gcd(tm_out, dbsz * dseq)
    assert row_s % tm_out == 0 and tm_s % 16 == 0
    y_p = _out(x_all, ya_p, ob_p, pb, *out_w, tm_out, alpha, 0)
    y_s = _out(x_all, ya_s, ob_s, pb, *out_w, tm_s, alpha, row_s)

    y_prompt = y_p.reshape(bsz, tp, D_MODEL)[:, N_META:t_p]
    y_sample = y_s.reshape(dbsz, dseq, D_MODEL)
    k_prompt = dk_p[:, :t_p][None]
    v_prompt = dv_p[:, :t_p][None]
    idx_k_prompt = ki_p[:, :t_p][None]
    gla_prompt = jnp.swapaxes(st_p, -1, -2)[None]
    k_sample = dk_s[None]
    v_sample = dv_s[None]
    idx_k_sample = ki_s[None]
    gla_sample = jnp.swapaxes(st_s, -1, -2)[None]
    return (y_prompt, y_sample, k_prompt, v_prompt, idx_k_prompt, gla_prompt,
            k_sample, v_sample, idx_k_sample, gla_sample)


def kernel(x_prompt, x_sample, cache_k, cache_v, cache_idx_k, state_gla, meta, ln_in_g, ln_in_b,
           w_in, gla_w2, gla_gate_b, gla_norm_g, idx_kn_g, idx_kn_b, w_gla, w_dsa, gate_b,
           w_out, ln_g, ln_b):
    return _forward(x_prompt, x_sample, cache_k, cache_v, cache_idx_k, state_gla, meta, ln_in_g, ln_in_b,
                    w_in, gla_w2, gla_gate_b, gla_norm_g, idx_kn_g, idx_kn_b, w_gla, w_dsa, gate_b,
                    w_out, ln_g, ln_b)
```

```python
import functools
import math

import numpy as np
import jax
import jax.numpy as jnp
from jax import lax
from jax.experimental import pallas as pl
from jax.experimental.pallas import tpu as pltpu

F32 = jnp.float32
BF16 = jnp.bfloat16
I32 = jnp.int32

D_MODEL = 1024
CHUNK = 64
N_META = 16
GLA_HEADS = 4
GLA_DK = 128
GLA_DV = 256
GLA_RANK = 16
GLA_TAU = 16.0
GLA_BLOCK = 64
DSA_HEADS = 16
DSA_KV_HEADS = 4
DSA_HEAD_DIM = 64
DSA_GROUP = DSA_HEADS // DSA_KV_HEADS
DSA_SCALE = DSA_HEAD_DIM ** -0.5
IDX_HEADS = 8
IDX_DIM = 64
IDX_W_SCALE = (IDX_HEADS ** -0.5) * (IDX_DIM ** -0.5)
TOPK_MAX = 256
NORM_EPS = 1e-5
GLA_QK = GLA_HEADS * GLA_DK
GLA_V = GLA_HEADS * GLA_DV
DSA_Q = DSA_HEADS * DSA_HEAD_DIM
DSA_KV = DSA_KV_HEADS * DSA_HEAD_DIM
IDX_Q = IDX_HEADS * IDX_DIM
CHUNK_SHIFT = CHUNK.bit_length() - 1
HEAD_SHIFT = DSA_HEAD_DIM.bit_length() - 1
SPLITS =(GLA_QK, GLA_QK, GLA_V, GLA_RANK, GLA_V,
          DSA_Q, DSA_KV, DSA_KV, IDX_Q, IDX_DIM, IDX_HEADS, DSA_Q,
          D_MODEL, D_MODEL)
_OFF = tuple(int(o) for o in np.cumsum((0,) + SPLITS))
(_GQ, _GK, _GV, _GLOW, _GR, _DQ, _DK, _DV, _IQ, _IK, _IW, _DZ, _MA, _MB) = range(14)

LANES = 128
VMEM_LIMIT_BYTES = 56 * 1024 * 1024

F_GQ, F_GK, F_DK, F_DV, F_SMALL = 0, 512, 1024, 1280, 1536
F_COLS = F_SMALL + LANES
SM_IK, SM_GLOW, SM_IW = 0, IDX_DIM, IDX_DIM + GLA_RANK
B_GV, B_GR, B_DQ, B_DZ, B_MA, B_MB, B_IQ = 0, 1024, 2048, 3072, 4096, 5120, 6144
B_COLS = B_IQ + IDX_Q
B_COL_TILE = B_COLS // 2

ROW_TILE = 2048
KEY_TILE = 512
NEG_BIG = -1e30
INT_MIN = -(2 ** 31)
NEG_INF_KEY = int(np.array(-np.inf, np.float32).view(np.int32)) ^ 0x7FFFFFFF


def _col(w, idx):
    return w[:, _OFF[idx]:_OFF[idx + 1]]


def _layer_norm_rows(x, g, b):
    mu = jnp.mean(x, axis=-1, keepdims=True)
    xc = x - mu
    var = jnp.mean(xc * xc, axis=-1, keepdims=True)
    return xc * lax.rsqrt(var + NORM_EPS) * g + b


def _dot(a, b):
    return jnp.dot(a, b, preferred_element_type=F32)


def _dot_nt(a, b):
    return lax.dot_general(a, b, (((1,), (1,)), ((), ())), preferred_element_type=F32)


def _dot_tn(a, b):
    return lax.dot_general(a, b, (((0,), (0,)), ((), ())), preferred_element_type=F32)


def _proj_f32_kernel(x_ref, g_ref, b_ref, w_ref, kg_ref, kb_ref, o_ref):
    hn = _layer_norm_rows(x_ref[...], g_ref[...], b_ref[...])
    y = _dot(hn.astype(BF16), w_ref[...])
    o_ref[...] = y
    ik = y[:, F_SMALL:F_SMALL + IDX_DIM]
    o_ref[:, F_SMALL:F_SMALL + IDX_DIM] = _layer_norm_rows(ik, kg_ref[...], kb_ref[...])


def _proj_f32(x, g, b, w, kg, kb, tm):
    rows = x.shape[0]
    return pl.pallas_call(
        _proj_f32_kernel,
        grid=(rows // tm,),
        in_specs=[
            pl.BlockSpec((tm, D_MODEL), lambda i: (i, 0)),
            pl.BlockSpec((1, D_MODEL), lambda i: (0, 0)),
            pl.BlockSpec((1, D_MODEL), lambda i: (0, 0)),
            pl.BlockSpec((D_MODEL, F_COLS), lambda i: (0, 0)),
            pl.BlockSpec((1, IDX_DIM), lambda i: (0, 0)),
            pl.BlockSpec((1, IDX_DIM), lambda i: (0, 0)),
        ],
        out_specs=pl.BlockSpec((tm, F_COLS), lambda i: (i, 0)),
        out_shape=jax.ShapeDtypeStruct((rows, F_COLS), F32),
        compiler_params=pltpu.CompilerParams(
            dimension_semantics=("parallel",), vmem_limit_bytes=VMEM_LIMIT_BYTES),
        name="proj_f32",
    )(x, g, b, w, kg, kb)


def _proj_bf16_kernel(x_ref, g_ref, b_ref, w_ref, o_ref, hn_ref):
    @pl.when(pl.program_id(1) == 0)
    def _():
        hn_ref[...] = _layer_norm_rows(x_ref[...], g_ref[...], b_ref[...]).astype(BF16)

    o_ref[...] = _dot(hn_ref[...], w_ref[...]).astype(BF16)


def _proj_bf16(x, g, b, w, tm, tn):
    rows = x.shape[0]
    return pl.pallas_call(
        _proj_bf16_kernel,
        grid=(rows // tm, B_COLS // tn),
        in_specs=[
            pl.BlockSpec((tm, D_MODEL), lambda i, j: (i, 0)),
            pl.BlockSpec((1, D_MODEL), lambda i, j: (0, 0)),
            pl.BlockSpec((1, D_MODEL), lambda i, j: (0, 0)),
            pl.BlockSpec((D_MODEL, tn), lambda i, j: (0, j)),
        ],
        out_specs=pl.BlockSpec((tm, tn), lambda i, j: (i, j)),
        out_shape=jax.ShapeDtypeStruct((rows, B_COLS), BF16),
        scratch_shapes=[pltpu.VMEM((tm, D_MODEL), BF16)],
        compiler_params=pltpu.CompilerParams(
            dimension_semantics=("parallel", "arbitrary"), vmem_limit_bytes=VMEM_LIMIT_BYTES),
        name="proj_bf16",
    )(x, g, b, w)


def _gla_levels(blk):
    levels = []
    s = blk // 2
    while s >= 1:
        levels.append(s)
        s //= 2
    return levels


def _gla_decay_matrix(blk):
    i = np.arange(blk)[:, None]
    t = np.arange(blk)[None, :]
    mats = [(t <= i).astype(np.float32), (t > i).astype(np.float32)]
    for s in _gla_levels(blk):
        mid = (i // (2 * s)) * (2 * s) + s - 1
        lower = (i // s) % 2 == 1
        m = np.where(lower, ((t > mid) & (t <= i)), False).astype(np.float32)
        n = np.where(~lower, ((t > i) & (t <= mid)), False).astype(np.float32)
        mats.append(m + n)
    return np.concatenate(mats, axis=0)


def _gla_kernel(q_ref, k_ref, sm_ref, v_ref, r_ref, c_ref, w2_ref, gb_ref, ng_ref, s0_ref,
                o_ref, st_ref, s_scr, *, blk, n_blk, t_valid):
    ti = pl.program_id(1)
    levels = _gla_levels(blk)

    @pl.when(ti == 0)
    def _():
        s_scr[...] = s0_ref[0]

    hb = GLA_HEADS * blk
    ri = lax.broadcasted_iota(I32, (hb, hb), 0)
    ci = lax.broadcasted_iota(I32, (hb, hb), 1)
    level_id = jnp.where(ri == ci, 0, -1)
    for li, s in enumerate(levels):
        sh = s.bit_length() - 1
        same = (ri >> (sh + 1)) == (ci >> (sh + 1))
        level_id = jnp.where(same & (((ri >> sh) & 1) == 1) & (((ci >> sh) & 1) == 0), 1 + li, level_id)

    def stack(x, width):
        return jnp.concatenate([x[:, h * width:(h + 1) * width] for h in range(GLA_HEADS)], axis=0)
    row_iota = lax.broadcasted_iota(I32, (blk, 1), 0)
    cmat = c_ref[...]
    w2 = w2_ref[...]
    gbias = gb_ref[...]
    ng = ng_ref[...]

    def block(j):
        r0 = pl.multiple_of(j * blk, blk)
        rows = pl.ds(r0, blk)
        valid = (ti * (n_blk * blk) + r0 + row_iota) < t_valid
        gq = q_ref[rows, :] * (GLA_DK ** -0.5)
        gk = jnp.where(valid, k_ref[rows, :], 0.0)
        x = _dot(sm_ref[rows, :].astype(BF16), w2) + gbias
        logf = (jnp.minimum(x, 0.0) - jnp.log1p(jnp.exp(-jnp.abs(x)))) * (1.0 / GLA_TAU)
        logf = jnp.where(valid, logf, 0.0)
        hi = logf.astype(BF16)
        r1 = logf - hi.astype(F32)
        mid = r1.astype(BF16)
        lo = (r1 - mid.astype(F32)).astype(BF16)
        e_all = _dot(cmat, jnp.concatenate([hi, mid, lo], axis=0))
        qs = stack(gq, GLA_DK)
        ks = stack(gk, GLA_DK)
        vs = stack(v_ref[rows, :], GLA_DV)
        b_s = stack(e_all[0:blk], GLA_DK)
        a = jnp.where(level_id == 0, _dot_nt(qs.astype(BF16), ks.astype(BF16)), 0.0)
        for li in range(len(levels)):
            e = jnp.exp(stack(e_all[(2 + li) * blk:(3 + li) * blk], GLA_DK))
            p = _dot_nt((qs * e).astype(BF16), (ks * e).astype(BF16))
            a = jnp.where(level_id == 1 + li, p, a)
        o_intra = _dot(a.astype(BF16), vs)
        qe = (qs * jnp.exp(b_s)).astype(BF16)
        kd = (ks * jnp.exp(stack(e_all[blk:2 * blk], GLA_DK))).astype(BF16)
        for h in range(GLA_HEADS):
            hs = slice(h * blk, (h + 1) * blk)
            vsl = slice(h * GLA_DV, (h + 1) * GLA_DV)
            st = s_scr[h]
            o = _dot_nt(qe[hs], st.astype(BF16)) + o_intra[hs]
            dec = jnp.exp(b_s[(h + 1) * blk - 1:(h + 1) * blk, :])
            s_scr[h] = st * dec + _dot_tn(vs[hs], kd[hs])
            on = o * lax.rsqrt(jnp.mean(o * o, axis=-1, keepdims=True) + NORM_EPS) * ng
            gr = r_ref[rows, vsl].astype(F32)
            o_ref[rows, vsl] = (on * (gr * jax.nn.sigmoid(gr))).astype(BF16)

    n_live = jnp.clip((t_valid - ti * (n_blk * blk) + blk - 1) // blk, 0, n_blk)

    @pl.when(n_live < n_blk)
    def _():
        o_ref[...] = jnp.zeros(o_ref.shape, o_ref.dtype)

    def block_pair(j, carry):
        block(2 * j)
        block(2 * j + 1)
        return carry

    lax.fori_loop(0, n_live // 2, block_pair, 0)

    @pl.when(n_live % 2 == 1)
    def _():
        block(n_live - 1)

    @pl.when(ti == pl.num_programs(1) - 1)
    def _():
        st_ref[0] = s_scr[...]


def _gla(pf, pb, cmat, w2pad, gbias, ng, s0t, *, n_seq, row0, rows_per_seq, t_valid, blk, n_blk):
    rb = blk * n_blk
    steps = rows_per_seq // rb
    base = row0 // rb

    def rmap(c):
        return lambda s, t: (base + s * steps + t, c)

    kern = functools.partial(_gla_kernel, blk=blk, n_blk=n_blk, t_valid=t_valid)
    return pl.pallas_call(
        kern,
        grid=(n_seq, steps),
        in_specs=[
            pl.BlockSpec((rb, GLA_QK), rmap(F_GQ // GLA_QK)),
            pl.BlockSpec((rb, GLA_QK), rmap(F_GK // GLA_QK)),
            pl.BlockSpec((rb, LANES), rmap(F_SMALL // LANES)),
            pl.BlockSpec((rb, GLA_V), rmap(B_GV // GLA_V)),
            pl.BlockSpec((rb, GLA_V), rmap(B_GR // GLA_V)),
            pl.BlockSpec(cmat.shape, lambda s, t: (0, 0)),
            pl.BlockSpec(w2pad.shape, lambda s, t: (0, 0)),
            pl.BlockSpec((1, GLA_QK), lambda s, t: (0, 0)),
            pl.BlockSpec((1, GLA_DV), lambda s, t: (0, 0)),
            pl.BlockSpec((1, GLA_HEADS, GLA_DV, GLA_DK), lambda s, t: (s, 0, 0, 0)),
        ],
        out_specs=[
            pl.BlockSpec((rb, GLA_V), lambda s, t: (s * steps + t, 0)),
            pl.BlockSpec((1, GLA_HEADS, GLA_DV, GLA_DK), lambda s, t: (s, 0, 0, 0)),
        ],
        out_shape=[
            jax.ShapeDtypeStruct((n_seq * rows_per_seq, GLA_V), BF16),
            jax.ShapeDtypeStruct((n_seq, GLA_HEADS, GLA_DV, GLA_DK), F32),
        ],
        scratch_shapes=[pltpu.VMEM((GLA_HEADS, GLA_DV, GLA_DK), F32)],
        compiler_params=pltpu.CompilerParams(
            dimension_semantics=("parallel", "arbitrary"), vmem_limit_bytes=VMEM_LIMIT_BYTES),
        name="gla",
    )(pf, pf, pf, pb, pb, cmat, w2pad, gbias, ng, s0t)


def _dsa_kernel(iq_ref, sm_ref, qd_ref, kit_ref, kt_ref, v_ref, o_ref,
                key_scr, m_scr, l_scr, acc_scr, *, tq, tk, n_keys, topk, n_kb_total):
    n_kb = n_kb_total
    lane_iota = lax.broadcasted_iota(I32, (1, tk), 1)

    iq = iq_ref[...]
    iq_hm = jnp.concatenate([iq[:, h * IDX_DIM:(h + 1) * IDX_DIM] for h in range(IDX_HEADS)], axis=0)
    w_i = sm_ref[:, SM_IW:SM_IW + IDX_HEADS] * IDX_W_SCALE
    w_cols = [jnp.broadcast_to(w_i[:, h:h + 1], (tq, LANES)) for h in range(IDX_HEADS)]

    def score_block(kb, carry):
        k0 = pl.multiple_of(kb * tk, tk)
        r = _dot(iq_hm, kit_ref[0, :, pl.ds(k0, tk)])
        sc = None
        for h in range(IDX_HEADS):
            rh = jnp.maximum(r[h * tq:(h + 1) * tq, :], 0.0)
            wh = jnp.concatenate([w_cols[h]] * (tk // LANES), axis=1)
            sc = rh * wh if sc is None else sc + rh * wh
        sc = jnp.where(k0 + lane_iota < n_keys, sc, -jnp.inf)
        bits = pltpu.bitcast(sc, I32)
        key_scr[:, pl.ds(k0, tk)] = bits ^ ((bits >> 31) & 0x7FFFFFFF)
        return carry

    lax.fori_loop(0, n_kb, score_block, 0)

    n_cols = n_kb * (tk // LANES)

    def count_ge(cand):
        cb = jnp.broadcast_to(cand, (tq, LANES))

        def body(kb, acc):
            for u in range(tk // LANES):
                c0 = pl.multiple_of(kb * tk + u * LANES, LANES)
                acc = acc + jnp.where(key_scr[:, pl.ds(c0, LANES)] >= cb, 1, 0)
            return acc

        acc = lax.fori_loop(0, n_kb, body, jnp.zeros((tq, LANES), I32))
        return jnp.sum(acc, axis=1, keepdims=True)

    bits_per_round = 4

    def search_cond(carry):
        i, lo, cge = carry
        return (i < 32) & (jnp.max(jnp.abs(cge - topk)) > 0)

    def search_round(carry):
        i, lo, cge = carry
        for u in range(bits_per_round):
            cand = lo + jnp.left_shift(jnp.int32(1), 31 - u - i)
            cnt = count_ge(cand)
            take = cnt >= topk
            lo = jnp.where(take, cand, lo)
            cge = jnp.where(take, cnt, cge)
        return i + bits_per_round, lo, cge

    lo0 = jnp.full((tq, 1), INT_MIN, I32)
    cge0 = jnp.zeros((tq, 1), I32) + n_cols * LANES
    _, thr, cge = lax.while_loop(search_cond, search_round, (jnp.int32(0), lo0, cge0))
    need_tie = (cge > topk) & (thr > NEG_INF_KEY)
    thr = jnp.maximum(thr, NEG_INF_KEY + 1)

    @pl.when(jnp.max(need_tie.astype(I32)) > 0)
    def _():
        want = topk - count_ge(thr + 1)
        thr_b = jnp.broadcast_to(thr, (tq, LANES))
        col_iota = lax.broadcasted_iota(I32, (tq, LANES), 1)

        def count_tied_below(m):
            mb = jnp.broadcast_to(m, (tq, LANES))

            def body(c, acc):
                c0 = pl.multiple_of(c * LANES, LANES)
                hit = (key_scr[:, pl.ds(c0, LANES)] == thr_b) & ((c0 + col_iota) < mb)
                return acc + jnp.where(hit, 1, 0)

            acc = lax.fori_loop(0, n_cols, body, jnp.zeros((tq, LANES), I32))
            return jnp.sum(acc, axis=1, keepdims=True)

        n_bits = int(math.ceil(math.log2(n_kb_total * tk + 1)))

        def idx_step(i, m):
            cand = m + jnp.left_shift(jnp.int32(1), n_bits - 1 - i)
            return jnp.where(count_tied_below(cand) <= want, cand, m)

        m_keep = lax.fori_loop(0, n_bits, idx_step, jnp.zeros((tq, 1), I32))
        mk_b = jnp.broadcast_to(jnp.where(need_tie, m_keep, jnp.int32(2 ** 30)), (tq, LANES))

        def demote(c, carry):
            c0 = pl.multiple_of(c * LANES, LANES)
            kv = key_scr[:, pl.ds(c0, LANES)]
            drop = (kv == thr_b) & ((c0 + col_iota) >= mk_b)
            key_scr[:, pl.ds(c0, LANES)] = jnp.where(drop, kv - 1, kv)
            return carry

        lax.fori_loop(0, n_cols, demote, 0)

    qd = qd_ref[...]
    win = lax.broadcasted_iota(I32, (tq, DSA_KV), 1) >> HEAD_SHIFT
    q_all = jnp.concatenate(
        [jnp.where(win == g, qd[:, r * DSA_KV:(r + 1) * DSA_KV].astype(F32), 0.0).astype(BF16)
         for g in range(DSA_KV_HEADS) for r in range(DSA_GROUP)], axis=0)
    m_scr[...] = jnp.full(m_scr.shape, NEG_BIG, F32)
    l_scr[...] = jnp.zeros(l_scr.shape, F32)
    acc_scr[...] = jnp.zeros(acc_scr.shape, F32)
    thr_t = jnp.broadcast_to(thr, (tq, LANES))

    def attend_block(kb, carry):
        k0 = pl.multiple_of(kb * tk, tk)
        sel = key_scr[:, pl.ds(k0, tk)] >= jnp.concatenate([thr_t] * (tk // LANES), axis=1)
        bias = jnp.where(sel, 0.0, NEG_BIG)
        s = _dot(q_all, kt_ref[0, :, pl.ds(k0, tk)]) + jnp.concatenate([bias] * DSA_HEADS, axis=0)
        m_old = m_scr[...]
        m_new = jnp.maximum(m_old, jnp.max(s, axis=1, keepdims=True))
        alpha = jnp.exp2(m_old - m_new)
        p = jnp.exp2(s - m_new)
        l_scr[...] = alpha * l_scr[...] + jnp.sum(p, axis=1, keepdims=True)
        acc_scr[...] = alpha * acc_scr[...] + _dot(p.astype(BF16), v_ref[0, pl.ds(k0, tk), :])
        m_scr[...] = m_new
        return carry

    lax.fori_loop(0, n_kb, attend_block, 0)

    o_all = acc_scr[...] / l_scr[...]
    for g in range(DSA_KV_HEADS):
        for r in range(DSA_GROUP):
            h = g * DSA_GROUP + r
            o_ref[:, h * DSA_HEAD_DIM:(h + 1) * DSA_HEAD_DIM] = (
                o_all[h * tq:(h + 1) * tq, g * DSA_HEAD_DIM:(g + 1) * DSA_HEAD_DIM].astype(BF16))


def _dsa(pf, pb, kit, kt, v, *, n_seq, row0, rows_per_seq, tq, tk, n_keys, topk):
    nk_pad = kit.shape[-1]
    n_kb_total = nk_pad // tk
    steps = rows_per_seq // tq
    base = row0 // tq

    def rmap(c):
        return lambda s, t: (base + s * steps + t, c)

    kern = functools.partial(_dsa_kernel, tq=tq, tk=tk, n_keys=n_keys, topk=topk, n_kb_total=n_kb_total)
    single = pl.Buffered(1)
    return pl.pallas_call(
        kern,
        grid=(n_seq, steps),
        in_specs=[
            pl.BlockSpec((tq, IDX_Q), rmap(B_IQ // IDX_Q)),
            pl.BlockSpec((tq, LANES), rmap(F_SMALL // LANES)),
            pl.BlockSpec((tq, DSA_Q), rmap(B_DQ // DSA_Q)),
            pl.BlockSpec((1, IDX_DIM, nk_pad), lambda s, t: (s, 0, 0), pipeline_mode=single),
            pl.BlockSpec((1, DSA_KV, nk_pad), lambda s, t: (s, 0, 0), pipeline_mode=single),
            pl.BlockSpec((1, nk_pad, DSA_KV), lambda s, t: (s, 0, 0), pipeline_mode=single),
        ],
        out_specs=pl.BlockSpec((tq, DSA_Q), lambda s, t: (s * steps + t, 0)),
        out_shape=jax.ShapeDtypeStruct((n_seq * rows_per_seq, DSA_Q), BF16),
        scratch_shapes=[
            pltpu.VMEM((tq, nk_pad), I32),
            pltpu.VMEM((DSA_HEADS * tq, 1), F32),
            pltpu.VMEM((DSA_HEADS * tq, 1), F32),
            pltpu.VMEM((DSA_HEADS * tq, DSA_KV), F32),
        ],
        compiler_params=pltpu.CompilerParams(
            dimension_semantics=("parallel", "arbitrary"), vmem_limit_bytes=VMEM_LIMIT_BYTES),
        name="dsa",
    )(pb, pf, pb, kit, kt, v)


BIT_GROUP = 256
SLAB_ROWS = 128
ATTEND_UNROLL = 4


def _slab(g):
    return g * DSA_HEAD_DIM // LANES * LANES


def _bit_transpose32(words):
    a = list(words)
    mask, j = 0x0000FFFF, 16
    while j:
        k = 0
        while k < 32:
            t = (a[k] ^ lax.shift_right_logical(a[k + j], jnp.int32(j))) & mask
            a[k] = a[k] ^ t
            a[k + j] = a[k + j] ^ (t << j)
            k = (k + j + 1) & ~j
        j >>= 1
        mask ^= (mask << j) & 0xFFFFFFFF
    return a


V_ROWS = 80
V_ONES_ROW = DSA_HEAD_DIM


def _dsa_t_kernel(iq_ref, sm_ref, qd_ref, ki_ref, k_ref, vt_ref, o_ref,
                  key_scr, m_scr, acc_scr, s_scr, plane_scr, active_scr, *, tq, tk, n_keys, topk):
    live = pl.program_id(1) * tq < n_keys

    @pl.when(live)
    def _():
        _dsa_t_body(iq_ref, sm_ref, qd_ref, ki_ref, k_ref, vt_ref, o_ref, key_scr, m_scr, acc_scr, s_scr,
                    plane_scr, active_scr,
                    tq=tq, tk=tk, n_keys=n_keys, topk=topk)

    @pl.when(jnp.logical_not(live))
    def _():
        o_ref[...] = jnp.zeros(o_ref.shape, o_ref.dtype)


def _dsa_t_body(iq_ref, sm_ref, qd_ref, ki_ref, k_ref, vt_ref, o_ref,
                key_scr, m_scr, acc_scr, s_scr, plane_scr, active_scr, *, tq, tk, n_keys, topk):
    qb = pl.program_id(1)
    q0 = qb * tq
    q_chunk_max = (q0 + tq - 1 - N_META) >> CHUNK_SHIFT
    limit = jnp.minimum(N_META + CHUNK * (q_chunk_max + 1), n_keys)
    n_kb = (limit + tk - 1) // tk
    n_full = jnp.minimum(jnp.minimum((q0 + N_META) // tk, n_keys // tk), n_kb)

    q_chunk = (q0 + lax.broadcasted_iota(I32, (1, tq), 1) - N_META) >> CHUNK_SHIFT
    row_iota = lax.broadcasted_iota(I32, (tk, tq), 0)
    plane_rows = key_scr.shape[0] // BIT_GROUP * 8

    @pl.when(qb == 0)
    def _():
        plane_scr[...] = jnp.zeros(plane_scr.shape, I32)

    iq = iq_ref[...]
    iq_hm = jnp.concatenate([iq[:, h * IDX_DIM:(h + 1) * IDX_DIM] for h in range(IDX_HEADS)], axis=0)
    w_t = sm_ref[...].T[SM_IW:SM_IW + IDX_HEADS, :] * IDX_W_SCALE
    iq_hm = iq_hm.astype(F32).T.astype(BF16)

    def score_block(kb, masked):
        k0 = pl.multiple_of(kb * tk, tk)
        r = _dot(ki_ref[0, pl.ds(k0, tk), :], iq_hm)
        sc = None
        for h in range(IDX_HEADS):
            t = jnp.maximum(r[:, h * tq:(h + 1) * tq], 0.0) * w_t[h:h + 1, :]
            sc = t if sc is None else sc + t
        if masked:
            key_pos = k0 + row_iota
            adm = (key_pos < n_keys) & (((key_pos - N_META) >> CHUNK_SHIFT) <= q_chunk)
            sc = jnp.where(adm, sc, -jnp.inf)
        bits = pltpu.bitcast(sc, I32)
        keys = bits ^ ((bits >> 31) & 0x7FFFFFFF)
        key_scr[pl.ds(k0, tk), :] = keys
        ukeys = keys ^ INT_MIN
        for u in range(tk // BIT_GROUP):
            words = [ukeys[u * BIT_GROUP + 8 * w:u * BIT_GROUP + 8 * w + 8, :] for w in range(32)]
            planes = _bit_transpose32(words)
            g_row = pl.multiple_of((kb * (tk // BIT_GROUP) + u) * 8, 8)
            for w in range(32):
                plane_scr[pl.ds(w * plane_rows + g_row, 8), :] = planes[w]

    def score_full(kb, carry):
        score_block(kb, False)
        return carry

    def score_edge(kb, carry):
        score_block(kb, True)
        return carry

    def score_pair(j, carry):
        score_block(2 * j, False)
        score_block(2 * j + 1, False)
        return carry

    lax.fori_loop(0, n_full // 2, score_pair, 0)
    lax.fori_loop(n_full // 2 * 2, n_full, score_full, 0)
    lax.fori_loop(n_full, n_kb, score_edge, 0)

    n_chunks = n_kb * (tk // LANES)

    sub_per_blk = tk // LANES

    def count_rows(pred):
        def body(kb, acc):
            for u in range(sub_per_blk):
                c0 = pl.multiple_of(kb * tk + u * LANES, LANES)
                acc = acc + jnp.where(pred(key_scr[pl.ds(c0, LANES), :], c0), 1, 0)
            return acc

        acc = lax.fori_loop(0, n_kb, body, jnp.zeros((LANES, tq), I32))
        return jnp.sum(acc, axis=0, keepdims=True)

    def count_ge(cand):
        return count_rows(lambda kv, c0: kv >= cand)

    n_groups = n_kb * (tk // BIT_GROUP)
    n_slabs = (n_groups * 8 + SLAB_ROWS - 1) // SLAB_ROWS
    slab_iota = lax.broadcasted_iota(I32, (SLAB_ROWS, tq), 0)

    def init_active(sl, carry):
        r0 = pl.multiple_of(sl * SLAB_ROWS, SLAB_ROWS)
        active_scr[pl.ds(r0, SLAB_ROWS), :] = jnp.where(r0 + slab_iota < n_groups * 8, -1, 0)
        return carry

    lax.fori_loop(0, n_slabs, init_active, 0)

    def select_pass(i, carry):
        thr_u, c_above, n_act, flip = carry
        cur = i * plane_rows
        prev = cur - plane_rows

        def body(sl, acc):
            r0 = pl.multiple_of(sl * SLAB_ROWS, SLAB_ROWS)
            rows = pl.ds(r0, SLAB_ROWS)
            act = active_scr[rows, :] & (plane_scr[pl.ds(pl.multiple_of(prev + r0, 8), SLAB_ROWS), :] ^ flip)
            active_scr[rows, :] = act
            return acc + lax.population_count(act & plane_scr[pl.ds(pl.multiple_of(cur + r0, 8), SLAB_ROWS), :])

        acc = lax.fori_loop(0, n_slabs, body, jnp.zeros((SLAB_ROWS, tq), I32))
        ones = jnp.sum(acc, axis=0, keepdims=True)
        take = c_above + ones >= topk
        thr_u = thr_u | jnp.where(take, jnp.left_shift(jnp.int32(1), 31 - i), 0)
        c_above = c_above + jnp.where(take, 0, ones)
        n_act = jnp.where(take, ones, n_act - ones)
        return thr_u, c_above, n_act, jnp.where(take, 0, -1)

    n_act = jnp.zeros((1, tq), I32) + n_groups * BIT_GROUP

    def count_top(sl, acc):
        rows = pl.ds(pl.multiple_of(sl * SLAB_ROWS, SLAB_ROWS), SLAB_ROWS)
        return acc + lax.population_count(active_scr[rows, :] & plane_scr[rows, :])

    acc0 = lax.fori_loop(0, n_slabs, count_top, jnp.zeros((SLAB_ROWS, tq), I32))
    ones0 = jnp.sum(acc0, axis=0, keepdims=True)
    take0 = ones0 >= topk
    carry0 = (jnp.where(take0, INT_MIN, 0), jnp.where(take0, 0, ones0),
              jnp.where(take0, ones0, n_act - ones0), jnp.where(take0, 0, -1))
    thr_u, c_above, n_act, _ = lax.fori_loop(1, 32, select_pass, carry0)
    thr = thr_u ^ INT_MIN
    cge = c_above + n_act
    need_tie = (cge > topk) & (thr > NEG_INF_KEY)
    thr = jnp.maximum(thr, NEG_INF_KEY + 1)

    @pl.when(jnp.max(need_tie.astype(I32)) > 0)
    def _():
        want = topk - count_ge(thr + 1)
        sub_iota = lax.broadcasted_iota(I32, (LANES, tq), 0)
        n_bits = int(math.ceil(math.log2(key_scr.shape[0] + 1)))

        def idx_step(i, m):
            cand = m + jnp.left_shift(jnp.int32(1), n_bits - 1 - i)
            tied_below = count_rows(lambda kv, c0: (kv == thr) & ((c0 + sub_iota) < cand))
            return jnp.where(tied_below <= want, cand, m)

        m_keep = lax.fori_loop(0, n_bits, idx_step, jnp.zeros((1, tq), I32))
        m_keep = jnp.where(need_tie, m_keep, jnp.int32(2 ** 30))

        def demote(c, carry):
            c0 = pl.multiple_of(c * LANES, LANES)
            kv = key_scr[pl.ds(c0, LANES), :]
            drop = (kv == thr) & ((c0 + sub_iota) >= m_keep)
            key_scr[pl.ds(c0, LANES), :] = jnp.where(drop, kv - 1, kv)
            return carry

        lax.fori_loop(0, n_chunks, demote, 0)

    qd = qd_ref[...]
    win = lax.broadcasted_iota(I32, (tq, LANES), 1) >> HEAD_SHIFT
    q_pad = []
    for g in range(DSA_KV_HEADS):
        lo = _slab(g)
        q_pad.append(jnp.concatenate(
            [jnp.where(win == g % 2, qd[:, r * DSA_KV + lo:r * DSA_KV + lo + LANES].astype(F32), 0.0)
             for r in range(DSA_GROUP)], axis=0))
    q_pad = [x.T.astype(BF16) for x in q_pad]
    m_scr[...] = jnp.full(m_scr.shape, NEG_BIG, F32)
    acc_scr[...] = jnp.zeros(acc_scr.shape, F32)
    s_scr[0] = _dot(k_ref[0, pl.ds(0, tk), 0:LANES], q_pad[0])

    def attend_block(kb):
        k0 = pl.multiple_of(kb * tk, tk)
        bias = jnp.where(key_scr[pl.ds(k0, tk), :] >= thr, 0.0, NEG_BIG)
        bias_g = jnp.concatenate([bias] * DSA_GROUP, axis=1)
        kblk = k_ref[0, pl.ds(k0, tk), :]
        k1 = pl.multiple_of(jnp.minimum(kb + 1, n_kb - 1) * tk, tk)
        for g in range(DSA_KV_HEADS):
            if g + 1 < DSA_KV_HEADS:
                s_scr[(g + 1) % 2] = _dot(kblk[:, _slab(g + 1):_slab(g + 1) + LANES], q_pad[g + 1])
            else:
                s_scr[0] = _dot(k_ref[0, pl.ds(k1, tk), 0:LANES], q_pad[0])
            s = s_scr[g % 2] + bias_g
            m_old = m_scr[g]
            m_new = jnp.maximum(m_old, jnp.max(s, axis=0, keepdims=True))
            alpha = jnp.exp2(m_old[0:1] - m_new[0:1])
            p = jnp.exp2(s - m_new[0:1]).astype(BF16)
            acc_scr[g] = acc_scr[g] * alpha + _dot(vt_ref[0, g, :, pl.ds(k0, tk)], p)
            m_scr[g] = m_new

    def attend_many(j, carry):
        for u in range(ATTEND_UNROLL):
            attend_block(ATTEND_UNROLL * j + u)
        return carry

    def attend_one(kb, carry):
        attend_block(kb)
        return carry

    n_many = n_kb // ATTEND_UNROLL
    lax.fori_loop(0, n_many, attend_many, 0)
    lax.fori_loop(n_many * ATTEND_UNROLL, n_kb, attend_one, 0)

    for g in range(DSA_KV_HEADS):
        acc = acc_scr[g]
        og = acc[0:DSA_HEAD_DIM] / acc[V_ONES_ROW:V_ONES_ROW + 1]
        for r in range(DSA_GROUP):
            c0 = (g * DSA_GROUP + r) * DSA_HEAD_DIM
            o_ref[:, c0:c0 + DSA_HEAD_DIM] = og[:, r * tq:(r + 1) * tq].T.astype(BF16)


def _dsa_t(pf, pb, ki, k, vt, *, n_seq, rows_per_seq, tk, n_keys, topk):
    tq = LANES
    nk_pad = k.shape[1]
    steps = rows_per_seq // tq

    def rmap(c):
        return lambda s, t: (s * steps + t, c)

    kern = functools.partial(_dsa_t_kernel, tq=tq, tk=tk, n_keys=n_keys, topk=topk)
    single = pl.Buffered(1)
    return pl.pallas_call(
        kern,
        grid=(n_seq, steps),
        in_specs=[
            pl.BlockSpec((tq, IDX_Q), rmap(B_IQ // IDX_Q)),
            pl.BlockSpec((tq, LANES), rmap(F_SMALL // LANES)),
            pl.BlockSpec((tq, DSA_Q), rmap(B_DQ // DSA_Q)),
            pl.BlockSpec((1, nk_pad, IDX_DIM), lambda s, t: (s, 0, 0), pipeline_mode=single),
            pl.BlockSpec((1, nk_pad, DSA_KV), lambda s, t: (s, 0, 0), pipeline_mode=single),
            pl.BlockSpec((1, DSA_KV_HEADS, V_ROWS, nk_pad), lambda s, t: (s, 0, 0, 0), pipeline_mode=single),
        ],
        out_specs=pl.BlockSpec((tq, DSA_Q), lambda s, t: (s * steps + t, 0)),
        out_shape=jax.ShapeDtypeStruct((n_seq * rows_per_seq, DSA_Q), BF16),
        scratch_shapes=[
            pltpu.VMEM((nk_pad, tq), I32),
            pltpu.VMEM((DSA_KV_HEADS, 8, DSA_GROUP * tq), F32),
            pltpu.VMEM((DSA_KV_HEADS, V_ROWS, DSA_GROUP * tq), F32),
            pltpu.VMEM((2, tk, DSA_GROUP * tq), F32),
            pltpu.VMEM((nk_pad + SLAB_ROWS, tq), I32),
            pltpu.VMEM((_round_up(nk_pad // BIT_GROUP * 8, SLAB_ROWS), tq), I32),
        ],
        compiler_params=pltpu.CompilerParams(
            dimension_semantics=("parallel", "arbitrary"), vmem_limit_bytes=VMEM_LIMIT_BYTES),
        name="dsa_t",
    )(pb, pf, pb, ki, k, vt)


def _out_kernel(x_ref, ya_ref, ob_ref, z_ref, ma_ref, mb_ref, wg_ref, wd_ref, wo_ref,
                gate_ref, lig_ref, lib_ref, lg_ref, lb_ref, o_ref, *, alpha):
    hn = _layer_norm_rows(x_ref[...], lig_ref[...], lib_ref[...])
    y_a = _dot(ya_ref[...], wg_ref[...])
    z = z_ref[...].astype(F32)
    yb_in = ob_ref[...].astype(F32) * (z * jax.nn.sigmoid(z))
    y_b = _dot(yb_in.astype(BF16), wd_ref[...])
    ga = jax.nn.sigmoid(ma_ref[...].astype(F32) + gate_ref[0:1, :])
    gb = jax.nn.sigmoid(mb_ref[...].astype(F32) + gate_ref[1:2, :])
    merged = ga * y_a + gb * y_b
    y = alpha * hn + _dot(merged.astype(BF16), wo_ref[...])
    o_ref[...] = _layer_norm_rows(y, lg_ref[...], lb_ref[...])


def _out(x, ya, ob, pb, wg, wd, wo, gate, lig, lib, lg, lb, tm, alpha, row0):
    rows = ya.shape[0]
    base = row0 // tm
    row = lambda c: pl.BlockSpec((tm, D_MODEL), lambda i, c=c: (i, c))
    off = lambda c: pl.BlockSpec((tm, D_MODEL), lambda i, c=c: (base + i, c))
    full = lambda a: pl.BlockSpec(a.shape, lambda i: (0, 0))
    return pl.pallas_call(
        functools.partial(_out_kernel, alpha=alpha),
        grid=(rows // tm,),
        in_specs=[off(0), row(0), row(0), off(B_DZ // D_MODEL), off(B_MA // D_MODEL), off(B_MB // D_MODEL),
                  full(wg), full(wd), full(wo), full(gate), full(lig), full(lib), full(lg), full(lb)],
        out_specs=pl.BlockSpec((tm, D_MODEL), lambda i: (i, 0)),
        out_shape=jax.ShapeDtypeStruct((rows, D_MODEL), F32),
        compiler_params=pltpu.CompilerParams(
            dimension_semantics=("parallel",), vmem_limit_bytes=VMEM_LIMIT_BYTES),
        name="out_proj",
    )(x, ya, ob, pb, pb, pb, wg, wd, wo, gate, lig, lib, lg, lb)


def _round_up(a, b):
    return -(-a // b) * b


def _forward(x_prompt, x_sample, cache_k, cache_v, cache_idx_k, state_gla, meta, ln_in_g, ln_in_b,
             w_in, gla_w2, gla_gate_b, gla_norm_g, idx_kn_g, idx_kn_b, w_gla, w_dsa, gate_b,
             w_out, ln_g, ln_b):
    depth = w_in.shape[0]
    assert depth == 1, "single-layer trunk"
    bsz, seq, _ = x_prompt.shape
    dbsz, dseq, _ = x_sample.shape
    past = cache_k.shape[2]
    t_p = N_META + seq
    tp = _round_up(t_p, KEY_TILE)
    assert tp % LANES == 0 and tp % GLA_BLOCK == 0
    row_s = bsz * tp
    rows = _round_up(row_s + dbsz * dseq, ROW_TILE)
    alpha = (2.0 * depth) ** 0.25

    pieces = []
    for b in range(bsz):
        pieces += [meta.astype(F32), x_prompt[b], jnp.zeros((tp - t_p, D_MODEL), F32)]
    pieces += [x_sample.reshape(dbsz * dseq, D_MODEL), jnp.zeros((rows - row_s - dbsz * dseq, D_MODEL), F32)]
    x_all = jnp.concatenate(pieces, axis=0)

    w = w_in[0]
    w_f = jnp.concatenate([_col(w, _GQ), _col(w, _GK), _col(w, _DK), _col(w, _DV), _col(w, _IK),
                           _col(w, _GLOW), _col(w, _IW),
                           jnp.zeros((D_MODEL, LANES - IDX_DIM - GLA_RANK - IDX_HEADS), F32)], axis=1).astype(BF16)
    w_dq = _col(w, _DQ).reshape(D_MODEL, DSA_KV_HEADS, DSA_GROUP, DSA_HEAD_DIM)
    w_dq = (jnp.swapaxes(w_dq, 1, 2) * (DSA_SCALE * math.log2(math.e))).reshape(D_MODEL, DSA_Q)
    w_b = jnp.concatenate([_col(w, _GV), _col(w, _GR), w_dq, _col(w, _DZ), _col(w, _MA), _col(w, _MB),
                           _col(w, _IQ)], axis=1).astype(BF16)
    lig = ln_in_g.reshape(1, D_MODEL)
    lib = ln_in_b.reshape(1, D_MODEL)

    pf = _proj_f32(x_all, lig, lib, w_f, idx_kn_g[0].reshape(1, IDX_DIM), idx_kn_b[0].reshape(1, IDX_DIM),
                   ROW_TILE // 4)
    pb = _proj_bf16(x_all, lig, lib, w_b, ROW_TILE // 4, B_COL_TILE)

    w2pad = jnp.zeros((LANES, GLA_QK), F32).at[SM_GLOW:SM_GLOW + GLA_RANK].set(gla_w2[0]).astype(BF16)
    gbias = gla_gate_b[0].reshape(1, GLA_QK)
    ng = gla_norm_g[0].reshape(1, GLA_DV)
    blk_p = min(GLA_BLOCK, t_p)
    cm_p = jnp.asarray(np.tile(_gla_decay_matrix(blk_p), (1, 3)), BF16)
    s0_p = jnp.zeros((bsz, GLA_HEADS, GLA_DV, GLA_DK), F32)
    ya_p, st_p = _gla(pf, pb, cm_p, w2pad, gbias, ng, s0_p, n_seq=bsz, row0=0, rows_per_seq=tp,
                      t_valid=t_p, blk=blk_p, n_blk=KEY_TILE // blk_p)
    blk_s = min(GLA_BLOCK, dseq)
    assert dseq % blk_s == 0 and row_s % dseq == 0
    cm_s = jnp.asarray(np.tile(_gla_decay_matrix(blk_s), (1, 3)), BF16)
    s0_s = jnp.swapaxes(state_gla[0], -1, -2)
    ya_s, st_s = _gla(pf, pb, cm_s, w2pad, gbias, ng, s0_s, n_seq=dbsz, row0=row_s, rows_per_seq=dseq,
                      t_valid=dseq, blk=blk_s, n_blk=dseq // blk_s)

    dk = pf[:, F_DK:F_DK + DSA_KV]
    dv = pf[:, F_DV:F_DV + DSA_KV]
    ki = pf[:, F_SMALL:F_SMALL + IDX_DIM]
    dk_p = dk[:row_s].reshape(bsz, tp, DSA_KV_HEADS, DSA_HEAD_DIM)
    dv_p = dv[:row_s].reshape(bsz, tp, DSA_KV_HEADS, DSA_HEAD_DIM)
    ki_p = ki[:row_s].reshape(bsz, tp, IDX_DIM)
    vt_p = jnp.concatenate([jnp.transpose(dv_p, (0, 2, 3, 1)),
                            jnp.ones((bsz, DSA_KV_HEADS, 1, tp), F32),
                            jnp.zeros((bsz, DSA_KV_HEADS, V_ROWS - DSA_HEAD_DIM - 1, tp), F32)], axis=2).astype(BF16)
    topk_p = min(TOPK_MAX, (t_p - N_META) // 4)
    ob_p = _dsa_t(pf, pb, ki_p.astype(BF16), dk[:row_s].reshape(bsz, tp, DSA_KV).astype(BF16), vt_p,
                  n_seq=bsz, rows_per_seq=tp, tk=KEY_TILE, n_keys=t_p, topk=topk_p)

    n_keys_s = past + dseq
    nk_s = _round_up(n_keys_s, KEY_TILE)
    dk_s = dk[row_s:row_s + dbsz * dseq].reshape(dbsz, dseq, DSA_KV_HEADS, DSA_HEAD_DIM)
    dv_s = dv[row_s:row_s + dbsz * dseq].reshape(dbsz, dseq, DSA_KV_HEADS, DSA_HEAD_DIM)
    ki_s = ki[row_s:row_s + dbsz * dseq].reshape(dbsz, dseq, IDX_DIM)
    kpad = jnp.zeros((dbsz, nk_s - n_keys_s, DSA_KV_HEADS, DSA_HEAD_DIM), F32)
    k_all = jnp.concatenate([cache_k[0], dk_s, kpad], axis=1)
    v_all = jnp.concatenate([cache_v[0], dv_s, kpad], axis=1)
    ki_all = jnp.concatenate([cache_idx_k[0], ki_s, jnp.zeros((dbsz, nk_s - n_keys_s, IDX_DIM), F32)], axis=1)
    kt_s = jnp.transpose(k_all, (0, 2, 3, 1)).reshape(dbsz, DSA_KV, nk_s).astype(BF16)
    v_s = v_all.reshape(dbsz, nk_s, DSA_KV).astype(BF16)
    kit_s = jnp.transpose(ki_all, (0, 2, 1)).astype(BF16)
    topk_s = min(TOPK_MAX, n_keys_s // 4)
    ob_s = _dsa(pf, pb, kit_s, kt_s, v_s, n_seq=dbsz, row0=row_s, rows_per_seq=dseq, tq=dseq, tk=KEY_TILE,
                n_keys=n_keys_s, topk=topk_s)

    out_w = (w_gla[0].astype(BF16), w_dsa[0].astype(BF16), w_out[0].astype(BF16), gate_b[0], lig, lib,
             ln_g[0].reshape(1, D_MODEL), ln_b[0].reshape(1, D_MODEL))
    tm_out = ROW_TILE // 4
    tm_s = math.gcd(tm_out, dbsz * dseq)
    assert row_s % tm_out == 0 and tm_s % 16 == 0
    y_p = _out(x_all, ya_p, ob_p, pb, *out_w, tm_out, alpha, 0)
    y_s = _out(x_all, ya_s, ob_s, pb, *out_w, tm_s, alpha, row_s)

    y_prompt = y_p.reshape(bsz, tp, D_MODEL)[:, N_META:t_p]
    y_sample = y_s.reshape(dbsz, dseq, D_MODEL)
    k_prompt = dk_p[:, :t_p][None]
    v_prompt = dv_p[:, :t_p][None]
    idx_k_prompt = ki_p[:, :t_p][None]
    gla_prompt = jnp.swapaxes(st_p, -1, -2)[None]
    k_sample = dk_s[None]
    v_sample = dv_s[None]
    idx_k_sample = ki_s[None]
    gla_sample = jnp.swapaxes(st_s, -1, -2)[None]
    return (y_prompt, y_sample, k_prompt, v_prompt, idx_k_prompt, gla_prompt,
            k_sample, v_sample, idx_k_sample, gla_sample)


def kernel(x_prompt, x_sample, cache_k, cache_v, cache_idx_k, state_gla, meta, ln_in_g, ln_in_b,
           w_in, gla_w2, gla_gate_b, gla_norm_g, idx_kn_g, idx_kn_b, w_gla, w_dsa, gate_b,
           w_out, ln_g, ln_b):
    return _forward(x_prompt, x_sample, cache_k, cache_v, cache_idx_k, state_gla, meta, ln_in_g, ln_in_b,
                    w_in, gla_w2, gla_gate_b, gla_norm_g, idx_kn_g, idx_kn_b, w_gla, w_dsa, gate_b,
                    w_out, ln_g, ln_b)
```

```python
import functools
import math

import numpy as np
import jax
import jax.numpy as jnp
from jax import lax
from jax.experimental import pallas as pl
from jax.experimental.pallas import tpu as pltpu

F32 = jnp.float32
BF16 = jnp.bfloat16
I32 = jnp.int32

D_MODEL = 1024
CHUNK = 64
N_META = 16
GLA_HEADS = 4
GLA_DK = 128
GLA_DV = 256
GLA_RANK = 16
GLA_TAU = 16.0
GLA_BLOCK = 64
DSA_HEADS = 16
DSA_KV_HEADS = 4
DSA_HEAD_DIM = 64
DSA_GROUP = DSA_HEADS // DSA_KV_HEADS
DSA_SCALE = DSA_HEAD_DIM ** -0.5
IDX_HEADS = 8
IDX_DIM = 64
IDX_W_SCALE = (IDX_HEADS ** -0.5) * (IDX_DIM ** -0.5)
TOPK_MAX = 256
NORM_EPS = 1e-5
GLA_QK = GLA_HEADS * GLA_DK
GLA_V = GLA_HEADS * GLA_DV
DSA_Q = DSA_HEADS * DSA_HEAD_DIM
DSA_KV = DSA_KV_HEADS * DSA_HEAD_DIM
IDX_Q = IDX_HEADS * IDX_DIM
CHUNK_SHIFT = CHUNK.bit_length() - 1
HEAD_SHIFT = DSA_HEAD_DIM.bit_length() - 1
SPLITS =(GLA_QK, GLA_QK, GLA_V, GLA_RANK, GLA_V,
          DSA_Q, DSA_KV, DSA_KV, IDX_Q, IDX_DIM, IDX_HEADS, DSA_Q,
          D_MODEL, D_MODEL)
_OFF = tuple(int(o) for o in np.cumsum((0,) + SPLITS))
(_GQ, _GK, _GV, _GLOW, _GR, _DQ, _DK, _DV, _IQ, _IK, _IW, _DZ, _MA, _MB) = range(14)

LANES = 128
VMEM_LIMIT_BYTES = 56 * 1024 * 1024

F_GQ, F_GK, F_DK, F_DV, F_SMALL = 0, 512, 1024, 1280, 1536
F_COLS = F_SMALL + LANES
SM_IK, SM_GLOW, SM_IW = 0, IDX_DIM, IDX_DIM + GLA_RANK
B_GV, B_GR, B_DQ, B_DZ, B_MA, B_MB, B_IQ = 0, 1024, 2048, 3072, 4096, 5120, 6144
B_COLS = B_IQ + IDX_Q
B_COL_TILE = B_COLS // 2

ROW_TILE = 2048
KEY_TILE = 512
NEG_BIG = -1e30
INT_MIN = -(2 ** 31)
NEG_INF_KEY = int(np.array(-np.inf, np.float32).view(np.int32)) ^ 0x7FFFFFFF


def _col(w, idx):
    return w[:, _OFF[idx]:_OFF[idx + 1]]


def _layer_norm_rows(x, g, b):
    mu = jnp.mean(x, axis=-1, keepdims=True)
    xc = x - mu
    var = jnp.mean(xc * xc, axis=-1, keepdims=True)
    return xc * lax.rsqrt(var + NORM_EPS) * g + b


def _dot(a, b):
    return jnp.dot(a, b, preferred_element_type=F32)


def _dot_nt(a, b):
    return lax.dot_general(a, b, (((1,), (1,)), ((), ())), preferred_element_type=F32)


def _dot_tn(a, b):
    return lax.dot_general(a, b, (((0,), (0,)), ((), ())), preferred_element_type=F32)


def _proj_f32_kernel(x_ref, g_ref, b_ref, w_ref, kg_ref, kb_ref, o_ref):
    hn = _layer_norm_rows(x_ref[...], g_ref[...], b_ref[...])
    y = _dot(hn.astype(BF16), w_ref[...])
    o_ref[...] = y
    ik = y[:, F_SMALL:F_SMALL + IDX_DIM]
    o_ref[:, F_SMALL:F_SMALL + IDX_DIM] = _layer_norm_rows(ik, kg_ref[...], kb_ref[...])


def _proj_f32(x, g, b, w, kg, kb, tm):
    rows = x.shape[0]
    return pl.pallas_call(
        _proj_f32_kernel,
        grid=(rows // tm,),
        in_specs=[
            pl.BlockSpec((tm, D_MODEL), lambda i: (i, 0)),
            pl.BlockSpec((1, D_MODEL), lambda i: (0, 0)),
            pl.BlockSpec((1, D_MODEL), lambda i: (0, 0)),
            pl.BlockSpec((D_MODEL, F_COLS), lambda i: (0, 0)),
            pl.BlockSpec((1, IDX_DIM), lambda i: (0, 0)),
            pl.BlockSpec((1, IDX_DIM), lambda i: (0, 0)),
        ],
        out_specs=pl.BlockSpec((tm, F_COLS), lambda i: (i, 0)),
        out_shape=jax.ShapeDtypeStruct((rows, F_COLS), F32),
        compiler_params=pltpu.CompilerParams(
            dimension_semantics=("parallel",), vmem_limit_bytes=VMEM_LIMIT_BYTES),
        name="proj_f32",
    )(x, g, b, w, kg, kb)


def _proj_bf16_kernel(x_ref, g_ref, b_ref, w_ref, o_ref, hn_ref):
    @pl.when(pl.program_id(1) == 0)
    def _():
        hn_ref[...] = _layer_norm_rows(x_ref[...], g_ref[...], b_ref[...]).astype(BF16)

    o_ref[...] = _dot(hn_ref[...], w_ref[...]).astype(BF16)


def _proj_bf16(x, g, b, w, tm, tn):
    rows = x.shape[0]
    return pl.pallas_call(
        _proj_bf16_kernel,
        grid=(rows // tm, B_COLS // tn),
        in_specs=[
            pl.BlockSpec((tm, D_MODEL), lambda i, j: (i, 0)),
            pl.BlockSpec((1, D_MODEL), lambda i, j: (0, 0)),
            pl.BlockSpec((1, D_MODEL), lambda i, j: (0, 0)),
            pl.BlockSpec((D_MODEL, tn), lambda i, j: (0, j)),
        ],
        out_specs=pl.BlockSpec((tm, tn), lambda i, j: (i, j)),
        out_shape=jax.ShapeDtypeStruct((rows, B_COLS), BF16),
        scratch_shapes=[pltpu.VMEM((tm, D_MODEL), BF16)],
        compiler_params=pltpu.CompilerParams(
            dimension_semantics=("parallel", "arbitrary"), vmem_limit_bytes=VMEM_LIMIT_BYTES),
        name="proj_bf16",
    )(x, g, b, w)


def _gla_levels(blk):
    levels = []
    s = blk // 2
    while s >= 1:
        levels.append(s)
        s //= 2
    return levels


def _gla_decay_matrix(blk):
    i = np.arange(blk)[:, None]
    t = np.arange(blk)[None, :]
    mats = [(t <= i).astype(np.float32), (t > i).astype(np.float32)]
    for s in _gla_levels(blk):
        mid = (i // (2 * s)) * (2 * s) + s - 1
        lower = (i // s) % 2 == 1
        m = np.where(lower, ((t > mid) & (t <= i)), False).astype(np.float32)
        n = np.where(~lower, ((t > i) & (t <= mid)), False).astype(np.float32)
        mats.append(m + n)
    return np.concatenate(mats, axis=0)


def _gla_kernel(q_ref, k_ref, sm_ref, v_ref, r_ref, c_ref, w2_ref, gb_ref, ng_ref, s0_ref,
                o_ref, st_ref, s_scr, *, blk, n_blk, t_valid):
    ti = pl.program_id(1)
    levels = _gla_levels(blk)

    @pl.when(ti == 0)
    def _():
        s_scr[...] = s0_ref[0]

    hb = GLA_HEADS * blk
    ri = lax.broadcasted_iota(I32, (hb, hb), 0)
    ci = lax.broadcasted_iota(I32, (hb, hb), 1)
    level_id = jnp.where(ri == ci, 0, -1)
    for li, s in enumerate(levels):
        sh = s.bit_length() - 1
        same = (ri >> (sh + 1)) == (ci >> (sh + 1))
        level_id = jnp.where(same & (((ri >> sh) & 1) == 1) & (((ci >> sh) & 1) == 0), 1 + li, level_id)

    def stack(x, width):
        return jnp.concatenate([x[:, h * width:(h + 1) * width] for h in range(GLA_HEADS)], axis=0)
    row_iota = lax.broadcasted_iota(I32, (blk, 1), 0)
    cmat = c_ref[...]
    w2 = w2_ref[...]
    gbias = gb_ref[...]
    ng = ng_ref[...]

    def block(j):
        r0 = pl.multiple_of(j * blk, blk)
        rows = pl.ds(r0, blk)
        valid = (ti * (n_blk * blk) + r0 + row_iota) < t_valid
        gq = q_ref[rows, :] * (GLA_DK ** -0.5)
        gk = jnp.where(valid, k_ref[rows, :], 0.0)
        x = _dot(sm_ref[rows, :].astype(BF16), w2) + gbias
        logf = (jnp.minimum(x, 0.0) - jnp.log1p(jnp.exp(-jnp.abs(x)))) * (1.0 / GLA_TAU)
        logf = jnp.where(valid, logf, 0.0)
        hi = logf.astype(BF16)
        r1 = logf - hi.astype(F32)
        mid = r1.astype(BF16)
        lo = (r1 - mid.astype(F32)).astype(BF16)
        e_all = _dot(cmat, jnp.concatenate([hi, mid, lo], axis=0))
        qs = stack(gq, GLA_DK)
        ks = stack(gk, GLA_DK)
        vs = stack(v_ref[rows, :], GLA_DV)
        b_s = stack(e_all[0:blk], GLA_DK)
        a = jnp.where(level_id == 0, _dot_nt(qs.astype(BF16), ks.astype(BF16)), 0.0)
        for li in range(len(levels)):
            e = jnp.exp(stack(e_all[(2 + li) * blk:(3 + li) * blk], GLA_DK))
            p = _dot_nt((qs * e).astype(BF16), (ks * e).astype(BF16))
            a = jnp.where(level_id == 1 + li, p, a)
        o_intra = _dot(a.astype(BF16), vs)
        qe = (qs * jnp.exp(b_s)).astype(BF16)
        kd = (ks * jnp.exp(stack(e_all[blk:2 * blk], GLA_DK))).astype(BF16)
        for h in range(GLA_HEADS):
            hs = slice(h * blk, (h + 1) * blk)
            vsl = slice(h * GLA_DV, (h + 1) * GLA_DV)
            st = s_scr[h]
            o = _dot_nt(qe[hs], st.astype(BF16)) + o_intra[hs]
            dec = jnp.exp(b_s[(h + 1) * blk - 1:(h + 1) * blk, :])
            s_scr[h] = st * dec + _dot_tn(vs[hs], kd[hs])
            on = o * lax.rsqrt(jnp.mean(o * o, axis=-1, keepdims=True) + NORM_EPS) * ng
            gr = r_ref[rows, vsl].astype(F32)
            o_ref[rows, vsl] = (on * (gr * jax.nn.sigmoid(gr))).astype(BF16)

    n_live = jnp.clip((t_valid - ti * (n_blk * blk) + blk - 1) // blk, 0, n_blk)

    @pl.when(n_live < n_blk)
    def _():
        o_ref[...] = jnp.zeros(o_ref.shape, o_ref.dtype)

    def block_pair(j, carry):
        block(2 * j)
        block(2 * j + 1)
        return carry

    lax.fori_loop(0, n_live // 2, block_pair, 0)

    @pl.when(n_live % 2 == 1)
    def _():
        block(n_live - 1)

    @pl.when(ti == pl.num_programs(1) - 1)
    def _():
        st_ref[0] = s_scr[...]


def _gla(pf, pb, cmat, w2pad, gbias, ng, s0t, *, n_seq, row0, rows_per_seq, t_valid, blk, n_blk):
    rb = blk * n_blk
    steps = rows_per_seq // rb
    base = row0 // rb

    def rmap(c):
        return lambda s, t: (base + s * steps + t, c)

    kern = functools.partial(_gla_kernel, blk=blk, n_blk=n_blk, t_valid=t_valid)
    return pl.pallas_call(
        kern,
        grid=(n_seq, steps),
        in_specs=[
            pl.BlockSpec((rb, GLA_QK), rmap(F_GQ // GLA_QK)),
            pl.BlockSpec((rb, GLA_QK), rmap(F_GK // GLA_QK)),
            pl.BlockSpec((rb, LANES), rmap(F_SMALL // LANES)),
            pl.BlockSpec((rb, GLA_V), rmap(B_GV // GLA_V)),
            pl.BlockSpec((rb, GLA_V), rmap(B_GR // GLA_V)),
            pl.BlockSpec(cmat.shape, lambda s, t: (0, 0)),
            pl.BlockSpec(w2pad.shape, lambda s, t: (0, 0)),
            pl.BlockSpec((1, GLA_QK), lambda s, t: (0, 0)),
            pl.BlockSpec((1, GLA_DV), lambda s, t: (0, 0)),
            pl.BlockSpec((1, GLA_HEADS, GLA_DV, GLA_DK), lambda s, t: (s, 0, 0, 0)),
        ],
        out_specs=[
            pl.BlockSpec((rb, GLA_V), lambda s, t: (s * steps + t, 0)),
            pl.BlockSpec((1, GLA_HEADS, GLA_DV, GLA_DK), lambda s, t: (s, 0, 0, 0)),
        ],
        out_shape=[
            jax.ShapeDtypeStruct((n_seq * rows_per_seq, GLA_V), BF16),
            jax.ShapeDtypeStruct((n_seq, GLA_HEADS, GLA_DV, GLA_DK), F32),
        ],
        scratch_shapes=[pltpu.VMEM((GLA_HEADS, GLA_DV, GLA_DK), F32)],
        compiler_params=pltpu.CompilerParams(
            dimension_semantics=("parallel", "arbitrary"), vmem_limit_bytes=VMEM_LIMIT_BYTES),
        name="gla",
    )(pf, pf, pf, pb, pb, cmat, w2pad, gbias, ng, s0t)


def _dsa_kernel(iq_ref, sm_ref, qd_ref, kit_ref, kt_ref, v_ref, o_ref,
                key_scr, m_scr, l_scr, acc_scr, *, tq, tk, n_keys, topk, n_kb_total):
    n_kb = n_kb_total
    lane_iota = lax.broadcasted_iota(I32, (1, tk), 1)

    iq = iq_ref[...]
    iq_hm = jnp.concatenate([iq[:, h * IDX_DIM:(h + 1) * IDX_DIM] for h in range(IDX_HEADS)], axis=0)
    w_i = sm_ref[:, SM_IW:SM_IW + IDX_HEADS] * IDX_W_SCALE
    w_cols = [jnp.broadcast_to(w_i[:, h:h + 1], (tq, LANES)) for h in range(IDX_HEADS)]

    def score_block(kb, carry):
        k0 = pl.multiple_of(kb * tk, tk)
        r = _dot(iq_hm, kit_ref[0, :, pl.ds(k0, tk)])
        sc = None
        for h in range(IDX_HEADS):
            rh = jnp.maximum(r[h * tq:(h + 1) * tq, :], 0.0)
            wh = jnp.concatenate([w_cols[h]] * (tk // LANES), axis=1)
            sc = rh * wh if sc is None else sc + rh * wh
        sc = jnp.where(k0 + lane_iota < n_keys, sc, -jnp.inf)
        bits = pltpu.bitcast(sc, I32)
        key_scr[:, pl.ds(k0, tk)] = bits ^ ((bits >> 31) & 0x7FFFFFFF)
        return carry

    lax.fori_loop(0, n_kb, score_block, 0)

    n_cols = n_kb * (tk // LANES)

    def count_ge(cand):
        cb = jnp.broadcast_to(cand, (tq, LANES))

        def body(kb, acc):
            for u in range(tk // LANES):
                c0 = pl.multiple_of(kb * tk + u * LANES, LANES)
                acc = acc + jnp.where(key_scr[:, pl.ds(c0, LANES)] >= cb, 1, 0)
            return acc

        acc = lax.fori_loop(0, n_kb, body, jnp.zeros((tq, LANES), I32))
        return jnp.sum(acc, axis=1, keepdims=True)

    bits_per_round = 4

    def search_cond(carry):
        i, lo, cge = carry
        return (i < 32) & (jnp.max(jnp.abs(cge - topk)) > 0)

    def search_round(carry):
        i, lo, cge = carry
        for u in range(bits_per_round):
            cand = lo + jnp.left_shift(jnp.int32(1), 31 - u - i)
            cnt = count_ge(cand)
            take = cnt >= topk
            lo = jnp.where(take, cand, lo)
            cge = jnp.where(take, cnt, cge)
        return i + bits_per_round, lo, cge

    lo0 = jnp.full((tq, 1), INT_MIN, I32)
    cge0 = jnp.zeros((tq, 1), I32) + n_cols * LANES
    _, thr, cge = lax.while_loop(search_cond, search_round, (jnp.int32(0), lo0, cge0))
    need_tie = (cge > topk) & (thr > NEG_INF_KEY)
    thr = jnp.maximum(thr, NEG_INF_KEY + 1)

    @pl.when(jnp.max(need_tie.astype(I32)) > 0)
    def _():
        want = topk - count_ge(thr + 1)
        thr_b = jnp.broadcast_to(thr, (tq, LANES))
        col_iota = lax.broadcasted_iota(I32, (tq, LANES), 1)

        def count_tied_below(m):
            mb = jnp.broadcast_to(m, (tq, LANES))

            def body(c, acc):
                c0 = pl.multiple_of(c * LANES, LANES)
                hit = (key_scr[:, pl.ds(c0, LANES)] == thr_b) & ((c0 + col_iota) < mb)
                return acc + jnp.where(hit, 1, 0)

            acc = lax.fori_loop(0, n_cols, body, jnp.zeros((tq, LANES), I32))
            return jnp.sum(acc, axis=1, keepdims=True)

        n_bits = int(math.ceil(math.log2(n_kb_total * tk + 1)))

        def idx_step(i, m):
            cand = m + jnp.left_shift(jnp.int32(1), n_bits - 1 - i)
            return jnp.where(count_tied_below(cand) <= want, cand, m)

        m_keep = lax.fori_loop(0, n_bits, idx_step, jnp.zeros((tq, 1), I32))
        mk_b = jnp.broadcast_to(jnp.where(need_tie, m_keep, jnp.int32(2 ** 30)), (tq, LANES))

        def demote(c, carry):
            c0 = pl.multiple_of(c * LANES, LANES)
            kv = key_scr[:, pl.ds(c0, LANES)]
            drop = (kv == thr_b) & ((c0 + col_iota) >= mk_b)
            key_scr[:, pl.ds(c0, LANES)] = jnp.where(drop, kv - 1, kv)
            return carry

        lax.fori_loop(0, n_cols, demote, 0)

    qd = qd_ref[...]
    win = lax.broadcasted_iota(I32, (tq, DSA_KV), 1) >> HEAD_SHIFT
    q_all = jnp.concatenate(
        [jnp.where(win == g, qd[:, r * DSA_KV:(r + 1) * DSA_KV].astype(F32), 0.0).astype(BF16)
         for g in range(DSA_KV_HEADS) for r in range(DSA_GROUP)], axis=0)
    m_scr[...] = jnp.full(m_scr.shape, NEG_BIG, F32)
    l_scr[...] = jnp.zeros(l_scr.shape, F32)
    acc_scr[...] = jnp.zeros(acc_scr.shape, F32)
    thr_t = jnp.broadcast_to(thr, (tq, LANES))

    def attend_block(kb, carry):
        k0 = pl.multiple_of(kb * tk, tk)
        sel = key_scr[:, pl.ds(k0, tk)] >= jnp.concatenate([thr_t] * (tk // LANES), axis=1)
        bias = jnp.where(sel, 0.0, NEG_BIG)
        s = _dot(q_all, kt_ref[0, :, pl.ds(k0, tk)]) + jnp.concatenate([bias] * DSA_HEADS, axis=0)
        m_old = m_scr[...]
        m_new = jnp.maximum(m_old, jnp.max(s, axis=1, keepdims=True))
        alpha = jnp.exp2(m_old - m_new)
        p = jnp.exp2(s - m_new)
        l_scr[...] = alpha * l_scr[...] + jnp.sum(p, axis=1, keepdims=True)
        acc_scr[...] = alpha * acc_scr[...] + _dot(p.astype(BF16), v_ref[0, pl.ds(k0, tk), :])
        m_scr[...] = m_new
        return carry

    lax.fori_loop(0, n_kb, attend_block, 0)

    o_all = acc_scr[...] / l_scr[...]
    for g in range(DSA_KV_HEADS):
        for r in range(DSA_GROUP):
            h = g * DSA_GROUP + r
            o_ref[:, h * DSA_HEAD_DIM:(h + 1) * DSA_HEAD_DIM] = (
                o_all[h * tq:(h + 1) * tq, g * DSA_HEAD_DIM:(g + 1) * DSA_HEAD_DIM].astype(BF16))


def _dsa(pf, pb, kit, kt, v, *, n_seq, row0, rows_per_seq, tq, tk, n_keys, topk):
    nk_pad = kit.shape[-1]
    n_kb_total = nk_pad // tk
    steps = rows_per_seq // tq
    base = row0 // tq

    def rmap(c):
        return lambda s, t: (base + s * steps + t, c)

    kern = functools.partial(_dsa_kernel, tq=tq, tk=tk, n_keys=n_keys, topk=topk, n_kb_total=n_kb_total)
    single = pl.Buffered(1)
    return pl.pallas_call(
        kern,
        grid=(n_seq, steps),
        in_specs=[
            pl.BlockSpec((tq, IDX_Q), rmap(B_IQ // IDX_Q)),
            pl.BlockSpec((tq, LANES), rmap(F_SMALL // LANES)),
            pl.BlockSpec((tq, DSA_Q), rmap(B_DQ // DSA_Q)),
            pl.BlockSpec((1, IDX_DIM, nk_pad), lambda s, t: (s, 0, 0), pipeline_mode=single),
            pl.BlockSpec((1, DSA_KV, nk_pad), lambda s, t: (s, 0, 0), pipeline_mode=single),
            pl.BlockSpec((1, nk_pad, DSA_KV), lambda s, t: (s, 0, 0), pipeline_mode=single),
        ],
        out_specs=pl.BlockSpec((tq, DSA_Q), lambda s, t: (s * steps + t, 0)),
        out_shape=jax.ShapeDtypeStruct((n_seq * rows_per_seq, DSA_Q), BF16),
        scratch_shapes=[
            pltpu.VMEM((tq, nk_pad), I32),
            pltpu.VMEM((DSA_HEADS * tq, 1), F32),
            pltpu.VMEM((DSA_HEADS * tq, 1), F32),
            pltpu.VMEM((DSA_HEADS * tq, DSA_KV), F32),
        ],
        compiler_params=pltpu.CompilerParams(
            dimension_semantics=("parallel", "arbitrary"), vmem_limit_bytes=VMEM_LIMIT_BYTES),
        name="dsa",
    )(pb, pf, pb, kit, kt, v)


BIT_GROUP = 256
SLAB_ROWS = 128
BLOCK_UNROLLS = (4, 2, 1)


def _for_blocks(block_fn, start, stop):
    for u in BLOCK_UNROLLS:
        trips = (stop - start) // u

        def body(j, carry, u=u, start=start):
            for t in range(u):
                block_fn(start + u * j + t)
            return carry

        lax.fori_loop(0, trips, body, 0)
        start = start + trips * u


def _slab(g):
    return g * DSA_HEAD_DIM // LANES * LANES


def _bit_transpose32(words):
    a = list(words)
    mask, j = 0x0000FFFF, 16
    while j:
        k = 0
        while k < 32:
            t = (a[k] ^ lax.shift_right_logical(a[k + j], jnp.int32(j))) & mask
            a[k] = a[k] ^ t
            a[k + j] = a[k + j] ^ (t << j)
            k = (k + j + 1) & ~j
        j >>= 1
        mask ^= (mask << j) & 0xFFFFFFFF
    return a


V_ROWS = 80
V_ONES_ROW = DSA_HEAD_DIM


def _dsa_t_kernel(iq_ref, sm_ref, qd_ref, ki_ref, k_ref, vt_ref, o_ref,
                  key_scr, m_scr, acc_scr, s_scr, plane_scr, active_scr, *, tq, tk, n_keys, topk):
    live = pl.program_id(1) * tq < n_keys

    @pl.when(live)
    def _():
        _dsa_t_body(iq_ref, sm_ref, qd_ref, ki_ref, k_ref, vt_ref, o_ref, key_scr, m_scr, acc_scr, s_scr,
                    plane_scr, active_scr,
                    tq=tq, tk=tk, n_keys=n_keys, topk=topk)

    @pl.when(jnp.logical_not(live))
    def _():
        o_ref[...] = jnp.zeros(o_ref.shape, o_ref.dtype)


def _dsa_t_body(iq_ref, sm_ref, qd_ref, ki_ref, k_ref, vt_ref, o_ref,
                key_scr, m_scr, acc_scr, s_scr, plane_scr, active_scr, *, tq, tk, n_keys, topk):
    qb = pl.program_id(1)
    q0 = qb * tq
    q_chunk_max = (q0 + tq - 1 - N_META) >> CHUNK_SHIFT
    limit = jnp.minimum(N_META + CHUNK * (q_chunk_max + 1), n_keys)
    n_kb = (limit + tk - 1) // tk
    n_full = jnp.minimum(jnp.minimum((q0 + N_META) // tk, n_keys // tk), n_kb)

    q_chunk = (q0 + lax.broadcasted_iota(I32, (1, tq), 1) - N_META) >> CHUNK_SHIFT
    row_iota = lax.broadcasted_iota(I32, (tk, tq), 0)
    plane_rows = key_scr.shape[0] // BIT_GROUP * 8

    @pl.when(qb == 0)
    def _():
        plane_scr[...] = jnp.zeros(plane_scr.shape, I32)

    iq = iq_ref[...]
    iq_hm = jnp.concatenate([iq[:, h * IDX_DIM:(h + 1) * IDX_DIM] for h in range(IDX_HEADS)], axis=0)
    w_t = sm_ref[...].T[SM_IW:SM_IW + IDX_HEADS, :] * IDX_W_SCALE
    iq_hm = iq_hm.astype(F32).T.astype(BF16)

    def score_block(kb, masked):
        k0 = pl.multiple_of(kb * tk, tk)
        r = _dot(ki_ref[0, pl.ds(k0, tk), :], iq_hm)
        sc = None
        for h in range(IDX_HEADS):
            t = jnp.maximum(r[:, h * tq:(h + 1) * tq], 0.0) * w_t[h:h + 1, :]
            sc = t if sc is None else sc + t
        if masked:
            key_pos = k0 + row_iota
            adm = (key_pos < n_keys) & (((key_pos - N_META) >> CHUNK_SHIFT) <= q_chunk)
            sc = jnp.where(adm, sc, -jnp.inf)
        bits = pltpu.bitcast(sc, I32)
        keys = bits ^ ((bits >> 31) & 0x7FFFFFFF)
        key_scr[pl.ds(k0, tk), :] = keys
        ukeys = keys ^ INT_MIN
        for u in range(tk // BIT_GROUP):
            words = [ukeys[u * BIT_GROUP + 8 * w:u * BIT_GROUP + 8 * w + 8, :] for w in range(32)]
            planes = _bit_transpose32(words)
            g_row = pl.multiple_of((kb * (tk // BIT_GROUP) + u) * 8, 8)
            for w in range(32):
                plane_scr[pl.ds(w * plane_rows + g_row, 8), :] = planes[w]

    def score_edge(kb, carry):
        score_block(kb, True)
        return carry

    _for_blocks(lambda kb: score_block(kb, False), 0, n_full)
    lax.fori_loop(n_full, n_kb, score_edge, 0)

    n_chunks = n_kb * (tk // LANES)

    sub_per_blk = tk // LANES

    def count_rows(pred):
        def body(kb, acc):
            for u in range(sub_per_blk):
                c0 = pl.multiple_of(kb * tk + u * LANES, LANES)
                acc = acc + jnp.where(pred(key_scr[pl.ds(c0, LANES), :], c0), 1, 0)
            return acc

        acc = lax.fori_loop(0, n_kb, body, jnp.zeros((LANES, tq), I32))
        return jnp.sum(acc, axis=0, keepdims=True)

    def count_ge(cand):
        return count_rows(lambda kv, c0: kv >= cand)

    n_groups = n_kb * (tk // BIT_GROUP)
    n_slabs = (n_groups * 8 + SLAB_ROWS - 1) // SLAB_ROWS
    slab_iota = lax.broadcasted_iota(I32, (SLAB_ROWS, tq), 0)

    def init_active(sl, carry):
        r0 = pl.multiple_of(sl * SLAB_ROWS, SLAB_ROWS)
        active_scr[pl.ds(r0, SLAB_ROWS), :] = jnp.where(r0 + slab_iota < n_groups * 8, -1, 0)
        return carry

    lax.fori_loop(0, n_slabs, init_active, 0)

    def select_pass(i, carry):
        thr_u, c_above, n_act, flip = carry
        cur = i * plane_rows
        prev = cur - plane_rows

        def body(sl, acc):
            r0 = pl.multiple_of(sl * SLAB_ROWS, SLAB_ROWS)
            rows = pl.ds(r0, SLAB_ROWS)
            act = active_scr[rows, :] & (plane_scr[pl.ds(pl.multiple_of(prev + r0, 8), SLAB_ROWS), :] ^ flip)
            active_scr[rows, :] = act
            return acc + lax.population_count(act & plane_scr[pl.ds(pl.multiple_of(cur + r0, 8), SLAB_ROWS), :])

        acc = lax.fori_loop(0, n_slabs, body, jnp.zeros((SLAB_ROWS, tq), I32))
        ones = jnp.sum(acc, axis=0, keepdims=True)
        take = c_above + ones >= topk
        thr_u = thr_u | jnp.where(take, jnp.left_shift(jnp.int32(1), 31 - i), 0)
        c_above = c_above + jnp.where(take, 0, ones)
        n_act = jnp.where(take, ones, n_act - ones)
        return thr_u, c_above, n_act, jnp.where(take, 0, -1)

    n_act = jnp.zeros((1, tq), I32) + n_groups * BIT_GROUP

    def count_top(sl, acc):
        rows = pl.ds(pl.multiple_of(sl * SLAB_ROWS, SLAB_ROWS), SLAB_ROWS)
        return acc + lax.population_count(active_scr[rows, :] & plane_scr[rows, :])

    acc0 = lax.fori_loop(0, n_slabs, count_top, jnp.zeros((SLAB_ROWS, tq), I32))
    ones0 = jnp.sum(acc0, axis=0, keepdims=True)
    take0 = ones0 >= topk
    carry0 = (jnp.where(take0, INT_MIN, 0), jnp.where(take0, 0, ones0),
              jnp.where(take0, ones0, n_act - ones0), jnp.where(take0, 0, -1))
    thr_u, c_above, n_act, _ = lax.fori_loop(1, 32, select_pass, carry0)
    thr = thr_u ^ INT_MIN
    cge = c_above + n_act
    need_tie = (cge > topk) & (thr > NEG_INF_KEY)
    thr = jnp.maximum(thr, NEG_INF_KEY + 1)

    @pl.when(jnp.max(need_tie.astype(I32)) > 0)
    def _():
        want = topk - count_ge(thr + 1)
        sub_iota = lax.broadcasted_iota(I32, (LANES, tq), 0)
        n_bits = int(math.ceil(math.log2(key_scr.shape[0] + 1)))

        def idx_step(i, m):
            cand = m + jnp.left_shift(jnp.int32(1), n_bits - 1 - i)
            tied_below = count_rows(lambda kv, c0: (kv == thr) & ((c0 + sub_iota) < cand))
            return jnp.where(tied_below <= want, cand, m)

        m_keep = lax.fori_loop(0, n_bits, idx_step, jnp.zeros((1, tq), I32))
        m_keep = jnp.where(need_tie, m_keep, jnp.int32(2 ** 30))

        def demote(c, carry):
            c0 = pl.multiple_of(c * LANES, LANES)
            kv = key_scr[pl.ds(c0, LANES), :]
            drop = (kv == thr) & ((c0 + sub_iota) >= m_keep)
            key_scr[pl.ds(c0, LANES), :] = jnp.where(drop, kv - 1, kv)
            return carry

        lax.fori_loop(0, n_chunks, demote, 0)

    qd = qd_ref[...]
    win = lax.broadcasted_iota(I32, (tq, LANES), 1) >> HEAD_SHIFT
    q_pad = []
    for g in range(DSA_KV_HEADS):
        lo = _slab(g)
        q_pad.append(jnp.concatenate(
            [jnp.where(win == g % 2, qd[:, r * DSA_KV + lo:r * DSA_KV + lo + LANES].astype(F32), 0.0)
             for r in range(DSA_GROUP)], axis=0))
    q_pad = [x.T.astype(BF16) for x in q_pad]
    m_scr[...] = jnp.full(m_scr.shape, NEG_BIG, F32)
    acc_scr[...] = jnp.zeros(acc_scr.shape, F32)
    s_scr[0] = _dot(k_ref[0, pl.ds(0, tk), 0:LANES], q_pad[0])

    def attend_block(kb):
        k0 = pl.multiple_of(kb * tk, tk)
        bias = jnp.where(key_scr[pl.ds(k0, tk), :] >= thr, 0.0, NEG_BIG)
        bias_g = jnp.concatenate([bias] * DSA_GROUP, axis=1)
        kblk = k_ref[0, pl.ds(k0, tk), :]
        k1 = pl.multiple_of(jnp.minimum(kb + 1, n_kb - 1) * tk, tk)
        for g in range(DSA_KV_HEADS):
            if g + 1 < DSA_KV_HEADS:
                s_scr[(g + 1) % 2] = _dot(kblk[:, _slab(g + 1):_slab(g + 1) + LANES], q_pad[g + 1])
            else:
                s_scr[0] = _dot(k_ref[0, pl.ds(k1, tk), 0:LANES], q_pad[0])
            s = s_scr[g % 2] + bias_g
            m_old = m_scr[g]
            m_new = jnp.maximum(m_old, jnp.max(s, axis=0, keepdims=True))
            alpha = jnp.exp2(m_old[0:1] - m_new[0:1])
            p = jnp.exp2(s - m_new[0:1]).astype(BF16)
            acc_scr[g] = acc_scr[g] * alpha + _dot(vt_ref[0, g, :, pl.ds(k0, tk)], p)
            m_scr[g] = m_new

    _for_blocks(attend_block, 0, n_kb)

    for g in range(DSA_KV_HEADS):
        acc = acc_scr[g]
        og = acc[0:DSA_HEAD_DIM] / acc[V_ONES_ROW:V_ONES_ROW + 1]
        for r in range(DSA_GROUP):
            c0 = (g * DSA_GROUP + r) * DSA_HEAD_DIM
            o_ref[:, c0:c0 + DSA_HEAD_DIM] = og[:, r * tq:(r + 1) * tq].T.astype(BF16)


def _dsa_t(pf, pb, ki, k, vt, *, n_seq, rows_per_seq, tk, n_keys, topk):
    tq = LANES
    nk_pad = k.shape[1]
    steps = rows_per_seq // tq

    def rmap(c):
        return lambda s, t: (s * steps + t, c)

    kern = functools.partial(_dsa_t_kernel, tq=tq, tk=tk, n_keys=n_keys, topk=topk)
    single = pl.Buffered(1)
    return pl.pallas_call(
        kern,
        grid=(n_seq, steps),
        in_specs=[
            pl.BlockSpec((tq, IDX_Q), rmap(B_IQ // IDX_Q)),
            pl.BlockSpec((tq, LANES), rmap(F_SMALL // LANES)),
            pl.BlockSpec((tq, DSA_Q), rmap(B_DQ // DSA_Q)),
            pl.BlockSpec((1, nk_pad, IDX_DIM), lambda s, t: (s, 0, 0), pipeline_mode=single),
            pl.BlockSpec((1, nk_pad, DSA_KV), lambda s, t: (s, 0, 0), pipeline_mode=single),
            pl.BlockSpec((1, DSA_KV_HEADS, V_ROWS, nk_pad), lambda s, t: (s, 0, 0, 0), pipeline_mode=single),
        ],
        out_specs=pl.BlockSpec((tq, DSA_Q), lambda s, t: (s * steps + t, 0)),
        out_shape=jax.ShapeDtypeStruct((n_seq * rows_per_seq, DSA_Q), BF16),
        scratch_shapes=[
            pltpu.VMEM((nk_pad, tq), I32),
            pltpu.VMEM((DSA_KV_HEADS, 8, DSA_GROUP * tq), F32),
            pltpu.VMEM((DSA_KV_HEADS, V_ROWS, DSA_GROUP * tq), F32),
            pltpu.VMEM((2, tk, DSA_GROUP * tq), F32),
            pltpu.VMEM((nk_pad + SLAB_ROWS, tq), I32),
            pltpu.VMEM((_round_up(nk_pad // BIT_GROUP * 8, SLAB_ROWS), tq), I32),
        ],
        compiler_params=pltpu.CompilerParams(
            dimension_semantics=("parallel", "arbitrary"), vmem_limit_bytes=VMEM_LIMIT_BYTES),
        name="dsa_t",
    )(pb, pf, pb, ki, k, vt)


def _out_kernel(x_ref, ya_ref, ob_ref, z_ref, ma_ref, mb_ref, wg_ref, wd_ref, wo_ref,
                gate_ref, lig_ref, lib_ref, lg_ref, lb_ref, o_ref, *, alpha):
    hn = _layer_norm_rows(x_ref[...], lig_ref[...], lib_ref[...])
    y_a = _dot(ya_ref[...], wg_ref[...])
    z = z_ref[...].astype(F32)
    yb_in = ob_ref[...].astype(F32) * (z * jax.nn.sigmoid(z))
    y_b = _dot(yb_in.astype(BF16), wd_ref[...])
    ga = jax.nn.sigmoid(ma_ref[...].astype(F32) + gate_ref[0:1, :])
    gb = jax.nn.sigmoid(mb_ref[...].astype(F32) + gate_ref[1:2, :])
    merged = ga * y_a + gb * y_b
    y = alpha * hn + _dot(merged.astype(BF16), wo_ref[...])
    o_ref[...] = _layer_norm_rows(y, lg_ref[...], lb_ref[...])


def _out(x, ya, ob, pb, wg, wd, wo, gate, lig, lib, lg, lb, tm, alpha, row0):
    rows = ya.shape[0]
    base = row0 // tm
    row = lambda c: pl.BlockSpec((tm, D_MODEL), lambda i, c=c: (i, c))
    off = lambda c: pl.BlockSpec((tm, D_MODEL), lambda i, c=c: (base + i, c))
    full = lambda a: pl.BlockSpec(a.shape, lambda i: (0, 0))
    return pl.pallas_call(
        functools.partial(_out_kernel, alpha=alpha),
        grid=(rows // tm,),
        in_specs=[off(0), row(0), row(0), off(B_DZ // D_MODEL), off(B_MA // D_MODEL), off(B_MB // D_MODEL),
                  full(wg), full(wd), full(wo), full(gate), full(lig), full(lib), full(lg), full(lb)],
        out_specs=pl.BlockSpec((tm, D_MODEL), lambda i: (i, 0)),
        out_shape=jax.ShapeDtypeStruct((rows, D_MODEL), F32),
        compiler_params=pltpu.CompilerParams(
            dimension_semantics=("parallel",), vmem_limit_bytes=VMEM_LIMIT_BYTES),
        name="out_proj",
    )(x, ya, ob, pb, pb, pb, wg, wd, wo, gate, lig, lib, lg, lb)


def _round_up(a, b):
    return -(-a // b) * b


def _forward(x_prompt, x_sample, cache_k, cache_v, cache_idx_k, state_gla, meta, ln_in_g, ln_in_b,
             w_in, gla_w2, gla_gate_b, gla_norm_g, idx_kn_g, idx_kn_b, w_gla, w_dsa, gate_b,
             w_out, ln_g, ln_b):
    depth = w_in.shape[0]
    assert depth == 1, "single-layer trunk"
    bsz, seq, _ = x_prompt.shape
    dbsz, dseq, _ = x_sample.shape
    past = cache_k.shape[2]
    t_p = N_META + seq
    tp = _round_up(t_p, KEY_TILE)
    assert tp % LANES == 0 and tp % GLA_BLOCK == 0
    row_s = bsz * tp
    rows = _round_up(row_s + dbsz * dseq, ROW_TILE)
    alpha = (2.0 * depth) ** 0.25

    pieces = []
    for b in range(bsz):
        pieces += [meta.astype(F32), x_prompt[b], jnp.zeros((tp - t_p, D_MODEL), F32)]
    pieces += [x_sample.reshape(dbsz * dseq, D_MODEL), jnp.zeros((rows - row_s - dbsz * dseq, D_MODEL), F32)]
    x_all = jnp.concatenate(pieces, axis=0)

    w = w_in[0]
    w_f = jnp.concatenate([_col(w, _GQ), _col(w, _GK), _col(w, _DK), _col(w, _DV), _col(w, _IK),
                           _col(w, _GLOW), _col(w, _IW),
                           jnp.zeros((D_MODEL, LANES - IDX_DIM - GLA_RANK - IDX_HEADS), F32)], axis=1).astype(BF16)
    w_dq = _col(w, _DQ).reshape(D_MODEL, DSA_KV_HEADS, DSA_GROUP, DSA_HEAD_DIM)
    w_dq = (jnp.swapaxes(w_dq, 1, 2) * (DSA_SCALE * math.log2(math.e))).reshape(D_MODEL, DSA_Q)
    w_b = jnp.concatenate([_col(w, _GV), _col(w, _GR), w_dq, _col(w, _DZ), _col(w, _MA), _col(w, _MB),
                           _col(w, _IQ)], axis=1).astype(BF16)
    lig = ln_in_g.reshape(1, D_MODEL)
    lib = ln_in_b.reshape(1, D_MODEL)

    pf = _proj_f32(x_all, lig, lib, w_f, idx_kn_g[0].reshape(1, IDX_DIM), idx_kn_b[0].reshape(1, IDX_DIM),
                   ROW_TILE // 4)
    pb = _proj_bf16(x_all, lig, lib, w_b, ROW_TILE // 4, B_COL_TILE)

    w2pad = jnp.zeros((LANES, GLA_QK), F32).at[SM_GLOW:SM_GLOW + GLA_RANK].set(gla_w2[0]).astype(BF16)
    gbias = gla_gate_b[0].reshape(1, GLA_QK)
    ng = gla_norm_g[0].reshape(1, GLA_DV)
    blk_p = min(GLA_BLOCK, t_p)
    cm_p = jnp.asarray(np.tile(_gla_decay_matrix(blk_p), (1, 3)), BF16)
    s0_p = jnp.zeros((bsz, GLA_HEADS, GLA_DV, GLA_DK), F32)
    ya_p, st_p = _gla(pf, pb, cm_p, w2pad, gbias, ng, s0_p, n_seq=bsz, row0=0, rows_per_seq=tp,
                      t_valid=t_p, blk=blk_p, n_blk=KEY_TILE // blk_p)
    blk_s = min(GLA_BLOCK, dseq)
    assert dseq % blk_s == 0 and row_s % dseq == 0
    cm_s = jnp.asarray(np.tile(_gla_decay_matrix(blk_s), (1, 3)), BF16)
    s0_s = jnp.swapaxes(state_gla[0], -1, -2)
    ya_s, st_s = _gla(pf, pb, cm_s, w2pad, gbias, ng, s0_s, n_seq=dbsz, row0=row_s, rows_per_seq=dseq,
                      t_valid=dseq, blk=blk_s, n_blk=dseq // blk_s)

    dk = pf[:, F_DK:F_DK + DSA_KV]
    dv = pf[:, F_DV:F_DV + DSA_KV]
    ki = pf[:, F_SMALL:F_SMALL + IDX_DIM]
    dk_p = dk[:row_s].reshape(bsz, tp, DSA_KV_HEADS, DSA_HEAD_DIM)
    dv_p = dv[:row_s].reshape(bsz, tp, DSA_KV_HEADS, DSA_HEAD_DIM)
    ki_p = ki[:row_s].reshape(bsz, tp, IDX_DIM)
    vt_p = jnp.concatenate([jnp.transpose(dv_p, (0, 2, 3, 1)),
                            jnp.ones((bsz, DSA_KV_HEADS, 1, tp), F32),
                            jnp.zeros((bsz, DSA_KV_HEADS, V_ROWS - DSA_HEAD_DIM - 1, tp), F32)], axis=2).astype(BF16)
    topk_p = min(TOPK_MAX, (t_p - N_META) // 4)
    ob_p = _dsa_t(pf, pb, ki_p.astype(BF16), dk[:row_s].reshape(bsz, tp, DSA_KV).astype(BF16), vt_p,
                  n_seq=bsz, rows_per_seq=tp, tk=KEY_TILE, n_keys=t_p, topk=topk_p)

    n_keys_s = past + dseq
    nk_s = _round_up(n_keys_s, KEY_TILE)
    dk_s = dk[row_s:row_s + dbsz * dseq].reshape(dbsz, dseq, DSA_KV_HEADS, DSA_HEAD_DIM)
    dv_s = dv[row_s:row_s + dbsz * dseq].reshape(dbsz, dseq, DSA_KV_HEADS, DSA_HEAD_DIM)
    ki_s = ki[row_s:row_s + dbsz * dseq].reshape(dbsz, dseq, IDX_DIM)
    kpad = jnp.zeros((dbsz, nk_s - n_keys_s, DSA_KV_HEADS, DSA_HEAD_DIM), F32)
    k_all = jnp.concatenate([cache_k[0], dk_s, kpad], axis=1)
    v_all = jnp.concatenate([cache_v[0], dv_s, kpad], axis=1)
    ki_all = jnp.concatenate([cache_idx_k[0], ki_s, jnp.zeros((dbsz, nk_s - n_keys_s, IDX_DIM), F32)], axis=1)
    kt_s = jnp.transpose(k_all, (0, 2, 3, 1)).reshape(dbsz, DSA_KV, nk_s).astype(BF16)
    v_s = v_all.reshape(dbsz, nk_s, DSA_KV).astype(BF16)
    kit_s = jnp.transpose(ki_all, (0, 2, 1)).astype(BF16)
    topk_s = min(TOPK_MAX, n_keys_s // 4)
    ob_s = _dsa(pf, pb, kit_s, kt_s, v_s, n_seq=dbsz, row0=row_s, rows_per_seq=dseq, tq=dseq, tk=KEY_TILE,
                n_keys=n_keys_s, topk=topk_s)

    out_w = (w_gla[0].astype(BF16), w_dsa[0].astype(BF16), w_out[0].astype(BF16), gate_b[0], lig, lib,
             ln_g[0].reshape(1, D_MODEL), ln_b[0].reshape(1, D_MODEL))
    tm_out = ROW_TILE // 4
    tm_s = math.gcd(tm_out, dbsz * dseq)
    assert row_s % tm_out == 0 and tm_s % 16 == 0
    y_p = _out(x_all, ya_p, ob_p, pb, *out_w, tm_out, alpha, 0)
    y_s = _out(x_all, ya_s, ob_s, pb, *out_w, tm_s, alpha, row_s)

    y_prompt = y_p.reshape(bsz, tp, D_MODEL)[:, N_META:t_p]
    y_sample = y_s.reshape(dbsz, dseq, D_MODEL)
    k_prompt = dk_p[:, :t_p][None]
    v_prompt = dv_p[:, :t_p][None]
    idx_k_prompt = ki_p[:, :t_p][None]
    gla_prompt = jnp.swapaxes(st_p, -1, -2)[None]
    k_sample = dk_s[None]
    v_sample = dv_s[None]
    idx_k_sample = ki_s[None]
    gla_sample = jnp.swapaxes(st_s, -1, -2)[None]
    return (y_prompt, y_sample, k_prompt, v_prompt, idx_k_prompt, gla_prompt,
            k_sample, v_sample, idx_k_sample, gla_sample)


def kernel(x_prompt, x_sample, cache_k, cache_v, cache_idx_k, state_gla, meta, ln_in_g, ln_in_b,
           w_in, gla_w2, gla_gate_b, gla_norm_g, idx_kn_g, idx_kn_b, w_gla, w_dsa, gate_b,
           w_out, ln_g, ln_b):
    return _forward(x_prompt, x_sample, cache_k, cache_v, cache_idx_k, state_gla, meta, ln_in_g, ln_in_b,
                    w_in, gla_w2, gla_gate_b, gla_norm_g, idx_kn_g, idx_kn_b, w_gla, w_dsa, gate_b,
                    w_out, ln_g, ln_b)
```

```python
import functools
import math

import numpy as np
import jax
import jax.numpy as jnp
from jax import lax
from jax.experimental import pallas as pl
from jax.experimental.pallas import tpu as pltpu

F32 = jnp.float32
BF16 = jnp.bfloat16
I32 = jnp.int32

D_MODEL = 1024
CHUNK = 64
N_META = 16
GLA_HEADS = 4
GLA_DK = 128
GLA_DV = 256
GLA_RANK = 16
GLA_TAU = 16.0
GLA_BLOCK = 64
DSA_HEADS = 16
DSA_KV_HEADS = 4
DSA_HEAD_DIM = 64
DSA_GROUP = DSA_HEADS // DSA_KV_HEADS
DSA_SCALE = DSA_HEAD_DIM ** -0.5
IDX_HEADS = 8
IDX_DIM = 64
IDX_W_SCALE = (IDX_HEADS ** -0.5) * (IDX_DIM ** -0.5)
TOPK_MAX = 256
NORM_EPS = 1e-5
GLA_QK = GLA_HEADS * GLA_DK
GLA_V = GLA_HEADS * GLA_DV
DSA_Q = DSA_HEADS * DSA_HEAD_DIM
DSA_KV = DSA_KV_HEADS * DSA_HEAD_DIM
IDX_Q = IDX_HEADS * IDX_DIM
CHUNK_SHIFT = CHUNK.bit_length() - 1
HEAD_SHIFT = DSA_HEAD_DIM.bit_length() - 1
SPLITS =(GLA_QK, GLA_QK, GLA_V, GLA_RANK, GLA_V,
          DSA_Q, DSA_KV, DSA_KV, IDX_Q, IDX_DIM, IDX_HEADS, DSA_Q,
          D_MODEL, D_MODEL)
_OFF = tuple(int(o) for o in np.cumsum((0,) + SPLITS))
(_GQ, _GK, _GV, _GLOW, _GR, _DQ, _DK, _DV, _IQ, _IK, _IW, _DZ, _MA, _MB) = range(14)

LANES = 128
VMEM_LIMIT_BYTES = 56 * 1024 * 1024

F_GQ, F_GK, F_DK, F_DV, F_SMALL = 0, 512, 1024, 1280, 1536
F_COLS = F_SMALL + LANES
SM_IK, SM_GLOW, SM_IW = 0, IDX_DIM, IDX_DIM + GLA_RANK
B_GV, B_GR, B_DQ, B_DZ, B_MA, B_MB, B_IQ = 0, 1024, 2048, 3072, 4096, 5120, 6144
B_COLS = B_IQ + IDX_Q
B_COL_TILE = B_COLS // 2

ROW_TILE = 2048
KEY_TILE = 512
NEG_BIG = -1e30
INT_MIN = -(2 ** 31)
NEG_INF_KEY = int(np.array(-np.inf, np.float32).view(np.int32)) ^ 0x7FFFFFFF


def _col(w, idx):
    return w[:, _OFF[idx]:_OFF[idx + 1]]


def _layer_norm_rows(x, g, b):
    mu = jnp.mean(x, axis=-1, keepdims=True)
    xc = x - mu
    var = jnp.mean(xc * xc, axis=-1, keepdims=True)
    return xc * lax.rsqrt(var + NORM_EPS) * g + b


def _dot(a, b):
    return jnp.dot(a, b, preferred_element_type=F32)


def _dot_nt(a, b):
    return lax.dot_general(a, b, (((1,), (1,)), ((), ())), preferred_element_type=F32)


def _dot_tn(a, b):
    return lax.dot_general(a, b, (((0,), (0,)), ((), ())), preferred_element_type=F32)


def _proj_f32_kernel(x_ref, g_ref, b_ref, w_ref, kg_ref, kb_ref, o_ref):
    hn = _layer_norm_rows(x_ref[...], g_ref[...], b_ref[...])
    y = _dot(hn.astype(BF16), w_ref[...])
    o_ref[...] = y
    ik = y[:, F_SMALL:F_SMALL + IDX_DIM]
    o_ref[:, F_SMALL:F_SMALL + IDX_DIM] = _layer_norm_rows(ik, kg_ref[...], kb_ref[...])


def _proj_f32(x, g, b, w, kg, kb, tm):
    rows = x.shape[0]
    return pl.pallas_call(
        _proj_f32_kernel,
        grid=(rows // tm,),
        in_specs=[
            pl.BlockSpec((tm, D_MODEL), lambda i: (i, 0)),
            pl.BlockSpec((1, D_MODEL), lambda i: (0, 0)),
            pl.BlockSpec((1, D_MODEL), lambda i: (0, 0)),
            pl.BlockSpec((D_MODEL, F_COLS), lambda i: (0, 0)),
            pl.BlockSpec((1, IDX_DIM), lambda i: (0, 0)),
            pl.BlockSpec((1, IDX_DIM), lambda i: (0, 0)),
        ],
        out_specs=pl.BlockSpec((tm, F_COLS), lambda i: (i, 0)),
        out_shape=jax.ShapeDtypeStruct((rows, F_COLS), F32),
        compiler_params=pltpu.CompilerParams(
            dimension_semantics=("parallel",), vmem_limit_bytes=VMEM_LIMIT_BYTES),
        name="proj_f32",
    )(x, g, b, w, kg, kb)


def _proj_bf16_kernel(x_ref, g_ref, b_ref, w_ref, o_ref, hn_ref):
    @pl.when(pl.program_id(1) == 0)
    def _():
        hn_ref[...] = _layer_norm_rows(x_ref[...], g_ref[...], b_ref[...]).astype(BF16)

    o_ref[...] = _dot(hn_ref[...], w_ref[...]).astype(BF16)


def _proj_bf16(x, g, b, w, tm, tn):
    rows = x.shape[0]
    return pl.pallas_call(
        _proj_bf16_kernel,
        grid=(rows // tm, B_COLS // tn),
        in_specs=[
            pl.BlockSpec((tm, D_MODEL), lambda i, j: (i, 0)),
            pl.BlockSpec((1, D_MODEL), lambda i, j: (0, 0)),
            pl.BlockSpec((1, D_MODEL), lambda i, j: (0, 0)),
            pl.BlockSpec((D_MODEL, tn), lambda i, j: (0, j)),
        ],
        out_specs=pl.BlockSpec((tm, tn), lambda i, j: (i, j)),
        out_shape=jax.ShapeDtypeStruct((rows, B_COLS), BF16),
        scratch_shapes=[pltpu.VMEM((tm, D_MODEL), BF16)],
        compiler_params=pltpu.CompilerParams(
            dimension_semantics=("parallel", "arbitrary"), vmem_limit_bytes=VMEM_LIMIT_BYTES),
        name="proj_bf16",
    )(x, g, b, w)


def _gla_levels(blk):
    levels = []
    s = blk // 2
    while s >= 1:
        levels.append(s)
        s //= 2
    return levels


def _gla_decay_matrix(blk):
    i = np.arange(blk)[:, None]
    t = np.arange(blk)[None, :]
    mats = [(t <= i).astype(np.float32), (t > i).astype(np.float32)]
    for s in _gla_levels(blk):
        mid = (i // (2 * s)) * (2 * s) + s - 1
        lower = (i // s) % 2 == 1
        m = np.where(lower, ((t > mid) & (t <= i)), False).astype(np.float32)
        n = np.where(~lower, ((t > i) & (t <= mid)), False).astype(np.float32)
        mats.append(m + n)
    return np.concatenate(mats, axis=0)


def _gla_kernel(q_ref, k_ref, sm_ref, v_ref, r_ref, c_ref, w2_ref, gb_ref, ng_ref, s0_ref,
                o_ref, st_ref, s_scr, *, blk, n_blk, t_valid):
    ti = pl.program_id(1)
    levels = _gla_levels(blk)

    @pl.when(ti == 0)
    def _():
        s_scr[...] = s0_ref[0]

    hb = GLA_HEADS * blk
    ri = lax.broadcasted_iota(I32, (hb, hb), 0)
    ci = lax.broadcasted_iota(I32, (hb, hb), 1)
    level_id = jnp.where(ri == ci, 0, -1)
    for li, s in enumerate(levels):
        sh = s.bit_length() - 1
        same = (ri >> (sh + 1)) == (ci >> (sh + 1))
        level_id = jnp.where(same & (((ri >> sh) & 1) == 1) & (((ci >> sh) & 1) == 0), 1 + li, level_id)

    def stack(x, width):
        return jnp.concatenate([x[:, h * width:(h + 1) * width] for h in range(GLA_HEADS)], axis=0)
    row_iota = lax.broadcasted_iota(I32, (blk, 1), 0)
    cmat = c_ref[...]
    w2 = w2_ref[...]
    gbias = gb_ref[...]
    ng = ng_ref[...]

    def block(j):
        r0 = pl.multiple_of(j * blk, blk)
        rows = pl.ds(r0, blk)
        valid = (ti * (n_blk * blk) + r0 + row_iota) < t_valid
        gq = q_ref[rows, :] * (GLA_DK ** -0.5)
        gk = jnp.where(valid, k_ref[rows, :], 0.0)
        x = _dot(sm_ref[rows, :].astype(BF16), w2) + gbias
        logf = (jnp.minimum(x, 0.0) - jnp.log1p(jnp.exp(-jnp.abs(x)))) * (1.0 / GLA_TAU)
        logf = jnp.where(valid, logf, 0.0)
        hi = logf.astype(BF16)
        r1 = logf - hi.astype(F32)
        mid = r1.astype(BF16)
        lo = (r1 - mid.astype(F32)).astype(BF16)
        e_all = _dot(cmat, jnp.concatenate([hi, mid, lo], axis=0))
        qs = stack(gq, GLA_DK)
        ks = stack(gk, GLA_DK)
        vs = stack(v_ref[rows, :], GLA_DV)
        b_s = stack(e_all[0:blk], GLA_DK)
        a = jnp.where(level_id == 0, _dot_nt(qs.astype(BF16), ks.astype(BF16)), 0.0)
        for li in range(len(levels)):
            e = jnp.exp(stack(e_all[(2 + li) * blk:(3 + li) * blk], GLA_DK))
            p = _dot_nt((qs * e).astype(BF16), (ks * e).astype(BF16))
            a = jnp.where(level_id == 1 + li, p, a)
        o_intra = _dot(a.astype(BF16), vs)
        qe = (qs * jnp.exp(b_s)).astype(BF16)
        kd = (ks * jnp.exp(stack(e_all[blk:2 * blk], GLA_DK))).astype(BF16)
        for h in range(GLA_HEADS):
            hs = slice(h * blk, (h + 1) * blk)
            vsl = slice(h * GLA_DV, (h + 1) * GLA_DV)
            st = s_scr[h]
            o = _dot_nt(qe[hs], st.astype(BF16)) + o_intra[hs]
            dec = jnp.exp(b_s[(h + 1) * blk - 1:(h + 1) * blk, :])
            s_scr[h] = st * dec + _dot_tn(vs[hs], kd[hs])
            on = o * lax.rsqrt(jnp.mean(o * o, axis=-1, keepdims=True) + NORM_EPS) * ng
            gr = r_ref[rows, vsl].astype(F32)
            o_ref[rows, vsl] = (on * (gr * jax.nn.sigmoid(gr))).astype(BF16)

    n_live = jnp.clip((t_valid - ti * (n_blk * blk) + blk - 1) // blk, 0, n_blk)

    @pl.when(n_live < n_blk)
    def _():
        o_ref[...] = jnp.zeros(o_ref.shape, o_ref.dtype)

    _for_blocks(block, 0, n_live, GLA_UNROLLS)

    @pl.when(ti == pl.num_programs(1) - 1)
    def _():
        st_ref[0] = s_scr[...]


def _gla(pf, pb, cmat, w2pad, gbias, ng, s0t, *, n_seq, row0, rows_per_seq, t_valid, blk, n_blk):
    rb = blk * n_blk
    steps = rows_per_seq // rb
    base = row0 // rb

    def rmap(c):
        return lambda s, t: (base + s * steps + t, c)

    kern = functools.partial(_gla_kernel, blk=blk, n_blk=n_blk, t_valid=t_valid)
    return pl.pallas_call(
        kern,
        grid=(n_seq, steps),
        in_specs=[
            pl.BlockSpec((rb, GLA_QK), rmap(F_GQ // GLA_QK)),
            pl.BlockSpec((rb, GLA_QK), rmap(F_GK // GLA_QK)),
            pl.BlockSpec((rb, LANES), rmap(F_SMALL // LANES)),
            pl.BlockSpec((rb, GLA_V), rmap(B_GV // GLA_V)),
            pl.BlockSpec((rb, GLA_V), rmap(B_GR // GLA_V)),
            pl.BlockSpec(cmat.shape, lambda s, t: (0, 0)),
            pl.BlockSpec(w2pad.shape, lambda s, t: (0, 0)),
            pl.BlockSpec((1, GLA_QK), lambda s, t: (0, 0)),
            pl.BlockSpec((1, GLA_DV), lambda s, t: (0, 0)),
            pl.BlockSpec((1, GLA_HEADS, GLA_DV, GLA_DK), lambda s, t: (s, 0, 0, 0)),
        ],
        out_specs=[
            pl.BlockSpec((rb, GLA_V), lambda s, t: (s * steps + t, 0)),
            pl.BlockSpec((1, GLA_HEADS, GLA_DV, GLA_DK), lambda s, t: (s, 0, 0, 0)),
        ],
        out_shape=[
            jax.ShapeDtypeStruct((n_seq * rows_per_seq, GLA_V), BF16),
            jax.ShapeDtypeStruct((n_seq, GLA_HEADS, GLA_DV, GLA_DK), F32),
        ],
        scratch_shapes=[pltpu.VMEM((GLA_HEADS, GLA_DV, GLA_DK), F32)],
        compiler_params=pltpu.CompilerParams(
            dimension_semantics=("parallel", "arbitrary"), vmem_limit_bytes=VMEM_LIMIT_BYTES),
        name="gla",
    )(pf, pf, pf, pb, pb, cmat, w2pad, gbias, ng, s0t)


def _dsa_kernel(iq_ref, sm_ref, qd_ref, kit_ref, kt_ref, v_ref, o_ref,
                key_scr, m_scr, l_scr, acc_scr, *, tq, tk, n_keys, topk, n_kb_total):
    n_kb = n_kb_total
    lane_iota = lax.broadcasted_iota(I32, (1, tk), 1)

    iq = iq_ref[...]
    iq_hm = jnp.concatenate([iq[:, h * IDX_DIM:(h + 1) * IDX_DIM] for h in range(IDX_HEADS)], axis=0)
    w_i = sm_ref[:, SM_IW:SM_IW + IDX_HEADS] * IDX_W_SCALE
    w_cols = [jnp.broadcast_to(w_i[:, h:h + 1], (tq, LANES)) for h in range(IDX_HEADS)]

    def score_block(kb, carry):
        k0 = pl.multiple_of(kb * tk, tk)
        r = _dot(iq_hm, kit_ref[0, :, pl.ds(k0, tk)])
        sc = None
        for h in range(IDX_HEADS):
            rh = jnp.maximum(r[h * tq:(h + 1) * tq, :], 0.0)
            wh = jnp.concatenate([w_cols[h]] * (tk // LANES), axis=1)
            sc = rh * wh if sc is None else sc + rh * wh
        sc = jnp.where(k0 + lane_iota < n_keys, sc, -jnp.inf)
        bits = pltpu.bitcast(sc, I32)
        key_scr[:, pl.ds(k0, tk)] = bits ^ ((bits >> 31) & 0x7FFFFFFF)
        return carry

    lax.fori_loop(0, n_kb, score_block, 0)

    n_cols = n_kb * (tk // LANES)

    def count_ge(cand):
        cb = jnp.broadcast_to(cand, (tq, LANES))

        def body(kb, acc):
            for u in range(tk // LANES):
                c0 = pl.multiple_of(kb * tk + u * LANES, LANES)
                acc = acc + jnp.where(key_scr[:, pl.ds(c0, LANES)] >= cb, 1, 0)
            return acc

        acc = lax.fori_loop(0, n_kb, body, jnp.zeros((tq, LANES), I32))
        return jnp.sum(acc, axis=1, keepdims=True)

    bits_per_round = 4

    def search_cond(carry):
        i, lo, cge = carry
        return (i < 32) & (jnp.max(jnp.abs(cge - topk)) > 0)

    def search_round(carry):
        i, lo, cge = carry
        for u in range(bits_per_round):
            cand = lo + jnp.left_shift(jnp.int32(1), 31 - u - i)
            cnt = count_ge(cand)
            take = cnt >= topk
            lo = jnp.where(take, cand, lo)
            cge = jnp.where(take, cnt, cge)
        return i + bits_per_round, lo, cge

    lo0 = jnp.full((tq, 1), INT_MIN, I32)
    cge0 = jnp.zeros((tq, 1), I32) + n_cols * LANES
    _, thr, cge = lax.while_loop(search_cond, search_round, (jnp.int32(0), lo0, cge0))
    need_tie = (cge > topk) & (thr > NEG_INF_KEY)
    thr = jnp.maximum(thr, NEG_INF_KEY + 1)

    @pl.when(jnp.max(need_tie.astype(I32)) > 0)
    def _():
        want = topk - count_ge(thr + 1)
        thr_b = jnp.broadcast_to(thr, (tq, LANES))
        col_iota = lax.broadcasted_iota(I32, (tq, LANES), 1)

        def count_tied_below(m):
            mb = jnp.broadcast_to(m, (tq, LANES))

            def body(c, acc):
                c0 = pl.multiple_of(c * LANES, LANES)
                hit = (key_scr[:, pl.ds(c0, LANES)] == thr_b) & ((c0 + col_iota) < mb)
                return acc + jnp.where(hit, 1, 0)

            acc = lax.fori_loop(0, n_cols, body, jnp.zeros((tq, LANES), I32))
            return jnp.sum(acc, axis=1, keepdims=True)

        n_bits = int(math.ceil(math.log2(n_kb_total * tk + 1)))

        def idx_step(i, m):
            cand = m + jnp.left_shift(jnp.int32(1), n_bits - 1 - i)
            return jnp.where(count_tied_below(cand) <= want, cand, m)

        m_keep = lax.fori_loop(0, n_bits, idx_step, jnp.zeros((tq, 1), I32))
        mk_b = jnp.broadcast_to(jnp.where(need_tie, m_keep, jnp.int32(2 ** 30)), (tq, LANES))

        def demote(c, carry):
            c0 = pl.multiple_of(c * LANES, LANES)
            kv = key_scr[:, pl.ds(c0, LANES)]
            drop = (kv == thr_b) & ((c0 + col_iota) >= mk_b)
            key_scr[:, pl.ds(c0, LANES)] = jnp.where(drop, kv - 1, kv)
            return carry

        lax.fori_loop(0, n_cols, demote, 0)

    qd = qd_ref[...]
    win = lax.broadcasted_iota(I32, (tq, DSA_KV), 1) >> HEAD_SHIFT
    q_all = jnp.concatenate(
        [jnp.where(win == g, qd[:, r * DSA_KV:(r + 1) * DSA_KV].astype(F32), 0.0).astype(BF16)
         for g in range(DSA_KV_HEADS) for r in range(DSA_GROUP)], axis=0)
    m_scr[...] = jnp.full(m_scr.shape, NEG_BIG, F32)
    l_scr[...] = jnp.zeros(l_scr.shape, F32)
    acc_scr[...] = jnp.zeros(acc_scr.shape, F32)
    thr_t = jnp.broadcast_to(thr, (tq, LANES))

    def attend_block(kb, carry):
        k0 = pl.multiple_of(kb * tk, tk)
        sel = key_scr[:, pl.ds(k0, tk)] >= jnp.concatenate([thr_t] * (tk // LANES), axis=1)
        bias = jnp.where(sel, 0.0, NEG_BIG)
        s = _dot(q_all, kt_ref[0, :, pl.ds(k0, tk)]) + jnp.concatenate([bias] * DSA_HEADS, axis=0)
        m_old = m_scr[...]
        m_new = jnp.maximum(m_old, jnp.max(s, axis=1, keepdims=True))
        alpha = jnp.exp2(m_old - m_new)
        p = jnp.exp2(s - m_new)
        l_scr[...] = alpha * l_scr[...] + jnp.sum(p, axis=1, keepdims=True)
        acc_scr[...] = alpha * acc_scr[...] + _dot(p.astype(BF16), v_ref[0, pl.ds(k0, tk), :])
        m_scr[...] = m_new
        return carry

    lax.fori_loop(0, n_kb, attend_block, 0)

    o_all = acc_scr[...] / l_scr[...]
    for g in range(DSA_KV_HEADS):
        for r in range(DSA_GROUP):
            h = g * DSA_GROUP + r
            o_ref[:, h * DSA_HEAD_DIM:(h + 1) * DSA_HEAD_DIM] = (
                o_all[h * tq:(h + 1) * tq, g * DSA_HEAD_DIM:(g + 1) * DSA_HEAD_DIM].astype(BF16))


def _dsa(pf, pb, kit, kt, v, *, n_seq, row0, rows_per_seq, tq, tk, n_keys, topk):
    nk_pad = kit.shape[-1]
    n_kb_total = nk_pad // tk
    steps = rows_per_seq // tq
    base = row0 // tq

    def rmap(c):
        return lambda s, t: (base + s * steps + t, c)

    kern = functools.partial(_dsa_kernel, tq=tq, tk=tk, n_keys=n_keys, topk=topk, n_kb_total=n_kb_total)
    single = pl.Buffered(1)
    return pl.pallas_call(
        kern,
        grid=(n_seq, steps),
        in_specs=[
            pl.BlockSpec((tq, IDX_Q), rmap(B_IQ // IDX_Q)),
            pl.BlockSpec((tq, LANES), rmap(F_SMALL // LANES)),
            pl.BlockSpec((tq, DSA_Q), rmap(B_DQ // DSA_Q)),
            pl.BlockSpec((1, IDX_DIM, nk_pad), lambda s, t: (s, 0, 0), pipeline_mode=single),
            pl.BlockSpec((1, DSA_KV, nk_pad), lambda s, t: (s, 0, 0), pipeline_mode=single),
            pl.BlockSpec((1, nk_pad, DSA_KV), lambda s, t: (s, 0, 0), pipeline_mode=single),
        ],
        out_specs=pl.BlockSpec((tq, DSA_Q), lambda s, t: (s * steps + t, 0)),
        out_shape=jax.ShapeDtypeStruct((n_seq * rows_per_seq, DSA_Q), BF16),
        scratch_shapes=[
            pltpu.VMEM((tq, nk_pad), I32),
            pltpu.VMEM((DSA_HEADS * tq, 1), F32),
            pltpu.VMEM((DSA_HEADS * tq, 1), F32),
            pltpu.VMEM((DSA_HEADS * tq, DSA_KV), F32),
        ],
        compiler_params=pltpu.CompilerParams(
            dimension_semantics=("parallel", "arbitrary"), vmem_limit_bytes=VMEM_LIMIT_BYTES),
        name="dsa",
    )(pb, pf, pb, kit, kt, v)


BIT_GROUP = 256
SLAB_ROWS = 128
BLOCK_UNROLLS = (8, 4, 2, 1)
GLA_UNROLLS = (4, 2, 1)


def _for_blocks(block_fn, start, stop, unrolls=BLOCK_UNROLLS):
    for u in unrolls:
        trips = (stop - start) // u

        def body(j, carry, u=u, start=start):
            for t in range(u):
                block_fn(start + u * j + t)
            return carry

        lax.fori_loop(0, trips, body, 0)
        start = start + trips * u


def _slab(g):
    return g * DSA_HEAD_DIM // LANES * LANES


def _bit_transpose32(words):
    a = list(words)
    mask, j = 0x0000FFFF, 16
    while j:
        k = 0
        while k < 32:
            t = (a[k] ^ lax.shift_right_logical(a[k + j], jnp.int32(j))) & mask
            a[k] = a[k] ^ t
            a[k + j] = a[k + j] ^ (t << j)
            k = (k + j + 1) & ~j
        j >>= 1
        mask ^= (mask << j) & 0xFFFFFFFF
    return a


V_ROWS = 80
V_ONES_ROW = DSA_HEAD_DIM


def _dsa_t_kernel(iq_ref, sm_ref, qd_ref, ki_ref, k_ref, vt_ref, o_ref,
                  key_scr, m_scr, acc_scr, s_scr, plane_scr, active_scr, *, tq, tk, n_keys, topk):
    live = pl.program_id(1) * tq < n_keys

    @pl.when(live)
    def _():
        _dsa_t_body(iq_ref, sm_ref, qd_ref, ki_ref, k_ref, vt_ref, o_ref, key_scr, m_scr, acc_scr, s_scr,
                    plane_scr, active_scr,
                    tq=tq, tk=tk, n_keys=n_keys, topk=topk)

    @pl.when(jnp.logical_not(live))
    def _():
        o_ref[...] = jnp.zeros(o_ref.shape, o_ref.dtype)


def _dsa_t_body(iq_ref, sm_ref, qd_ref, ki_ref, k_ref, vt_ref, o_ref,
                key_scr, m_scr, acc_scr, s_scr, plane_scr, active_scr, *, tq, tk, n_keys, topk):
    qb = pl.program_id(1)
    q0 = qb * tq
    q_chunk_max = (q0 + tq - 1 - N_META) >> CHUNK_SHIFT
    limit = jnp.minimum(N_META + CHUNK * (q_chunk_max + 1), n_keys)
    n_kb = (limit + tk - 1) // tk
    n_full = jnp.minimum(jnp.minimum((q0 + N_META) // tk, n_keys // tk), n_kb)

    q_chunk = (q0 + lax.broadcasted_iota(I32, (1, tq), 1) - N_META) >> CHUNK_SHIFT
    row_iota = lax.broadcasted_iota(I32, (tk, tq), 0)
    plane_rows = key_scr.shape[0] // BIT_GROUP * 8

    @pl.when(qb == 0)
    def _():
        plane_scr[...] = jnp.zeros(plane_scr.shape, I32)

    iq = iq_ref[...]
    iq_hm = jnp.concatenate([iq[:, h * IDX_DIM:(h + 1) * IDX_DIM] for h in range(IDX_HEADS)], axis=0)
    w_t = sm_ref[...].T[SM_IW:SM_IW + IDX_HEADS, :] * IDX_W_SCALE
    iq_hm = iq_hm.astype(F32).T.astype(BF16)

    def score_block(kb, masked):
        k0 = pl.multiple_of(kb * tk, tk)
        r = _dot(ki_ref[0, pl.ds(k0, tk), :], iq_hm)
        sc = None
        for h in range(IDX_HEADS):
            t = jnp.maximum(r[:, h * tq:(h + 1) * tq], 0.0) * w_t[h:h + 1, :]
            sc = t if sc is None else sc + t
        if masked:
            key_pos = k0 + row_iota
            adm = (key_pos < n_keys) & (((key_pos - N_META) >> CHUNK_SHIFT) <= q_chunk)
            sc = jnp.where(adm, sc, -jnp.inf)
        bits = pltpu.bitcast(sc, I32)
        keys = bits ^ ((bits >> 31) & 0x7FFFFFFF)
        key_scr[pl.ds(k0, tk), :] = keys
        ukeys = keys ^ INT_MIN
        for u in range(tk // BIT_GROUP):
            words = [ukeys[u * BIT_GROUP + 8 * w:u * BIT_GROUP + 8 * w + 8, :] for w in range(32)]
            planes = _bit_transpose32(words)
            g_row = pl.multiple_of((kb * (tk // BIT_GROUP) + u) * 8, 8)
            for w in range(32):
                plane_scr[pl.ds(w * plane_rows + g_row, 8), :] = planes[w]

    def score_edge(kb, carry):
        score_block(kb, True)
        return carry

    _for_blocks(lambda kb: score_block(kb, False), 0, n_full)
    lax.fori_loop(n_full, n_kb, score_edge, 0)

    n_chunks = n_kb * (tk // LANES)
    sub_per_blk = tk // LANES

    def count_rows(pred):
        def body(kb, acc):
            for u in range(sub_per_blk):
                c0 = pl.multiple_of(kb * tk + u * LANES, LANES)
                acc = acc + jnp.where(pred(key_scr[pl.ds(c0, LANES), :], c0), 1, 0)
            return acc

        acc = lax.fori_loop(0, n_kb, body, jnp.zeros((LANES, tq), I32))
        return jnp.sum(acc, axis=0, keepdims=True)

    def count_ge(cand):
        return count_rows(lambda kv, c0: kv >= cand)

    n_groups = n_kb * (tk // BIT_GROUP)
    n_slabs = (n_groups * 8 + SLAB_ROWS - 1) // SLAB_ROWS
    slab_iota = lax.broadcasted_iota(I32, (SLAB_ROWS, tq), 0)

    def init_active(sl, carry):
        r0 = pl.multiple_of(sl * SLAB_ROWS, SLAB_ROWS)
        active_scr[pl.ds(r0, SLAB_ROWS), :] = jnp.where(r0 + slab_iota < n_groups * 8, -1, 0)
        return carry

    lax.fori_loop(0, n_slabs, init_active, 0)

    def select_pass(i, carry):
        thr_u, c_above, n_act, flip = carry
        cur = i * plane_rows
        prev = cur - plane_rows

        def body(sl, acc):
            r0 = pl.multiple_of(sl * SLAB_ROWS, SLAB_ROWS)
            rows = pl.ds(r0, SLAB_ROWS)
            act = active_scr[rows, :] & (plane_scr[pl.ds(pl.multiple_of(prev + r0, 8), SLAB_ROWS), :] ^ flip)
            active_scr[rows, :] = act
            return acc + lax.population_count(act & plane_scr[pl.ds(pl.multiple_of(cur + r0, 8), SLAB_ROWS), :])

        acc = lax.fori_loop(0, n_slabs, body, jnp.zeros((SLAB_ROWS, tq), I32))
        ones = jnp.sum(acc, axis=0, keepdims=True)
        take = c_above + ones >= topk
        thr_u = thr_u | jnp.where(take, jnp.left_shift(jnp.int32(1), 31 - i), 0)
        c_above = c_above + jnp.where(take, 0, ones)
        n_act = jnp.where(take, ones, n_act - ones)
        return thr_u, c_above, n_act, jnp.where(take, 0, -1)

    n_act = jnp.zeros((1, tq), I32) + n_groups * BIT_GROUP

    def count_top(sl, acc):
        rows = pl.ds(pl.multiple_of(sl * SLAB_ROWS, SLAB_ROWS), SLAB_ROWS)
        return acc + lax.population_count(active_scr[rows, :] & plane_scr[rows, :])

    acc0 = lax.fori_loop(0, n_slabs, count_top, jnp.zeros((SLAB_ROWS, tq), I32))
    ones0 = jnp.sum(acc0, axis=0, keepdims=True)
    take0 = ones0 >= topk
    carry0 = (jnp.where(take0, INT_MIN, 0), jnp.where(take0, 0, ones0),
              jnp.where(take0, ones0, n_act - ones0), jnp.where(take0, 0, -1))
    thr_u, c_above, n_act, _ = lax.fori_loop(1, 32, select_pass, carry0)
    thr = thr_u ^ INT_MIN
    cge = c_above + n_act
    need_tie = (cge > topk) & (thr > NEG_INF_KEY)
    thr = jnp.maximum(thr, NEG_INF_KEY + 1)

    @pl.when(jnp.max(need_tie.astype(I32)) > 0)
    def _():
        want = topk - count_ge(thr + 1)
        sub_iota = lax.broadcasted_iota(I32, (LANES, tq), 0)
        n_bits = int(math.ceil(math.log2(key_scr.shape[0] + 1)))

        def idx_step(i, m):
            cand = m + jnp.left_shift(jnp.int32(1), n_bits - 1 - i)
            tied_below = count_rows(lambda kv, c0: (kv == thr) & ((c0 + sub_iota) < cand))
            return jnp.where(tied_below <= want, cand, m)

        m_keep = lax.fori_loop(0, n_bits, idx_step, jnp.zeros((1, tq), I32))
        m_keep = jnp.where(need_tie, m_keep, jnp.int32(2 ** 30))

        def demote(c, carry):
            c0 = pl.multiple_of(c * LANES, LANES)
            kv = key_scr[pl.ds(c0, LANES), :]
            drop = (kv == thr) & ((c0 + sub_iota) >= m_keep)
            key_scr[pl.ds(c0, LANES), :] = jnp.where(drop, kv - 1, kv)
            return carry

        lax.fori_loop(0, n_chunks, demote, 0)

    qd = qd_ref[...]
    win = lax.broadcasted_iota(I32, (tq, LANES), 1) >> HEAD_SHIFT
    q_pad = []
    for g in range(DSA_KV_HEADS):
        lo = _slab(g)
        q_pad.append(jnp.concatenate(
            [jnp.where(win == g % 2, qd[:, r * DSA_KV + lo:r * DSA_KV + lo + LANES].astype(F32), 0.0)
             for r in range(DSA_GROUP)], axis=0))
    q_pad = [x.T.astype(BF16) for x in q_pad]
    m_scr[...] = jnp.full(m_scr.shape, NEG_BIG, F32)
    acc_scr[...] = jnp.zeros(acc_scr.shape, F32)
    s_scr[0] = _dot(k_ref[0, pl.ds(0, tk), 0:LANES], q_pad[0])

    def attend_block(kb):
        k0 = pl.multiple_of(kb * tk, tk)
        bias = jnp.where(key_scr[pl.ds(k0, tk), :] >= thr, 0.0, NEG_BIG)
        bias_g = jnp.concatenate([bias] * DSA_GROUP, axis=1)
        kblk = k_ref[0, pl.ds(k0, tk), :]
        k1 = pl.multiple_of(jnp.minimum(kb + 1, n_kb - 1) * tk, tk)
        for g in range(DSA_KV_HEADS):
            if g + 1 < DSA_KV_HEADS:
                s_scr[(g + 1) % 2] = _dot(kblk[:, _slab(g + 1):_slab(g + 1) + LANES], q_pad[g + 1])
            else:
                s_scr[0] = _dot(k_ref[0, pl.ds(k1, tk), 0:LANES], q_pad[0])
            s = s_scr[g % 2] + bias_g
            m_old = m_scr[g]
            m_new = jnp.maximum(m_old, jnp.max(s, axis=0, keepdims=True))
            alpha = jnp.exp2(m_old[0:1] - m_new[0:1])
            p = jnp.exp2(s - m_new[0:1]).astype(BF16)
            acc_scr[g] = acc_scr[g] * alpha + _dot(vt_ref[0, g, :, pl.ds(k0, tk)], p)
            m_scr[g] = m_new

    _for_blocks(attend_block, 0, n_kb)

    for g in range(DSA_KV_HEADS):
        acc = acc_scr[g]
        og = acc[0:DSA_HEAD_DIM] / acc[V_ONES_ROW:V_ONES_ROW + 1]
        for r in range(DSA_GROUP):
            c0 = (g * DSA_GROUP + r) * DSA_HEAD_DIM
            o_ref[:, c0:c0 + DSA_HEAD_DIM] = og[:, r * tq:(r + 1) * tq].T.astype(BF16)


def _dsa_t(pf, pb, ki, k, vt, *, n_seq, rows_per_seq, tk, n_keys, topk):
    tq = LANES
    nk_pad = k.shape[1]
    steps = rows_per_seq // tq

    def rmap(c):
        return lambda s, t: (s * steps + t, c)

    kern = functools.partial(_dsa_t_kernel, tq=tq, tk=tk, n_keys=n_keys, topk=topk)
    single = pl.Buffered(1)
    return pl.pallas_call(
        kern,
        grid=(n_seq, steps),
        in_specs=[
            pl.BlockSpec((tq, IDX_Q), rmap(B_IQ // IDX_Q)),
            pl.BlockSpec((tq, LANES), rmap(F_SMALL // LANES)),
            pl.BlockSpec((tq, DSA_Q), rmap(B_DQ // DSA_Q)),
            pl.BlockSpec((1, nk_pad, IDX_DIM), lambda s, t: (s, 0, 0), pipeline_mode=single),
            pl.BlockSpec((1, nk_pad, DSA_KV), lambda s, t: (s, 0, 0), pipeline_mode=single),
            pl.BlockSpec((1, DSA_KV_HEADS, V_ROWS, nk_pad), lambda s, t: (s, 0, 0, 0), pipeline_mode=single),
        ],
        out_specs=pl.BlockSpec((tq, DSA_Q), lambda s, t: (s * steps + t, 0)),
        out_shape=jax.ShapeDtypeStruct((n_seq * rows_per_seq, DSA_Q), BF16),
        scratch_shapes=[
            pltpu.VMEM((nk_pad, tq), I32),
            pltpu.VMEM((DSA_KV_HEADS, 8, DSA_GROUP * tq), F32),
            pltpu.VMEM((DSA_KV_HEADS, V_ROWS, DSA_GROUP * tq), F32),
            pltpu.VMEM((2, tk, DSA_GROUP * tq), F32),
            pltpu.VMEM((nk_pad + SLAB_ROWS, tq), I32),
            pltpu.VMEM((_round_up(nk_pad // BIT_GROUP * 8, SLAB_ROWS), tq), I32),
        ],
        compiler_params=pltpu.CompilerParams(
            dimension_semantics=("parallel", "arbitrary"), vmem_limit_bytes=VMEM_LIMIT_BYTES),
        name="dsa_t",
    )(pb, pf, pb, ki, k, vt)


def _out_kernel(x_ref, ya_ref, ob_ref, z_ref, ma_ref, mb_ref, wg_ref, wd_ref, wo_ref,
                gate_ref, lig_ref, lib_ref, lg_ref, lb_ref, o_ref, *, alpha):
    hn = _layer_norm_rows(x_ref[...], lig_ref[...], lib_ref[...])
    y_a = _dot(ya_ref[...], wg_ref[...])
    z = z_ref[...].astype(F32)
    yb_in = ob_ref[...].astype(F32) * (z * jax.nn.sigmoid(z))
    y_b = _dot(yb_in.astype(BF16), wd_ref[...])
    ga = jax.nn.sigmoid(ma_ref[...].astype(F32) + gate_ref[0:1, :])
    gb = jax.nn.sigmoid(mb_ref[...].astype(F32) + gate_ref[1:2, :])
    merged = ga * y_a + gb * y_b
    y = alpha * hn + _dot(merged.astype(BF16), wo_ref[...])
    o_ref[...] = _layer_norm_rows(y, lg_ref[...], lb_ref[...])


def _out(x, ya, ob, pb, wg, wd, wo, gate, lig, lib, lg, lb, tm, alpha, row0):
    rows = ya.shape[0]
    base = row0 // tm
    row = lambda c: pl.BlockSpec((tm, D_MODEL), lambda i, c=c: (i, c))
    off = lambda c: pl.BlockSpec((tm, D_MODEL), lambda i, c=c: (base + i, c))
    full = lambda a: pl.BlockSpec(a.shape, lambda i: (0, 0))
    return pl.pallas_call(
        functools.partial(_out_kernel, alpha=alpha),
        grid=(rows // tm,),
        in_specs=[off(0), row(0), row(0), off(B_DZ // D_MODEL), off(B_MA // D_MODEL), off(B_MB // D_MODEL),
                  full(wg), full(wd), full(wo), full(gate), full(lig), full(lib), full(lg), full(lb)],
        out_specs=pl.BlockSpec((tm, D_MODEL), lambda i: (i, 0)),
        out_shape=jax.ShapeDtypeStruct((rows, D_MODEL), F32),
        compiler_params=pltpu.CompilerParams(
            dimension_semantics=("parallel",), vmem_limit_bytes=VMEM_LIMIT_BYTES),
        name="out_proj",
    )(x, ya, ob, pb, pb, pb, wg, wd, wo, gate, lig, lib, lg, lb)


def _round_up(a, b):
    return -(-a // b) * b


def _forward(x_prompt, x_sample, cache_k, cache_v, cache_idx_k, state_gla, meta, ln_in_g, ln_in_b,
             w_in, gla_w2, gla_gate_b, gla_norm_g, idx_kn_g, idx_kn_b, w_gla, w_dsa, gate_b,
             w_out, ln_g, ln_b):
    depth = w_in.shape[0]
    assert depth == 1, "single-layer trunk"
    bsz, seq, _ = x_prompt.shape
    dbsz, dseq, _ = x_sample.shape
    past = cache_k.shape[2]
    t_p = N_META + seq
    tp = _round_up(t_p, KEY_TILE)
    assert tp % LANES == 0 and tp % GLA_BLOCK == 0
    row_s = bsz * tp
    rows = _round_up(row_s + dbsz * dseq, ROW_TILE)
    alpha = (2.0 * depth) ** 0.25

    pieces = []
    for b in range(bsz):
        pieces += [meta.astype(F32), x_prompt[b], jnp.zeros((tp - t_p, D_MODEL), F32)]
    pieces += [x_sample.reshape(dbsz * dseq, D_MODEL), jnp.zeros((rows - row_s - dbsz * dseq, D_MODEL), F32)]
    x_all = jnp.concatenate(pieces, axis=0)

    w = w_in[0]
    w_f = jnp.concatenate([_col(w, _GQ), _col(w, _GK), _col(w, _DK), _col(w, _DV), _col(w, _IK),
                           _col(w, _GLOW), _col(w, _IW),
                           jnp.zeros((D_MODEL, LANES - IDX_DIM - GLA_RANK - IDX_HEADS), F32)], axis=1).astype(BF16)
    w_dq = _col(w, _DQ).reshape(D_MODEL, DSA_KV_HEADS, DSA_GROUP, DSA_HEAD_DIM)
    w_dq = (jnp.swapaxes(w_dq, 1, 2) * (DSA_SCALE * math.log2(math.e))).reshape(D_MODEL, DSA_Q)
    w_b = jnp.concatenate([_col(w, _GV), _col(w, _GR), w_dq, _col(w, _DZ), _col(w, _MA), _col(w, _MB),
                           _col(w, _IQ)], axis=1).astype(BF16)
    lig = ln_in_g.reshape(1, D_MODEL)
    lib = ln_in_b.reshape(1, D_MODEL)

    pf = _proj_f32(x_all, lig, lib, w_f, idx_kn_g[0].reshape(1, IDX_DIM), idx_kn_b[0].reshape(1, IDX_DIM),
                   ROW_TILE // 4)
    pb = _proj_bf16(x_all, lig, lib, w_b, ROW_TILE // 4, B_COL_TILE)

    w2pad = jnp.zeros((LANES, GLA_QK), F32).at[SM_GLOW:SM_GLOW + GLA_RANK].set(gla_w2[0]).astype(BF16)
    gbias = gla_gate_b[0].reshape(1, GLA_QK)
    ng = gla_norm_g[0].reshape(1, GLA_DV)
    blk_p = min(GLA_BLOCK, t_p)
    cm_p = jnp.asarray(np.tile(_gla_decay_matrix(blk_p), (1, 3)), BF16)
    s0_p = jnp.zeros((bsz, GLA_HEADS, GLA_DV, GLA_DK), F32)
    ya_p, st_p = _gla(pf, pb, cm_p, w2pad, gbias, ng, s0_p, n_seq=bsz, row0=0, rows_per_seq=tp,
                      t_valid=t_p, blk=blk_p, n_blk=KEY_TILE // blk_p)
    blk_s = min(GLA_BLOCK, dseq)
    assert dseq % blk_s == 0 and row_s % dseq == 0
    cm_s = jnp.asarray(np.tile(_gla_decay_matrix(blk_s), (1, 3)), BF16)
    s0_s = jnp.swapaxes(state_gla[0], -1, -2)
    ya_s, st_s = _gla(pf, pb, cm_s, w2pad, gbias, ng, s0_s, n_seq=dbsz, row0=row_s, rows_per_seq=dseq,
                      t_valid=dseq, blk=blk_s, n_blk=dseq // blk_s)

    dk = pf[:, F_DK:F_DK + DSA_KV]
    dv = pf[:, F_DV:F_DV + DSA_KV]
    ki = pf[:, F_SMALL:F_SMALL + IDX_DIM]
    dk_p = dk[:row_s].reshape(bsz, tp, DSA_KV_HEADS, DSA_HEAD_DIM)
    dv_p = dv[:row_s].reshape(bsz, tp, DSA_KV_HEADS, DSA_HEAD_DIM)
    ki_p = ki[:row_s].reshape(bsz, tp, IDX_DIM)
    vt_p = jnp.concatenate([jnp.transpose(dv_p, (0, 2, 3, 1)),
                            jnp.ones((bsz, DSA_KV_HEADS, 1, tp), F32),
                            jnp.zeros((bsz, DSA_KV_HEADS, V_ROWS - DSA_HEAD_DIM - 1, tp), F32)], axis=2).astype(BF16)
    topk_p = min(TOPK_MAX, (t_p - N_META) // 4)
    ob_p = _dsa_t(pf, pb, ki_p.astype(BF16), dk[:row_s].reshape(bsz, tp, DSA_KV).astype(BF16), vt_p,
                  n_seq=bsz, rows_per_seq=tp, tk=KEY_TILE, n_keys=t_p, topk=topk_p)

    n_keys_s = past + dseq
    nk_s = _round_up(n_keys_s, KEY_TILE)
    dk_s = dk[row_s:row_s + dbsz * dseq].reshape(dbsz, dseq, DSA_KV_HEADS, DSA_HEAD_DIM)
    dv_s = dv[row_s:row_s + dbsz * dseq].reshape(dbsz, dseq, DSA_KV_HEADS, DSA_HEAD_DIM)
    ki_s = ki[row_s:row_s + dbsz * dseq].reshape(dbsz, dseq, IDX_DIM)
    kpad = jnp.zeros((dbsz, nk_s - n_keys_s, DSA_KV_HEADS, DSA_HEAD_DIM), F32)
    k_all = jnp.concatenate([cache_k[0], dk_s, kpad], axis=1)
    v_all = jnp.concatenate([cache_v[0], dv_s, kpad], axis=1)
    ki_all = jnp.concatenate([cache_idx_k[0], ki_s, jnp.zeros((dbsz, nk_s - n_keys_s, IDX_DIM), F32)], axis=1)
    kt_s = jnp.transpose(k_all, (0, 2, 3, 1)).reshape(dbsz, DSA_KV, nk_s).astype(BF16)
    v_s = v_all.reshape(dbsz, nk_s, DSA_KV).astype(BF16)
    kit_s = jnp.transpose(ki_all, (0, 2, 1)).astype(BF16)
    topk_s = min(TOPK_MAX, n_keys_s // 4)
    ob_s = _dsa(pf, pb, kit_s, kt_s, v_s, n_seq=dbsz, row0=row_s, rows_per_seq=dseq, tq=dseq, tk=KEY_TILE,
                n_keys=n_keys_s, topk=topk_s)

    out_w = (w_gla[0].astype(BF16), w_dsa[0].astype(BF16), w_out[0].astype(BF16), gate_b[0], lig, lib,
             ln_g[0].reshape(1, D_MODEL), ln_b[0].reshape(1, D_MODEL))
    tm_out = ROW_TILE // 4
    tm_s = math.gcd(tm_out, dbsz * dseq)
    assert row_s % tm_out == 0 and tm_s % 16 == 0
    y_p = _out(x_all, ya_p, ob_p, pb, *out_w, tm_out, alpha, 0)
    y_s = _out(x_all, ya_s, ob_s, pb, *out_w, tm_s, alpha, row_s)

    y_prompt = y_p.reshape(bsz, tp, D_MODEL)[:, N_META:t_p]
    y_sample = y_s.reshape(dbsz, dseq, D_MODEL)
    k_prompt = dk_p[:, :t_p][None]
    v_prompt = dv_p[:, :t_p][None]
    idx_k_prompt = ki_p[:, :t_p][None]
    gla_prompt = jnp.swapaxes(st_p, -1, -2)[None]
    k_sample = dk_s[None]
    v_sample = dv_s[None]
    idx_k_sample = ki_s[None]
    gla_sample = jnp.swapaxes(st_s, -1, -2)[None]
    return (y_prompt, y_sample, k_prompt, v_prompt, idx_k_prompt, gla_prompt,
            k_sample, v_sample, idx_k_sample, gla_sample)


def kernel(x_prompt, x_sample, cache_k, cache_v, cache_idx_k, state_gla, meta, ln_in_g, ln_in_b,
           w_in, gla_w2, gla_gate_b, gla_norm_g, idx_kn_g, idx_kn_b, w_gla, w_dsa, gate_b,
           w_out, ln_g, ln_b):
    return _forward(x_prompt, x_sample, cache_k, cache_v, cache_idx_k, state_gla, meta, ln_in_g, ln_in_b,
                    w_in, gla_w2, gla_gate_b, gla_norm_g, idx_kn_g, idx_kn_b, w_gla, w_dsa, gate_b,
                    w_out, ln_g, ln_b)
```

```python
import functools
import math

import numpy as np
import jax
import jax.numpy as jnp
from jax import lax
from jax.experimental import pallas as pl
from jax.experimental.pallas import tpu as pltpu

F32 = jnp.float32
BF16 = jnp.bfloat16
I32 = jnp.int32

D_MODEL = 1024
CHUNK = 64
N_META = 16
GLA_HEADS = 4
GLA_DK = 128
GLA_DV = 256
GLA_RANK = 16
GLA_TAU = 16.0
GLA_BLOCK = 64
DSA_HEADS = 16
DSA_KV_HEADS = 4
DSA_HEAD_DIM = 64
DSA_GROUP = DSA_HEADS // DSA_KV_HEADS
DSA_SCALE = DSA_HEAD_DIM ** -0.5
IDX_HEADS = 8
IDX_DIM = 64
IDX_W_SCALE = (IDX_HEADS ** -0.5) * (IDX_DIM ** -0.5)
TOPK_MAX = 256
NORM_EPS = 1e-5
GLA_QK = GLA_HEADS * GLA_DK
GLA_V = GLA_HEADS * GLA_DV
DSA_Q = DSA_HEADS * DSA_HEAD_DIM
DSA_KV = DSA_KV_HEADS * DSA_HEAD_DIM
IDX_Q = IDX_HEADS * IDX_DIM
CHUNK_SHIFT = CHUNK.bit_length() - 1
HEAD_SHIFT = DSA_HEAD_DIM.bit_length() - 1
SPLITS =(GLA_QK, GLA_QK, GLA_V, GLA_RANK, GLA_V,
          DSA_Q, DSA_KV, DSA_KV, IDX_Q, IDX_DIM, IDX_HEADS, DSA_Q,
          D_MODEL, D_MODEL)
_OFF = tuple(int(o) for o in np.cumsum((0,) + SPLITS))
(_GQ, _GK, _GV, _GLOW, _GR, _DQ, _DK, _DV, _IQ, _IK, _IW, _DZ, _MA, _MB) = range(14)

LANES = 128
VMEM_LIMIT_BYTES = 56 * 1024 * 1024

F_GQ, F_GK, F_DK, F_DV, F_SMALL = 0, 512, 1024, 1280, 1536
F_COLS = F_SMALL + LANES
SM_IK, SM_GLOW, SM_IW = 0, IDX_DIM, IDX_DIM + GLA_RANK
B_GV, B_GR, B_DQ, B_DZ, B_MA, B_MB, B_IQ = 0, 1024, 2048, 3072, 4096, 5120, 6144
B_COLS = B_IQ + IDX_Q
B_COL_TILE = B_COLS // 2

ROW_TILE = 2048
KEY_TILE = 512
NEG_BIG = -1e30
INT_MIN = -(2 ** 31)
NEG_INF_KEY = int(np.array(-np.inf, np.float32).view(np.int32)) ^ 0x7FFFFFFF


def _col(w, idx):
    return w[:, _OFF[idx]:_OFF[idx + 1]]


def _layer_norm_rows(x, g, b):
    mu = jnp.mean(x, axis=-1, keepdims=True)
    xc = x - mu
    var = jnp.mean(xc * xc, axis=-1, keepdims=True)
    return xc * lax.rsqrt(var + NORM_EPS) * g + b


def _dot(a, b):
    return jnp.dot(a, b, preferred_element_type=F32)


def _dot_nt(a, b):
    return lax.dot_general(a, b, (((1,), (1,)), ((), ())), preferred_element_type=F32)


def _dot_tn(a, b):
    return lax.dot_general(a, b, (((0,), (0,)), ((), ())), preferred_element_type=F32)


def _proj_f32_kernel(x_ref, g_ref, b_ref, w_ref, kg_ref, kb_ref, o_ref):
    hn = _layer_norm_rows(x_ref[...], g_ref[...], b_ref[...])
    y = _dot(hn.astype(BF16), w_ref[...])
    o_ref[...] = y
    ik = y[:, F_SMALL:F_SMALL + IDX_DIM]
    o_ref[:, F_SMALL:F_SMALL + IDX_DIM] = _layer_norm_rows(ik, kg_ref[...], kb_ref[...])


def _proj_f32(x, g, b, w, kg, kb, tm):
    rows = x.shape[0]
    return pl.pallas_call(
        _proj_f32_kernel,
        grid=(rows // tm,),
        in_specs=[
            pl.BlockSpec((tm, D_MODEL), lambda i: (i, 0)),
            pl.BlockSpec((1, D_MODEL), lambda i: (0, 0)),
            pl.BlockSpec((1, D_MODEL), lambda i: (0, 0)),
            pl.BlockSpec((D_MODEL, F_COLS), lambda i: (0, 0)),
            pl.BlockSpec((1, IDX_DIM), lambda i: (0, 0)),
            pl.BlockSpec((1, IDX_DIM), lambda i: (0, 0)),
        ],
        out_specs=pl.BlockSpec((tm, F_COLS), lambda i: (i, 0)),
        out_shape=jax.ShapeDtypeStruct((rows, F_COLS), F32),
        compiler_params=pltpu.CompilerParams(
            dimension_semantics=("parallel",), vmem_limit_bytes=VMEM_LIMIT_BYTES),
        name="proj_f32",
    )(x, g, b, w, kg, kb)


def _proj_bf16_kernel(x_ref, g_ref, b_ref, w_ref, o_ref, hn_ref):
    @pl.when(pl.program_id(1) == 0)
    def _():
        hn_ref[...] = _layer_norm_rows(x_ref[...], g_ref[...], b_ref[...]).astype(BF16)

    o_ref[...] = _dot(hn_ref[...], w_ref[...]).astype(BF16)


def _proj_bf16(x, g, b, w, tm, tn):
    rows = x.shape[0]
    return pl.pallas_call(
        _proj_bf16_kernel,
        grid=(rows // tm, B_COLS // tn),
        in_specs=[
            pl.BlockSpec((tm, D_MODEL), lambda i, j: (i, 0)),
            pl.BlockSpec((1, D_MODEL), lambda i, j: (0, 0)),
            pl.BlockSpec((1, D_MODEL), lambda i, j: (0, 0)),
            pl.BlockSpec((D_MODEL, tn), lambda i, j: (0, j)),
        ],
        out_specs=pl.BlockSpec((tm, tn), lambda i, j: (i, j)),
        out_shape=jax.ShapeDtypeStruct((rows, B_COLS), BF16),
        scratch_shapes=[pltpu.VMEM((tm, D_MODEL), BF16)],
        compiler_params=pltpu.CompilerParams(
            dimension_semantics=("parallel", "arbitrary"), vmem_limit_bytes=VMEM_LIMIT_BYTES),
        name="proj_bf16",
    )(x, g, b, w)


def _gla_levels(blk):
    levels = []
    s = blk // 2
    while s >= 1:
        levels.append(s)
        s //= 2
    return levels


def _gla_decay_matrix(blk):
    i = np.arange(blk)[:, None]
    t = np.arange(blk)[None, :]
    mats = [(t <= i).astype(np.float32), (t > i).astype(np.float32)]
    for s in _gla_levels(blk):
        mid = (i // (2 * s)) * (2 * s) + s - 1
        lower = (i // s) % 2 == 1
        m = np.where(lower, ((t > mid) & (t <= i)), False).astype(np.float32)
        n = np.where(~lower, ((t > i) & (t <= mid)), False).astype(np.float32)
        mats.append(m + n)
    return np.concatenate(mats, axis=0)


def _gla_kernel(q_ref, k_ref, sm_ref, v_ref, r_ref, c_ref, w2_ref, gb_ref, ng_ref, s0_ref,
                o_ref, st_ref, s_scr, *, blk, n_blk, t_valid):
    ti = pl.program_id(1)
    levels = _gla_levels(blk)

    @pl.when(ti == 0)
    def _():
        s_scr[...] = s0_ref[0]

    hb = GLA_HEADS * blk
    ri = lax.broadcasted_iota(I32, (hb, hb), 0)
    ci = lax.broadcasted_iota(I32, (hb, hb), 1)
    level_id = jnp.where(ri == ci, 0, -1)
    for li, s in enumerate(levels):
        sh = s.bit_length() - 1
        same = (ri >> (sh + 1)) == (ci >> (sh + 1))
        level_id = jnp.where(same & (((ri >> sh) & 1) == 1) & (((ci >> sh) & 1) == 0), 1 + li, level_id)

    def stack(x, width):
        return jnp.concatenate([x[:, h * width:(h + 1) * width] for h in range(GLA_HEADS)], axis=0)
    row_iota = lax.broadcasted_iota(I32, (blk, 1), 0)
    cmat = c_ref[...]
    w2 = w2_ref[...]
    gbias = gb_ref[...]
    ng = ng_ref[...]

    def block(j):
        r0 = pl.multiple_of(j * blk, blk)
        rows = pl.ds(r0, blk)
        valid = (ti * (n_blk * blk) + r0 + row_iota) < t_valid
        gq = q_ref[rows, :] * (GLA_DK ** -0.5)
        gk = jnp.where(valid, k_ref[rows, :], 0.0)
        x = _dot(sm_ref[rows, :].astype(BF16), w2) + gbias
        logf = (jnp.minimum(x, 0.0) - jnp.log1p(jnp.exp(-jnp.abs(x)))) * (1.0 / GLA_TAU)
        logf = jnp.where(valid, logf, 0.0)
        hi = logf.astype(BF16)
        r1 = logf - hi.astype(F32)
        mid = r1.astype(BF16)
        lo = (r1 - mid.astype(F32)).astype(BF16)
        e_all = _dot(cmat, jnp.concatenate([hi, mid, lo], axis=0))
        qs = stack(gq, GLA_DK)
        ks = stack(gk, GLA_DK)
        vs = stack(v_ref[rows, :], GLA_DV)
        b_s = stack(e_all[0:blk], GLA_DK)
        a = jnp.where(level_id == 0, _dot_nt(qs.astype(BF16), ks.astype(BF16)), 0.0)
        for li in range(len(levels)):
            e = jnp.exp(stack(e_all[(2 + li) * blk:(3 + li) * blk], GLA_DK))
            p = _dot_nt((qs * e).astype(BF16), (ks * e).astype(BF16))
            a = jnp.where(level_id == 1 + li, p, a)
        o_intra = _dot(a.astype(BF16), vs)
        qe = (qs * jnp.exp(b_s)).astype(BF16)
        kd = (ks * jnp.exp(stack(e_all[blk:2 * blk], GLA_DK))).astype(BF16)
        for h in range(GLA_HEADS):
            hs = slice(h * blk, (h + 1) * blk)
            vsl = slice(h * GLA_DV, (h + 1) * GLA_DV)
            st = s_scr[h]
            o = _dot_nt(qe[hs], st.astype(BF16)) + o_intra[hs]
            dec = jnp.exp(b_s[(h + 1) * blk - 1:(h + 1) * blk, :])
            s_scr[h] = st * dec + _dot_tn(vs[hs], kd[hs])
            on = o * lax.rsqrt(jnp.mean(o * o, axis=-1, keepdims=True) + NORM_EPS) * ng
            gr = r_ref[rows, vsl].astype(F32)
            o_ref[rows, vsl] = (on * (gr * jax.nn.sigmoid(gr))).astype(BF16)

    n_live = jnp.clip((t_valid - ti * (n_blk * blk) + blk - 1) // blk, 0, n_blk)

    @pl.when(n_live < n_blk)
    def _():
        o_ref[...] = jnp.zeros(o_ref.shape, o_ref.dtype)

    _for_blocks(block, 0, n_live, GLA_UNROLLS)

    @pl.when(ti == pl.num_programs(1) - 1)
    def _():
        st_ref[0] = s_scr[...]


def _gla(pf, pb, cmat, w2pad, gbias, ng, s0t, *, n_seq, row0, rows_per_seq, t_valid, blk, n_blk):
    rb = blk * n_blk
    steps = rows_per_seq // rb
    base = row0 // rb

    def rmap(c):
        return lambda s, t: (base + s * steps + t, c)

    kern = functools.partial(_gla_kernel, blk=blk, n_blk=n_blk, t_valid=t_valid)
    return pl.pallas_call(
        kern,
        grid=(n_seq, steps),
        in_specs=[
            pl.BlockSpec((rb, GLA_QK), rmap(F_GQ // GLA_QK)),
            pl.BlockSpec((rb, GLA_QK), rmap(F_GK // GLA_QK)),
            pl.BlockSpec((rb, LANES), rmap(F_SMALL // LANES)),
            pl.BlockSpec((rb, GLA_V), rmap(B_GV // GLA_V)),
            pl.BlockSpec((rb, GLA_V), rmap(B_GR // GLA_V)),
            pl.BlockSpec(cmat.shape, lambda s, t: (0, 0)),
            pl.BlockSpec(w2pad.shape, lambda s, t: (0, 0)),
            pl.BlockSpec((1, GLA_QK), lambda s, t: (0, 0)),
            pl.BlockSpec((1, GLA_DV), lambda s, t: (0, 0)),
            pl.BlockSpec((1, GLA_HEADS, GLA_DV, GLA_DK), lambda s, t: (s, 0, 0, 0)),
        ],
        out_specs=[
            pl.BlockSpec((rb, GLA_V), lambda s, t: (s * steps + t, 0)),
            pl.BlockSpec((1, GLA_HEADS, GLA_DV, GLA_DK), lambda s, t: (s, 0, 0, 0)),
        ],
        out_shape=[
            jax.ShapeDtypeStruct((n_seq * rows_per_seq, GLA_V), BF16),
            jax.ShapeDtypeStruct((n_seq, GLA_HEADS, GLA_DV, GLA_DK), F32),
        ],
        scratch_shapes=[pltpu.VMEM((GLA_HEADS, GLA_DV, GLA_DK), F32)],
        compiler_params=pltpu.CompilerParams(
            dimension_semantics=("parallel", "arbitrary"), vmem_limit_bytes=VMEM_LIMIT_BYTES),
        name="gla",
    )(pf, pf, pf, pb, pb, cmat, w2pad, gbias, ng, s0t)


def _dsa_kernel(iq_ref, sm_ref, qd_ref, kit_ref, kt_ref, v_ref, o_ref,
                key_scr, m_scr, l_scr, acc_scr, *, tq, tk, n_keys, topk, n_kb_total):
    n_kb = n_kb_total
    lane_iota = lax.broadcasted_iota(I32, (1, tk), 1)

    iq = iq_ref[...]
    iq_hm = jnp.concatenate([iq[:, h * IDX_DIM:(h + 1) * IDX_DIM] for h in range(IDX_HEADS)], axis=0)
    w_i = sm_ref[:, SM_IW:SM_IW + IDX_HEADS] * IDX_W_SCALE
    w_cols = [jnp.broadcast_to(w_i[:, h:h + 1], (tq, LANES)) for h in range(IDX_HEADS)]

    def score_block(kb, carry):
        k0 = pl.multiple_of(kb * tk, tk)
        r = _dot(iq_hm, kit_ref[0, :, pl.ds(k0, tk)])
        sc = None
        for h in range(IDX_HEADS):
            rh = jnp.maximum(r[h * tq:(h + 1) * tq, :], 0.0)
            wh = jnp.concatenate([w_cols[h]] * (tk // LANES), axis=1)
            sc = rh * wh if sc is None else sc + rh * wh
        sc = jnp.where(k0 + lane_iota < n_keys, sc, -jnp.inf)
        bits = pltpu.bitcast(sc, I32)
        key_scr[:, pl.ds(k0, tk)] = bits ^ ((bits >> 31) & 0x7FFFFFFF)
        return carry

    lax.fori_loop(0, n_kb, score_block, 0)

    n_cols = n_kb * (tk // LANES)

    def count_ge(cand):
        cb = jnp.broadcast_to(cand, (tq, LANES))

        def body(kb, acc):
            for u in range(tk // LANES):
                c0 = pl.multiple_of(kb * tk + u * LANES, LANES)
                acc = acc + jnp.where(key_scr[:, pl.ds(c0, LANES)] >= cb, 1, 0)
            return acc

        acc = lax.fori_loop(0, n_kb, body, jnp.zeros((tq, LANES), I32))
        return jnp.sum(acc, axis=1, keepdims=True)

    bits_per_round = 4

    def search_cond(carry):
        i, lo, cge = carry
        return (i < 32) & (jnp.max(jnp.abs(cge - topk)) > 0)

    def search_round(carry):
        i, lo, cge = carry
        for u in range(bits_per_round):
            cand = lo + jnp.left_shift(jnp.int32(1), 31 - u - i)
            cnt = count_ge(cand)
            take = cnt >= topk
            lo = jnp.where(take, cand, lo)
            cge = jnp.where(take, cnt, cge)
        return i + bits_per_round, lo, cge

    lo0 = jnp.full((tq, 1), INT_MIN, I32)
    cge0 = jnp.zeros((tq, 1), I32) + n_cols * LANES
    _, thr, cge = lax.while_loop(search_cond, search_round, (jnp.int32(0), lo0, cge0))
    need_tie = (cge > topk) & (thr > NEG_INF_KEY)
    thr = jnp.maximum(thr, NEG_INF_KEY + 1)

    @pl.when(jnp.max(need_tie.astype(I32)) > 0)
    def _():
        want = topk - count_ge(thr + 1)
        thr_b = jnp.broadcast_to(thr, (tq, LANES))
        col_iota = lax.broadcasted_iota(I32, (tq, LANES), 1)

        def count_tied_below(m):
            mb = jnp.broadcast_to(m, (tq, LANES))

            def body(c, acc):
                c0 = pl.multiple_of(c * LANES, LANES)
                hit = (key_scr[:, pl.ds(c0, LANES)] == thr_b) & ((c0 + col_iota) < mb)
                return acc + jnp.where(hit, 1, 0)

            acc = lax.fori_loop(0, n_cols, body, jnp.zeros((tq, LANES), I32))
            return jnp.sum(acc, axis=1, keepdims=True)

        n_bits = int(math.ceil(math.log2(n_kb_total * tk + 1)))

        def idx_step(i, m):
            cand = m + jnp.left_shift(jnp.int32(1), n_bits - 1 - i)
            return jnp.where(count_tied_below(cand) <= want, cand, m)

        m_keep = lax.fori_loop(0, n_bits, idx_step, jnp.zeros((tq, 1), I32))
        mk_b = jnp.broadcast_to(jnp.where(need_tie, m_keep, jnp.int32(2 ** 30)), (tq, LANES))

        def demote(c, carry):
            c0 = pl.multiple_of(c * LANES, LANES)
            kv = key_scr[:, pl.ds(c0, LANES)]
            drop = (kv == thr_b) & ((c0 + col_iota) >= mk_b)
            key_scr[:, pl.ds(c0, LANES)] = jnp.where(drop, kv - 1, kv)
            return carry

        lax.fori_loop(0, n_cols, demote, 0)

    qd = qd_ref[...]
    win = lax.broadcasted_iota(I32, (tq, DSA_KV), 1) >> HEAD_SHIFT
    q_all = jnp.concatenate(
        [jnp.where(win == g, qd[:, r * DSA_KV:(r + 1) * DSA_KV].astype(F32), 0.0).astype(BF16)
         for g in range(DSA_KV_HEADS) for r in range(DSA_GROUP)], axis=0)
    m_scr[...] = jnp.full(m_scr.shape, NEG_BIG, F32)
    l_scr[...] = jnp.zeros(l_scr.shape, F32)
    acc_scr[...] = jnp.zeros(acc_scr.shape, F32)
    thr_t = jnp.broadcast_to(thr, (tq, LANES))

    def attend_block(kb, carry):
        k0 = pl.multiple_of(kb * tk, tk)
        sel = key_scr[:, pl.ds(k0, tk)] >= jnp.concatenate([thr_t] * (tk // LANES), axis=1)
        bias = jnp.where(sel, 0.0, NEG_BIG)
        s = _dot(q_all, kt_ref[0, :, pl.ds(k0, tk)]) + jnp.concatenate([bias] * DSA_HEADS, axis=0)
        m_old = m_scr[...]
        m_new = jnp.maximum(m_old, jnp.max(s, axis=1, keepdims=True))
        alpha = jnp.exp2(m_old - m_new)
        p = jnp.exp2(s - m_new)
        l_scr[...] = alpha * l_scr[...] + jnp.sum(p, axis=1, keepdims=True)
        acc_scr[...] = alpha * acc_scr[...] + _dot(p.astype(BF16), v_ref[0, pl.ds(k0, tk), :])
        m_scr[...] = m_new
        return carry

    lax.fori_loop(0, n_kb, attend_block, 0)

    o_all = acc_scr[...] / l_scr[...]
    for g in range(DSA_KV_HEADS):
        for r in range(DSA_GROUP):
            h = g * DSA_GROUP + r
            o_ref[:, h * DSA_HEAD_DIM:(h + 1) * DSA_HEAD_DIM] = (
                o_all[h * tq:(h + 1) * tq, g * DSA_HEAD_DIM:(g + 1) * DSA_HEAD_DIM].astype(BF16))


def _dsa(pf, pb, kit, kt, v, *, n_seq, row0, rows_per_seq, tq, tk, n_keys, topk):
    nk_pad = kit.shape[-1]
    n_kb_total = nk_pad // tk
    steps = rows_per_seq // tq
    base = row0 // tq

    def rmap(c):
        return lambda s, t: (base + s * steps + t, c)

    kern = functools.partial(_dsa_kernel, tq=tq, tk=tk, n_keys=n_keys, topk=topk, n_kb_total=n_kb_total)
    single = pl.Buffered(1)
    return pl.pallas_call(
        kern,
        grid=(n_seq, steps),
        in_specs=[
            pl.BlockSpec((tq, IDX_Q), rmap(B_IQ // IDX_Q)),
            pl.BlockSpec((tq, LANES), rmap(F_SMALL // LANES)),
            pl.BlockSpec((tq, DSA_Q), rmap(B_DQ // DSA_Q)),
            pl.BlockSpec((1, IDX_DIM, nk_pad), lambda s, t: (s, 0, 0), pipeline_mode=single),
            pl.BlockSpec((1, DSA_KV, nk_pad), lambda s, t: (s, 0, 0), pipeline_mode=single),
            pl.BlockSpec((1, nk_pad, DSA_KV), lambda s, t: (s, 0, 0), pipeline_mode=single),
        ],
        out_specs=pl.BlockSpec((tq, DSA_Q), lambda s, t: (s * steps + t, 0)),
        out_shape=jax.ShapeDtypeStruct((n_seq * rows_per_seq, DSA_Q), BF16),
        scratch_shapes=[
            pltpu.VMEM((tq, nk_pad), I32),
            pltpu.VMEM((DSA_HEADS * tq, 1), F32),
            pltpu.VMEM((DSA_HEADS * tq, 1), F32),
            pltpu.VMEM((DSA_HEADS * tq, DSA_KV), F32),
        ],
        compiler_params=pltpu.CompilerParams(
            dimension_semantics=("parallel", "arbitrary"), vmem_limit_bytes=VMEM_LIMIT_BYTES),
        name="dsa",
    )(pb, pf, pb, kit, kt, v)


BIT_GROUP = 256
SLAB_ROWS = 128
BLOCK_UNROLLS = (16, 8, 4, 2, 1)
GLA_UNROLLS = (8, 4, 2, 1)


def _for_blocks(block_fn, start, stop, unrolls=BLOCK_UNROLLS):
    for u in unrolls:
        trips = (stop - start) // u

        def body(j, carry, u=u, start=start):
            for t in range(u):
                block_fn(start + u * j + t)
            return carry

        lax.fori_loop(0, trips, body, 0)
        start = start + trips * u


def _slab(g):
    return g * DSA_HEAD_DIM // LANES * LANES


def _bit_transpose32(words):
    a = list(words)
    mask, j = 0x0000FFFF, 16
    while j:
        k = 0
        while k < 32:
            t = (a[k] ^ lax.shift_right_logical(a[k + j], jnp.int32(j))) & mask
            a[k] = a[k] ^ t
            a[k + j] = a[k + j] ^ (t << j)
            k = (k + j + 1) & ~j
        j >>= 1
        mask ^= (mask << j) & 0xFFFFFFFF
    return a


V_ROWS = 80
V_ONES_ROW = DSA_HEAD_DIM


def _dsa_t_kernel(iq_ref, sm_ref, qd_ref, ki_ref, k_ref, vt_ref, o_ref,
                  key_scr, m_scr, acc_scr, s_scr, plane_scr, active_scr, *, tq, tk, n_keys, topk):
    live = pl.program_id(1) * tq < n_keys

    @pl.when(live)
    def _():
        _dsa_t_body(iq_ref, sm_ref, qd_ref, ki_ref, k_ref, vt_ref, o_ref, key_scr, m_scr, acc_scr, s_scr,
                    plane_scr, active_scr,
                    tq=tq, tk=tk, n_keys=n_keys, topk=topk)

    @pl.when(jnp.logical_not(live))
    def _():
        o_ref[...] = jnp.zeros(o_ref.shape, o_ref.dtype)


def _dsa_t_body(iq_ref, sm_ref, qd_ref, ki_ref, k_ref, vt_ref, o_ref,
                key_scr, m_scr, acc_scr, s_scr, plane_scr, active_scr, *, tq, tk, n_keys, topk):
    qb = pl.program_id(1)
    q0 = qb * tq
    q_chunk_max = (q0 + tq - 1 - N_META) >> CHUNK_SHIFT
    limit = jnp.minimum(N_META + CHUNK * (q_chunk_max + 1), n_keys)
    n_kb = (limit + tk - 1) // tk
    n_full = jnp.minimum(jnp.minimum((q0 + N_META) // tk, n_keys // tk), n_kb)

    q_chunk = (q0 + lax.broadcasted_iota(I32, (1, tq), 1) - N_META) >> CHUNK_SHIFT
    row_iota = lax.broadcasted_iota(I32, (tk, tq), 0)
    plane_rows = key_scr.shape[0] // BIT_GROUP * 8

    @pl.when(qb == 0)
    def _():
        plane_scr[...] = jnp.zeros(plane_scr.shape, I32)

    iq = iq_ref[...]
    iq_hm = jnp.concatenate([iq[:, h * IDX_DIM:(h + 1) * IDX_DIM] for h in range(IDX_HEADS)], axis=0)
    w_t = sm_ref[...].T[SM_IW:SM_IW + IDX_HEADS, :] * IDX_W_SCALE
    iq_hm = iq_hm.astype(F32).T.astype(BF16)

    def score_block(kb, masked):
        k0 = pl.multiple_of(kb * tk, tk)
        r = _dot(ki_ref[0, pl.ds(k0, tk), :], iq_hm)
        sc = None
        for h in range(IDX_HEADS):
            t = jnp.maximum(r[:, h * tq:(h + 1) * tq], 0.0) * w_t[h:h + 1, :]
            sc = t if sc is None else sc + t
        if masked:
            key_pos = k0 + row_iota
            adm = (key_pos < n_keys) & (((key_pos - N_META) >> CHUNK_SHIFT) <= q_chunk)
            sc = jnp.where(adm, sc, -jnp.inf)
        bits = pltpu.bitcast(sc, I32)
        keys = bits ^ ((bits >> 31) & 0x7FFFFFFF)
        key_scr[pl.ds(k0, tk), :] = keys
        ukeys = keys ^ INT_MIN
        for u in range(tk // BIT_GROUP):
            words = [ukeys[u * BIT_GROUP + 8 * w:u * BIT_GROUP + 8 * w + 8, :] for w in range(32)]
            planes = _bit_transpose32(words)
            g_row = pl.multiple_of((kb * (tk // BIT_GROUP) + u) * 8, 8)
            for w in range(32):
                plane_scr[pl.ds(w * plane_rows + g_row, 8), :] = planes[w]

    def score_edge(kb, carry):
        score_block(kb, True)
        return carry

    _for_blocks(lambda kb: score_block(kb, False), 0, n_full)
    lax.fori_loop(n_full, n_kb, score_edge, 0)

    n_chunks = n_kb * (tk // LANES)
    sub_per_blk = tk // LANES

    def count_rows(pred):
        def body(kb, acc):
            for u in range(sub_per_blk):
                c0 = pl.multiple_of(kb * tk + u * LANES, LANES)
                acc = acc + jnp.where(pred(key_scr[pl.ds(c0, LANES), :], c0), 1, 0)
            return acc

        acc = lax.fori_loop(0, n_kb, body, jnp.zeros((LANES, tq), I32))
        return jnp.sum(acc, axis=0, keepdims=True)

    def count_ge(cand):
        return count_rows(lambda kv, c0: kv >= cand)

    n_groups = n_kb * (tk // BIT_GROUP)
    n_slabs = (n_groups * 8 + SLAB_ROWS - 1) // SLAB_ROWS
    slab_iota = lax.broadcasted_iota(I32, (SLAB_ROWS, tq), 0)

    def init_active(sl, carry):
        r0 = pl.multiple_of(sl * SLAB_ROWS, SLAB_ROWS)
        active_scr[pl.ds(r0, SLAB_ROWS), :] = jnp.where(r0 + slab_iota < n_groups * 8, -1, 0)
        return carry

    lax.fori_loop(0, n_slabs, init_active, 0)

    def select_pass(i, carry):
        thr_u, c_above, n_act, flip = carry
        cur = i * plane_rows
        prev = cur - plane_rows

        def body(sl, acc):
            r0 = pl.multiple_of(sl * SLAB_ROWS, SLAB_ROWS)
            rows = pl.ds(r0, SLAB_ROWS)
            act = active_scr[rows, :] & (plane_scr[pl.ds(pl.multiple_of(prev + r0, 8), SLAB_ROWS), :] ^ flip)
            active_scr[rows, :] = act
            return acc + lax.population_count(act & plane_scr[pl.ds(pl.multiple_of(cur + r0, 8), SLAB_ROWS), :])

        acc = lax.fori_loop(0, n_slabs, body, jnp.zeros((SLAB_ROWS, tq), I32))
        ones = jnp.sum(acc, axis=0, keepdims=True)
        take = c_above + ones >= topk
        thr_u = thr_u | jnp.where(take, jnp.left_shift(jnp.int32(1), 31 - i), 0)
        c_above = c_above + jnp.where(take, 0, ones)
        n_act = jnp.where(take, ones, n_act - ones)
        return thr_u, c_above, n_act, jnp.where(take, 0, -1)

    n_act = jnp.zeros((1, tq), I32) + n_groups * BIT_GROUP

    def count_top(sl, acc):
        rows = pl.ds(pl.multiple_of(sl * SLAB_ROWS, SLAB_ROWS), SLAB_ROWS)
        return acc + lax.population_count(active_scr[rows, :] & plane_scr[rows, :])

    acc0 = lax.fori_loop(0, n_slabs, count_top, jnp.zeros((SLAB_ROWS, tq), I32))
    ones0 = jnp.sum(acc0, axis=0, keepdims=True)
    take0 = ones0 >= topk
    carry0 = (jnp.where(take0, INT_MIN, 0), jnp.where(take0, 0, ones0),
              jnp.where(take0, ones0, n_act - ones0), jnp.where(take0, 0, -1))
    thr_u, c_above, n_act, _ = lax.fori_loop(1, 32, select_pass, carry0)
    thr = thr_u ^ INT_MIN
    cge = c_above + n_act
    need_tie = (cge > topk) & (thr > NEG_INF_KEY)
    thr = jnp.maximum(thr, NEG_INF_KEY + 1)

    @pl.when(jnp.max(need_tie.astype(I32)) > 0)
    def _():
        want = topk - count_ge(thr + 1)
        sub_iota = lax.broadcasted_iota(I32, (LANES, tq), 0)
        n_bits = int(math.ceil(math.log2(key_scr.shape[0] + 1)))

        def idx_step(i, m):
            cand = m + jnp.left_shift(jnp.int32(1), n_bits - 1 - i)
            tied_below = count_rows(lambda kv, c0: (kv == thr) & ((c0 + sub_iota) < cand))
            return jnp.where(tied_below <= want, cand, m)

        m_keep = lax.fori_loop(0, n_bits, idx_step, jnp.zeros((1, tq), I32))
        m_keep = jnp.where(need_tie, m_keep, jnp.int32(2 ** 30))

        def demote(c, carry):
            c0 = pl.multiple_of(c * LANES, LANES)
            kv = key_scr[pl.ds(c0, LANES), :]
            drop = (kv == thr) & ((c0 + sub_iota) >= m_keep)
            key_scr[pl.ds(c0, LANES), :] = jnp.where(drop, kv - 1, kv)
            return carry

        lax.fori_loop(0, n_chunks, demote, 0)

    qd = qd_ref[...]
    win = lax.broadcasted_iota(I32, (tq, LANES), 1) >> HEAD_SHIFT
    q_pad = []
    for g in range(DSA_KV_HEADS):
        lo = _slab(g)
        q_pad.append(jnp.concatenate(
            [jnp.where(win == g % 2, qd[:, r * DSA_KV + lo:r * DSA_KV + lo + LANES].astype(F32), 0.0)
             for r in range(DSA_GROUP)], axis=0))
    q_pad = [x.T.astype(BF16) for x in q_pad]
    m_scr[...] = jnp.full(m_scr.shape, NEG_BIG, F32)
    acc_scr[...] = jnp.zeros(acc_scr.shape, F32)
    s_scr[0] = _dot(k_ref[0, pl.ds(0, tk), 0:LANES], q_pad[0])

    def attend_block(kb):
        k0 = pl.multiple_of(kb * tk, tk)
        bias = jnp.where(key_scr[pl.ds(k0, tk), :] >= thr, 0.0, NEG_BIG)
        bias_g = jnp.concatenate([bias] * DSA_GROUP, axis=1)
        kblk = k_ref[0, pl.ds(k0, tk), :]
        k1 = pl.multiple_of(jnp.minimum(kb + 1, n_kb - 1) * tk, tk)
        for g in range(DSA_KV_HEADS):
            if g + 1 < DSA_KV_HEADS:
                s_scr[(g + 1) % 2] = _dot(kblk[:, _slab(g + 1):_slab(g + 1) + LANES], q_pad[g + 1])
            else:
                s_scr[0] = _dot(k_ref[0, pl.ds(k1, tk), 0:LANES], q_pad[0])
            s = s_scr[g % 2] + bias_g
            m_old = m_scr[g]
            m_new = jnp.maximum(m_old, jnp.max(s, axis=0, keepdims=True))
            alpha = jnp.exp2(m_old[0:1] - m_new[0:1])
            p = jnp.exp2(s - m_new[0:1]).astype(BF16)
            acc_scr[g] = acc_scr[g] * alpha + _dot(vt_ref[0, g, :, pl.ds(k0, tk)], p)
            m_scr[g] = m_new

    _for_blocks(attend_block, 0, n_kb)

    for g in range(DSA_KV_HEADS):
        acc = acc_scr[g]
        og = acc[0:DSA_HEAD_DIM] / acc[V_ONES_ROW:V_ONES_ROW + 1]
        for r in range(DSA_GROUP):
            c0 = (g * DSA_GROUP + r) * DSA_HEAD_DIM
            o_ref[:, c0:c0 + DSA_HEAD_DIM] = og[:, r * tq:(r + 1) * tq].T.astype(BF16)


def _dsa_t(pf, pb, ki, k, vt, *, n_seq, rows_per_seq, tk, n_keys, topk):
    tq = LANES
    nk_pad = k.shape[1]
    steps = rows_per_seq // tq

    def rmap(c):
        return lambda s, t: (s * steps + t, c)

    kern = functools.partial(_dsa_t_kernel, tq=tq, tk=tk, n_keys=n_keys, topk=topk)
    single = pl.Buffered(1)
    return pl.pallas_call(
        kern,
        grid=(n_seq, steps),
        in_specs=[
            pl.BlockSpec((tq, IDX_Q), rmap(B_IQ // IDX_Q)),
            pl.BlockSpec((tq, LANES), rmap(F_SMALL // LANES)),
            pl.BlockSpec((tq, DSA_Q), rmap(B_DQ // DSA_Q)),
            pl.BlockSpec((1, nk_pad, IDX_DIM), lambda s, t: (s, 0, 0), pipeline_mode=single),
            pl.BlockSpec((1, nk_pad, DSA_KV), lambda s, t: (s, 0, 0), pipeline_mode=single),
            pl.BlockSpec((1, DSA_KV_HEADS, V_ROWS, nk_pad), lambda s, t: (s, 0, 0, 0), pipeline_mode=single),
        ],
        out_specs=pl.BlockSpec((tq, DSA_Q), lambda s, t: (s * steps + t, 0)),
        out_shape=jax.ShapeDtypeStruct((n_seq * rows_per_seq, DSA_Q), BF16),
        scratch_shapes=[
            pltpu.VMEM((nk_pad, tq), I32),
            pltpu.VMEM((DSA_KV_HEADS, 8, DSA_GROUP * tq), F32),
            pltpu.VMEM((DSA_KV_HEADS, V_ROWS, DSA_GROUP * tq), F32),
            pltpu.VMEM((2, tk, DSA_GROUP * tq), F32),
            pltpu.VMEM((nk_pad + SLAB_ROWS, tq), I32),
            pltpu.VMEM((_round_up(nk_pad // BIT_GROUP * 8, SLAB_ROWS), tq), I32),
        ],
        compiler_params=pltpu.CompilerParams(
            dimension_semantics=("parallel", "arbitrary"), vmem_limit_bytes=VMEM_LIMIT_BYTES),
        name="dsa_t",
    )(pb, pf, pb, ki, k, vt)


def _out_kernel(x_ref, ya_ref, ob_ref, z_ref, ma_ref, mb_ref, wg_ref, wd_ref, wo_ref,
                gate_ref, lig_ref, lib_ref, lg_ref, lb_ref, o_ref, *, alpha):
    hn = _layer_norm_rows(x_ref[...], lig_ref[...], lib_ref[...])
    y_a = _dot(ya_ref[...], wg_ref[...])
    z = z_ref[...].astype(F32)
    yb_in = ob_ref[...].astype(F32) * (z * jax.nn.sigmoid(z))
    y_b = _dot(yb_in.astype(BF16), wd_ref[...])
    ga = jax.nn.sigmoid(ma_ref[...].astype(F32) + gate_ref[0:1, :])
    gb = jax.nn.sigmoid(mb_ref[...].astype(F32) + gate_ref[1:2, :])
    merged = ga * y_a + gb * y_b
    y = alpha * hn + _dot(merged.astype(BF16), wo_ref[...])
    o_ref[...] = _layer_norm_rows(y, lg_ref[...], lb_ref[...])


def _out(x, ya, ob, pb, wg, wd, wo, gate, lig, lib, lg, lb, tm, alpha, row0):
    rows = ya.shape[0]
    base = row0 // tm
    row = lambda c: pl.BlockSpec((tm, D_MODEL), lambda i, c=c: (i, c))
    off = lambda c: pl.BlockSpec((tm, D_MODEL), lambda i, c=c: (base + i, c))
    full = lambda a: pl.BlockSpec(a.shape, lambda i: (0, 0))
    return pl.pallas_call(
        functools.partial(_out_kernel, alpha=alpha),
        grid=(rows // tm,),
        in_specs=[off(0), row(0), row(0), off(B_DZ // D_MODEL), off(B_MA // D_MODEL), off(B_MB // D_MODEL),
                  full(wg), full(wd), full(wo), full(gate), full(lig), full(lib), full(lg), full(lb)],
        out_specs=pl.BlockSpec((tm, D_MODEL), lambda i: (i, 0)),
        out_shape=jax.ShapeDtypeStruct((rows, D_MODEL), F32),
        compiler_params=pltpu.CompilerParams(
            dimension_semantics=("parallel",), vmem_limit_bytes=VMEM_LIMIT_BYTES),
        name="out_proj",
    )(x, ya, ob, pb, pb, pb, wg, wd, wo, gate, lig, lib, lg, lb)


def _round_up(a, b):
    return -(-a // b) * b


def _forward(x_prompt, x_sample, cache_k, cache_v, cache_idx_k, state_gla, meta, ln_in_g, ln_in_b,
             w_in, gla_w2, gla_gate_b, gla_norm_g, idx_kn_g, idx_kn_b, w_gla, w_dsa, gate_b,
             w_out, ln_g, ln_b):
    depth = w_in.shape[0]
    assert depth == 1, "single-layer trunk"
    bsz, seq, _ = x_prompt.shape
    dbsz, dseq, _ = x_sample.shape
    past = cache_k.shape[2]
    t_p = N_META + seq
    tp = _round_up(t_p, KEY_TILE)
    assert tp % LANES == 0 and tp % GLA_BLOCK == 0
    row_s = bsz * tp
    rows = _round_up(row_s + dbsz * dseq, ROW_TILE)
    alpha = (2.0 * depth) ** 0.25

    pieces = []
    for b in range(bsz):
        pieces += [meta.astype(F32), x_prompt[b], jnp.zeros((tp - t_p, D_MODEL), F32)]
    pieces += [x_sample.reshape(dbsz * dseq, D_MODEL), jnp.zeros((rows - row_s - dbsz * dseq, D_MODEL), F32)]
    x_all = jnp.concatenate(pieces, axis=0)

    w = w_in[0]
    w_f = jnp.concatenate([_col(w, _GQ), _col(w, _GK), _col(w, _DK), _col(w, _DV), _col(w, _IK),
                           _col(w, _GLOW), _col(w, _IW),
                           jnp.zeros((D_MODEL, LANES - IDX_DIM - GLA_RANK - IDX_HEADS), F32)], axis=1).astype(BF16)
    w_dq = _col(w, _DQ).reshape(D_MODEL, DSA_KV_HEADS, DSA_GROUP, DSA_HEAD_DIM)
    w_dq = (jnp.swapaxes(w_dq, 1, 2) * (DSA_SCALE * math.log2(math.e))).reshape(D_MODEL, DSA_Q)
    w_b = jnp.concatenate([_col(w, _GV), _col(w, _GR), w_dq, _col(w, _DZ), _col(w, _MA), _col(w, _MB),
                           _col(w, _IQ)], axis=1).astype(BF16)
    lig = ln_in_g.reshape(1, D_MODEL)
    lib = ln_in_b.reshape(1, D_MODEL)

    pf = _proj_f32(x_all, lig, lib, w_f, idx_kn_g[0].reshape(1, IDX_DIM), idx_kn_b[0].reshape(1, IDX_DIM),
                   ROW_TILE // 4)
    pb = _proj_bf16(x_all, lig, lib, w_b, ROW_TILE // 4, B_COL_TILE)

    w2pad = jnp.zeros((LANES, GLA_QK), F32).at[SM_GLOW:SM_GLOW + GLA_RANK].set(gla_w2[0]).astype(BF16)
    gbias = gla_gate_b[0].reshape(1, GLA_QK)
    ng = gla_norm_g[0].reshape(1, GLA_DV)
    blk_p = min(GLA_BLOCK, t_p)
    cm_p = jnp.asarray(np.tile(_gla_decay_matrix(blk_p), (1, 3)), BF16)
    s0_p = jnp.zeros((bsz, GLA_HEADS, GLA_DV, GLA_DK), F32)
    ya_p, st_p = _gla(pf, pb, cm_p, w2pad, gbias, ng, s0_p, n_seq=bsz, row0=0, rows_per_seq=tp,
                      t_valid=t_p, blk=blk_p, n_blk=KEY_TILE // blk_p)
    blk_s = min(GLA_BLOCK, dseq)
    assert dseq % blk_s == 0 and row_s % dseq == 0
    cm_s = jnp.asarray(np.tile(_gla_decay_matrix(blk_s), (1, 3)), BF16)
    s0_s = jnp.swapaxes(state_gla[0], -1, -2)
    ya_s, st_s = _gla(pf, pb, cm_s, w2pad, gbias, ng, s0_s, n_seq=dbsz, row0=row_s, rows_per_seq=dseq,
                      t_valid=dseq, blk=blk_s, n_blk=dseq // blk_s)

    dk = pf[:, F_DK:F_DK + DSA_KV]
    dv = pf[:, F_DV:F_DV + DSA_KV]
    ki = pf[:, F_SMALL:F_SMALL + IDX_DIM]
    dk_p = dk[:row_s].reshape(bsz, tp, DSA_KV_HEADS, DSA_HEAD_DIM)
    dv_p = dv[:row_s].reshape(bsz, tp, DSA_KV_HEADS, DSA_HEAD_DIM)
    ki_p = ki[:row_s].reshape(bsz, tp, IDX_DIM)
    vt_p = jnp.concatenate([jnp.transpose(dv_p, (0, 2, 3, 1)),
                            jnp.ones((bsz, DSA_KV_HEADS, 1, tp), F32),
                            jnp.zeros((bsz, DSA_KV_HEADS, V_ROWS - DSA_HEAD_DIM - 1, tp), F32)], axis=2).astype(BF16)
    topk_p = min(TOPK_MAX, (t_p - N_META) // 4)
    ob_p = _dsa_t(pf, pb, ki_p.astype(BF16), dk[:row_s].reshape(bsz, tp, DSA_KV).astype(BF16), vt_p,
                  n_seq=bsz, rows_per_seq=tp, tk=KEY_TILE, n_keys=t_p, topk=topk_p)

    n_keys_s = past + dseq
    nk_s = _round_up(n_keys_s, KEY_TILE)
    dk_s = dk[row_s:row_s + dbsz * dseq].reshape(dbsz, dseq, DSA_KV_HEADS, DSA_HEAD_DIM)
    dv_s = dv[row_s:row_s + dbsz * dseq].reshape(dbsz, dseq, DSA_KV_HEADS, DSA_HEAD_DIM)
    ki_s = ki[row_s:row_s + dbsz * dseq].reshape(dbsz, dseq, IDX_DIM)
    kpad = jnp.zeros((dbsz, nk_s - n_keys_s, DSA_KV_HEADS, DSA_HEAD_DIM), F32)
    k_all = jnp.concatenate([cache_k[0], dk_s, kpad], axis=1)
    v_all = jnp.concatenate([cache_v[0], dv_s, kpad], axis=1)
    ki_all = jnp.concatenate([cache_idx_k[0], ki_s, jnp.zeros((dbsz, nk_s - n_keys_s, IDX_DIM), F32)], axis=1)
    kt_s = jnp.transpose(k_all, (0, 2, 3, 1)).reshape(dbsz, DSA_KV, nk_s).astype(BF16)
    v_s = v_all.reshape(dbsz, nk_s, DSA_KV).astype(BF16)
    kit_s = jnp.transpose(ki_all, (0, 2, 1)).astype(BF16)
    topk_s = min(TOPK_MAX, n_keys_s // 4)
    ob_s = _dsa(pf, pb, kit_s, kt_s, v_s, n_seq=dbsz, row0=row_s, rows_per_seq=dseq, tq=dseq, tk=KEY_TILE,
                n_keys=n_keys_s, topk=topk_s)

    out_w = (w_gla[0].astype(BF16), w_dsa[0].astype(BF16), w_out[0].astype(BF16), gate_b[0], lig, lib,
             ln_g[0].reshape(1, D_MODEL), ln_b[0].reshape(1, D_MODEL))
    tm_out = ROW_TILE // 4
    tm_s = math.gcd(tm_out, dbsz * dseq)
    assert row_s % tm_out == 0 and tm_s % 16 == 0
    y_p = _out(x_all, ya_p, ob_p, pb, *out_w, tm_out, alpha, 0)
    y_s = _out(x_all, ya_s, ob_s, pb, *out_w, tm_s, alpha, row_s)

    y_prompt = y_p.reshape(bsz, tp, D_MODEL)[:, N_META:t_p]
    y_sample = y_s.reshape(dbsz, dseq, D_MODEL)
    k_prompt = dk_p[:, :t_p][None]
    v_prompt = dv_p[:, :t_p][None]
    idx_k_prompt = ki_p[:, :t_p][None]
    gla_prompt = jnp.swapaxes(st_p, -1, -2)[None]
    k_sample = dk_s[None]
    v_sample = dv_s[None]
    idx_k_sample = ki_s[None]
    gla_sample = jnp.swapaxes(st_s, -1, -2)[None]
    return (y_prompt, y_sample, k_prompt, v_prompt, idx_k_prompt, gla_prompt,
            k_sample, v_sample, idx_k_sample, gla_sample)


def kernel(x_prompt, x_sample, cache_k, cache_v, cache_idx_k, state_gla, meta, ln_in_g, ln_in_b,
           w_in, gla_w2, gla_gate_b, gla_norm_g, idx_kn_g, idx_kn_b, w_gla, w_dsa, gate_b,
           w_out, ln_g, ln_b):
    return _forward(x_prompt, x_sample, cache_k, cache_v, cache_idx_k, state_gla, meta, ln_in_g, ln_in_b,
                    w_in, gla_w2, gla_gate_b, gla_norm_g, idx_kn_g, idx_kn_b, w_gla, w_dsa, gate_b,
                    w_out, ln_g, ln_b)
```

```python
import functools
import math

import numpy as np
import jax
import jax.numpy as jnp
from jax import lax
from jax.experimental import pallas as pl
from jax.experimental.pallas import tpu as pltpu

F32 = jnp.float32
BF16 = jnp.bfloat16
I32 = jnp.int32

D_MODEL = 1024
CHUNK = 64
N_META = 16
GLA_HEADS = 4
GLA_DK = 128
GLA_DV = 256
GLA_RANK = 16
GLA_TAU = 16.0
GLA_BLOCK = 64
DSA_HEADS = 16
DSA_KV_HEADS = 4
DSA_HEAD_DIM = 64
DSA_GROUP = DSA_HEADS // DSA_KV_HEADS
DSA_SCALE = DSA_HEAD_DIM ** -0.5
IDX_HEADS = 8
IDX_DIM = 64
IDX_W_SCALE = (IDX_HEADS ** -0.5) * (IDX_DIM ** -0.5)
TOPK_MAX = 256
NORM_EPS = 1e-5
GLA_QK = GLA_HEADS * GLA_DK
GLA_V = GLA_HEADS * GLA_DV
DSA_Q = DSA_HEADS * DSA_HEAD_DIM
DSA_KV = DSA_KV_HEADS * DSA_HEAD_DIM
IDX_Q = IDX_HEADS * IDX_DIM
CHUNK_SHIFT = CHUNK.bit_length() - 1
HEAD_SHIFT = DSA_HEAD_DIM.bit_length() - 1
SPLITS =(GLA_QK, GLA_QK, GLA_V, GLA_RANK, GLA_V,
          DSA_Q, DSA_KV, DSA_KV, IDX_Q, IDX_DIM, IDX_HEADS, DSA_Q,
          D_MODEL, D_MODEL)
_OFF = tuple(int(o) for o in np.cumsum((0,) + SPLITS))
(_GQ, _GK, _GV, _GLOW, _GR, _DQ, _DK, _DV, _IQ, _IK, _IW, _DZ, _MA, _MB) = range(14)

LANES = 128
VMEM_LIMIT_BYTES = 56 * 1024 * 1024

F_GQ, F_GK, F_DK, F_DV, F_SMALL = 0, 512, 1024, 1280, 1536
F_COLS = F_SMALL + LANES
SM_IK, SM_GLOW, SM_IW = 0, IDX_DIM, IDX_DIM + GLA_RANK
B_GV, B_GR, B_DQ, B_DZ, B_MA, B_MB, B_IQ = 0, 1024, 2048, 3072, 4096, 5120, 6144
B_COLS = B_IQ + IDX_Q
B_COL_TILE = B_COLS // 2

ROW_TILE = 2048
KEY_TILE = 512
NEG_BIG = -1e30
INT_MIN = -(2 ** 31)
NEG_INF_KEY = int(np.array(-np.inf, np.float32).view(np.int32)) ^ 0x7FFFFFFF


def _col(w, idx):
    return w[:, _OFF[idx]:_OFF[idx + 1]]


def _layer_norm_rows(x, g, b):
    mu = jnp.mean(x, axis=-1, keepdims=True)
    xc = x - mu
    var = jnp.mean(xc * xc, axis=-1, keepdims=True)
    return xc * lax.rsqrt(var + NORM_EPS) * g + b


def _dot(a, b):
    return jnp.dot(a, b, preferred_element_type=F32)


def _dot_nt(a, b):
    return lax.dot_general(a, b, (((1,), (1,)), ((), ())), preferred_element_type=F32)


def _dot_tn(a, b):
    return lax.dot_general(a, b, (((0,), (0,)), ((), ())), preferred_element_type=F32)


def _proj_kernel(x_ref, g_ref, b_ref, w_ref, wf_ref, kg_ref, kb_ref, o_ref, of_ref, hn_ref):
    @pl.when(pl.program_id(1) == 0)
    def _():
        hn = _layer_norm_rows(x_ref[...], g_ref[...], b_ref[...]).astype(BF16)
        hn_ref[...] = hn
        y = _dot(hn, wf_ref[...])
        of_ref[...] = y
        ik = y[:, F_SMALL:F_SMALL + IDX_DIM]
        of_ref[:, F_SMALL:F_SMALL + IDX_DIM] = _layer_norm_rows(ik, kg_ref[...], kb_ref[...])

    o_ref[...] = _dot(hn_ref[...], w_ref[...]).astype(BF16)


def _proj(x, g, b, w, wf, kg, kb, tm, tn):
    rows = x.shape[0]
    return pl.pallas_call(
        _proj_kernel,
        grid=(rows // tm, B_COLS // tn),
        in_specs=[
            pl.BlockSpec((tm, D_MODEL), lambda i, j: (i, 0)),
            pl.BlockSpec((1, D_MODEL), lambda i, j: (0, 0)),
            pl.BlockSpec((1, D_MODEL), lambda i, j: (0, 0)),
            pl.BlockSpec((D_MODEL, tn), lambda i, j: (0, j)),
            pl.BlockSpec((D_MODEL, F_COLS), lambda i, j: (0, 0)),
            pl.BlockSpec((1, IDX_DIM), lambda i, j: (0, 0)),
            pl.BlockSpec((1, IDX_DIM), lambda i, j: (0, 0)),
        ],
        out_specs=[pl.BlockSpec((tm, tn), lambda i, j: (i, j)),
                   pl.BlockSpec((tm, F_COLS), lambda i, j: (i, 0))],
        out_shape=[jax.ShapeDtypeStruct((rows, B_COLS), BF16), jax.ShapeDtypeStruct((rows, F_COLS), F32)],
        scratch_shapes=[pltpu.VMEM((tm, D_MODEL), BF16)],
        compiler_params=pltpu.CompilerParams(
            dimension_semantics=("parallel", "arbitrary"), vmem_limit_bytes=VMEM_LIMIT_BYTES),
        name="proj",
    )(x, g, b, w, wf, kg, kb)


def _gla_levels(blk):
    levels = []
    s = blk // 2
    while s >= 1:
        levels.append(s)
        s //= 2
    return levels


def _gla_decay_matrix(blk):
    i = np.arange(blk)[:, None]
    t = np.arange(blk)[None, :]
    mats = [(t <= i).astype(np.float32), (t > i).astype(np.float32)]
    for s in _gla_levels(blk):
        mid = (i // (2 * s)) * (2 * s) + s - 1
        lower = (i // s) % 2 == 1
        m = np.where(lower, ((t > mid) & (t <= i)), False).astype(np.float32)
        n = np.where(~lower, ((t > i) & (t <= mid)), False).astype(np.float32)
        mats.append(m + n)
    return np.concatenate(mats, axis=0)


def _gla_kernel(q_ref, k_ref, sm_ref, v_ref, r_ref, c_ref, w2_ref, gb_ref, ng_ref, s0_ref,
                o_ref, st_ref, s_scr, *, blk, n_blk, t_valid):
    ti = pl.program_id(1)
    levels = _gla_levels(blk)

    @pl.when(ti == 0)
    def _():
        s_scr[...] = s0_ref[0]

    hb = GLA_HEADS * blk
    ri = lax.broadcasted_iota(I32, (hb, hb), 0)
    ci = lax.broadcasted_iota(I32, (hb, hb), 1)
    level_id = jnp.where(ri == ci, 0, -1)
    for li, s in enumerate(levels):
        sh = s.bit_length() - 1
        same = (ri >> (sh + 1)) == (ci >> (sh + 1))
        level_id = jnp.where(same & (((ri >> sh) & 1) == 1) & (((ci >> sh) & 1) == 0), 1 + li, level_id)

    def stack(x, width):
        return jnp.concatenate([x[:, h * width:(h + 1) * width] for h in range(GLA_HEADS)], axis=0)
    row_iota = lax.broadcasted_iota(I32, (blk, 1), 0)
    cmat = c_ref[...]
    w2 = w2_ref[...]
    gbias = gb_ref[...]
    ng = ng_ref[...]

    def block(j):
        r0 = pl.multiple_of(j * blk, blk)
        rows = pl.ds(r0, blk)
        valid = (ti * (n_blk * blk) + r0 + row_iota) < t_valid
        gq = q_ref[rows, :] * (GLA_DK ** -0.5)
        gk = jnp.where(valid, k_ref[rows, :], 0.0)
        x = _dot(sm_ref[rows, :].astype(BF16), w2) + gbias
        logf = (jnp.minimum(x, 0.0) - jnp.log1p(jnp.exp(-jnp.abs(x)))) * (1.0 / GLA_TAU)
        logf = jnp.where(valid, logf, 0.0)
        hi = logf.astype(BF16)
        r1 = logf - hi.astype(F32)
        mid = r1.astype(BF16)
        lo = (r1 - mid.astype(F32)).astype(BF16)
        e_all = _dot(cmat, jnp.concatenate([hi, mid, lo], axis=0))
        qs = stack(gq, GLA_DK)
        ks = stack(gk, GLA_DK)
        vs = stack(v_ref[rows, :], GLA_DV)
        b_s = stack(e_all[0:blk], GLA_DK)
        a = jnp.where(level_id == 0, _dot_nt(qs.astype(BF16), ks.astype(BF16)), 0.0)
        for li in range(len(levels)):
            e = jnp.exp(stack(e_all[(2 + li) * blk:(3 + li) * blk], GLA_DK))
            p = _dot_nt((qs * e).astype(BF16), (ks * e).astype(BF16))
            a = jnp.where(level_id == 1 + li, p, a)
        o_intra = _dot(a.astype(BF16), vs)
        qe = (qs * jnp.exp(b_s)).astype(BF16)
        kd = (ks * jnp.exp(stack(e_all[blk:2 * blk], GLA_DK))).astype(BF16)
        for h in range(GLA_HEADS):
            hs = slice(h * blk, (h + 1) * blk)
            vsl = slice(h * GLA_DV, (h + 1) * GLA_DV)
            st = s_scr[h]
            o = _dot_nt(qe[hs], st.astype(BF16)) + o_intra[hs]
            dec = jnp.exp(b_s[(h + 1) * blk - 1:(h + 1) * blk, :])
            s_scr[h] = st * dec + _dot_tn(vs[hs], kd[hs])
            on = o * lax.rsqrt(jnp.mean(o * o, axis=-1, keepdims=True) + NORM_EPS) * ng
            gr = r_ref[rows, vsl].astype(F32)
            o_ref[rows, vsl] = (on * (gr * jax.nn.sigmoid(gr))).astype(BF16)

    n_live = jnp.clip((t_valid - ti * (n_blk * blk) + blk - 1) // blk, 0, n_blk)

    @pl.when(n_live < n_blk)
    def _():
        o_ref[...] = jnp.zeros(o_ref.shape, o_ref.dtype)

    _for_blocks(block, 0, n_live, GLA_UNROLLS)

    @pl.when(ti == pl.num_programs(1) - 1)
    def _():
        st_ref[0] = s_scr[...]


def _gla(pf, pb, cmat, w2pad, gbias, ng, s0t, *, n_seq, row0, rows_per_seq, t_valid, blk, n_blk):
    rb = blk * n_blk
    steps = rows_per_seq // rb
    base = row0 // rb

    def rmap(c):
        return lambda s, t: (base + s * steps + t, c)

    kern = functools.partial(_gla_kernel, blk=blk, n_blk=n_blk, t_valid=t_valid)
    return pl.pallas_call(
        kern,
        grid=(n_seq, steps),
        in_specs=[
            pl.BlockSpec((rb, GLA_QK), rmap(F_GQ // GLA_QK)),
            pl.BlockSpec((rb, GLA_QK), rmap(F_GK // GLA_QK)),
            pl.BlockSpec((rb, LANES), rmap(F_SMALL // LANES)),
            pl.BlockSpec((rb, GLA_V), rmap(B_GV // GLA_V)),
            pl.BlockSpec((rb, GLA_V), rmap(B_GR // GLA_V)),
            pl.BlockSpec(cmat.shape, lambda s, t: (0, 0)),
            pl.BlockSpec(w2pad.shape, lambda s, t: (0, 0)),
            pl.BlockSpec((1, GLA_QK), lambda s, t: (0, 0)),
            pl.BlockSpec((1, GLA_DV), lambda s, t: (0, 0)),
            pl.BlockSpec((1, GLA_HEADS, GLA_DV, GLA_DK), lambda s, t: (s, 0, 0, 0)),
        ],
        out_specs=[
            pl.BlockSpec((rb, GLA_V), lambda s, t: (s * steps + t, 0)),
            pl.BlockSpec((1, GLA_HEADS, GLA_DV, GLA_DK), lambda s, t: (s, 0, 0, 0)),
        ],
        out_shape=[
            jax.ShapeDtypeStruct((n_seq * rows_per_seq, GLA_V), BF16),
            jax.ShapeDtypeStruct((n_seq, GLA_HEADS, GLA_DV, GLA_DK), F32),
        ],
        scratch_shapes=[pltpu.VMEM((GLA_HEADS, GLA_DV, GLA_DK), F32)],
        compiler_params=pltpu.CompilerParams(
            dimension_semantics=("parallel", "arbitrary"), vmem_limit_bytes=VMEM_LIMIT_BYTES),
        name="gla",
    )(pf, pf, pf, pb, pb, cmat, w2pad, gbias, ng, s0t)


def _dsa_kernel(iq_ref, sm_ref, qd_ref, kit_ref, kt_ref, v_ref, o_ref,
                key_scr, m_scr, l_scr, acc_scr, *, tq, tk, n_keys, topk, n_kb_total):
    n_kb = n_kb_total
    lane_iota = lax.broadcasted_iota(I32, (1, tk), 1)

    iq = iq_ref[...]
    iq_hm = jnp.concatenate([iq[:, h * IDX_DIM:(h + 1) * IDX_DIM] for h in range(IDX_HEADS)], axis=0)
    w_i = sm_ref[:, SM_IW:SM_IW + IDX_HEADS] * IDX_W_SCALE
    w_cols = [jnp.broadcast_to(w_i[:, h:h + 1], (tq, LANES)) for h in range(IDX_HEADS)]

    def score_block(kb, carry):
        k0 = pl.multiple_of(kb * tk, tk)
        r = _dot(iq_hm, kit_ref[0, :, pl.ds(k0, tk)])
        sc = None
        for h in range(IDX_HEADS):
            rh = jnp.maximum(r[h * tq:(h + 1) * tq, :], 0.0)
            wh = jnp.concatenate([w_cols[h]] * (tk // LANES), axis=1)
            sc = rh * wh if sc is None else sc + rh * wh
        sc = jnp.where(k0 + lane_iota < n_keys, sc, -jnp.inf)
        bits = pltpu.bitcast(sc, I32)
        key_scr[:, pl.ds(k0, tk)] = bits ^ ((bits >> 31) & 0x7FFFFFFF)
        return carry

    lax.fori_loop(0, n_kb, score_block, 0)

    n_cols = n_kb * (tk // LANES)

    def count_ge(cand):
        cb = jnp.broadcast_to(cand, (tq, LANES))

        def body(kb, acc):
            for u in range(tk // LANES):
                c0 = pl.multiple_of(kb * tk + u * LANES, LANES)
                acc = acc + jnp.where(key_scr[:, pl.ds(c0, LANES)] >= cb, 1, 0)
            return acc

        acc = lax.fori_loop(0, n_kb, body, jnp.zeros((tq, LANES), I32))
        return jnp.sum(acc, axis=1, keepdims=True)

    bits_per_round = 4

    def search_cond(carry):
        i, lo, cge = carry
        return (i < 32) & (jnp.max(jnp.abs(cge - topk)) > 0)

    def search_round(carry):
        i, lo, cge = carry
        for u in range(bits_per_round):
            cand = lo + jnp.left_shift(jnp.int32(1), 31 - u - i)
            cnt = count_ge(cand)
            take = cnt >= topk
            lo = jnp.where(take, cand, lo)
            cge = jnp.where(take, cnt, cge)
        return i + bits_per_round, lo, cge

    lo0 = jnp.full((tq, 1), INT_MIN, I32)
    cge0 = jnp.zeros((tq, 1), I32) + n_cols * LANES
    _, thr, cge = lax.while_loop(search_cond, search_round, (jnp.int32(0), lo0, cge0))
    need_tie = (cge > topk) & (thr > NEG_INF_KEY)
    thr = jnp.maximum(thr, NEG_INF_KEY + 1)

    @pl.when(jnp.max(need_tie.astype(I32)) > 0)
    def _():
        want = topk - count_ge(thr + 1)
        thr_b = jnp.broadcast_to(thr, (tq, LANES))
        col_iota = lax.broadcasted_iota(I32, (tq, LANES), 1)

        def count_tied_below(m):
            mb = jnp.broadcast_to(m, (tq, LANES))

            def body(c, acc):
                c0 = pl.multiple_of(c * LANES, LANES)
                hit = (key_scr[:, pl.ds(c0, LANES)] == thr_b) & ((c0 + col_iota) < mb)
                return acc + jnp.where(hit, 1, 0)

            acc = lax.fori_loop(0, n_cols, body, jnp.zeros((tq, LANES), I32))
            return jnp.sum(acc, axis=1, keepdims=True)

        n_bits = int(math.ceil(math.log2(n_kb_total * tk + 1)))

        def idx_step(i, m):
            cand = m + jnp.left_shift(jnp.int32(1), n_bits - 1 - i)
            return jnp.where(count_tied_below(cand) <= want, cand, m)

        m_keep = lax.fori_loop(0, n_bits, idx_step, jnp.zeros((tq, 1), I32))
        mk_b = jnp.broadcast_to(jnp.where(need_tie, m_keep, jnp.int32(2 ** 30)), (tq, LANES))

        def demote(c, carry):
            c0 = pl.multiple_of(c * LANES, LANES)
            kv = key_scr[:, pl.ds(c0, LANES)]
            drop = (kv == thr_b) & ((c0 + col_iota) >= mk_b)
            key_scr[:, pl.ds(c0, LANES)] = jnp.where(drop, kv - 1, kv)
            return carry

        lax.fori_loop(0, n_cols, demote, 0)

    qd = qd_ref[...]
    win = lax.broadcasted_iota(I32, (tq, DSA_KV), 1) >> HEAD_SHIFT
    q_all = jnp.concatenate(
        [jnp.where(win == g, qd[:, r * DSA_KV:(r + 1) * DSA_KV].astype(F32), 0.0).astype(BF16)
         for g in range(DSA_KV_HEADS) for r in range(DSA_GROUP)], axis=0)
    m_scr[...] = jnp.full(m_scr.shape, NEG_BIG, F32)
    l_scr[...] = jnp.zeros(l_scr.shape, F32)
    acc_scr[...] = jnp.zeros(acc_scr.shape, F32)
    thr_t = jnp.broadcast_to(thr, (tq, LANES))

    def attend_block(kb, carry):
        k0 = pl.multiple_of(kb * tk, tk)
        sel = key_scr[:, pl.ds(k0, tk)] >= jnp.concatenate([thr_t] * (tk // LANES), axis=1)
        bias = jnp.where(sel, 0.0, NEG_BIG)
        s = _dot(q_all, kt_ref[0, :, pl.ds(k0, tk)]) + jnp.concatenate([bias] * DSA_HEADS, axis=0)
        m_old = m_scr[...]
        m_new = jnp.maximum(m_old, jnp.max(s, axis=1, keepdims=True))
        alpha = jnp.exp2(m_old - m_new)
        p = jnp.exp2(s - m_new)
        l_scr[...] = alpha * l_scr[...] + jnp.sum(p, axis=1, keepdims=True)
        acc_scr[...] = alpha * acc_scr[...] + _dot(p.astype(BF16), v_ref[0, pl.ds(k0, tk), :])
        m_scr[...] = m_new
        return carry

    lax.fori_loop(0, n_kb, attend_block, 0)

    o_all = acc_scr[...] / l_scr[...]
    for g in range(DSA_KV_HEADS):
        for r in range(DSA_GROUP):
            h = g * DSA_GROUP + r
            o_ref[:, h * DSA_HEAD_DIM:(h + 1) * DSA_HEAD_DIM] = (
                o_all[h * tq:(h + 1) * tq, g * DSA_HEAD_DIM:(g + 1) * DSA_HEAD_DIM].astype(BF16))


def _dsa(pf, pb, kit, kt, v, *, n_seq, row0, rows_per_seq, tq, tk, n_keys, topk):
    nk_pad = kit.shape[-1]
    n_kb_total = nk_pad // tk
    steps = rows_per_seq // tq
    base = row0 // tq

    def rmap(c):
        return lambda s, t: (base + s * steps + t, c)

    kern = functools.partial(_dsa_kernel, tq=tq, tk=tk, n_keys=n_keys, topk=topk, n_kb_total=n_kb_total)
    single = pl.Buffered(1)
    return pl.pallas_call(
        kern,
        grid=(n_seq, steps),
        in_specs=[
            pl.BlockSpec((tq, IDX_Q), rmap(B_IQ // IDX_Q)),
            pl.BlockSpec((tq, LANES), rmap(F_SMALL // LANES)),
            pl.BlockSpec((tq, DSA_Q), rmap(B_DQ // DSA_Q)),
            pl.BlockSpec((1, IDX_DIM, nk_pad), lambda s, t: (s, 0, 0), pipeline_mode=single),
            pl.BlockSpec((1, DSA_KV, nk_pad), lambda s, t: (s, 0, 0), pipeline_mode=single),
            pl.BlockSpec((1, nk_pad, DSA_KV), lambda s, t: (s, 0, 0), pipeline_mode=single),
        ],
        out_specs=pl.BlockSpec((tq, DSA_Q), lambda s, t: (s * steps + t, 0)),
        out_shape=jax.ShapeDtypeStruct((n_seq * rows_per_seq, DSA_Q), BF16),
        scratch_shapes=[
            pltpu.VMEM((tq, nk_pad), I32),
            pltpu.VMEM((DSA_HEADS * tq, 1), F32),
            pltpu.VMEM((DSA_HEADS * tq, 1), F32),
            pltpu.VMEM((DSA_HEADS * tq, DSA_KV), F32),
        ],
        compiler_params=pltpu.CompilerParams(
            dimension_semantics=("parallel", "arbitrary"), vmem_limit_bytes=VMEM_LIMIT_BYTES),
        name="dsa",
    )(pb, pf, pb, kit, kt, v)


BIT_GROUP = 256
SLAB_ROWS = 128
BLOCK_UNROLLS = (8, 4, 2, 1)
GLA_UNROLLS = (4, 2, 1)


def _for_blocks(block_fn, start, stop, unrolls=BLOCK_UNROLLS):
    for u in unrolls:
        trips = (stop - start) // u

        def body(j, carry, u=u, start=start):
            for t in range(u):
                block_fn(start + u * j + t)
            return carry

        lax.fori_loop(0, trips, body, 0)
        start = start + trips * u


def _slab(g):
    return g * DSA_HEAD_DIM // LANES * LANES


def _bit_transpose32(words):
    a = list(words)
    mask, j = 0x0000FFFF, 16
    while j:
        k = 0
        while k < 32:
            t = (a[k] ^ lax.shift_right_logical(a[k + j], jnp.int32(j))) & mask
            a[k] = a[k] ^ t
            a[k + j] = a[k + j] ^ (t << j)
            k = (k + j + 1) & ~j
        j >>= 1
        mask ^= (mask << j) & 0xFFFFFFFF
    return a


V_ROWS = 80
V_ONES_ROW = DSA_HEAD_DIM


def _dsa_t_kernel(iq_ref, sm_ref, qd_ref, ki_ref, k_ref, vt_ref, o_ref,
                  key_scr, m_scr, acc_scr, s_scr, plane_scr, active_scr, *, tq, tk, n_keys, topk):
    live = pl.program_id(1) * tq < n_keys

    @pl.when(live)
    def _():
        _dsa_t_body(iq_ref, sm_ref, qd_ref, ki_ref, k_ref, vt_ref, o_ref, key_scr, m_scr, acc_scr, s_scr,
                    plane_scr, active_scr,
                    tq=tq, tk=tk, n_keys=n_keys, topk=topk)

    @pl.when(jnp.logical_not(live))
    def _():
        o_ref[...] = jnp.zeros(o_ref.shape, o_ref.dtype)


def _dsa_t_body(iq_ref, sm_ref, qd_ref, ki_ref, k_ref, vt_ref, o_ref,
                key_scr, m_scr, acc_scr, s_scr, plane_scr, active_scr, *, tq, tk, n_keys, topk):
    qb = pl.program_id(1)
    q0 = qb * tq
    q_chunk_max = (q0 + tq - 1 - N_META) >> CHUNK_SHIFT
    limit = jnp.minimum(N_META + CHUNK * (q_chunk_max + 1), n_keys)
    n_kb = (limit + tk - 1) // tk
    n_full = jnp.minimum(jnp.minimum((q0 + N_META) // tk, n_keys // tk), n_kb)

    q_chunk = (q0 + lax.broadcasted_iota(I32, (1, tq), 1) - N_META) >> CHUNK_SHIFT
    row_iota = lax.broadcasted_iota(I32, (tk, tq), 0)
    plane_rows = key_scr.shape[0] // BIT_GROUP * 8

    @pl.when(qb == 0)
    def _():
        plane_scr[...] = jnp.zeros(plane_scr.shape, I32)

    iq = iq_ref[...]
    iq_hm = jnp.concatenate([iq[:, h * IDX_DIM:(h + 1) * IDX_DIM] for h in range(IDX_HEADS)], axis=0)
    w_t = sm_ref[...].T[SM_IW:SM_IW + IDX_HEADS, :] * IDX_W_SCALE
    iq_hm = iq_hm.astype(F32).T.astype(BF16)

    def score_block(kb, masked):
        k0 = pl.multiple_of(kb * tk, tk)
        r = _dot(ki_ref[0, pl.ds(k0, tk), :], iq_hm)
        sc = None
        for h in range(IDX_HEADS):
            t = jnp.maximum(r[:, h * tq:(h + 1) * tq], 0.0) * w_t[h:h + 1, :]
            sc = t if sc is None else sc + t
        if masked:
            key_pos = k0 + row_iota
            adm = (key_pos < n_keys) & (((key_pos - N_META) >> CHUNK_SHIFT) <= q_chunk)
            sc = jnp.where(adm, sc, -jnp.inf)
        bits = pltpu.bitcast(sc, I32)
        keys = bits ^ ((bits >> 31) & 0x7FFFFFFF)
        key_scr[pl.ds(k0, tk), :] = keys
        ukeys = keys ^ INT_MIN
        for u in range(tk // BIT_GROUP):
            words = [ukeys[u * BIT_GROUP + 8 * w:u * BIT_GROUP + 8 * w + 8, :] for w in range(32)]
            planes = _bit_transpose32(words)
            g_row = pl.multiple_of((kb * (tk // BIT_GROUP) + u) * 8, 8)
            for w in range(32):
                plane_scr[pl.ds(w * plane_rows + g_row, 8), :] = planes[w]

    def score_edge(kb, carry):
        score_block(kb, True)
        return carry

    _for_blocks(lambda kb: score_block(kb, False), 0, n_full)
    lax.fori_loop(n_full, n_kb, score_edge, 0)

    n_chunks = n_kb * (tk // LANES)
    sub_per_blk = tk // LANES

    def count_rows(pred):
        def body(kb, acc):
            for u in range(sub_per_blk):
                c0 = pl.multiple_of(kb * tk + u * LANES, LANES)
                acc = acc + jnp.where(pred(key_scr[pl.ds(c0, LANES), :], c0), 1, 0)
            return acc

        acc = lax.fori_loop(0, n_kb, body, jnp.zeros((LANES, tq), I32))
        return jnp.sum(acc, axis=0, keepdims=True)

    def count_ge(cand):
        return count_rows(lambda kv, c0: kv >= cand)

    n_groups = n_kb * (tk // BIT_GROUP)
    n_slabs = (n_groups * 8 + SLAB_ROWS - 1) // SLAB_ROWS
    slab_iota = lax.broadcasted_iota(I32, (SLAB_ROWS, tq), 0)

    def init_active(sl, carry):
        r0 = pl.multiple_of(sl * SLAB_ROWS, SLAB_ROWS)
        active_scr[pl.ds(r0, SLAB_ROWS), :] = jnp.where(r0 + slab_iota < n_groups * 8, -1, 0)
        return carry

    lax.fori_loop(0, n_slabs, init_active, 0)

    def select_pass(i, carry):
        thr_u, c_above, n_act, flip = carry
        cur = i * plane_rows
        prev = cur - plane_rows

        def body(sl, acc):
            r0 = pl.multiple_of(sl * SLAB_ROWS, SLAB_ROWS)
            rows = pl.ds(r0, SLAB_ROWS)
            act = active_scr[rows, :] & (plane_scr[pl.ds(pl.multiple_of(prev + r0, 8), SLAB_ROWS), :] ^ flip)
            active_scr[rows, :] = act
            return acc + lax.population_count(act & plane_scr[pl.ds(pl.multiple_of(cur + r0, 8), SLAB_ROWS), :])

        acc = lax.fori_loop(0, n_slabs, body, jnp.zeros((SLAB_ROWS, tq), I32))
        ones = jnp.sum(acc, axis=0, keepdims=True)
        take = c_above + ones >= topk
        thr_u = thr_u | jnp.where(take, jnp.left_shift(jnp.int32(1), 31 - i), 0)
        c_above = c_above + jnp.where(take, 0, ones)
        n_act = jnp.where(take, ones, n_act - ones)
        return thr_u, c_above, n_act, jnp.where(take, 0, -1)

    n_act = jnp.zeros((1, tq), I32) + n_groups * BIT_GROUP

    def count_top(sl, acc):
        rows = pl.ds(pl.multiple_of(sl * SLAB_ROWS, SLAB_ROWS), SLAB_ROWS)
        return acc + lax.population_count(active_scr[rows, :] & plane_scr[rows, :])

    acc0 = lax.fori_loop(0, n_slabs, count_top, jnp.zeros((SLAB_ROWS, tq), I32))
    ones0 = jnp.sum(acc0, axis=0, keepdims=True)
    take0 = ones0 >= topk
    carry0 = (jnp.where(take0, INT_MIN, 0), jnp.where(take0, 0, ones0),
              jnp.where(take0, ones0, n_act - ones0), jnp.where(take0, 0, -1))
    thr_u, c_above, n_act, _ = lax.fori_loop(1, 32, select_pass, carry0)
    thr = thr_u ^ INT_MIN
    cge = c_above + n_act
    need_tie = (cge > topk) & (thr > NEG_INF_KEY)
    thr = jnp.maximum(thr, NEG_INF_KEY + 1)

    @pl.when(jnp.max(need_tie.astype(I32)) > 0)
    def _():
        want = topk - count_ge(thr + 1)
        sub_iota = lax.broadcasted_iota(I32, (LANES, tq), 0)
        n_bits = int(math.ceil(math.log2(key_scr.shape[0] + 1)))

        def idx_step(i, m):
            cand = m + jnp.left_shift(jnp.int32(1), n_bits - 1 - i)
            tied_below = count_rows(lambda kv, c0: (kv == thr) & ((c0 + sub_iota) < cand))
            return jnp.where(tied_below <= want, cand, m)

        m_keep = lax.fori_loop(0, n_bits, idx_step, jnp.zeros((1, tq), I32))
        m_keep = jnp.where(need_tie, m_keep, jnp.int32(2 ** 30))

        def demote(c, carry):
            c0 = pl.multiple_of(c * LANES, LANES)
            kv = key_scr[pl.ds(c0, LANES), :]
            drop = (kv == thr) & ((c0 + sub_iota) >= m_keep)
            key_scr[pl.ds(c0, LANES), :] = jnp.where(drop, kv - 1, kv)
            return carry

        lax.fori_loop(0, n_chunks, demote, 0)

    qd = qd_ref[...]
    win = lax.broadcasted_iota(I32, (tq, LANES), 1) >> HEAD_SHIFT
    q_pad = []
    for g in range(DSA_KV_HEADS):
        lo = _slab(g)
        q_pad.append(jnp.concatenate(
            [jnp.where(win == g % 2, qd[:, r * DSA_KV + lo:r * DSA_KV + lo + LANES].astype(F32), 0.0)
             for r in range(DSA_GROUP)], axis=0))
    q_pad = [x.T.astype(BF16) for x in q_pad]
    m_scr[...] = jnp.full(m_scr.shape, NEG_BIG, F32)
    acc_scr[...] = jnp.zeros(acc_scr.shape, F32)
    s_scr[0] = _dot(k_ref[0, pl.ds(0, tk), 0:LANES], q_pad[0])

    def attend_block(kb):
        k0 = pl.multiple_of(kb * tk, tk)
        bias = jnp.where(key_scr[pl.ds(k0, tk), :] >= thr, 0.0, NEG_BIG)
        bias_g = jnp.concatenate([bias] * DSA_GROUP, axis=1)
        kblk = k_ref[0, pl.ds(k0, tk), :]
        k1 = pl.multiple_of(jnp.minimum(kb + 1, n_kb - 1) * tk, tk)
        for g in range(DSA_KV_HEADS):
            if g + 1 < DSA_KV_HEADS:
                s_scr[(g + 1) % 2] = _dot(kblk[:, _slab(g + 1):_slab(g + 1) + LANES], q_pad[g + 1])
            else:
                s_scr[0] = _dot(k_ref[0, pl.ds(k1, tk), 0:LANES], q_pad[0])
            s = s_scr[g % 2] + bias_g
            m_old = m_scr[g]
            m_new = jnp.maximum(m_old, jnp.max(s, axis=0, keepdims=True))
            alpha = jnp.exp2(m_old[0:1] - m_new[0:1])
            p = jnp.exp2(s - m_new[0:1]).astype(BF16)
            acc_scr[g] = acc_scr[g] * alpha + _dot(vt_ref[0, g, :, pl.ds(k0, tk)], p)
            m_scr[g] = m_new

    _for_blocks(attend_block, 0, n_kb)

    for g in range(DSA_KV_HEADS):
        acc = acc_scr[g]
        og = acc[0:DSA_HEAD_DIM] / acc[V_ONES_ROW:V_ONES_ROW + 1]
        for r in range(DSA_GROUP):
            c0 = (g * DSA_GROUP + r) * DSA_HEAD_DIM
            o_ref[:, c0:c0 + DSA_HEAD_DIM] = og[:, r * tq:(r + 1) * tq].T.astype(BF16)


def _dsa_t(pf, pb, ki, k, vt, *, n_seq, rows_per_seq, tk, n_keys, topk):
    tq = LANES
    nk_pad = k.shape[1]
    steps = rows_per_seq // tq

    def rmap(c):
        return lambda s, t: (s * steps + t, c)

    kern = functools.partial(_dsa_t_kernel, tq=tq, tk=tk, n_keys=n_keys, topk=topk)
    single = pl.Buffered(1)
    return pl.pallas_call(
        kern,
        grid=(n_seq, steps),
        in_specs=[
            pl.BlockSpec((tq, IDX_Q), rmap(B_IQ // IDX_Q)),
            pl.BlockSpec((tq, LANES), rmap(F_SMALL // LANES)),
            pl.BlockSpec((tq, DSA_Q), rmap(B_DQ // DSA_Q)),
            pl.BlockSpec((1, nk_pad, IDX_DIM), lambda s, t: (s, 0, 0), pipeline_mode=single),
            pl.BlockSpec((1, nk_pad, DSA_KV), lambda s, t: (s, 0, 0), pipeline_mode=single),
            pl.BlockSpec((1, DSA_KV_HEADS, V_ROWS, nk_pad), lambda s, t: (s, 0, 0, 0), pipeline_mode=single),
        ],
        out_specs=pl.BlockSpec((tq, DSA_Q), lambda s, t: (s * steps + t, 0)),
        out_shape=jax.ShapeDtypeStruct((n_seq * rows_per_seq, DSA_Q), BF16),
        scratch_shapes=[
            pltpu.VMEM((nk_pad, tq), I32),
            pltpu.VMEM((DSA_KV_HEADS, 8, DSA_GROUP * tq), F32),
            pltpu.VMEM((DSA_KV_HEADS, V_ROWS, DSA_GROUP * tq), F32),
            pltpu.VMEM((2, tk, DSA_GROUP * tq), F32),
            pltpu.VMEM((nk_pad + SLAB_ROWS, tq), I32),
            pltpu.VMEM((_round_up(nk_pad // BIT_GROUP * 8, SLAB_ROWS), tq), I32),
        ],
        compiler_params=pltpu.CompilerParams(
            dimension_semantics=("parallel", "arbitrary"), vmem_limit_bytes=VMEM_LIMIT_BYTES),
        name="dsa_t",
    )(pb, pf, pb, ki, k, vt)


def _out_kernel(x_ref, ya_ref, ob_ref, z_ref, ma_ref, mb_ref, wg_ref, wd_ref, wo_ref,
                gate_ref, lig_ref, lib_ref, lg_ref, lb_ref, o_ref, *, alpha):
    hn = _layer_norm_rows(x_ref[...], lig_ref[...], lib_ref[...])
    y_a = _dot(ya_ref[...], wg_ref[...])
    z = z_ref[...].astype(F32)
    yb_in = ob_ref[...].astype(F32) * (z * jax.nn.sigmoid(z))
    y_b = _dot(yb_in.astype(BF16), wd_ref[...])
    ga = jax.nn.sigmoid(ma_ref[...].astype(F32) + gate_ref[0:1, :])
    gb = jax.nn.sigmoid(mb_ref[...].astype(F32) + gate_ref[1:2, :])
    merged = ga * y_a + gb * y_b
    y = alpha * hn + _dot(merged.astype(BF16), wo_ref[...])
    o_ref[...] = _layer_norm_rows(y, lg_ref[...], lb_ref[...])


def _out(x, ya, ob, pb, wg, wd, wo, gate, lig, lib, lg, lb, tm, alpha, row0):
    rows = ya.shape[0]
    base = row0 // tm
    row = lambda c: pl.BlockSpec((tm, D_MODEL), lambda i, c=c: (i, c))
    off = lambda c: pl.BlockSpec((tm, D_MODEL), lambda i, c=c: (base + i, c))
    full = lambda a: pl.BlockSpec(a.shape, lambda i: (0, 0))
    return pl.pallas_call(
        functools.partial(_out_kernel, alpha=alpha),
        grid=(rows // tm,),
        in_specs=[off(0), row(0), row(0), off(B_DZ // D_MODEL), off(B_MA // D_MODEL), off(B_MB // D_MODEL),
                  full(wg), full(wd), full(wo), full(gate), full(lig), full(lib), full(lg), full(lb)],
        out_specs=pl.BlockSpec((tm, D_MODEL), lambda i: (i, 0)),
        out_shape=jax.ShapeDtypeStruct((rows, D_MODEL), F32),
        compiler_params=pltpu.CompilerParams(
            dimension_semantics=("parallel",), vmem_limit_bytes=VMEM_LIMIT_BYTES),
        name="out_proj",
    )(x, ya, ob, pb, pb, pb, wg, wd, wo, gate, lig, lib, lg, lb)


def _round_up(a, b):
    return -(-a // b) * b


def _forward(x_prompt, x_sample, cache_k, cache_v, cache_idx_k, state_gla, meta, ln_in_g, ln_in_b,
             w_in, gla_w2, gla_gate_b, gla_norm_g, idx_kn_g, idx_kn_b, w_gla, w_dsa, gate_b,
             w_out, ln_g, ln_b):
    depth = w_in.shape[0]
    assert depth == 1, "single-layer trunk"
    bsz, seq, _ = x_prompt.shape
    dbsz, dseq, _ = x_sample.shape
    past = cache_k.shape[2]
    t_p = N_META + seq
    tp = _round_up(t_p, KEY_TILE)
    assert tp % LANES == 0 and tp % GLA_BLOCK == 0
    row_s = bsz * tp
    rows = _round_up(row_s + dbsz * dseq, ROW_TILE)
    alpha = (2.0 * depth) ** 0.25

    pieces = []
    for b in range(bsz):
        pieces += [meta.astype(F32), x_prompt[b], jnp.zeros((tp - t_p, D_MODEL), F32)]
    pieces += [x_sample.reshape(dbsz * dseq, D_MODEL), jnp.zeros((rows - row_s - dbsz * dseq, D_MODEL), F32)]
    x_all = jnp.concatenate(pieces, axis=0)

    w = w_in[0]
    w_f = jnp.concatenate([_col(w, _GQ), _col(w, _GK), _col(w, _DK), _col(w, _DV), _col(w, _IK),
                           _col(w, _GLOW), _col(w, _IW),
                           jnp.zeros((D_MODEL, LANES - IDX_DIM - GLA_RANK - IDX_HEADS), F32)], axis=1).astype(BF16)
    w_dq = _col(w, _DQ).reshape(D_MODEL, DSA_KV_HEADS, DSA_GROUP, DSA_HEAD_DIM)
    w_dq = (jnp.swapaxes(w_dq, 1, 2) * (DSA_SCALE * math.log2(math.e))).reshape(D_MODEL, DSA_Q)
    w_b = jnp.concatenate([_col(w, _GV), _col(w, _GR), w_dq, _col(w, _DZ), _col(w, _MA), _col(w, _MB),
                           _col(w, _IQ)], axis=1).astype(BF16)
    lig = ln_in_g.reshape(1, D_MODEL)
    lib = ln_in_b.reshape(1, D_MODEL)

    pb, pf = _proj(x_all, lig, lib, w_b, w_f, idx_kn_g[0].reshape(1, IDX_DIM), idx_kn_b[0].reshape(1, IDX_DIM),
                   ROW_TILE // 4, B_COL_TILE)

    w2pad = jnp.zeros((LANES, GLA_QK), F32).at[SM_GLOW:SM_GLOW + GLA_RANK].set(gla_w2[0]).astype(BF16)
    gbias = gla_gate_b[0].reshape(1, GLA_QK)
    ng = gla_norm_g[0].reshape(1, GLA_DV)
    blk_p = min(GLA_BLOCK, t_p)
    cm_p = jnp.asarray(np.tile(_gla_decay_matrix(blk_p), (1, 3)), BF16)
    s0_p = jnp.zeros((bsz, GLA_HEADS, GLA_DV, GLA_DK), F32)
    ya_p, st_p = _gla(pf, pb, cm_p, w2pad, gbias, ng, s0_p, n_seq=bsz, row0=0, rows_per_seq=tp,
                      t_valid=t_p, blk=blk_p, n_blk=KEY_TILE // blk_p)
    blk_s = min(GLA_BLOCK, dseq)
    assert dseq % blk_s == 0 and row_s % dseq == 0
    cm_s = jnp.asarray(np.tile(_gla_decay_matrix(blk_s), (1, 3)), BF16)
    s0_s = jnp.swapaxes(state_gla[0], -1, -2)
    ya_s, st_s = _gla(pf, pb, cm_s, w2pad, gbias, ng, s0_s, n_seq=dbsz, row0=row_s, rows_per_seq=dseq,
                      t_valid=dseq, blk=blk_s, n_blk=dseq // blk_s)

    dk = pf[:, F_DK:F_DK + DSA_KV]
    dv = pf[:, F_DV:F_DV + DSA_KV]
    ki = pf[:, F_SMALL:F_SMALL + IDX_DIM]
    dk_p = dk[:row_s].reshape(bsz, tp, DSA_KV_HEADS, DSA_HEAD_DIM)
    dv_p = dv[:row_s].reshape(bsz, tp, DSA_KV_HEADS, DSA_HEAD_DIM)
    ki_p = ki[:row_s].reshape(bsz, tp, IDX_DIM)
    vt_p = jnp.concatenate([jnp.transpose(dv_p, (0, 2, 3, 1)),
                            jnp.ones((bsz, DSA_KV_HEADS, 1, tp), F32),
                            jnp.zeros((bsz, DSA_KV_HEADS, V_ROWS - DSA_HEAD_DIM - 1, tp), F32)], axis=2).astype(BF16)
    topk_p = min(TOPK_MAX, (t_p - N_META) // 4)
    ob_p = _dsa_t(pf, pb, ki_p.astype(BF16), dk[:row_s].reshape(bsz, tp, DSA_KV).astype(BF16), vt_p,
                  n_seq=bsz, rows_per_seq=tp, tk=KEY_TILE, n_keys=t_p, topk=topk_p)

    n_keys_s = past + dseq
    nk_s = _round_up(n_keys_s, KEY_TILE)
    dk_s = dk[row_s:row_s + dbsz * dseq].reshape(dbsz, dseq, DSA_KV_HEADS, DSA_HEAD_DIM)
    dv_s = dv[row_s:row_s + dbsz * dseq].reshape(dbsz, dseq, DSA_KV_HEADS, DSA_HEAD_DIM)
    ki_s = ki[row_s:row_s + dbsz * dseq].reshape(dbsz, dseq, IDX_DIM)
    kpad = jnp.zeros((dbsz, nk_s - n_keys_s, DSA_KV_HEADS, DSA_HEAD_DIM), F32)
    k_all = jnp.concatenate([cache_k[0], dk_s, kpad], axis=1)
    v_all = jnp.concatenate([cache_v[0], dv_s, kpad], axis=1)
    ki_all = jnp.concatenate([cache_idx_k[0], ki_s, jnp.zeros((dbsz, nk_s - n_keys_s, IDX_DIM), F32)], axis=1)
    kt_s = jnp.transpose(k_all, (0, 2, 3, 1)).reshape(dbsz, DSA_KV, nk_s).astype(BF16)
    v_s = v_all.reshape(dbsz, nk_s, DSA_KV).astype(BF16)
    kit_s = jnp.transpose(ki_all, (0, 2, 1)).astype(BF16)
    topk_s = min(TOPK_MAX, n_keys_s // 4)
    ob_s = _dsa(pf, pb, kit_s, kt_s, v_s, n_seq=dbsz, row0=row_s, rows_per_seq=dseq, tq=dseq, tk=KEY_TILE,
                n_keys=n_keys_s, topk=topk_s)

    out_w = (w_gla[0].astype(BF16), w_dsa[0].astype(BF16), w_out[0].astype(BF16), gate_b[0], lig, lib,
             ln_g[0].reshape(1, D_MODEL), ln_b[0].reshape(1, D_MODEL))
    tm_out = ROW_TILE // 4
    tm_s = math.gcd(tm_out, dbsz * dseq)
    assert row_s % tm_out == 0 and tm_s % 16 == 0
    y_p = _out(x_all, ya_p, ob_p, pb, *out_w, tm_out, alpha, 0)
    y_s = _out(x_all, ya_s, ob_s, pb, *out_w, tm_s, alpha, row_s)

    y_prompt = y_p.reshape(bsz, tp, D_MODEL)[:, N_META:t_p]
    y_sample = y_s.reshape(dbsz, dseq, D_MODEL)
    k_prompt = dk_p[:, :t_p][None]
    v_prompt = dv_p[:, :t_p][None]
    idx_k_prompt = ki_p[:, :t_p][None]
    gla_prompt = jnp.swapaxes(st_p, -1, -2)[None]
    k_sample = dk_s[None]
    v_sample = dv_s[None]
    idx_k_sample = ki_s[None]
    gla_sample = jnp.swapaxes(st_s, -1, -2)[None]
    return (y_prompt, y_sample, k_prompt, v_prompt, idx_k_prompt, gla_prompt,
            k_sample, v_sample, idx_k_sample, gla_sample)


def kernel(x_prompt, x_sample, cache_k, cache_v, cache_idx_k, state_gla, meta, ln_in_g, ln_in_b,
           w_in, gla_w2, gla_gate_b, gla_norm_g, idx_kn_g, idx_kn_b, w_gla, w_dsa, gate_b,
           w_out, ln_g, ln_b):
    return _forward(x_prompt, x_sample, cache_k, cache_v, cache_idx_k, state_gla, meta, ln_in_g, ln_in_b,
                    w_in, gla_w2, gla_gate_b, gla_norm_g, idx_kn_g, idx_kn_b, w_gla, w_dsa, gate_b,
                    w_out, ln_g, ln_b)
```
